```python
import jax, jax.numpy as jnp
from jax import lax
import numpy as np

D_MODEL = 1024
BATCH = 16
SEQ = 256
DEPTH = 1
DEC_BATCH = 8
DEC_SEQ = 1024
PAST_LEN = 256

GRID_W = 64
CHUNK = 128
EPS = 1e-6
SSD_WIDTH = D_MODEL
SSD_HEAD_DIM = 64
SSD_HEADS = SSD_WIDTH // SSD_HEAD_DIM
SSD_GROUPS = 4
SSD_STATE = 128
CONV_K = 3
CONV_CH = SSD_WIDTH + 2 * SSD_GROUPS * SSD_STATE
RET_HEADS = 8
RET_QK_DIM = 64
RET_V_DIM = 128
RET_QK_WIDTH = RET_HEADS * RET_QK_DIM
RET_V_WIDTH = RET_HEADS * RET_V_DIM
MIX_WIDTH = SSD_WIDTH + RET_V_WIDTH
ROPE_BASE = 10000.0
IN_COLS = SSD_WIDTH + CONV_CH + 2 * SSD_HEADS + 2 * RET_QK_WIDTH + 2 * RET_V_WIDTH

kernel_name = 'hybrid_ssd_retention_flow_step'


def _rms(x, w):
    xf = x.astype(jnp.float32)
    y = xf * lax.rsqrt(jnp.mean(xf * xf, axis=-1, keepdims=True) + EPS)
    return (y * w).astype(x.dtype)


def _dwconv(x, w, b):
    y = lax.conv_general_dilated(x, w[:, None, :].astype(x.dtype), (1,),
                                 [(CONV_K // 2, CONV_K // 2)],
                                 dimension_numbers=('NWC', 'WIO', 'NWC'),
                                 feature_group_count=x.shape[-1])
    return y + b.astype(x.dtype)


def _grid_angles(L):
    rows = L // GRID_W
    row = jnp.repeat(jnp.arange(rows, dtype=jnp.float32), GRID_W)
    col = jnp.tile(jnp.arange(GRID_W, dtype=jnp.float32), rows)
    half = RET_QK_DIM // 2
    inv = ROPE_BASE ** (-jnp.arange(0, half, 2, dtype=jnp.float32) / half)
    return row[:, None] * inv, col[:, None] * inv


def _rot(x, ang):
    x1, x2 = jnp.split(x, 2, axis=-1)
    c = jnp.cos(ang)[None, :, None, :]
    s = jnp.sin(ang)[None, :, None, :]
    return jnp.concatenate([x1 * c - x2 * s, x2 * c + x1 * s], axis=-1)


def _grid_rope(x, angles):
    xr, xc = jnp.split(x, 2, axis=-1)
    return jnp.concatenate([_rot(xr, angles[0]), _rot(xc, angles[1])], axis=-1)


def _chunk_scan(q, k, v, log_a, s0):
    Bsz, L, H, _ = q.shape
    P = v.shape[-1]
    nc = L // CHUNK

    def to_chunks(t):
        return jnp.moveaxis(t.reshape((Bsz, nc, CHUNK) + t.shape[2:]), 1, 0)

    causal = jnp.tril(jnp.ones((CHUNK, CHUNK), dtype=bool))[None, :, :, None]

    def step(s, inp):
        qi, ki, vi, ai = inp
        cum = jnp.cumsum(ai, axis=1)
        seg = cum[:, :, None, :] - cum[:, None, :, :]
        decay = jnp.where(causal, jnp.exp(jnp.where(causal, seg, 0.0)), 0.0)
        scores = jnp.einsum('bihn,bjhn->bijh', qi, ki) * decay
        y = (jnp.einsum('bijh,bjhp->bihp', scores, vi)
             + jnp.einsum('bihn,bhnp->bihp', qi, s) * jnp.exp(cum)[..., None])
        tail = jnp.exp(cum[:, -1:, :] - cum)
        s_new = (s * jnp.exp(cum[:, -1, :])[:, :, None, None]
                 + jnp.einsum('bjhn,bjhp,bjh->bhnp', ki, vi, tail))
        return s_new, y

    s_fin, ys = lax.scan(step, s0, (to_chunks(q), to_chunks(k), to_chunks(v), to_chunks(log_a)))
    y = jnp.moveaxis(ys, 0, 1).reshape(Bsz, L, H, P)
    return y, s_fin


def _bidir(q, k_f, k_b, v, la_f, la_b, s0_f, s0_b):
    y_f, s_f = _chunk_scan(q, k_f, v, la_f, s0_f)
    fl = lambda t: jnp.flip(t, axis=1)
    y_b, s_b = _chunk_scan(fl(q), fl(k_b), fl(v), fl(la_b), s0_b)
    return y_f + fl(y_b), s_f, s_b


def _mixer(h, s0_ssd, s0_ret, angles, w_in, conv_w, conv_b, A_log, dt_bias, D_skip,
           ssd_norm_w, ret_decay, ret_norm_w, w_out):
    f32 = jnp.float32
    Bsz, L, _ = h.shape
    u = h @ w_in
    sizes = [SSD_WIDTH, CONV_CH, 2 * SSD_HEADS, RET_QK_WIDTH, RET_QK_WIDTH, RET_V_WIDTH]
    z, xbc, dt_raw, q, k, v, g = jnp.split(u, [int(i) for i in np.cumsum(sizes)], axis=-1)

    xbc = jax.nn.silu(_dwconv(xbc, conv_w, conv_b))
    gn = SSD_GROUPS * SSD_STATE
    xs, Bm, Cm = jnp.split(xbc, [SSD_WIDTH, SSD_WIDTH + gn], axis=-1)
    xs = xs.reshape(Bsz, L, SSD_HEADS, SSD_HEAD_DIM).astype(f32)
    rep = SSD_HEADS // SSD_GROUPS
    Bm = jnp.repeat(Bm.reshape(Bsz, L, SSD_GROUPS, SSD_STATE).astype(f32), rep, axis=2)
    Cm = jnp.repeat(Cm.reshape(Bsz, L, SSD_GROUPS, SSD_STATE).astype(f32), rep, axis=2)
    dt = jax.nn.softplus(dt_raw.reshape(Bsz, L, 2, SSD_HEADS).astype(f32) + dt_bias.astype(f32))
    log_a = dt * (-jnp.exp(A_log.astype(f32)))
    k_dir = Bm[:, :, None] * dt[..., None]
    y_s, sf, sb = _bidir(Cm, k_dir[:, :, 0], k_dir[:, :, 1], xs, log_a[:, :, 0], log_a[:, :, 1],
                         s0_ssd[:, 0].astype(f32), s0_ssd[:, 1].astype(f32))
    y_s = (y_s + D_skip.astype(f32)[:, None] * xs).reshape(Bsz, L, SSD_WIDTH)
    y_s = _rms(y_s * jax.nn.silu(z.astype(f32)), ssd_norm_w)

    qr = q.reshape(Bsz, L, RET_HEADS, RET_QK_DIM).astype(f32)
    kr = k.reshape(Bsz, L, RET_HEADS, RET_QK_DIM).astype(f32) * (RET_QK_DIM ** -0.5)
    if angles is not None:
        qr = _grid_rope(qr, angles)
        kr = _grid_rope(kr, angles)
    vr = v.reshape(Bsz, L, RET_HEADS, RET_V_DIM).astype(f32)
    lam = -jnp.exp(ret_decay.astype(f32))
    la_f = jnp.broadcast_to(lam[0], (Bsz, L, RET_HEADS))
    la_b = jnp.broadcast_to(lam[1], (Bsz, L, RET_HEADS))
    y_r, rf, rb = _bidir(qr, kr, kr, vr, la_f, la_b,
                         s0_ret[:, 0].astype(f32), s0_ret[:, 1].astype(f32))
    mu = jnp.mean(y_r, axis=-1, keepdims=True)
    var = jnp.mean(jnp.square(y_r - mu), axis=-1, keepdims=True)
    y_r = ((y_r - mu) * lax.rsqrt(var + EPS)).reshape(Bsz, L, RET_V_WIDTH)
    y_r = y_r * ret_norm_w * jax.nn.silu(g.astype(f32))

    mix = jnp.concatenate([y_s, y_r], axis=-1).astype(h.dtype)
    out = mix @ w_out
    return out, jnp.stack([sf, sb], axis=1), jnp.stack([rf, rb], axis=1)


def _layer(x, cond, s0_ssd, s0_ret, angles, w_mod, b_mod, norm_pre_w, norm_post_w, w_in,
           conv_w, conv_b, A_log, dt_bias, D_skip, ssd_norm_w, ret_decay, ret_norm_w, w_out):
    mod = jax.nn.silu(cond) @ w_mod + b_mod
    shift, scale, gate = jnp.split(mod, 3, axis=-1)
    h = _rms(x, norm_pre_w) * (1 + scale[:, None, :]) + shift[:, None, :]
    out, s_ssd, s_ret = _mixer(h.astype(x.dtype), s0_ssd, s0_ret, angles, w_in, conv_w, conv_b,
                               A_log, dt_bias, D_skip, ssd_norm_w, ret_decay, ret_norm_w, w_out)
    y = x + gate[:, None, :] * _rms(out, norm_post_w)
    return y.astype(x.dtype), s_ssd, s_ret


def setup_inputs(seed: int = 0) -> dict:
    key = jax.random.key(seed)
    ks = jax.random.split(key, 20)
    f32 = jnp.float32
    nrm = lambda k, s: jax.random.normal(k, s, f32)
    dt = jnp.exp(jax.random.uniform(ks[12], (DEPTH, 2, SSD_HEADS), f32)
                 * (jnp.log(0.1) - jnp.log(0.001)) + jnp.log(0.001))
    gammas = 1.0 - 2.0 ** (-5.0 - jnp.arange(RET_HEADS, dtype=f32))
    ret_base = jnp.log(-jnp.log(gammas))
    return {
        'x_prompt': nrm(ks[0], (BATCH, SEQ, D_MODEL)),
        'x_sample': nrm(ks[1], (DEC_BATCH, DEC_SEQ, D_MODEL)),
        'state_ssd': 0.1 * nrm(ks[2], (DEC_BATCH, DEPTH, 2, SSD_HEADS, SSD_STATE, SSD_HEAD_DIM)),
        'state_ret': 0.1 * nrm(ks[3], (DEC_BATCH, DEPTH, 2, RET_HEADS, RET_QK_DIM, RET_V_DIM)),
        'c': nrm(ks[4], (DEC_BATCH, D_MODEL)),
        'c_ctx': nrm(ks[5], (D_MODEL,)),
        'w_mod': nrm(ks[6], (DEPTH, D_MODEL, 3 * D_MODEL)) * D_MODEL ** -0.5,
        'b_mod': 0.02 * nrm(ks[7], (DEPTH, 3 * D_MODEL)),
        'norm_pre_w': 1.0 + 0.02 * nrm(ks[8], (DEPTH, D_MODEL)),
        'norm_post_w': 1.0 + 0.02 * nrm(ks[9], (DEPTH, D_MODEL)),
        'w_in': nrm(ks[10], (DEPTH, D_MODEL, IN_COLS)) * D_MODEL ** -0.5,
        'conv_w': nrm(ks[11], (DEPTH, CONV_K, CONV_CH)) * CONV_K ** -0.5,
        'conv_b': 0.02 * nrm(ks[13], (DEPTH, CONV_CH)),
        'ssd_A_log': jnp.log(jax.random.uniform(ks[14], (DEPTH, 2, SSD_HEADS), f32, 1.0, 16.0)),
        'ssd_dt_bias': dt + jnp.log(-jnp.expm1(-dt)),
        'ssd_D': 1.0 + 0.02 * nrm(ks[15], (DEPTH, SSD_HEADS)),
        'ssd_norm_w': 1.0 + 0.02 * nrm(ks[16], (DEPTH, SSD_WIDTH)),
        'ret_decay': ret_base + 0.01 * nrm(ks[17], (DEPTH, 2, RET_HEADS)),
        'ret_norm_w': 1.0 + 0.02 * nrm(ks[18], (DEPTH, RET_V_WIDTH)),
        'w_out': nrm(ks[19], (DEPTH, MIX_WIDTH, D_MODEL)) * MIX_WIDTH ** -0.5,
    }


def reference(x_prompt, x_sample, state_ssd, state_ret, c, c_ctx, w_mod, b_mod, norm_pre_w,
              norm_post_w, w_in, conv_w, conv_b, ssd_A_log, ssd_dt_bias, ssd_D, ssd_norm_w,
              ret_decay, ret_norm_w, w_out):
    angles = _grid_angles(x_sample.shape[1])
    nb = x_prompt.shape[0]
    zeros_ssd = jnp.zeros((nb, 2, SSD_HEADS, SSD_STATE, SSD_HEAD_DIM), jnp.float32)
    zeros_ret = jnp.zeros((nb, 2, RET_HEADS, RET_QK_DIM, RET_V_DIM), jnp.float32)
    yp, ys = x_prompt, x_sample
    ssd_states, ret_states = [], []
    for l in range(DEPTH):
        p = (w_mod[l], b_mod[l], norm_pre_w[l], norm_post_w[l], w_in[l], conv_w[l], conv_b[l],
             ssd_A_log[l], ssd_dt_bias[l], ssd_D[l], ssd_norm_w[l], ret_decay[l], ret_norm_w[l],
             w_out[l])
        yp, s_ssd, s_ret = _layer(yp, c_ctx[None, :], zeros_ssd, zeros_ret, None, *p)
        ys, _, _ = _layer(ys, c, state_ssd[:, l], state_ret[:, l], angles, *p)
        ssd_states.append(s_ssd)
        ret_states.append(s_ret)
    new_state_ssd = jnp.stack(ssd_states, axis=1)
    new_state_ret = jnp.stack(ret_states, axis=1)
    return (yp, ys, new_state_ssd, new_state_ret)
```

```python
import functools

import jax
import jax.numpy as jnp
from jax import lax
from jax.experimental import pallas as pl
from jax.experimental.pallas import tpu as pltpu

F32 = jnp.float32
BF16 = jnp.bfloat16

D_MODEL = 1024
CHUNK = 128
GRID_W = 64
EPS = 1e-6
SSD_WIDTH = 1024
SSD_HEAD_DIM = 64
SSD_HEADS = 16
SSD_GROUPS = 4
SSD_STATE = 128
CONV_CH = SSD_WIDTH + 2 * SSD_GROUPS * SSD_STATE
RET_HEADS = 8
RET_QK_DIM = 64
RET_V_DIM = 128
RET_QK_WIDTH = RET_HEADS * RET_QK_DIM
RET_V_WIDTH = RET_HEADS * RET_V_DIM
MIX_WIDTH = SSD_WIDTH + RET_V_WIDTH
ROPE_BASE = 10000.0
LANES = 128
DT_PAD = LANES

OFF_Z = 0
OFF_XBC = OFF_Z + SSD_WIDTH
OFF_DT = OFF_XBC + CONV_CH
OFF_Q = OFF_DT + DT_PAD
OFF_K = OFF_Q + RET_QK_WIDTH
OFF_V = OFF_K + RET_QK_WIDTH
OFF_G = OFF_V + RET_V_WIDTH
IN_COLS_PACKED = OFF_G + RET_V_WIDTH

NEG_BIG = -1e30
VMEM_LIMIT = 56 * 1024 * 1024


def _silu(x):
    return x * (1.0 / (1.0 + jnp.exp(-x)))


def _dot(a, b):
    return jnp.dot(a, b, preferred_element_type=F32)


def _dot_nt(a, b):
    return lax.dot_general(a, b, (((1,), (1,)), ((), ())), preferred_element_type=F32)


def _mod_kernel(cond_ref, w_ref, b_ref, out_ref):
    a = _silu(cond_ref[...]).astype(BF16)
    out_ref[...] = _dot(a, w_ref[...].astype(BF16)) + b_ref[...]


def _mod_call(cond, w_mod, b_mod):
    rows = cond.shape[0]
    tn = 1024
    return pl.pallas_call(
        _mod_kernel,
        grid=(3 * D_MODEL // tn,),
        in_specs=[
            pl.BlockSpec((rows, D_MODEL), lambda j: (0, 0)),
            pl.BlockSpec((D_MODEL, tn), lambda j: (0, j)),
            pl.BlockSpec((1, tn), lambda j: (0, j)),
        ],
        out_specs=pl.BlockSpec((rows, tn), lambda j: (0, j)),
        out_shape=jax.ShapeDtypeStruct((rows, 3 * D_MODEL), F32),
        name="mod",
    )(cond, w_mod, b_mod)


def _modnorm(x, mod_ref, npw_ref):
    ms = jnp.mean(x * x, axis=-1, keepdims=True)
    xn = x * lax.rsqrt(ms + EPS) * npw_ref[...]
    shift = mod_ref[0, :, 0:D_MODEL]
    scale = mod_ref[0, :, D_MODEL:2 * D_MODEL]
    return (xn * (1.0 + scale) + shift).astype(BF16)


def _in_proj_kernel(*refs, tl, n_tiles, rope):
    if rope:
        (x_ref, xp_ref, xn_ref, mod_ref, npw_ref, w_ref, cw_ref, cb_ref, dtb_ref,
         cos_ref, sin_ref, zs_ref, xbc_ref, dt_ref, q_ref, k_ref, v_ref, gs_ref, h_ref) = refs
    else:
        (x_ref, xp_ref, xn_ref, mod_ref, npw_ref, w_ref, cw_ref, cb_ref, dtb_ref,
         zs_ref, xbc_ref, dt_ref, q_ref, k_ref, v_ref, gs_ref, h_ref) = refs
    t = pl.program_id(1)
    h_ref[...] = _modnorm(x_ref[0], mod_ref, npw_ref)
    piece = 512

    for c0 in range(0, SSD_WIDTH, piece):
        acc = _dot(h_ref[...], w_ref[:, OFF_Z + c0:OFF_Z + c0 + piece])
        zs_ref[0, :, c0:c0 + piece] = _silu(acc).astype(BF16)
    for c0 in range(0, RET_V_WIDTH, piece):
        acc = _dot(h_ref[...], w_ref[:, OFF_G + c0:OFF_G + c0 + piece])
        gs_ref[0, :, c0:c0 + piece] = _silu(acc).astype(BF16)
        acc = _dot(h_ref[...], w_ref[:, OFF_V + c0:OFF_V + c0 + piece])
        v_ref[0, :, c0:c0 + piece] = acc.astype(BF16)

    acc = _dot(h_ref[...], w_ref[:, OFF_DT:OFF_DT + DT_PAD]) + dtb_ref[...]
    dt_ref[0] = jnp.maximum(acc, 0.0) + jnp.log(1.0 + jnp.exp(-jnp.abs(acc)))

    if n_tiles > 1:
        halo = jnp.concatenate([xp_ref[0], xn_ref[0]], axis=0)
        hh = _modnorm(halo, mod_ref, npw_ref)
        pv = jnp.where(t > 0, 1.0, 0.0).astype(F32)
        nv = jnp.where(t < n_tiles - 1, 1.0, 0.0).astype(F32)
    row = lax.broadcasted_iota(jnp.int32, (tl, piece), 0)
    for c0 in range(0, CONV_CH, piece):
        wcols = w_ref[:, OFF_XBC + c0:OFF_XBC + c0 + piece]
        acc = _dot(h_ref[...], wcols)
        up = pltpu.roll(acc, 1, axis=0)
        dn = pltpu.roll(acc, tl - 1, axis=0)
        if n_tiles > 1:
            uh = _dot(hh, wcols)
            up = jnp.where(row == 0, uh[7:8, :] * pv, up)
            dn = jnp.where(row == tl - 1, uh[8:9, :] * nv, dn)
        else:
            up = jnp.where(row == 0, 0.0, up)
            dn = jnp.where(row == tl - 1, 0.0, dn)
        y = (cw_ref[0:1, c0:c0 + piece] * up + cw_ref[1:2, c0:c0 + piece] * acc
             + cw_ref[2:3, c0:c0 + piece] * dn + cb_ref[:, c0:c0 + piece])
        xbc_ref[0, :, c0:c0 + piece] = _silu(y).astype(BF16)

    lane = lax.broadcasted_iota(jnp.int32, (tl, LANES), 1)
    first_half = (lane % 32) < 16
    for off, ref, scl in ((OFF_Q, q_ref, 1.0), (OFF_K, k_ref, RET_QK_DIM ** -0.5)):
        acc = _dot(h_ref[...], w_ref[:, off:off + RET_QK_WIDTH]) * scl
        if rope:
            for s0 in range(0, RET_QK_WIDTH, LANES):
                xs = acc[:, s0:s0 + LANES]
                partner = jnp.where(first_half, pltpu.roll(xs, LANES - 16, axis=1),
                                    pltpu.roll(xs, 16, axis=1))
                ref[0, :, s0:s0 + LANES] = (xs * cos_ref[...] + partner * sin_ref[...]).astype(BF16)
        else:
            ref[0] = acc.astype(BF16)


def _in_proj_call(x, mod, npw, w_packed, conv_w, conv_b, dtb, cos_t, sin_t):
    nb, L, _ = x.shape
    tl = min(L, 512)
    n_tiles = L // tl
    g8 = tl // 8
    rope = cos_t is not None
    per_seq_mod = mod.shape[0] > 1
    mod_map = (lambda s, t: (s, 0, 0)) if per_seq_mod else (lambda s, t: (0, 0, 0))
    const2 = lambda s, t: (0, 0)
    in_specs = [
        pl.BlockSpec((1, tl, D_MODEL), lambda s, t: (s, t, 0)),
        pl.BlockSpec((1, 8, D_MODEL), lambda s, t: (s, jnp.maximum(t * g8 - 1, 0), 0)),
        pl.BlockSpec((1, 8, D_MODEL), lambda s, t: (s, jnp.minimum((t + 1) * g8, L // 8 - 1), 0)),
        pl.BlockSpec((1, 1, 3 * D_MODEL), mod_map),
        pl.BlockSpec((1, D_MODEL), const2),
        pl.BlockSpec((D_MODEL, IN_COLS_PACKED), const2, pipeline_mode=pl.Buffered(1)),
        pl.BlockSpec((3, CONV_CH), const2),
        pl.BlockSpec((1, CONV_CH), const2),
        pl.BlockSpec((1, DT_PAD), const2),
    ]
    args = [x, x, x, mod, npw, w_packed, conv_w, conv_b, dtb]
    if rope:
        in_specs += [pl.BlockSpec((tl, LANES), lambda s, t: (t, 0)),
                     pl.BlockSpec((tl, LANES), lambda s, t: (t, 0))]
        args += [cos_t, sin_t]
    widths = (SSD_WIDTH, CONV_CH, DT_PAD, RET_QK_WIDTH, RET_QK_WIDTH, RET_V_WIDTH, RET_V_WIDTH)
    dtypes = (BF16, BF16, F32, BF16, BF16, BF16, BF16)
    out_specs = [pl.BlockSpec((1, tl, w), lambda s, t: (s, t, 0)) for w in widths]
    out_shape = [jax.ShapeDtypeStruct((nb, L, w), d) for w, d in zip(widths, dtypes)]
    return pl.pallas_call(
        functools.partial(_in_proj_kernel, tl=tl, n_tiles=n_tiles, rope=rope),
        grid=(nb, n_tiles),
        in_specs=in_specs,
        out_specs=out_specs,
        out_shape=out_shape,
        scratch_shapes=[pltpu.VMEM((tl, D_MODEL), BF16)],
        compiler_params=pltpu.CompilerParams(
            dimension_semantics=("parallel", "parallel"), vmem_limit_bytes=VMEM_LIMIT),
        name="in_proj_rope" if rope else "in_proj",
    )(*args)


def _cumsum_rows(tri_bf, la):
    hi = la.astype(BF16)
    r1 = la - hi.astype(F32)
    mid = r1.astype(BF16)
    lo = (r1 - mid.astype(F32)).astype(BF16)
    return _dot(tri_bf, hi) + _dot(tri_bf, mid) + _dot(tri_bf, lo)


def _mixer_kernel(*refs, nc, has_s0, emit_state):
    Q = CHUNK
    it = iter(refs)
    zs_ref, xbc_ref, dt_ref, q_ref, k_ref, v_ref, gs_ref = (next(it) for _ in range(7))
    nega_ref, dsk_ref, snw_ref, rnw_ref, lam_ref = (next(it) for _ in range(5))
    if has_s0:
        s0s_ref, s0r_ref = next(it), next(it)
    mix_ref = next(it)
    if emit_state:
        os_ref, or_ref = next(it), next(it)
    (yas_ref, yar_ref, ss_ref, sr_ref, dsum_ref, ef_ref, eb_ref, tf_ref, tb_ref,
     ar_ref) = (next(it) for _ in range(10))

    b = pl.program_id(0)
    s = pl.program_id(1)
    phase2 = s >= nc
    c = jnp.where(phase2, 2 * nc - 1 - s, s)
    r0 = pl.multiple_of(c * Q, Q)

    rowq = lax.broadcasted_iota(jnp.int32, (Q, Q), 0)
    colq = lax.broadcasted_iota(jnp.int32, (Q, Q), 1)
    lower = rowq >= colq
    upper = colq >= rowq
    lane = lax.broadcasted_iota(jnp.int32, (Q, LANES), 1)
    lo_half = lane < 64
    lo_half1 = lax.broadcasted_iota(jnp.int32, (1, LANES), 1) < 64

    @pl.when((b == 0) & (s == 0))
    def _init_tables():
        diff = (rowq - colq).astype(F32)
        rowf = lax.broadcasted_iota(jnp.int32, (Q, LANES), 0).astype(F32)
        for h in range(RET_HEADS):
            lf = lam_ref[h:h + 1, :]
            lb = lam_ref[RET_HEADS + h:RET_HEADS + h + 1, :]
            e = jnp.where(lower, lf * diff, lb * (-diff))
            dsum_ref[h] = jnp.exp(e) * jnp.where(rowq == colq, 2.0, 1.0)
            ef_ref[h] = jnp.exp(lf * (rowf + 1.0))
            eb_ref[h] = jnp.exp(lb * (Q - rowf))
            tf_ref[h] = jnp.exp(lf * (Q - 1.0 - rowf))
            tb_ref[h] = jnp.exp(lb * rowf)
        ar_ref[...] = jnp.exp(lam_ref[...] * float(Q))

    @pl.when(s == 0)
    def _init_state():
        if has_s0:
            ss_ref[...] = s0s_ref[0]
            sr_ref[...] = s0r_ref[0, 0]
        else:
            ss_ref[...] = jnp.zeros_like(ss_ref)
            sr_ref[...] = jnp.zeros_like(sr_ref)

    dt = dt_ref[0]
    la = dt * nega_ref[...]

    def pairw(w, c0):
        return jnp.where(lo_half, w[:, c0:c0 + 1], w[:, c0 + 1:c0 + 2])

    def pairrow(w, c0):
        return jnp.where(lo_half1, w[:, c0:c0 + 1], w[:, c0 + 1:c0 + 2])

    def ssd_state_update(d, wtail, arow):
        col0 = SSD_HEADS * d
        for g in range(SSD_GROUPS):
            bg = xbc_ref[0, :, SSD_WIDTH + g * SSD_STATE:SSD_WIDTH + (g + 1) * SSD_STATE]
            bgt = bg.astype(F32).T.astype(BF16)
            xts, ars = [], []
            for p in (2 * g, 2 * g + 1):
                xp = xbc_ref[0, :, p * LANES:(p + 1) * LANES].astype(F32)
                xts.append((xp * pairw(wtail, col0 + 2 * p)).astype(BF16))
                ars.append(pairrow(arow, col0 + 2 * p))
            ds = _dot(bgt, jnp.concatenate(xts, axis=1))
            gc = slice(g * 256, (g + 1) * 256)
            ss_ref[d, :, gc] = ss_ref[d, :, gc] * jnp.concatenate(ars, axis=1) + ds

    def ret_state_update(d, tail_ref):
        for pr in range(RET_HEADS // 2):
            kp = k_ref[0, :, pr * LANES:(pr + 1) * LANES]
            kt = kp.astype(F32).T.astype(BF16)
            for hh in range(2):
                h = 2 * pr + hh
                vh = v_ref[0, :, h * LANES:(h + 1) * LANES].astype(F32)
                vt = (vh * tail_ref[h]).astype(BF16)
                ds = _dot(kt[hh * 64:(hh + 1) * 64, :], vt)
                a = ar_ref[d * RET_HEADS + h:d * RET_HEADS + h + 1, :]
                sr_ref[d, h] = sr_ref[d, h] * a + ds

    @pl.when(jnp.logical_not(phase2))
    def _phase1():
        tl_bf = jnp.where(lower, 1.0, 0.0).astype(BF16)
        tu_bf = jnp.where(upper, 1.0, 0.0).astype(BF16)
        cum = _cumsum_rows(tl_bf, la)
        rev = _cumsum_rows(tu_bf, la)
        cum_t = cum.T
        rev_t = rev.T
        ecum = jnp.exp(cum)
        last = cum[Q - 1:Q, :]
        wtail_f = dt * jnp.exp(last - cum)
        a_f = jnp.exp(last)

        for g in range(SSD_GROUPS):
            cg = xbc_ref[0, :, SSD_WIDTH + 512 + g * SSD_STATE:SSD_WIDTH + 512 + (g + 1) * SSD_STATE]
            bg = xbc_ref[0, :, SSD_WIDTH + g * SSD_STATE:SSD_WIDTH + (g + 1) * SSD_STATE]
            gm = _dot_nt(cg, bg)
            cs = _dot(cg, ss_ref[0, :, g * 256:(g + 1) * 256].astype(BF16))
            for pi, p in enumerate((2 * g, 2 * g + 1)):
                xp = xbc_ref[0, :, p * LANES:(p + 1) * LANES].astype(F32)
                x2 = jnp.concatenate([(xp * pairw(dt, 2 * p)).astype(BF16),
                                      (xp * pairw(dt, SSD_HEADS + 2 * p)).astype(BF16)], axis=0)
                res = []
                for h in (2 * p, 2 * p + 1):
                    hb = SSD_HEADS + h
                    lf = jnp.exp(jnp.where(lower, cum[:, h:h + 1] - cum_t[h:h + 1, :], NEG_BIG))
                    lb = jnp.exp(jnp.where(upper, rev[:, hb:hb + 1] - rev_t[hb:hb + 1, :], NEG_BIG))
                    m = jnp.concatenate([(gm * lf).astype(BF16), (gm * lb).astype(BF16)], axis=1)
                    res.append(_dot(m, x2))
                y = jnp.where(lo_half, res[0], res[1])
                y = y + cs[:, pi * LANES:(pi + 1) * LANES] * pairw(ecum, 2 * p)
                yas_ref[pl.ds(r0, Q), p * LANES:(p + 1) * LANES] = y
        ssd_state_update(0, wtail_f, a_f)

        for pr in range(RET_HEADS // 2):
            qp = q_ref[0, :, pr * LANES:(pr + 1) * LANES]
            kp = k_ref[0, :, pr * LANES:(pr + 1) * LANES]
            sp = jnp.concatenate([sr_ref[0, 2 * pr], sr_ref[0, 2 * pr + 1]], axis=0).astype(BF16)
            for hh in range(2):
                h = 2 * pr + hh
                qm = jnp.where(lo_half if hh == 0 else jnp.logical_not(lo_half), qp, jnp.zeros_like(qp))
                a = _dot_nt(qm, kp)
                ad = (a * dsum_ref[h]).astype(BF16)
                vh = v_ref[0, :, h * LANES:(h + 1) * LANES]
                y = _dot(ad, vh) + _dot(qm, sp) * ef_ref[h]
                yar_ref[pl.ds(r0, Q), h * LANES:(h + 1) * LANES] = y
        ret_state_update(0, tf_ref)

        if emit_state:
            @pl.when(s == nc - 1)
            def _emit_fwd():
                os_ref[0, 0] = ss_ref[0]
                or_ref[0, 0, 0] = sr_ref[0]

    @pl.when(phase2)
    def _phase2():
        tu_bf = jnp.where(upper, 1.0, 0.0).astype(BF16)
        rev = _cumsum_rows(tu_bf, la)
        erev = jnp.exp(rev)
        first = rev[0:1, :]
        wtail_b = dt * jnp.exp(first - rev)
        a_b = jnp.exp(first)

        parts = []
        for g in range(SSD_GROUPS):
            cg = xbc_ref[0, :, SSD_WIDTH + 512 + g * SSD_STATE:SSD_WIDTH + 512 + (g + 1) * SSD_STATE]
            cs = _dot(cg, ss_ref[1, :, g * 256:(g + 1) * 256].astype(BF16))
            for pi, p in enumerate((2 * g, 2 * g + 1)):
                pc = slice(p * LANES, (p + 1) * LANES)
                xp = xbc_ref[0, :, pc].astype(F32)
                y = (yas_ref[pl.ds(r0, Q), pc] + cs[:, pi * LANES:(pi + 1) * LANES]
                     * pairw(erev, SSD_HEADS + 2 * p) + dsk_ref[:, pc] * xp)
                parts.append(y * zs_ref[0, :, pc].astype(F32))
        ssq = parts[0] * parts[0]
        for y in parts[1:]:
            ssq = ssq + y * y
        inv = lax.rsqrt(jnp.sum(ssq, axis=-1, keepdims=True) * (1.0 / SSD_WIDTH) + EPS)
        for p, y in enumerate(parts):
            pc = slice(p * LANES, (p + 1) * LANES)
            mix_ref[0, :, pc] = (y * inv * snw_ref[:, pc]).astype(BF16)
        ssd_state_update(1, wtail_b, a_b)

        for pr in range(RET_HEADS // 2):
            qp = q_ref[0, :, pr * LANES:(pr + 1) * LANES]
            sp = jnp.concatenate([sr_ref[1, 2 * pr], sr_ref[1, 2 * pr + 1]], axis=0).astype(BF16)
            for hh in range(2):
                h = 2 * pr + hh
                hc = slice(h * LANES, (h + 1) * LANES)
                qm = jnp.where(lo_half if hh == 0 else jnp.logical_not(lo_half), qp, jnp.zeros_like(qp))
                y = yar_ref[pl.ds(r0, Q), hc] + _dot(qm, sp) * eb_ref[h]
                mu = jnp.mean(y, axis=-1, keepdims=True)
                yc = y - mu
                var = jnp.mean(yc * yc, axis=-1, keepdims=True)
                yn = yc * lax.rsqrt(var + EPS) * rnw_ref[:, hc] * gs_ref[0, :, hc].astype(F32)
                mix_ref[0, :, SSD_WIDTH + h * LANES:SSD_WIDTH + (h + 1) * LANES] = yn.astype(BF16)
        ret_state_update(1, tb_ref)

        if emit_state:
            @pl.when(s == 2 * nc - 1)
            def _emit_bwd():
                os_ref[0, 1] = ss_ref[1]
                or_ref[0, 0, 1] = sr_ref[1]


def _mixer_call(zs, xbc, dt, q, k, v, gs, nega, dsk, snw, rnw, lamx, s0_ssd, s0_ret, emit_state):
    nb, L, _ = zs.shape
    nc = L // CHUNK
    Q = CHUNK
    has_s0 = s0_ssd is not None

    def cidx(s):
        return jnp.where(s < nc, s, 2 * nc - 1 - s)

    def chunk_map(b, s):
        return (b, cidx(s), 0)

    def phase2_map(b, s):
        return (b, jnp.where(s < nc, nc - 1, 2 * nc - 1 - s), 0)

    const2 = lambda b, s: (0, 0)
    in_specs = [
        pl.BlockSpec((1, Q, SSD_WIDTH), phase2_map),
        pl.BlockSpec((1, Q, CONV_CH), chunk_map),
        pl.BlockSpec((1, Q, DT_PAD), chunk_map),
        pl.BlockSpec((1, Q, RET_QK_WIDTH), chunk_map),
        pl.BlockSpec((1, Q, RET_QK_WIDTH), chunk_map),
        pl.BlockSpec((1, Q, RET_V_WIDTH), chunk_map),
        pl.BlockSpec((1, Q, RET_V_WIDTH), phase2_map),
        pl.BlockSpec((1, DT_PAD), const2),
        pl.BlockSpec((1, SSD_WIDTH), const2),
        pl.BlockSpec((1, SSD_WIDTH), const2),
        pl.BlockSpec((1, RET_V_WIDTH), const2),
        pl.BlockSpec((2 * RET_HEADS, LANES), const2),
    ]
    args = [zs, xbc, dt, q, k, v, gs, nega, dsk, snw, rnw, lamx]
    if has_s0:
        in_specs += [
            pl.BlockSpec((1, 2, SSD_STATE, SSD_WIDTH), lambda b, s: (b, 0, 0, 0)),
            pl.BlockSpec((1, 1, 2, RET_HEADS, RET_QK_DIM, RET_V_DIM), lambda b, s: (b, 0, 0, 0, 0, 0)),
        ]
        args += [s0_ssd, s0_ret]
    out_specs = [pl.BlockSpec((1, Q, MIX_WIDTH), phase2_map)]
    out_shape = [jax.ShapeDtypeStruct((nb, L, MIX_WIDTH), BF16)]
    if emit_state:
        out_specs += [
            pl.BlockSpec((1, 2, SSD_STATE, SSD_WIDTH), lambda b, s: (b, 0, 0, 0)),
            pl.BlockSpec((1, 1, 2, RET_HEADS, RET_QK_DIM, RET_V_DIM), lambda b, s: (b, 0, 0, 0, 0, 0)),
        ]
        out_shape += [
            jax.ShapeDtypeStruct((nb, 2, SSD_STATE, SSD_WIDTH), F32),
            jax.ShapeDtypeStruct((nb, 1, 2, RET_HEADS, RET_QK_DIM, RET_V_DIM), F32),
        ]
    scratch = [
        pltpu.VMEM((L, SSD_WIDTH), F32),
        pltpu.VMEM((L, RET_V_WIDTH), F32),
        pltpu.VMEM((2, SSD_STATE, SSD_WIDTH), F32),
        pltpu.VMEM((2, RET_HEADS, RET_QK_DIM, RET_V_DIM), F32),
        pltpu.VMEM((RET_HEADS, Q, Q), F32),
        pltpu.VMEM((RET_HEADS, Q, LANES), F32),
        pltpu.VMEM((RET_HEADS, Q, LANES), F32),
        pltpu.VMEM((RET_HEADS, Q, LANES), F32),
        pltpu.VMEM((RET_HEADS, Q, LANES), F32),
        pltpu.VMEM((2 * RET_HEADS, LANES), F32),
    ]
    outs = pl.pallas_call(
        functools.partial(_mixer_kernel, nc=nc, has_s0=has_s0, emit_state=emit_state),
        grid=(nb, 2 * nc),
        in_specs=in_specs,
        out_specs=out_specs,
        out_shape=out_shape,
        scratch_shapes=scratch,
        compiler_params=pltpu.CompilerParams(
            dimension_semantics=("arbitrary", "arbitrary"), vmem_limit_bytes=VMEM_LIMIT),
        name="mixer_sample" if has_s0 else "mixer_prompt",
    )(*args)
    return outs


def _out_proj_kernel(mix_ref, w_ref, x_ref, mod_ref, npw_ref, y_ref):
    out = _dot(mix_ref[0], w_ref[...])
    ms = jnp.mean(out * out, axis=-1, keepdims=True)
    o = out * lax.rsqrt(ms + EPS) * npw_ref[...]
    gate = mod_ref[0, :, 2 * D_MODEL:3 * D_MODEL]
    y_ref[0] = x_ref[0] + gate * o


def _out_proj_call(mix, w_out_bf, x, mod, npw):
    nb, L, _ = x.shape
    tl = min(L, 512)
    per_seq_mod = mod.shape[0] > 1
    mod_map = (lambda s, t: (s, 0, 0)) if per_seq_mod else (lambda s, t: (0, 0, 0))
    return pl.pallas_call(
        _out_proj_kernel,
        grid=(nb, L // tl),
        in_specs=[
            pl.BlockSpec((1, tl, MIX_WIDTH), lambda s, t: (s, t, 0)),
            pl.BlockSpec((MIX_WIDTH, D_MODEL), lambda s, t: (0, 0)),
            pl.BlockSpec((1, tl, D_MODEL), lambda s, t: (s, t, 0)),
            pl.BlockSpec((1, 1, 3 * D_MODEL), mod_map),
            pl.BlockSpec((1, D_MODEL), lambda s, t: (0, 0)),
        ],
        out_specs=pl.BlockSpec((1, tl, D_MODEL), lambda s, t: (s, t, 0)),
        out_shape=jax.ShapeDtypeStruct((nb, L, D_MODEL), F32),
        compiler_params=pltpu.CompilerParams(
            dimension_semantics=("parallel", "parallel"), vmem_limit_bytes=VMEM_LIMIT),
        name="out_proj",
    )(mix, w_out_bf, x, mod, npw)


def _rope_tables(L):
    rows = L // GRID_W
    row = jnp.repeat(jnp.arange(rows, dtype=F32), GRID_W)
    col = jnp.tile(jnp.arange(GRID_W, dtype=F32), rows)
    half = RET_QK_DIM // 2
    inv = ROPE_BASE ** (-jnp.arange(0, half, 2, dtype=F32) / half)
    ang_r = row[:, None] * inv
    ang_c = col[:, None] * inv
    cos_h = jnp.concatenate([jnp.cos(ang_r), jnp.cos(ang_r), jnp.cos(ang_c), jnp.cos(ang_c)], axis=-1)
    sin_h = jnp.concatenate([-jnp.sin(ang_r), jnp.sin(ang_r), -jnp.sin(ang_c), jnp.sin(ang_c)], axis=-1)
    return jnp.tile(cos_h, (1, 2)), jnp.tile(sin_h, (1, 2))


def kernel(x_prompt, x_sample, state_ssd, state_ret, c, c_ctx, w_mod, b_mod, norm_pre_w, norm_post_w,
           w_in, conv_w, conv_b, ssd_A_log, ssd_dt_bias, ssd_D, ssd_norm_w, ret_decay, ret_norm_w, w_out):
    nb_p = x_prompt.shape[0]
    nb_s = x_sample.shape[0]
    l = 0

    wi = w_in[l]
    n_dt = 2 * SSD_HEADS
    o_dt = SSD_WIDTH + CONV_CH
    w_packed = jnp.concatenate([
        wi[:, :o_dt + n_dt],
        jnp.zeros((D_MODEL, DT_PAD - n_dt), F32),
        wi[:, o_dt + n_dt:],
    ], axis=1).astype(BF16)
    w_out_bf = w_out[l].astype(BF16)
    dtb = jnp.pad(ssd_dt_bias[l].reshape(1, n_dt), ((0, 0), (0, DT_PAD - n_dt)))
    nega = jnp.pad(-jnp.exp(ssd_A_log[l].reshape(1, n_dt)), ((0, 0), (0, DT_PAD - n_dt)))
    dsk = jnp.repeat(ssd_D[l], SSD_HEAD_DIM).reshape(1, SSD_WIDTH)
    lamx = jnp.broadcast_to(-jnp.exp(ret_decay[l].reshape(2 * RET_HEADS, 1)), (2 * RET_HEADS, LANES))
    npre = norm_pre_w[l].reshape(1, D_MODEL)
    npost = norm_post_w[l].reshape(1, D_MODEL)
    snw = ssd_norm_w[l].reshape(1, SSD_WIDTH)
    rnw = ret_norm_w[l].reshape(1, RET_V_WIDTH)
    cb = conv_b[l].reshape(1, CONV_CH)
    cos_t, sin_t = _rope_tables(x_sample.shape[1])

    cond = jnp.concatenate([c, c_ctx[None, :], jnp.zeros((16 - nb_s - 1, D_MODEL), F32)], axis=0)
    mod = _mod_call(cond, w_mod[l], b_mod[l].reshape(1, 3 * D_MODEL))
    mod_s = mod[:nb_s].reshape(nb_s, 1, 3 * D_MODEL)
    mod_p = mod[nb_s:nb_s + 1].reshape(1, 1, 3 * D_MODEL)

    s0_ssd = state_ssd[:, l].transpose(0, 1, 3, 2, 4).reshape(nb_s, 2, SSD_STATE, SSD_WIDTH)

    pp = _in_proj_call(x_prompt, mod_p, npre, w_packed, conv_w[l], cb, dtb, None, None)
    mix_p, st_ssd, st_ret = _mixer_call(*pp, nega, dsk, snw, rnw, lamx, None, None, True)
    y_p = _out_proj_call(mix_p, w_out_bf, x_prompt, mod_p, npost)

    ps = _in_proj_call(x_sample, mod_s, npre, w_packed, conv_w[l], cb, dtb, cos_t, sin_t)
    (mix_s,) = _mixer_call(*ps, nega, dsk, snw, rnw, lamx, s0_ssd, state_ret, False)
    y_s = _out_proj_call(mix_s, w_out_bf, x_sample, mod_s, npost)

    new_ssd = (st_ssd.reshape(nb_p, 2, SSD_STATE, SSD_HEADS, SSD_HEAD_DIM)
               .transpose(0, 1, 3, 2, 4)[:, None])
    return (y_p, y_s, new_ssd, st_ret)
```

```python
import functools

import jax
import jax.numpy as jnp
from jax import lax
from jax.experimental import pallas as pl
from jax.experimental.pallas import tpu as pltpu

F32 = jnp.float32
BF16 = jnp.bfloat16

D_MODEL = 1024
CHUNK = 128
GRID_W = 64
EPS = 1e-6
SSD_WIDTH = 1024
SSD_HEAD_DIM = 64
SSD_HEADS = 16
SSD_GROUPS = 4
SSD_STATE = 128
CONV_CH = SSD_WIDTH + 2 * SSD_GROUPS * SSD_STATE
RET_HEADS = 8
RET_QK_DIM = 64
RET_V_DIM = 128
RET_QK_WIDTH = RET_HEADS * RET_QK_DIM
RET_V_WIDTH = RET_HEADS * RET_V_DIM
MIX_WIDTH = SSD_WIDTH + RET_V_WIDTH
ROPE_BASE = 10000.0
LANES = 128
DT_PAD = LANES
GROUP_W = SSD_WIDTH // SSD_GROUPS
PAIR_W = 2 * RET_V_DIM

OFF_Z = 0
OFF_XBC = OFF_Z + SSD_WIDTH
OFF_DT = OFF_XBC + CONV_CH
OFF_Q = OFF_DT + DT_PAD
OFF_K = OFF_Q + RET_QK_WIDTH
OFF_V = OFF_K + RET_QK_WIDTH
OFF_G = OFF_V + RET_V_WIDTH
IN_COLS_PACKED = OFF_G + RET_V_WIDTH

NEG_BIG = -1e30
VMEM_LIMIT = 56 * 1024 * 1024


def _silu(x):
    return x * (1.0 / (1.0 + jnp.exp(-x)))


def _dot(a, b):
    return jnp.dot(a, b, preferred_element_type=F32)


def _dot_nt(a, b):
    return lax.dot_general(a, b, (((1,), (1,)), ((), ())), preferred_element_type=F32)


def _split3(x):
    hi = x.astype(BF16)
    r1 = x - hi.astype(F32)
    mid = r1.astype(BF16)
    lo = (r1 - mid.astype(F32)).astype(BF16)
    return hi, mid, lo


def _mod_kernel(cond_ref, w_ref, b_ref, out_ref):
    a = _silu(cond_ref[...]).astype(BF16)
    out_ref[...] = _dot(a, w_ref[...].astype(BF16)) + b_ref[...]


def _mod_call(cond, w_mod, b_mod):
    rows = cond.shape[0]
    tn = 1024
    return pl.pallas_call(
        _mod_kernel,
        grid=(3 * D_MODEL // tn,),
        in_specs=[
            pl.BlockSpec((rows, D_MODEL), lambda j: (0, 0)),
            pl.BlockSpec((D_MODEL, tn), lambda j: (0, j)),
            pl.BlockSpec((1, tn), lambda j: (0, j)),
        ],
        out_specs=pl.BlockSpec((rows, tn), lambda j: (0, j)),
        out_shape=jax.ShapeDtypeStruct((rows, 3 * D_MODEL), F32),
        name="mod",
    )(cond, w_mod, b_mod)


def _modnorm(x, mod_ref, npw_ref):
    ms = jnp.mean(x * x, axis=-1, keepdims=True)
    xn = x * lax.rsqrt(ms + EPS) * npw_ref[...]
    shift = mod_ref[0, :, 0:D_MODEL]
    scale = mod_ref[0, :, D_MODEL:2 * D_MODEL]
    return (xn * (1.0 + scale) + shift).astype(BF16)


def _in_proj_kernel(*refs, tl, n_tiles, rope):
    if rope:
        (x_ref, xp_ref, xn_ref, mod_ref, npw_ref, w_ref, cw_ref, cb_ref, dtb_ref,
         cos_ref, sin_ref, zs_ref, xbc_ref, dt_ref, q_ref, k_ref, v_ref, gs_ref, h_ref) = refs
    else:
        (x_ref, xp_ref, xn_ref, mod_ref, npw_ref, w_ref, cw_ref, cb_ref, dtb_ref,
         zs_ref, xbc_ref, dt_ref, q_ref, k_ref, v_ref, gs_ref, h_ref) = refs
    t = pl.program_id(1)
    h_ref[...] = _modnorm(x_ref[0], mod_ref, npw_ref)
    piece = 512

    for c0 in range(0, SSD_WIDTH, piece):
        acc = _dot(h_ref[...], w_ref[:, OFF_Z + c0:OFF_Z + c0 + piece])
        zs_ref[0, :, c0:c0 + piece] = _silu(acc).astype(BF16)
    for c0 in range(0, RET_V_WIDTH, piece):
        acc = _dot(h_ref[...], w_ref[:, OFF_G + c0:OFF_G + c0 + piece])
        gs_ref[0, :, c0:c0 + piece] = _silu(acc).astype(BF16)
        acc = _dot(h_ref[...], w_ref[:, OFF_V + c0:OFF_V + c0 + piece])
        v_ref[0, :, c0:c0 + piece] = acc.astype(BF16)

    acc = _dot(h_ref[...], w_ref[:, OFF_DT:OFF_DT + DT_PAD]) + dtb_ref[...]
    dt_ref[0] = jnp.maximum(acc, 0.0) + jnp.log(1.0 + jnp.exp(-jnp.abs(acc)))

    if n_tiles > 1:
        halo = jnp.concatenate([xp_ref[0], xn_ref[0]], axis=0)
        hh = _modnorm(halo, mod_ref, npw_ref)
        pv = jnp.where(t > 0, 1.0, 0.0).astype(F32)
        nv = jnp.where(t < n_tiles - 1, 1.0, 0.0).astype(F32)
    row = lax.broadcasted_iota(jnp.int32, (tl, piece), 0)
    for c0 in range(0, CONV_CH, piece):
        wcols = w_ref[:, OFF_XBC + c0:OFF_XBC + c0 + piece]
        acc = _dot(h_ref[...], wcols)
        up = pltpu.roll(acc, 1, axis=0)
        dn = pltpu.roll(acc, tl - 1, axis=0)
        if n_tiles > 1:
            uh = _dot(hh, wcols)
            up = jnp.where(row == 0, uh[7:8, :] * pv, up)
            dn = jnp.where(row == tl - 1, uh[8:9, :] * nv, dn)
        else:
            up = jnp.where(row == 0, 0.0, up)
            dn = jnp.where(row == tl - 1, 0.0, dn)
        y = (cw_ref[0:1, c0:c0 + piece] * up + cw_ref[1:2, c0:c0 + piece] * acc
             + cw_ref[2:3, c0:c0 + piece] * dn + cb_ref[:, c0:c0 + piece])
        xbc_ref[0, :, c0:c0 + piece] = _silu(y).astype(BF16)

    lane = lax.broadcasted_iota(jnp.int32, (tl, LANES), 1)
    first_half = (lane % 32) < 16
    for off, ref, scl in ((OFF_Q, q_ref, 1.0), (OFF_K, k_ref, RET_QK_DIM ** -0.5)):
        acc = _dot(h_ref[...], w_ref[:, off:off + RET_QK_WIDTH]) * scl
        if rope:
            for s0 in range(0, RET_QK_WIDTH, LANES):
                xs = acc[:, s0:s0 + LANES]
                partner = jnp.where(first_half, pltpu.roll(xs, LANES - 16, axis=1),
                                    pltpu.roll(xs, 16, axis=1))
                ref[0, :, s0:s0 + LANES] = (xs * cos_ref[...] + partner * sin_ref[...]).astype(BF16)
        else:
            ref[0] = acc.astype(BF16)


def _in_proj_call(x, mod, npw, w_packed, conv_w, conv_b, dtb, cos_t, sin_t):
    nb, L, _ = x.shape
    tl = min(L, 512)
    n_tiles = L // tl
    g8 = tl // 8
    rope = cos_t is not None
    per_seq_mod = mod.shape[0] > 1
    mod_map = (lambda s, t: (s, 0, 0)) if per_seq_mod else (lambda s, t: (0, 0, 0))
    const2 = lambda s, t: (0, 0)
    in_specs = [
        pl.BlockSpec((1, tl, D_MODEL), lambda s, t: (s, t, 0)),
        pl.BlockSpec((1, 8, D_MODEL), lambda s, t: (s, jnp.maximum(t * g8 - 1, 0), 0)),
        pl.BlockSpec((1, 8, D_MODEL), lambda s, t: (s, jnp.minimum((t + 1) * g8, L // 8 - 1), 0)),
        pl.BlockSpec((1, 1, 3 * D_MODEL), mod_map),
        pl.BlockSpec((1, D_MODEL), const2),
        pl.BlockSpec((D_MODEL, IN_COLS_PACKED), const2, pipeline_mode=pl.Buffered(1)),
        pl.BlockSpec((3, CONV_CH), const2),
        pl.BlockSpec((1, CONV_CH), const2),
        pl.BlockSpec((1, DT_PAD), const2),
    ]
    args = [x, x, x, mod, npw, w_packed, conv_w, conv_b, dtb]
    if rope:
        in_specs += [pl.BlockSpec((tl, LANES), lambda s, t: (t, 0)),
                     pl.BlockSpec((tl, LANES), lambda s, t: (t, 0))]
        args += [cos_t, sin_t]
    widths = (SSD_WIDTH, CONV_CH, DT_PAD, RET_QK_WIDTH, RET_QK_WIDTH, RET_V_WIDTH, RET_V_WIDTH)
    dtypes = (BF16, BF16, F32, BF16, BF16, BF16, BF16)
    out_specs = [pl.BlockSpec((1, tl, w), lambda s, t: (s, t, 0)) for w in widths]
    out_shape = [jax.ShapeDtypeStruct((nb, L, w), d) for w, d in zip(widths, dtypes)]
    return pl.pallas_call(
        functools.partial(_in_proj_kernel, tl=tl, n_tiles=n_tiles, rope=rope),
        grid=(nb, n_tiles),
        in_specs=in_specs,
        out_specs=out_specs,
        out_shape=out_shape,
        scratch_shapes=[pltpu.VMEM((tl, D_MODEL), BF16)],
        compiler_params=pltpu.CompilerParams(
            dimension_semantics=("parallel", "parallel"), vmem_limit_bytes=VMEM_LIMIT),
        name="in_proj_rope" if rope else "in_proj",
    )(*args)


def _mixer_kernel(*refs, nc, has_s0, emit_state):
    Q = CHUNK
    H = SSD_HEADS
    it = iter(refs)
    zs_ref, xbc_ref, dt_ref, q_ref, k_ref, v_ref, gs_ref = (next(it) for _ in range(7))
    nega_ref, dsk_ref, snw_ref, rnw_ref, lam_ref = (next(it) for _ in range(5))
    if has_s0:
        s0s_ref, s0r_ref = next(it), next(it)
    mix_ref = next(it)
    if emit_state:
        os_ref, or_ref = next(it), next(it)
    (yas_ref, yar_ref, ss_ref, sr_ref, dsum_ref, ef_ref, eb_ref, tf_ref, tb_ref,
     ar_ref, exf_ref, exb_ref) = (next(it) for _ in range(12))

    b = pl.program_id(0)
    s = pl.program_id(1)
    phase2 = s >= nc
    c = jnp.where(phase2, 2 * nc - 1 - s, s)
    r0 = pl.multiple_of(c * Q, Q)

    rowq = lax.broadcasted_iota(jnp.int32, (Q, Q), 0)
    colq = lax.broadcasted_iota(jnp.int32, (Q, Q), 1)

    @pl.when((b == 0) & (s == 0))
    def _init_tables():
        diff = (rowq - colq).astype(F32)
        rowf = lax.broadcasted_iota(jnp.int32, (Q, LANES), 0).astype(F32)
        for h in range(RET_HEADS):
            pr, hh = divmod(h, 2)
            lf = lam_ref[h:h + 1, :]
            lb = lam_ref[RET_HEADS + h:RET_HEADS + h + 1, :]
            e = jnp.where(rowq >= colq, lf * diff, lb * (-diff))
            dsum_ref[pr, :, hh * Q:(hh + 1) * Q] = jnp.exp(e) * jnp.where(rowq == colq, 2.0, 1.0)
            hc = slice(hh * LANES, (hh + 1) * LANES)
            ef_ref[pr, :, hc] = jnp.exp(lf * (rowf + 1.0))
            eb_ref[pr, :, hc] = jnp.exp(lb * (Q - rowf))
            tf_ref[pr, :, hc] = jnp.exp(lf * (Q - 1.0 - rowf))
            tb_ref[pr, :, hc] = jnp.exp(lb * rowf)
        ar_ref[...] = jnp.exp(lam_ref[...] * float(Q))
        er = lax.broadcasted_iota(jnp.int32, (LANES, SSD_WIDTH), 0)
        ec = lax.broadcasted_iota(jnp.int32, (LANES, SSD_WIDTH), 1) // SSD_HEAD_DIM
        exf_ref[...] = jnp.where(er == ec, 1.0, 0.0).astype(BF16)
        exb_ref[...] = jnp.where(er == ec + H, 1.0, 0.0).astype(BF16)

    @pl.when(s == 0)
    def _init_state():
        if has_s0:
            ss_ref[...] = s0s_ref[0]
            sr_ref[...] = s0r_ref[0, 0]
        else:
            ss_ref[...] = jnp.zeros_like(ss_ref)
            sr_ref[...] = jnp.zeros_like(sr_ref)

    dt = dt_ref[0]
    la = dt * nega_ref[...]
    tl_bf = jnp.where(rowq >= colq, 1.0, 0.0).astype(BF16)
    c3 = _dot(tl_bf, jnp.concatenate(_split3(la), axis=1))
    cum = c3[:, 0:LANES] + c3[:, LANES:2 * LANES] + c3[:, 2 * LANES:3 * LANES]
    tot = cum[Q - 1:Q, :]
    rev = tot - cum + la

    def expand(w, a_row, ex_ref):
        a3 = jnp.concatenate(_split3(jnp.broadcast_to(a_row, (16, LANES))), axis=0)
        out = _dot(jnp.concatenate([w.astype(BF16), a3], axis=0), ex_ref[...])
        n = w.shape[0]
        return out[0:n], out[n:n + 1] + out[n + 16:n + 17] + out[n + 32:n + 33]

    def ssd_state_update(d, wtail_x, a_x):
        for g in range(SSD_GROUPS):
            bg = xbc_ref[0, :, SSD_WIDTH + g * SSD_STATE:SSD_WIDTH + (g + 1) * SSD_STATE]
            bgt = bg.astype(F32).T.astype(BF16)
            gc = slice(g * GROUP_W, (g + 1) * GROUP_W)
            xt = (xbc_ref[0, :, gc].astype(F32) * wtail_x[:, gc]).astype(BF16)
            ss_ref[d, :, gc] = ss_ref[d, :, gc] * a_x[:, gc] + _dot(bgt, xt)

    def ret_state_update(d, tail_ref):
        for pr in range(RET_HEADS // 2):
            kt = k_ref[0, :, pr * LANES:(pr + 1) * LANES].astype(F32).T.astype(BF16)
            pc = slice(pr * PAIR_W, (pr + 1) * PAIR_W)
            vt = (v_ref[0, :, pc].astype(F32) * tail_ref[pr]).astype(BF16)
            ds = _dot(kt, vt)
            for hh in range(2):
                h = 2 * pr + hh
                a = ar_ref[d * RET_HEADS + h:d * RET_HEADS + h + 1, :]
                sr_ref[d, h] = (sr_ref[d, h] * a
                                + ds[hh * RET_QK_DIM:(hh + 1) * RET_QK_DIM, hh * LANES:(hh + 1) * LANES])

    def ret_state_blockdiag(d, pr):
        z = jnp.zeros((RET_QK_DIM, RET_V_DIM), F32)
        top = jnp.concatenate([sr_ref[d, 2 * pr], z], axis=1)
        bot = jnp.concatenate([z, sr_ref[d, 2 * pr + 1]], axis=1)
        return jnp.concatenate([top, bot], axis=0).astype(BF16)

    @pl.when(jnp.logical_not(phase2))
    def _phase1():
        dt_t = dt.T
        ldt_t = jnp.log(dt_t)
        sub = lax.broadcasted_iota(jnp.int32, (LANES, Q), 0)
        adj_t = jnp.where(sub < H, cum.T, rev.T) - ldt_t
        dg_t = jnp.log(dt_t[0:H, :] + dt_t[H:2 * H, :])
        wtail_f = dt * jnp.exp(tot - cum)
        ew_x, af_x = expand(jnp.concatenate([jnp.exp(cum), wtail_f], axis=0), jnp.exp(tot), exf_ref)
        ecum_x = ew_x[0:Q]
        wtail_x = ew_x[Q:2 * Q]
        lt = rowq > colq
        gt = rowq < colq
        lane_g = lax.broadcasted_iota(jnp.int32, (Q, GROUP_W), 1) // SSD_HEAD_DIM

        for g in range(SSD_GROUPS):
            cg = xbc_ref[0, :, SSD_WIDTH + 512 + g * SSD_STATE:SSD_WIDTH + 512 + (g + 1) * SSD_STATE]
            bg = xbc_ref[0, :, SSD_WIDTH + g * SSD_STATE:SSD_WIDTH + (g + 1) * SSD_STATE]
            gc = slice(g * GROUP_W, (g + 1) * GROUP_W)
            gm = _dot_nt(cg, bg)
            xg = xbc_ref[0, :, gc]
            ws, xs = [], []
            for j in range(4):
                h = 4 * g + j
                hb = H + h
                arg = jnp.where(lt, cum[:, h:h + 1] - adj_t[h:h + 1, :],
                                jnp.where(gt, rev[:, hb:hb + 1] - adj_t[hb:hb + 1, :], dg_t[h:h + 1, :]))
                ws.append((gm * jnp.exp(arg)).astype(BF16))
                xs.append(jnp.where(lane_g == j, xg, jnp.zeros_like(xg)))
            y = _dot(jnp.concatenate(ws, axis=1), jnp.concatenate(xs, axis=0))
            cs = _dot(cg, ss_ref[0, :, gc].astype(BF16))
            yas_ref[pl.ds(r0, Q), gc] = y + cs * ecum_x[:, gc]
        ssd_state_update(0, wtail_x, af_x)

        sub_k = lax.broadcasted_iota(jnp.int32, (LANES, Q), 0) < RET_QK_DIM
        lane_v = lax.broadcasted_iota(jnp.int32, (Q, PAIR_W), 1) < RET_V_DIM
        for pr in range(RET_HEADS // 2):
            qp = q_ref[0, :, pr * LANES:(pr + 1) * LANES]
            kt = k_ref[0, :, pr * LANES:(pr + 1) * LANES].astype(F32).T.astype(BF16)
            zk = jnp.zeros_like(kt)
            kbd = jnp.concatenate([jnp.where(sub_k, kt, zk), jnp.where(sub_k, zk, kt)], axis=1)
            a2 = _dot(qp, kbd)
            ad = (a2 * dsum_ref[pr]).astype(BF16)
            pc = slice(pr * PAIR_W, (pr + 1) * PAIR_W)
            vp = v_ref[0, :, pc]
            zv = jnp.zeros_like(vp)
            vbd = jnp.concatenate([jnp.where(lane_v, vp, zv), jnp.where(lane_v, zv, vp)], axis=0)
            y = _dot(ad, vbd) + _dot(qp, ret_state_blockdiag(0, pr)) * ef_ref[pr]
            yar_ref[pl.ds(r0, Q), pc] = y
        ret_state_update(0, tf_ref)

        if emit_state:
            @pl.when(s == nc - 1)
            def _emit_fwd():
                os_ref[0, 0] = ss_ref[0]
                or_ref[0, 0, 0] = sr_ref[0]

    @pl.when(phase2)
    def _phase2():
        first = rev[0:1, :]
        wtail_b = dt * jnp.exp(first - rev)
        ew_x, ab_x = expand(jnp.concatenate([jnp.exp(rev), wtail_b], axis=0), jnp.exp(first), exb_ref)
        erev_x = ew_x[0:Q]
        wtail_x = ew_x[Q:2 * Q]

        parts = []
        ssq = None
        for g in range(SSD_GROUPS):
            cg = xbc_ref[0, :, SSD_WIDTH + 512 + g * SSD_STATE:SSD_WIDTH + 512 + (g + 1) * SSD_STATE]
            gc = slice(g * GROUP_W, (g + 1) * GROUP_W)
            cs = _dot(cg, ss_ref[1, :, gc].astype(BF16))
            y = (yas_ref[pl.ds(r0, Q), gc] + cs * erev_x[:, gc]
                 + dsk_ref[:, gc] * xbc_ref[0, :, gc].astype(F32))
            y = y * zs_ref[0, :, gc].astype(F32)
            parts.append(y)
            ssq = y * y if ssq is None else ssq + y * y
        inv = lax.rsqrt(jnp.sum(ssq, axis=-1, keepdims=True) * (1.0 / SSD_WIDTH) + EPS)
        for g, y in enumerate(parts):
            gc = slice(g * GROUP_W, (g + 1) * GROUP_W)
            mix_ref[0, :, gc] = (y * inv * snw_ref[:, gc]).astype(BF16)
        ssd_state_update(1, wtail_x, ab_x)

        for pr in range(RET_HEADS // 2):
            qp = q_ref[0, :, pr * LANES:(pr + 1) * LANES]
            pc = slice(pr * PAIR_W, (pr + 1) * PAIR_W)
            y2 = yar_ref[pl.ds(r0, Q), pc] + _dot(qp, ret_state_blockdiag(1, pr)) * eb_ref[pr]
            for hh in range(2):
                h = 2 * pr + hh
                hc = slice(h * LANES, (h + 1) * LANES)
                y = y2[:, hh * LANES:(hh + 1) * LANES]
                mu = jnp.mean(y, axis=-1, keepdims=True)
                yc = y - mu
                var = jnp.mean(yc * yc, axis=-1, keepdims=True)
                yn = yc * lax.rsqrt(var + EPS) * rnw_ref[:, hc] * gs_ref[0, :, hc].astype(F32)
                mix_ref[0, :, SSD_WIDTH + h * LANES:SSD_WIDTH + (h + 1) * LANES] = yn.astype(BF16)
        ret_state_update(1, tb_ref)

        if emit_state:
            @pl.when(s == 2 * nc - 1)
            def _emit_bwd():
                os_ref[0, 1] = ss_ref[1]
                or_ref[0, 0, 1] = sr_ref[1]


def _mixer_call(zs, xbc, dt, q, k, v, gs, nega, dsk, snw, rnw, lamx, s0_ssd, s0_ret, emit_state):
    nb, L, _ = zs.shape
    nc = L // CHUNK
    Q = CHUNK
    has_s0 = s0_ssd is not None

    def cidx(s):
        return jnp.where(s < nc, s, 2 * nc - 1 - s)

    def chunk_map(b, s):
        return (b, cidx(s), 0)

    def phase2_map(b, s):
        return (b, jnp.where(s < nc, nc - 1, 2 * nc - 1 - s), 0)

    const2 = lambda b, s: (0, 0)
    in_specs = [
        pl.BlockSpec((1, Q, SSD_WIDTH), phase2_map),
        pl.BlockSpec((1, Q, CONV_CH), chunk_map),
        pl.BlockSpec((1, Q, DT_PAD), chunk_map),
        pl.BlockSpec((1, Q, RET_QK_WIDTH), chunk_map),
        pl.BlockSpec((1, Q, RET_QK_WIDTH), chunk_map),
        pl.BlockSpec((1, Q, RET_V_WIDTH), chunk_map),
        pl.BlockSpec((1, Q, RET_V_WIDTH), phase2_map),
        pl.BlockSpec((1, DT_PAD), const2),
        pl.BlockSpec((1, SSD_WIDTH), const2),
        pl.BlockSpec((1, SSD_WIDTH), const2),
        pl.BlockSpec((1, RET_V_WIDTH), const2),
        pl.BlockSpec((2 * RET_HEADS, LANES), const2),
    ]
    args = [zs, xbc, dt, q, k, v, gs, nega, dsk, snw, rnw, lamx]
    if has_s0:
        in_specs += [
            pl.BlockSpec((1, 2, SSD_STATE, SSD_WIDTH), lambda b, s: (b, 0, 0, 0)),
            pl.BlockSpec((1, 1, 2, RET_HEADS, RET_QK_DIM, RET_V_DIM), lambda b, s: (b, 0, 0, 0, 0, 0)),
        ]
        args += [s0_ssd, s0_ret]
    out_specs = [pl.BlockSpec((1, Q, MIX_WIDTH), phase2_map)]
    out_shape = [jax.ShapeDtypeStruct((nb, L, MIX_WIDTH), BF16)]
    if emit_state:
        out_specs += [
            pl.BlockSpec((1, 2, SSD_STATE, SSD_WIDTH), lambda b, s: (b, 0, 0, 0)),
            pl.BlockSpec((1, 1, 2, RET_HEADS, RET_QK_DIM, RET_V_DIM), lambda b, s: (b, 0, 0, 0, 0, 0)),
        ]
        out_shape += [
            jax.ShapeDtypeStruct((nb, 2, SSD_STATE, SSD_WIDTH), F32),
            jax.ShapeDtypeStruct((nb, 1, 2, RET_HEADS, RET_QK_DIM, RET_V_DIM), F32),
        ]
    n_pairs = RET_HEADS // 2
    scratch = [
        pltpu.VMEM((L, SSD_WIDTH), F32),
        pltpu.VMEM((L, RET_V_WIDTH), F32),
        pltpu.VMEM((2, SSD_STATE, SSD_WIDTH), F32),
        pltpu.VMEM((2, RET_HEADS, RET_QK_DIM, RET_V_DIM), F32),
        pltpu.VMEM((n_pairs, Q, 2 * Q), F32),
        pltpu.VMEM((n_pairs, Q, PAIR_W), F32),
        pltpu.VMEM((n_pairs, Q, PAIR_W), F32),
        pltpu.VMEM((n_pairs, Q, PAIR_W), F32),
        pltpu.VMEM((n_pairs, Q, PAIR_W), F32),
        pltpu.VMEM((2 * RET_HEADS, LANES), F32),
        pltpu.VMEM((LANES, SSD_WIDTH), BF16),
        pltpu.VMEM((LANES, SSD_WIDTH), BF16),
    ]
    outs = pl.pallas_call(
        functools.partial(_mixer_kernel, nc=nc, has_s0=has_s0, emit_state=emit_state),
        grid=(nb, 2 * nc),
        in_specs=in_specs,
        out_specs=out_specs,
        out_shape=out_shape,
        scratch_shapes=scratch,
        compiler_params=pltpu.CompilerParams(
            dimension_semantics=("arbitrary", "arbitrary"), vmem_limit_bytes=VMEM_LIMIT),
        name="mixer_sample" if has_s0 else "mixer_prompt",
    )(*args)
    return outs


def _out_proj_kernel(mix_ref, w_ref, x_ref, mod_ref, npw_ref, y_ref):
    out = _dot(mix_ref[0], w_ref[...])
    ms = jnp.mean(out * out, axis=-1, keepdims=True)
    o = out * lax.rsqrt(ms + EPS) * npw_ref[...]
    gate = mod_ref[0, :, 2 * D_MODEL:3 * D_MODEL]
    y_ref[0] = x_ref[0] + gate * o


def _out_proj_call(mix, w_out_bf, x, mod, npw):
    nb, L, _ = x.shape
    tl = min(L, 512)
    per_seq_mod = mod.shape[0] > 1
    mod_map = (lambda s, t: (s, 0, 0)) if per_seq_mod else (lambda s, t: (0, 0, 0))
    return pl.pallas_call(
        _out_proj_kernel,
        grid=(nb, L // tl),
        in_specs=[
            pl.BlockSpec((1, tl, MIX_WIDTH), lambda s, t: (s, t, 0)),
            pl.BlockSpec((MIX_WIDTH, D_MODEL), lambda s, t: (0, 0)),
            pl.BlockSpec((1, tl, D_MODEL), lambda s, t: (s, t, 0)),
            pl.BlockSpec((1, 1, 3 * D_MODEL), mod_map),
            pl.BlockSpec((1, D_MODEL), lambda s, t: (0, 0)),
        ],
        out_specs=pl.BlockSpec((1, tl, D_MODEL), lambda s, t: (s, t, 0)),
        out_shape=jax.ShapeDtypeStruct((nb, L, D_MODEL), F32),
        compiler_params=pltpu.CompilerParams(
            dimension_semantics=("parallel", "parallel"), vmem_limit_bytes=VMEM_LIMIT),
        name="out_proj",
    )(mix, w_out_bf, x, mod, npw)


def _rope_tables(L):
    rows = L // GRID_W
    row = jnp.repeat(jnp.arange(rows, dtype=F32), GRID_W)
    col = jnp.tile(jnp.arange(GRID_W, dtype=F32), rows)
    half = RET_QK_DIM // 2
    inv = ROPE_BASE ** (-jnp.arange(0, half, 2, dtype=F32) / half)
    ang_r = row[:, None] * inv
    ang_c = col[:, None] * inv
    cos_h = jnp.concatenate([jnp.cos(ang_r), jnp.cos(ang_r), jnp.cos(ang_c), jnp.cos(ang_c)], axis=-1)
    sin_h = jnp.concatenate([-jnp.sin(ang_r), jnp.sin(ang_r), -jnp.sin(ang_c), jnp.sin(ang_c)], axis=-1)
    return jnp.tile(cos_h, (1, 2)), jnp.tile(sin_h, (1, 2))


def kernel(x_prompt, x_sample, state_ssd, state_ret, c, c_ctx, w_mod, b_mod, norm_pre_w, norm_post_w,
           w_in, conv_w, conv_b, ssd_A_log, ssd_dt_bias, ssd_D, ssd_norm_w, ret_decay, ret_norm_w, w_out):
    nb_p = x_prompt.shape[0]
    nb_s = x_sample.shape[0]
    l = 0

    wi = w_in[l]
    n_dt = 2 * SSD_HEADS
    o_dt = SSD_WIDTH + CONV_CH
    w_packed = jnp.concatenate([
        wi[:, :o_dt + n_dt],
        jnp.zeros((D_MODEL, DT_PAD - n_dt), F32),
        wi[:, o_dt + n_dt:],
    ], axis=1).astype(BF16)
    w_out_bf = w_out[l].astype(BF16)
    dtb = jnp.pad(ssd_dt_bias[l].reshape(1, n_dt), ((0, 0), (0, DT_PAD - n_dt)))
    nega = jnp.pad(-jnp.exp(ssd_A_log[l].reshape(1, n_dt)), ((0, 0), (0, DT_PAD - n_dt)))
    dsk = jnp.repeat(ssd_D[l], SSD_HEAD_DIM).reshape(1, SSD_WIDTH)
    lamx = jnp.broadcast_to(-jnp.exp(ret_decay[l].reshape(2 * RET_HEADS, 1)), (2 * RET_HEADS, LANES))
    npre = norm_pre_w[l].reshape(1, D_MODEL)
    npost = norm_post_w[l].reshape(1, D_MODEL)
    snw = ssd_norm_w[l].reshape(1, SSD_WIDTH)
    rnw = ret_norm_w[l].reshape(1, RET_V_WIDTH)
    cb = conv_b[l].reshape(1, CONV_CH)
    cos_t, sin_t = _rope_tables(x_sample.shape[1])

    cond = jnp.concatenate([c, c_ctx[None, :], jnp.zeros((16 - nb_s - 1, D_MODEL), F32)], axis=0)
    mod = _mod_call(cond, w_mod[l], b_mod[l].reshape(1, 3 * D_MODEL))
    mod_s = mod[:nb_s].reshape(nb_s, 1, 3 * D_MODEL)
    mod_p = mod[nb_s:nb_s + 1].reshape(1, 1, 3 * D_MODEL)

    s0_ssd = state_ssd[:, l].transpose(0, 1, 3, 2, 4).reshape(nb_s, 2, SSD_STATE, SSD_WIDTH)

    pp = _in_proj_call(x_prompt, mod_p, npre, w_packed, conv_w[l], cb, dtb, None, None)
    mix_p, st_ssd, st_ret = _mixer_call(*pp, nega, dsk, snw, rnw, lamx, None, None, True)
    y_p = _out_proj_call(mix_p, w_out_bf, x_prompt, mod_p, npost)

    ps = _in_proj_call(x_sample, mod_s, npre, w_packed, conv_w[l], cb, dtb, cos_t, sin_t)
    (mix_s,) = _mixer_call(*ps, nega, dsk, snw, rnw, lamx, s0_ssd, state_ret, False)
    y_s = _out_proj_call(mix_s, w_out_bf, x_sample, mod_s, npost)

    new_ssd = (st_ssd.reshape(nb_p, 2, SSD_STATE, SSD_HEADS, SSD_HEAD_DIM)
               .transpose(0, 1, 3, 2, 4)[:, None])
    return (y_p, y_s, new_ssd, st_ret)
```

```python
import functools

import jax
import jax.numpy as jnp
from jax import lax
from jax.experimental import pallas as pl
from jax.experimental.pallas import tpu as pltpu

F32 = jnp.float32
BF16 = jnp.bfloat16

D_MODEL = 1024
CHUNK = 128
GRID_W = 64
EPS = 1e-6
SSD_WIDTH = 1024
SSD_HEAD_DIM = 64
SSD_HEADS = 16
SSD_GROUPS = 4
SSD_STATE = 128
CONV_CH = SSD_WIDTH + 2 * SSD_GROUPS * SSD_STATE
RET_HEADS = 8
RET_QK_DIM = 64
RET_V_DIM = 128
RET_QK_WIDTH = RET_HEADS * RET_QK_DIM
RET_V_WIDTH = RET_HEADS * RET_V_DIM
MIX_WIDTH = SSD_WIDTH + RET_V_WIDTH
ROPE_BASE = 10000.0
LANES = 128
DT_PAD = LANES
GROUP_W = SSD_WIDTH // SSD_GROUPS
PAIR_W = 2 * RET_V_DIM

OFF_Z = 0
OFF_XBC = OFF_Z + SSD_WIDTH
OFF_DT = OFF_XBC + CONV_CH
OFF_Q = OFF_DT + DT_PAD
OFF_K = OFF_Q + RET_QK_WIDTH
OFF_V = OFF_K + RET_QK_WIDTH
OFF_G = OFF_V + RET_V_WIDTH
IN_COLS_PACKED = OFF_G + RET_V_WIDTH

NEG_BIG = -1e30
VMEM_LIMIT = 56 * 1024 * 1024


def _silu(x):
    return x * (1.0 / (1.0 + jnp.exp(-x)))


def _dot(a, b):
    return jnp.dot(a, b, preferred_element_type=F32)


def _dot_nt(a, b):
    return lax.dot_general(a, b, (((1,), (1,)), ((), ())), preferred_element_type=F32)


def _split3(x):
    hi = x.astype(BF16)
    r1 = x - hi.astype(F32)
    mid = r1.astype(BF16)
    lo = (r1 - mid.astype(F32)).astype(BF16)
    return hi, mid, lo


def _pack_kernel(w_ref, out_ref):
    n_dt = 2 * SSD_HEADS
    out_ref[:, 0:OFF_DT] = w_ref[:, 0:OFF_DT].astype(BF16)
    head = w_ref[:, OFF_DT:OFF_DT + DT_PAD]
    lane = lax.broadcasted_iota(jnp.int32, head.shape, 1)
    out_ref[:, OFF_DT:OFF_Q] = jnp.where(lane < n_dt, head, 0.0).astype(BF16)
    out_ref[:, OFF_Q:IN_COLS_PACKED] = w_ref[:, OFF_DT + n_dt:].astype(BF16)


def _pack_call(w_in):
    rows = 128
    in_cols = w_in.shape[1]
    return pl.pallas_call(
        _pack_kernel,
        grid=(D_MODEL // rows,),
        in_specs=[pl.BlockSpec((rows, in_cols), lambda i: (i, 0))],
        out_specs=pl.BlockSpec((rows, IN_COLS_PACKED), lambda i: (i, 0)),
        out_shape=jax.ShapeDtypeStruct((D_MODEL, IN_COLS_PACKED), BF16),
        name="pack_w_in",
    )(w_in)


def _mod_kernel(cond_ref, w_ref, b_ref, out_ref):
    a = _silu(cond_ref[...]).astype(BF16)
    out_ref[...] = _dot(a, w_ref[...].astype(BF16)) + b_ref[...]


def _mod_call(cond, w_mod, b_mod):
    rows = cond.shape[0]
    tn = 1024
    return pl.pallas_call(
        _mod_kernel,
        grid=(3 * D_MODEL // tn,),
        in_specs=[
            pl.BlockSpec((rows, D_MODEL), lambda j: (0, 0)),
            pl.BlockSpec((D_MODEL, tn), lambda j: (0, j)),
            pl.BlockSpec((1, tn), lambda j: (0, j)),
        ],
        out_specs=pl.BlockSpec((rows, tn), lambda j: (0, j)),
        out_shape=jax.ShapeDtypeStruct((rows, 3 * D_MODEL), F32),
        name="mod",
    )(cond, w_mod, b_mod)


def _modnorm(x, mod_ref, npw_ref):
    ms = jnp.mean(x * x, axis=-1, keepdims=True)
    xn = x * lax.rsqrt(ms + EPS) * npw_ref[...]
    shift = mod_ref[0, :, 0:D_MODEL]
    scale = mod_ref[0, :, D_MODEL:2 * D_MODEL]
    return (xn * (1.0 + scale) + shift).astype(BF16)


def _in_proj_kernel(*refs, tl, n_tiles, rope):
    if rope:
        (x_ref, xp_ref, xn_ref, mod_ref, npw_ref, w_ref, cw_ref, cb_ref, dtb_ref,
         cos_ref, sin_ref, zs_ref, xbc_ref, dt_ref, q_ref, k_ref, v_ref, gs_ref, h_ref) = refs
    else:
        (x_ref, xp_ref, xn_ref, mod_ref, npw_ref, w_ref, cw_ref, cb_ref, dtb_ref,
         zs_ref, xbc_ref, dt_ref, q_ref, k_ref, v_ref, gs_ref, h_ref) = refs
    t = pl.program_id(1)
    h_ref[...] = _modnorm(x_ref[0], mod_ref, npw_ref)
    piece = 256

    if n_tiles > 1:
        halo = jnp.concatenate([xp_ref[0], xn_ref[0]], axis=0)
        hh = _modnorm(halo, mod_ref, npw_ref)
        pv = jnp.where(t > 0, 1.0, 0.0).astype(F32)
        nv = jnp.where(t < n_tiles - 1, 1.0, 0.0).astype(F32)
    row = lax.broadcasted_iota(jnp.int32, (tl, piece), 0)

    def sec_xbc(c0):
        wcols = w_ref[:, OFF_XBC + c0:OFF_XBC + c0 + piece]
        acc = _dot(h_ref[...], wcols)
        up = pltpu.roll(acc, 1, axis=0)
        dn = pltpu.roll(acc, tl - 1, axis=0)
        if n_tiles > 1:
            uh = _dot(hh, wcols)
            up = jnp.where(row == 0, uh[7:8, :] * pv, up)
            dn = jnp.where(row == tl - 1, uh[8:9, :] * nv, dn)
        else:
            up = jnp.where(row == 0, 0.0, up)
            dn = jnp.where(row == tl - 1, 0.0, dn)
        y = (cw_ref[0:1, c0:c0 + piece] * up + cw_ref[1:2, c0:c0 + piece] * acc
             + cw_ref[2:3, c0:c0 + piece] * dn + cb_ref[:, c0:c0 + piece])
        xbc_ref[0, :, c0:c0 + piece] = _silu(y).astype(BF16)

    lane = lax.broadcasted_iota(jnp.int32, (tl, LANES), 1)
    first_half = (lane % 32) < 16

    def sec_qk(off, ref, scl, c0):
        acc = _dot(h_ref[...], w_ref[:, off + c0:off + c0 + piece]) * scl
        if rope:
            for s0 in range(0, piece, LANES):
                xs = acc[:, s0:s0 + LANES]
                partner = jnp.where(first_half, pltpu.roll(xs, LANES - 16, axis=1),
                                    pltpu.roll(xs, 16, axis=1))
                ref[0, :, c0 + s0:c0 + s0 + LANES] = (
                    xs * cos_ref[...] + partner * sin_ref[...]).astype(BF16)
        else:
            ref[0, :, c0:c0 + piece] = acc.astype(BF16)

    def sec_q(c0):
        sec_qk(OFF_Q, q_ref, 1.0, c0)

    def sec_k(c0):
        sec_qk(OFF_K, k_ref, RET_QK_DIM ** -0.5, c0)

    def sec_z(c0):
        acc = _dot(h_ref[...], w_ref[:, OFF_Z + c0:OFF_Z + c0 + piece])
        zs_ref[0, :, c0:c0 + piece] = _silu(acc).astype(BF16)

    def sec_g(c0):
        acc = _dot(h_ref[...], w_ref[:, OFF_G + c0:OFF_G + c0 + piece])
        gs_ref[0, :, c0:c0 + piece] = _silu(acc).astype(BF16)

    def sec_v(c0):
        acc = _dot(h_ref[...], w_ref[:, OFF_V + c0:OFF_V + c0 + piece])
        v_ref[0, :, c0:c0 + piece] = acc.astype(BF16)

    def sec_dt(c0):
        acc = _dot(h_ref[...], w_ref[:, OFF_DT:OFF_DT + DT_PAD]) + dtb_ref[...]
        dt_ref[0] = jnp.maximum(acc, 0.0) + jnp.log(1.0 + jnp.exp(-jnp.abs(acc)))

    def pieces(fn, width):
        return [(fn, c0) for c0 in range(0, width, piece)]

    heavy = (pieces(sec_z, SSD_WIDTH) + pieces(sec_g, RET_V_WIDTH) + pieces(sec_xbc, CONV_CH)
             + pieces(sec_q, RET_QK_WIDTH) + pieces(sec_k, RET_QK_WIDTH))
    light = pieces(sec_v, RET_V_WIDTH) + [(sec_dt, 0)]
    stride = len(heavy) // len(light)
    order = []
    for i, item in enumerate(heavy):
        order.append(item)
        if i % stride == stride - 1 and light:
            order.append(light.pop(0))
    order += light
    for fn, c0 in order:
        fn(c0)


def _in_proj_call(x, mod, npw, w_packed, conv_w, conv_b, dtb, cos_t, sin_t):
    nb, L, _ = x.shape
    tl = min(L, 512)
    n_tiles = L // tl
    g8 = tl // 8
    rope = cos_t is not None
    per_seq_mod = mod.shape[0] > 1
    mod_map = (lambda s, t: (s, 0, 0)) if per_seq_mod else (lambda s, t: (0, 0, 0))
    const2 = lambda s, t: (0, 0)
    in_specs = [
        pl.BlockSpec((1, tl, D_MODEL), lambda s, t: (s, t, 0)),
        pl.BlockSpec((1, 8, D_MODEL), lambda s, t: (s, jnp.maximum(t * g8 - 1, 0), 0)),
        pl.BlockSpec((1, 8, D_MODEL), lambda s, t: (s, jnp.minimum((t + 1) * g8, L // 8 - 1), 0)),
        pl.BlockSpec((1, 1, 3 * D_MODEL), mod_map),
        pl.BlockSpec((1, D_MODEL), const2),
        pl.BlockSpec((D_MODEL, IN_COLS_PACKED), const2, pipeline_mode=pl.Buffered(1)),
        pl.BlockSpec((3, CONV_CH), const2),
        pl.BlockSpec((1, CONV_CH), const2),
        pl.BlockSpec((1, DT_PAD), const2),
    ]
    args = [x, x, x, mod, npw, w_packed, conv_w, conv_b, dtb]
    if rope:
        in_specs += [pl.BlockSpec((tl, LANES), lambda s, t: (t, 0)),
                     pl.BlockSpec((tl, LANES), lambda s, t: (t, 0))]
        args += [cos_t, sin_t]
    widths = (SSD_WIDTH, CONV_CH, DT_PAD, RET_QK_WIDTH, RET_QK_WIDTH, RET_V_WIDTH, RET_V_WIDTH)
    dtypes = (BF16, BF16, F32, BF16, BF16, BF16, BF16)
    out_specs = [pl.BlockSpec((1, tl, w), lambda s, t: (s, t, 0)) for w in widths]
    out_shape = [jax.ShapeDtypeStruct((nb, L, w), d) for w, d in zip(widths, dtypes)]
    return pl.pallas_call(
        functools.partial(_in_proj_kernel, tl=tl, n_tiles=n_tiles, rope=rope),
        grid=(nb, n_tiles),
        in_specs=in_specs,
        out_specs=out_specs,
        out_shape=out_shape,
        scratch_shapes=[pltpu.VMEM((tl, D_MODEL), BF16)],
        compiler_params=pltpu.CompilerParams(
            dimension_semantics=("parallel", "parallel"), vmem_limit_bytes=VMEM_LIMIT),
        name="in_proj_rope" if rope else "in_proj",
    )(*args)


def _mixer_kernel(*refs, nc, has_s0, emit_state):
    Q = CHUNK
    H = SSD_HEADS
    it = iter(refs)
    zs_ref, xbc_ref, dt_ref, q_ref, k_ref, v_ref, gs_ref = (next(it) for _ in range(7))
    nega_ref, dsk_ref, snw_ref, rnw_ref, lam_ref = (next(it) for _ in range(5))
    if has_s0:
        s0s_ref, s0r_ref = next(it), next(it)
    mix_ref = next(it)
    if emit_state:
        os_ref, or_ref = next(it), next(it)
    (yas_ref, yar_ref, ss_ref, sr_ref, dsum_ref, ef_ref, eb_ref, tf_ref, tb_ref,
     ar_ref, exf_ref, exb_ref, cum_ref) = (next(it) for _ in range(13))

    b = pl.program_id(0)
    s = pl.program_id(1)
    phase2 = s >= nc
    c = jnp.where(phase2, 2 * nc - 1 - s, s)
    r0 = pl.multiple_of(c * Q, Q)

    rowq = lax.broadcasted_iota(jnp.int32, (Q, Q), 0)
    colq = lax.broadcasted_iota(jnp.int32, (Q, Q), 1)

    @pl.when((b == 0) & (s == 0))
    def _init_tables():
        diff = (rowq - colq).astype(F32)
        rowf = lax.broadcasted_iota(jnp.int32, (Q, LANES), 0).astype(F32)
        for h in range(RET_HEADS):
            pr, hh = divmod(h, 2)
            lf = lam_ref[h:h + 1, :]
            lb = lam_ref[RET_HEADS + h:RET_HEADS + h + 1, :]
            e = jnp.where(rowq >= colq, lf * diff, lb * (-diff))
            dsum_ref[pr, :, hh * Q:(hh + 1) * Q] = jnp.exp(e) * jnp.where(rowq == colq, 2.0, 1.0)
            hc = slice(hh * LANES, (hh + 1) * LANES)
            ef_ref[pr, :, hc] = jnp.exp(lf * (rowf + 1.0))
            eb_ref[pr, :, hc] = jnp.exp(lb * (Q - rowf))
            tf_ref[pr, :, hc] = jnp.exp(lf * (Q - 1.0 - rowf))
            tb_ref[pr, :, hc] = jnp.exp(lb * rowf)
        ar_ref[...] = jnp.exp(lam_ref[...] * float(Q))
        er = lax.broadcasted_iota(jnp.int32, (LANES, SSD_WIDTH), 0)
        ec = lax.broadcasted_iota(jnp.int32, (LANES, SSD_WIDTH), 1) // SSD_HEAD_DIM
        exf_ref[...] = jnp.where(er == ec, 1.0, 0.0).astype(BF16)
        exb_ref[...] = jnp.where(er == ec + H, 1.0, 0.0).astype(BF16)

    @pl.when(s == 0)
    def _init_state():
        if has_s0:
            for d in range(2):
                for p in range(SSD_HEADS // 2):
                    ss_ref[d, :, p * LANES:(p + 1) * LANES] = jnp.concatenate(
                        [s0s_ref[0, 0, d, 2 * p], s0s_ref[0, 0, d, 2 * p + 1]], axis=1)
            sr_ref[...] = s0r_ref[0, 0]
        else:
            ss_ref[...] = jnp.zeros_like(ss_ref)
            sr_ref[...] = jnp.zeros_like(sr_ref)

    def expand(w, a_row, ex_ref):
        a3 = jnp.concatenate(_split3(jnp.broadcast_to(a_row, (16, LANES))), axis=0)
        out = _dot(jnp.concatenate([w.astype(BF16), a3], axis=0), ex_ref[...])
        n = w.shape[0]
        return out[0:n], out[n:n + 1] + out[n + 16:n + 17] + out[n + 32:n + 33]

    def ssd_state_update(d, wtail_x, a_x):
        for g in range(SSD_GROUPS):
            bg = xbc_ref[0, :, SSD_WIDTH + g * SSD_STATE:SSD_WIDTH + (g + 1) * SSD_STATE]
            bgt = bg.astype(F32).T.astype(BF16)
            gc = slice(g * GROUP_W, (g + 1) * GROUP_W)
            xt = (xbc_ref[0, :, gc].astype(F32) * wtail_x[:, gc]).astype(BF16)
            ss_ref[d, :, gc] = ss_ref[d, :, gc] * a_x[:, gc] + _dot(bgt, xt)

    def ret_state_update(d, tail_ref):
        for pr in range(RET_HEADS // 2):
            kt = k_ref[0, :, pr * LANES:(pr + 1) * LANES].astype(F32).T.astype(BF16)
            pc = slice(pr * PAIR_W, (pr + 1) * PAIR_W)
            vt = (v_ref[0, :, pc].astype(F32) * tail_ref[pr]).astype(BF16)
            ds = _dot(kt, vt)
            for hh in range(2):
                h = 2 * pr + hh
                a = ar_ref[d * RET_HEADS + h:d * RET_HEADS + h + 1, :]
                sr_ref[d, h] = (sr_ref[d, h] * a
                                + ds[hh * RET_QK_DIM:(hh + 1) * RET_QK_DIM, hh * LANES:(hh + 1) * LANES])

    def ret_state_blockdiag(d, pr):
        z = jnp.zeros((RET_QK_DIM, RET_V_DIM), F32)
        top = jnp.concatenate([sr_ref[d, 2 * pr], z], axis=1)
        bot = jnp.concatenate([z, sr_ref[d, 2 * pr + 1]], axis=1)
        return jnp.concatenate([top, bot], axis=0).astype(BF16)

    @pl.when(jnp.logical_not(phase2))
    def _phase1():
        dt = dt_ref[0]
        la = dt * nega_ref[...]
        tl_bf = jnp.where(rowq >= colq, 1.0, 0.0).astype(BF16)
        c3 = _dot(tl_bf, jnp.concatenate(_split3(la), axis=1))
        cum = c3[:, 0:LANES] + c3[:, LANES:2 * LANES] + c3[:, 2 * LANES:3 * LANES]
        cum_ref[pl.ds(r0, Q), :] = cum
        tot = cum[Q - 1:Q, :]
        rev = tot - cum + la

        sub_k = lax.broadcasted_iota(jnp.int32, (LANES, Q), 0) < RET_QK_DIM
        lane_v = lax.broadcasted_iota(jnp.int32, (Q, PAIR_W), 1) < RET_V_DIM
        for pr in range(RET_HEADS // 2):
            qp = q_ref[0, :, pr * LANES:(pr + 1) * LANES]
            kt = k_ref[0, :, pr * LANES:(pr + 1) * LANES].astype(F32).T.astype(BF16)
            zk = jnp.zeros_like(kt)
            kbd = jnp.concatenate([jnp.where(sub_k, kt, zk), jnp.where(sub_k, zk, kt)], axis=1)
            a2 = _dot(qp, kbd)
            ad = (a2 * dsum_ref[pr]).astype(BF16)
            pc = slice(pr * PAIR_W, (pr + 1) * PAIR_W)
            vp = v_ref[0, :, pc]
            zv = jnp.zeros_like(vp)
            vbd = jnp.concatenate([jnp.where(lane_v, vp, zv), jnp.where(lane_v, zv, vp)], axis=0)
            y = _dot(ad, vbd) + _dot(qp, ret_state_blockdiag(0, pr)) * ef_ref[pr]
            yar_ref[pl.ds(r0, Q), pc] = y
        ret_state_update(0, tf_ref)

        gms, css = [], []
        for g in range(SSD_GROUPS):
            cg = xbc_ref[0, :, SSD_WIDTH + 512 + g * SSD_STATE:SSD_WIDTH + 512 + (g + 1) * SSD_STATE]
            bg = xbc_ref[0, :, SSD_WIDTH + g * SSD_STATE:SSD_WIDTH + (g + 1) * SSD_STATE]
            gms.append(_dot_nt(cg, bg))
            css.append(_dot(cg, ss_ref[0, :, g * GROUP_W:(g + 1) * GROUP_W].astype(BF16)))

        dt_t = dt.T
        ldt_t = jnp.log(dt_t)
        sub = lax.broadcasted_iota(jnp.int32, (LANES, Q), 0)
        adj_t = jnp.where(sub < H, cum.T, rev.T) - ldt_t
        dg_t = jnp.log(dt_t[0:H, :] + dt_t[H:2 * H, :])
        wtail_f = dt * jnp.exp(tot - cum)
        ew_x, af_x = expand(jnp.concatenate([jnp.exp(cum), wtail_f], axis=0), jnp.exp(tot), exf_ref)
        ecum_x = ew_x[0:Q]
        wtail_x = ew_x[Q:2 * Q]
        lt = rowq > colq
        gt = rowq < colq
        lane_g = lax.broadcasted_iota(jnp.int32, (Q, GROUP_W), 1) // SSD_HEAD_DIM

        for g in range(SSD_GROUPS):
            gc = slice(g * GROUP_W, (g + 1) * GROUP_W)
            xg = xbc_ref[0, :, gc]
            ws, xs = [], []
            for j in range(4):
                h = 4 * g + j
                hb = H + h
                arg = jnp.where(lt, cum[:, h:h + 1] - adj_t[h:h + 1, :],
                                jnp.where(gt, rev[:, hb:hb + 1] - adj_t[hb:hb + 1, :], dg_t[h:h + 1, :]))
                ws.append((gms[g] * jnp.exp(arg)).astype(BF16))
                xs.append(jnp.where(lane_g == j, xg, jnp.zeros_like(xg)))
            y = _dot(jnp.concatenate(ws, axis=1), jnp.concatenate(xs, axis=0))
            yas_ref[pl.ds(r0, Q), gc] = y + css[g] * ecum_x[:, gc]
        ssd_state_update(0, wtail_x, af_x)

        if emit_state:
            @pl.when(s == nc - 1)
            def _emit_fwd():
                for h in range(H):
                    os_ref[0, 0, 0, h] = ss_ref[0, :, h * SSD_HEAD_DIM:(h + 1) * SSD_HEAD_DIM]
                or_ref[0, 0, 0] = sr_ref[0]

    @pl.when(phase2)
    def _phase2():
        for pr in range(RET_HEADS // 2):
            qp = q_ref[0, :, pr * LANES:(pr + 1) * LANES]
            pc = slice(pr * PAIR_W, (pr + 1) * PAIR_W)
            y2 = yar_ref[pl.ds(r0, Q), pc] + _dot(qp, ret_state_blockdiag(1, pr)) * eb_ref[pr]
            for hh in range(2):
                h = 2 * pr + hh
                hc = slice(h * LANES, (h + 1) * LANES)
                y = y2[:, hh * LANES:(hh + 1) * LANES]
                mu = jnp.mean(y, axis=-1, keepdims=True)
                yc = y - mu
                var = jnp.mean(yc * yc, axis=-1, keepdims=True)
                yn = yc * lax.rsqrt(var + EPS) * rnw_ref[:, hc] * gs_ref[0, :, hc].astype(F32)
                mix_ref[0, :, SSD_WIDTH + h * LANES:SSD_WIDTH + (h + 1) * LANES] = yn.astype(BF16)
        ret_state_update(1, tb_ref)

        dt = dt_ref[0]
        cum = cum_ref[pl.ds(r0, Q), :]
        rev = cum[Q - 1:Q, :] - cum + dt * nega_ref[...]
        first = rev[0:1, :]
        wtail_b = dt * jnp.exp(first - rev)
        ew_x, ab_x = expand(jnp.concatenate([jnp.exp(rev), wtail_b], axis=0), jnp.exp(first), exb_ref)
        erev_x = ew_x[0:Q]
        wtail_x = ew_x[Q:2 * Q]

        parts = []
        ssq = None
        for g in range(SSD_GROUPS):
            cg = xbc_ref[0, :, SSD_WIDTH + 512 + g * SSD_STATE:SSD_WIDTH + 512 + (g + 1) * SSD_STATE]
            gc = slice(g * GROUP_W, (g + 1) * GROUP_W)
            cs = _dot(cg, ss_ref[1, :, gc].astype(BF16))
            y = (yas_ref[pl.ds(r0, Q), gc] + cs * erev_x[:, gc]
                 + dsk_ref[:, gc] * xbc_ref[0, :, gc].astype(F32))
            y = y * zs_ref[0, :, gc].astype(F32)
            parts.append(y)
            ssq = y * y if ssq is None else ssq + y * y
        inv = lax.rsqrt(jnp.sum(ssq, axis=-1, keepdims=True) * (1.0 / SSD_WIDTH) + EPS)
        for g, y in enumerate(parts):
            gc = slice(g * GROUP_W, (g + 1) * GROUP_W)
            mix_ref[0, :, gc] = (y * inv * snw_ref[:, gc]).astype(BF16)
        ssd_state_update(1, wtail_x, ab_x)

        if emit_state:
            @pl.when(s == 2 * nc - 1)
            def _emit_bwd():
                for h in range(H):
                    os_ref[0, 0, 1, h] = ss_ref[1, :, h * SSD_HEAD_DIM:(h + 1) * SSD_HEAD_DIM]
                or_ref[0, 0, 1] = sr_ref[1]


def _mixer_call(zs, xbc, dt, q, k, v, gs, nega, dsk, snw, rnw, lamx, s0_ssd, s0_ret, emit_state):
    nb, L, _ = zs.shape
    nc = L // CHUNK
    Q = CHUNK
    has_s0 = s0_ssd is not None

    def cidx(s):
        return jnp.where(s < nc, s, 2 * nc - 1 - s)

    def chunk_map(b, s):
        return (b, cidx(s), 0)

    def phase2_map(b, s):
        return (b, jnp.where(s < nc, nc - 1, 2 * nc - 1 - s), 0)

    const2 = lambda b, s: (0, 0)
    in_specs = [
        pl.BlockSpec((1, Q, SSD_WIDTH), phase2_map),
        pl.BlockSpec((1, Q, CONV_CH), chunk_map),
        pl.BlockSpec((1, Q, DT_PAD), chunk_map),
        pl.BlockSpec((1, Q, RET_QK_WIDTH), chunk_map),
        pl.BlockSpec((1, Q, RET_QK_WIDTH), chunk_map),
        pl.BlockSpec((1, Q, RET_V_WIDTH), chunk_map),
        pl.BlockSpec((1, Q, RET_V_WIDTH), phase2_map),
        pl.BlockSpec((1, DT_PAD), const2),
        pl.BlockSpec((1, SSD_WIDTH), const2),
        pl.BlockSpec((1, SSD_WIDTH), const2),
        pl.BlockSpec((1, RET_V_WIDTH), const2),
        pl.BlockSpec((2 * RET_HEADS, LANES), const2),
    ]
    args = [zs, xbc, dt, q, k, v, gs, nega, dsk, snw, rnw, lamx]
    state_map = lambda b, s: (b, 0, 0, 0, 0, 0)
    ssd_state_block = (1, 1, 2, SSD_HEADS, SSD_STATE, SSD_HEAD_DIM)
    ret_state_block = (1, 1, 2, RET_HEADS, RET_QK_DIM, RET_V_DIM)
    if has_s0:
        in_specs += [pl.BlockSpec(ssd_state_block, state_map), pl.BlockSpec(ret_state_block, state_map)]
        args += [s0_ssd, s0_ret]
    out_specs = [pl.BlockSpec((1, Q, MIX_WIDTH), phase2_map)]
    out_shape = [jax.ShapeDtypeStruct((nb, L, MIX_WIDTH), BF16)]
    if emit_state:
        out_specs += [pl.BlockSpec(ssd_state_block, state_map), pl.BlockSpec(ret_state_block, state_map)]
        out_shape += [jax.ShapeDtypeStruct((nb,) + ssd_state_block[1:], F32),
                      jax.ShapeDtypeStruct((nb,) + ret_state_block[1:], F32)]
    n_pairs = RET_HEADS // 2
    scratch = [
        pltpu.VMEM((L, SSD_WIDTH), F32),
        pltpu.VMEM((L, RET_V_WIDTH), F32),
        pltpu.VMEM((2, SSD_STATE, SSD_WIDTH), F32),
        pltpu.VMEM((2, RET_HEADS, RET_QK_DIM, RET_V_DIM), F32),
        pltpu.VMEM((n_pairs, Q, 2 * Q), F32),
        pltpu.VMEM((n_pairs, Q, PAIR_W), F32),
        pltpu.VMEM((n_pairs, Q, PAIR_W), F32),
        pltpu.VMEM((n_pairs, Q, PAIR_W), F32),
        pltpu.VMEM((n_pairs, Q, PAIR_W), F32),
        pltpu.VMEM((2 * RET_HEADS, LANES), F32),
        pltpu.VMEM((LANES, SSD_WIDTH), BF16),
        pltpu.VMEM((LANES, SSD_WIDTH), BF16),
        pltpu.VMEM((L, DT_PAD), F32),
    ]
    outs = pl.pallas_call(
        functools.partial(_mixer_kernel, nc=nc, has_s0=has_s0, emit_state=emit_state),
        grid=(nb, 2 * nc),
        in_specs=in_specs,
        out_specs=out_specs,
        out_shape=out_shape,
        scratch_shapes=scratch,
        compiler_params=pltpu.CompilerParams(
            dimension_semantics=("arbitrary", "arbitrary"), vmem_limit_bytes=VMEM_LIMIT),
        name="mixer_sample" if has_s0 else "mixer_prompt",
    )(*args)
    return outs


def _out_proj_kernel(mix_ref, w_ref, x_ref, mod_ref, npw_ref, y_ref):
    out = _dot(mix_ref[0], w_ref[...])
    ms = jnp.mean(out * out, axis=-1, keepdims=True)
    o = out * lax.rsqrt(ms + EPS) * npw_ref[...]
    gate = mod_ref[0, :, 2 * D_MODEL:3 * D_MODEL]
    y_ref[0] = x_ref[0] + gate * o


def _out_proj_call(mix, w_out_bf, x, mod, npw):
    nb, L, _ = x.shape
    tl = min(L, 512)
    per_seq_mod = mod.shape[0] > 1
    mod_map = (lambda s, t: (s, 0, 0)) if per_seq_mod else (lambda s, t: (0, 0, 0))
    return pl.pallas_call(
        _out_proj_kernel,
        grid=(nb, L // tl),
        in_specs=[
            pl.BlockSpec((1, tl, MIX_WIDTH), lambda s, t: (s, t, 0)),
            pl.BlockSpec((MIX_WIDTH, D_MODEL), lambda s, t: (0, 0)),
            pl.BlockSpec((1, tl, D_MODEL), lambda s, t: (s, t, 0)),
            pl.BlockSpec((1, 1, 3 * D_MODEL), mod_map),
            pl.BlockSpec((1, D_MODEL), lambda s, t: (0, 0)),
        ],
        out_specs=pl.BlockSpec((1, tl, D_MODEL), lambda s, t: (s, t, 0)),
        out_shape=jax.ShapeDtypeStruct((nb, L, D_MODEL), F32),
        compiler_params=pltpu.CompilerParams(
            dimension_semantics=("parallel", "parallel"), vmem_limit_bytes=VMEM_LIMIT),
        name="out_proj",
    )(mix, w_out_bf, x, mod, npw)


def _rope_tables(L):
    rows = L // GRID_W
    row = jnp.repeat(jnp.arange(rows, dtype=F32), GRID_W)
    col = jnp.tile(jnp.arange(GRID_W, dtype=F32), rows)
    half = RET_QK_DIM // 2
    inv = ROPE_BASE ** (-jnp.arange(0, half, 2, dtype=F32) / half)
    ang_r = row[:, None] * inv
    ang_c = col[:, None] * inv
    cos_h = jnp.concatenate([jnp.cos(ang_r), jnp.cos(ang_r), jnp.cos(ang_c), jnp.cos(ang_c)], axis=-1)
    sin_h = jnp.concatenate([-jnp.sin(ang_r), jnp.sin(ang_r), -jnp.sin(ang_c), jnp.sin(ang_c)], axis=-1)
    return jnp.tile(cos_h, (1, 2)), jnp.tile(sin_h, (1, 2))


def kernel(x_prompt, x_sample, state_ssd, state_ret, c, c_ctx, w_mod, b_mod, norm_pre_w, norm_post_w,
           w_in, conv_w, conv_b, ssd_A_log, ssd_dt_bias, ssd_D, ssd_norm_w, ret_decay, ret_norm_w, w_out):
    nb_p = x_prompt.shape[0]
    nb_s = x_sample.shape[0]
    l = 0

    n_dt = 2 * SSD_HEADS
    w_packed = _pack_call(w_in[l])
    w_out_bf = w_out[l].astype(BF16)
    dtb = jnp.pad(ssd_dt_bias[l].reshape(1, n_dt), ((0, 0), (0, DT_PAD - n_dt)))
    nega = jnp.pad(-jnp.exp(ssd_A_log[l].reshape(1, n_dt)), ((0, 0), (0, DT_PAD - n_dt)))
    dsk = jnp.repeat(ssd_D[l], SSD_HEAD_DIM).reshape(1, SSD_WIDTH)
    lamx = jnp.broadcast_to(-jnp.exp(ret_decay[l].reshape(2 * RET_HEADS, 1)), (2 * RET_HEADS, LANES))
    npre = norm_pre_w[l].reshape(1, D_MODEL)
    npost = norm_post_w[l].reshape(1, D_MODEL)
    snw = ssd_norm_w[l].reshape(1, SSD_WIDTH)
    rnw = ret_norm_w[l].reshape(1, RET_V_WIDTH)
    cb = conv_b[l].reshape(1, CONV_CH)
    cos_t, sin_t = _rope_tables(x_sample.shape[1])

    cond = jnp.concatenate([c, c_ctx[None, :], jnp.zeros((16 - nb_s - 1, D_MODEL), F32)], axis=0)
    mod = _mod_call(cond, w_mod[l], b_mod[l].reshape(1, 3 * D_MODEL))
    mod_s = mod[:nb_s].reshape(nb_s, 1, 3 * D_MODEL)
    mod_p = mod[nb_s:nb_s + 1].reshape(1, 1, 3 * D_MODEL)

    pp = _in_proj_call(x_prompt, mod_p, npre, w_packed, conv_w[l], cb, dtb, None, None)
    mix_p, st_ssd, st_ret = _mixer_call(*pp, nega, dsk, snw, rnw, lamx, None, None, True)
    y_p = _out_proj_call(mix_p, w_out_bf, x_prompt, mod_p, npost)

    ps = _in_proj_call(x_sample, mod_s, npre, w_packed, conv_w[l], cb, dtb, cos_t, sin_t)
    (mix_s,) = _mixer_call(*ps, nega, dsk, snw, rnw, lamx, state_ssd, state_ret, False)
    y_s = _out_proj_call(mix_s, w_out_bf, x_sample, mod_s, npost)
    return (y_p, y_s, st_ssd, st_ret)
```

```python
import functools

import jax
import jax.numpy as jnp
from jax import lax
from jax.experimental import pallas as pl
from jax.experimental.pallas import tpu as pltpu

F32 = jnp.float32
BF16 = jnp.bfloat16

D_MODEL = 1024
CHUNK = 128
GRID_W = 64
EPS = 1e-6
SSD_WIDTH = 1024
SSD_HEAD_DIM = 64
SSD_HEADS = 16
SSD_GROUPS = 4
SSD_STATE = 128
CONV_CH = SSD_WIDTH + 2 * SSD_GROUPS * SSD_STATE
RET_HEADS = 8
RET_QK_DIM = 64
RET_V_DIM = 128
RET_QK_WIDTH = RET_HEADS * RET_QK_DIM
RET_V_WIDTH = RET_HEADS * RET_V_DIM
MIX_WIDTH = SSD_WIDTH + RET_V_WIDTH
ROPE_BASE = 10000.0
LANES = 128
DT_PAD = LANES
GROUP_W = SSD_WIDTH // SSD_GROUPS
PAIR_W = 2 * RET_V_DIM

OFF_Z = 0
OFF_XBC = OFF_Z + SSD_WIDTH
OFF_Q = OFF_XBC + CONV_CH
OFF_K = OFF_Q + RET_QK_WIDTH
OFF_V = OFF_K + RET_QK_WIDTH
OFF_G = OFF_V + RET_V_WIDTH
OFF_DT = OFF_G + RET_V_WIDTH
IN_COLS_PACKED = OFF_DT + DT_PAD

LOG2E = 1.4426950408889634
VMEM_LIMIT = 56 * 1024 * 1024


def _silu(x):
    return x * (1.0 / (1.0 + jnp.exp(-x)))


def _dot(a, b):
    return jnp.dot(a, b, preferred_element_type=F32)


def _dot_nt(a, b):
    return lax.dot_general(a, b, (((1,), (1,)), ((), ())), preferred_element_type=F32)


def _split3(x):
    hi = x.astype(BF16)
    r1 = x - hi.astype(F32)
    mid = r1.astype(BF16)
    lo = (r1 - mid.astype(F32)).astype(BF16)
    return hi, mid, lo


PACK_COLS = 512
PACK_STEPS = -(-IN_COLS_PACKED // PACK_COLS)


def _pack_kernel(wt_ref, out_ref):
    t = wt_ref[...].T
    lane = lax.broadcasted_iota(jnp.int32, t.shape, 1)
    is_dt = pl.program_id(0) == PACK_STEPS - 1
    t = jnp.where(jnp.logical_and(is_dt, lane >= 2 * SSD_HEADS), 0.0, t)
    out_ref[...] = t.astype(BF16)


def _pack_call(w_in_t):
    n_head = (OFF_Q - OFF_Z) // PACK_COLS
    src_dt = SSD_WIDTH + CONV_CH
    src_tail = src_dt + 2 * SSD_HEADS

    def src_row(i):
        t8 = jnp.where(i < n_head, i * (PACK_COLS // 8),
                       jnp.where(i < PACK_STEPS - 1, src_tail // 8 + (i - n_head) * (PACK_COLS // 8),
                                 src_dt // 8))
        return t8 * 8

    return pl.pallas_call(
        _pack_kernel,
        grid=(PACK_STEPS,),
        in_specs=[pl.BlockSpec((pl.Element(PACK_COLS), pl.Element(D_MODEL)), lambda i: (src_row(i), 0))],
        out_specs=pl.BlockSpec((D_MODEL, PACK_COLS), lambda i: (0, i)),
        out_shape=jax.ShapeDtypeStruct((D_MODEL, IN_COLS_PACKED), BF16),
        name="pack_w_in",
    )(w_in_t)


def _scale_rows_kernel(w_ref, s_ref, out_ref):
    out_ref[...] = (w_ref[...] * s_ref[...]).astype(BF16)


def _scale_rows_call(w, row_scale):
    rows, cols = w.shape
    tr = 512
    return pl.pallas_call(
        _scale_rows_kernel,
        grid=(rows // tr,),
        in_specs=[pl.BlockSpec((tr, cols), lambda i: (i, 0)), pl.BlockSpec((tr, 1), lambda i: (i, 0))],
        out_specs=pl.BlockSpec((tr, cols), lambda i: (i, 0)),
        out_shape=jax.ShapeDtypeStruct((rows, cols), BF16),
        name="fold_norm_w_out",
    )(w, row_scale)


def _mod_kernel(cond_ref, w_ref, b_ref, out_ref):
    a = _silu(cond_ref[...]).astype(BF16)
    out_ref[...] = _dot(a, w_ref[...].astype(BF16)) + b_ref[...]


def _mod_call(cond, w_mod, b_mod):
    rows = cond.shape[0]
    tn = 1024
    return pl.pallas_call(
        _mod_kernel,
        grid=(3 * D_MODEL // tn,),
        in_specs=[
            pl.BlockSpec((rows, D_MODEL), lambda j: (0, 0)),
            pl.BlockSpec((D_MODEL, tn), lambda j: (0, j)),
            pl.BlockSpec((1, tn), lambda j: (0, j)),
        ],
        out_specs=pl.BlockSpec((rows, tn), lambda j: (0, j)),
        out_shape=jax.ShapeDtypeStruct((rows, 3 * D_MODEL), F32),
        name="mod",
    )(cond, w_mod, b_mod)


def _modnorm(x, mod_ref, npw_ref):
    ms = jnp.mean(x * x, axis=-1, keepdims=True)
    xn = x * lax.rsqrt(ms + EPS) * npw_ref[...]
    shift = mod_ref[0, :, 0:D_MODEL]
    scale = mod_ref[0, :, D_MODEL:2 * D_MODEL]
    return (xn * (1.0 + scale) + shift).astype(BF16)


def _in_proj_kernel(*refs, tl, n_tiles, rope):
    if rope:
        (x_ref, xp_ref, xn_ref, mod_ref, npw_ref, w_ref, cw_ref, cb_ref, dtb_ref,
         cos_ref, sin_ref, zs_ref, xbc_ref, dt_ref, q_ref, k_ref, v_ref, gs_ref, h_ref) = refs
    else:
        (x_ref, xp_ref, xn_ref, mod_ref, npw_ref, w_ref, cw_ref, cb_ref, dtb_ref,
         zs_ref, xbc_ref, dt_ref, q_ref, k_ref, v_ref, gs_ref, h_ref) = refs
    t = pl.program_id(1)
    h_ref[...] = _modnorm(x_ref[0], mod_ref, npw_ref)
    piece = 256

    if n_tiles > 1:
        halo = jnp.concatenate([xp_ref[0], xn_ref[0]], axis=0)
        hh = _modnorm(halo, mod_ref, npw_ref)
        pv = jnp.where(t > 0, 1.0, 0.0).astype(F32)
        nv = jnp.where(t < n_tiles - 1, 1.0, 0.0).astype(F32)
    row = lax.broadcasted_iota(jnp.int32, (tl, piece), 0)

    def sec_xbc(c0):
        wcols = w_ref[:, OFF_XBC + c0:OFF_XBC + c0 + piece]
        acc = _dot(h_ref[...], wcols)
        up = pltpu.roll(acc, 1, axis=0)
        dn = pltpu.roll(acc, tl - 1, axis=0)
        if n_tiles > 1:
            uh = _dot(hh, wcols)
            up = jnp.where(row == 0, uh[7:8, :] * pv, up)
            dn = jnp.where(row == tl - 1, uh[8:9, :] * nv, dn)
        else:
            up = jnp.where(row == 0, 0.0, up)
            dn = jnp.where(row == tl - 1, 0.0, dn)
        y = (cw_ref[0:1, c0:c0 + piece] * up + cw_ref[1:2, c0:c0 + piece] * acc
             + cw_ref[2:3, c0:c0 + piece] * dn + cb_ref[:, c0:c0 + piece])
        xbc_ref[0, :, c0:c0 + piece] = _silu(y).astype(BF16)

    lane = lax.broadcasted_iota(jnp.int32, (tl, LANES), 1)
    first_half = (lane % 32) < 16

    def sec_qk(off, ref, scl, c0):
        acc = _dot(h_ref[...], w_ref[:, off + c0:off + c0 + piece]) * scl
        if rope:
            for s0 in range(0, piece, LANES):
                xs = acc[:, s0:s0 + LANES]
                partner = jnp.where(first_half, pltpu.roll(xs, LANES - 16, axis=1),
                                    pltpu.roll(xs, 16, axis=1))
                ref[0, :, c0 + s0:c0 + s0 + LANES] = (
                    xs * cos_ref[...] + partner * sin_ref[...]).astype(BF16)
        else:
            ref[0, :, c0:c0 + piece] = acc.astype(BF16)

    def sec_q(c0):
        sec_qk(OFF_Q, q_ref, 1.0, c0)

    def sec_k(c0):
        sec_qk(OFF_K, k_ref, RET_QK_DIM ** -0.5, c0)

    def sec_z(c0):
        acc = _dot(h_ref[...], w_ref[:, OFF_Z + c0:OFF_Z + c0 + piece])
        zs_ref[0, :, c0:c0 + piece] = _silu(acc).astype(BF16)

    def sec_g(c0):
        acc = _dot(h_ref[...], w_ref[:, OFF_G + c0:OFF_G + c0 + piece])
        gs_ref[0, :, c0:c0 + piece] = _silu(acc).astype(BF16)

    def sec_v(c0):
        acc = _dot(h_ref[...], w_ref[:, OFF_V + c0:OFF_V + c0 + piece])
        v_ref[0, :, c0:c0 + piece] = acc.astype(BF16)

    def sec_dt(c0):
        acc = _dot(h_ref[...], w_ref[:, OFF_DT:OFF_DT + DT_PAD]) + dtb_ref[...]
        dt_ref[0] = jnp.maximum(acc, 0.0) + jnp.log(1.0 + jnp.exp(-jnp.abs(acc)))

    def pieces(fn, width):
        return [(fn, c0) for c0 in range(0, width, piece)]

    heavy = (pieces(sec_z, SSD_WIDTH) + pieces(sec_g, RET_V_WIDTH) + pieces(sec_xbc, CONV_CH)
             + pieces(sec_q, RET_QK_WIDTH) + pieces(sec_k, RET_QK_WIDTH))
    light = pieces(sec_v, RET_V_WIDTH) + [(sec_dt, 0)]
    stride = len(heavy) // len(light)
    order = []
    for i, item in enumerate(heavy):
        order.append(item)
        if i % stride == stride - 1 and light:
            order.append(light.pop(0))
    order += light
    for fn, c0 in order:
        fn(c0)


def _in_proj_call(x, mod, npw, w_packed, conv_w, conv_b, dtb, cos_t, sin_t):
    nb, L, _ = x.shape
    tl = min(L, 512)
    n_tiles = L // tl
    g8 = tl // 8
    rope = cos_t is not None
    per_seq_mod = mod.shape[0] > 1
    mod_map = (lambda s, t: (s, 0, 0)) if per_seq_mod else (lambda s, t: (0, 0, 0))
    const2 = lambda s, t: (0, 0)
    in_specs = [
        pl.BlockSpec((1, tl, D_MODEL), lambda s, t: (s, t, 0)),
        pl.BlockSpec((1, 8, D_MODEL), lambda s, t: (s, jnp.maximum(t * g8 - 1, 0), 0)),
        pl.BlockSpec((1, 8, D_MODEL), lambda s, t: (s, jnp.minimum((t + 1) * g8, L // 8 - 1), 0)),
        pl.BlockSpec((1, 1, 3 * D_MODEL), mod_map),
        pl.BlockSpec((1, D_MODEL), const2),
        pl.BlockSpec((D_MODEL, IN_COLS_PACKED), const2, pipeline_mode=pl.Buffered(1)),
        pl.BlockSpec((3, CONV_CH), const2),
        pl.BlockSpec((1, CONV_CH), const2),
        pl.BlockSpec((1, DT_PAD), const2),
    ]
    args = [x, x, x, mod, npw, w_packed, conv_w, conv_b, dtb]
    if rope:
        in_specs += [pl.BlockSpec((tl, LANES), lambda s, t: (t, 0)),
                     pl.BlockSpec((tl, LANES), lambda s, t: (t, 0))]
        args += [cos_t, sin_t]
    widths = (SSD_WIDTH, CONV_CH, DT_PAD, RET_QK_WIDTH, RET_QK_WIDTH, RET_V_WIDTH, RET_V_WIDTH)
    dtypes = (BF16, BF16, F32, BF16, BF16, BF16, BF16)
    out_specs = [pl.BlockSpec((1, tl, w), lambda s, t: (s, t, 0)) for w in widths]
    out_shape = [jax.ShapeDtypeStruct((nb, L, w), d) for w, d in zip(widths, dtypes)]
    return pl.pallas_call(
        functools.partial(_in_proj_kernel, tl=tl, n_tiles=n_tiles, rope=rope),
        grid=(nb, n_tiles),
        in_specs=in_specs,
        out_specs=out_specs,
        out_shape=out_shape,
        scratch_shapes=[pltpu.VMEM((tl, D_MODEL), BF16)],
        compiler_params=pltpu.CompilerParams(
            dimension_semantics=("parallel", "parallel"), vmem_limit_bytes=VMEM_LIMIT),
        name="in_proj_rope" if rope else "in_proj",
    )(*args)


def _mixer_kernel(*refs, nc, has_s0, emit_state):
    Q = CHUNK
    H = SSD_HEADS
    it = iter(refs)
    zs_ref, xbc_ref, dt_ref, q_ref, k_ref, v_ref, gs_ref = (next(it) for _ in range(7))
    nega_ref, dsk_ref, lam_ref = (next(it) for _ in range(3))
    if has_s0:
        s0s_ref, s0r_ref = next(it), next(it)
    mix_ref = next(it)
    if emit_state:
        os_ref, or_ref = next(it), next(it)
    (yas_ref, yar_ref, ss_ref, sr_ref, dsum_ref, ef_ref, eb_ref, tf_ref, tb_ref,
     ar_ref, exf_ref, exb_ref, cum_ref) = (next(it) for _ in range(13))

    b = pl.program_id(0)
    s = pl.program_id(1)
    phase2 = s >= nc
    c = jnp.where(phase2, 2 * nc - 1 - s, s)
    r0 = pl.multiple_of(c * Q, Q)

    rowq = lax.broadcasted_iota(jnp.int32, (Q, Q), 0)
    colq = lax.broadcasted_iota(jnp.int32, (Q, Q), 1)

    @pl.when((b == 0) & (s == 0))
    def _init_tables():
        diff = (rowq - colq).astype(F32)
        rowf = lax.broadcasted_iota(jnp.int32, (Q, LANES), 0).astype(F32)
        for h in range(RET_HEADS):
            pr, hh = divmod(h, 2)
            lf = lam_ref[h:h + 1, :]
            lb = lam_ref[RET_HEADS + h:RET_HEADS + h + 1, :]
            e = jnp.where(rowq >= colq, lf * diff, lb * (-diff))
            dsum_ref[pr, :, hh * Q:(hh + 1) * Q] = jnp.exp(e) * jnp.where(rowq == colq, 2.0, 1.0)
            hc = slice(hh * LANES, (hh + 1) * LANES)
            ef_ref[pr, :, hc] = jnp.exp(lf * (rowf + 1.0))
            eb_ref[pr, :, hc] = jnp.exp(lb * (Q - rowf))
            tf_ref[pr, :, hc] = jnp.exp(lf * (Q - 1.0 - rowf))
            tb_ref[pr, :, hc] = jnp.exp(lb * rowf)
        ar_ref[...] = jnp.exp(lam_ref[...] * float(Q))
        er = lax.broadcasted_iota(jnp.int32, (LANES, SSD_WIDTH), 0)
        ec = lax.broadcasted_iota(jnp.int32, (LANES, SSD_WIDTH), 1) // SSD_HEAD_DIM
        exf_ref[...] = jnp.where(er == ec, 1.0, 0.0).astype(BF16)
        exb_ref[...] = jnp.where(er == ec + H, 1.0, 0.0).astype(BF16)

    @pl.when(s == 0)
    def _init_state():
        if has_s0:
            for d in range(2):
                for p in range(SSD_HEADS // 2):
                    pair_t = jnp.concatenate(
                        [s0s_ref[0, 0, d, 2 * p], s0s_ref[0, 0, d, 2 * p + 1]], axis=0)
                    ss_ref[d, :, p * LANES:(p + 1) * LANES] = pair_t.T
            sr_ref[...] = s0r_ref[0, 0]
        else:
            ss_ref[...] = jnp.zeros_like(ss_ref)
            sr_ref[...] = jnp.zeros_like(sr_ref)

    def expand(w, a_row, ex_ref):
        a3 = jnp.concatenate(_split3(jnp.broadcast_to(a_row, (16, LANES))), axis=0)
        out = _dot(jnp.concatenate([w.astype(BF16), a3], axis=0), ex_ref[...])
        n = w.shape[0]
        return out[0:n], out[n:n + 1] + out[n + 16:n + 17] + out[n + 32:n + 33]

    def ssd_state_update(d, wtail_x, a_x):
        for g in range(SSD_GROUPS):
            bg = xbc_ref[0, :, SSD_WIDTH + g * SSD_STATE:SSD_WIDTH + (g + 1) * SSD_STATE]
            bgt = bg.astype(F32).T.astype(BF16)
            gc = slice(g * GROUP_W, (g + 1) * GROUP_W)
            xt = xbc_ref[0, :, gc] * wtail_x[:, gc].astype(BF16)
            ss_ref[d, :, gc] = ss_ref[d, :, gc] * a_x[:, gc] + _dot(bgt, xt)

    def ret_state_update(d, tail_ref):
        for pr in range(RET_HEADS // 2):
            kt = k_ref[0, :, pr * LANES:(pr + 1) * LANES].astype(F32).T.astype(BF16)
            pc = slice(pr * PAIR_W, (pr + 1) * PAIR_W)
            vt = (v_ref[0, :, pc].astype(F32) * tail_ref[pr]).astype(BF16)
            ds = _dot(kt, vt)
            for hh in range(2):
                h = 2 * pr + hh
                a = ar_ref[d * RET_HEADS + h:d * RET_HEADS + h + 1, :]
                sr_ref[d, h] = (sr_ref[d, h] * a
                                + ds[hh * RET_QK_DIM:(hh + 1) * RET_QK_DIM, hh * LANES:(hh + 1) * LANES])

    def emit_ssd_state(d):
        for p in range(SSD_HEADS // 2):
            pair_t = ss_ref[d, :, p * LANES:(p + 1) * LANES].T
            os_ref[0, 0, d, 2 * p] = pair_t[0:SSD_HEAD_DIM]
            os_ref[0, 0, d, 2 * p + 1] = pair_t[SSD_HEAD_DIM:2 * SSD_HEAD_DIM]

    def ret_state_blockdiag(d, pr):
        z = jnp.zeros((RET_QK_DIM, RET_V_DIM), F32)
        top = jnp.concatenate([sr_ref[d, 2 * pr], z], axis=1)
        bot = jnp.concatenate([z, sr_ref[d, 2 * pr + 1]], axis=1)
        return jnp.concatenate([top, bot], axis=0).astype(BF16)

    @pl.when(jnp.logical_not(phase2))
    def _phase1():
        dt = dt_ref[0]
        la = dt * nega_ref[...]
        tl_bf = jnp.where(rowq >= colq, 1.0, 0.0).astype(BF16)
        c3 = _dot(tl_bf, jnp.concatenate(_split3(la), axis=1))
        cum = c3[:, 0:LANES] + c3[:, LANES:2 * LANES] + c3[:, 2 * LANES:3 * LANES]
        cum_ref[pl.ds(r0, Q), :] = cum
        tot = cum[Q - 1:Q, :]
        rev = tot - cum + la

        sub_k = lax.broadcasted_iota(jnp.int32, (LANES, Q), 0) < RET_QK_DIM
        lane_v = lax.broadcasted_iota(jnp.int32, (Q, PAIR_W), 1) < RET_V_DIM
        for pr in range(RET_HEADS // 2):
            qp = q_ref[0, :, pr * LANES:(pr + 1) * LANES]
            kt = k_ref[0, :, pr * LANES:(pr + 1) * LANES].astype(F32).T.astype(BF16)
            zk = jnp.zeros_like(kt)
            kbd = jnp.concatenate([jnp.where(sub_k, kt, zk), jnp.where(sub_k, zk, kt)], axis=1)
            a2 = _dot(qp, kbd)
            ad = (a2 * dsum_ref[pr]).astype(BF16)
            pc = slice(pr * PAIR_W, (pr + 1) * PAIR_W)
            vp = v_ref[0, :, pc]
            zv = jnp.zeros_like(vp)
            vbd = jnp.concatenate([jnp.where(lane_v, vp, zv), jnp.where(lane_v, zv, vp)], axis=0)
            y = _dot(ad, vbd) + _dot(qp, ret_state_blockdiag(0, pr)) * ef_ref[pr]
            yar_ref[pl.ds(r0, Q), pc] = y
        ret_state_update(0, tf_ref)

        gms, css = [], []
        for g in range(SSD_GROUPS):
            cg = xbc_ref[0, :, SSD_WIDTH + 512 + g * SSD_STATE:SSD_WIDTH + 512 + (g + 1) * SSD_STATE]
            bg = xbc_ref[0, :, SSD_WIDTH + g * SSD_STATE:SSD_WIDTH + (g + 1) * SSD_STATE]
            gms.append(_dot_nt(cg, bg))
            css.append(_dot(cg, ss_ref[0, :, g * GROUP_W:(g + 1) * GROUP_W].astype(BF16)))

        dt_t = dt.T
        ldt_t = jnp.log(dt_t)
        sub = lax.broadcasted_iota(jnp.int32, (LANES, Q), 0)
        adj_t = (jnp.where(sub < H, cum.T, rev.T) - ldt_t) * LOG2E
        dg_t = jnp.log(dt_t[0:H, :] + dt_t[H:2 * H, :]) * LOG2E
        cum2 = cum * LOG2E
        rev2 = rev * LOG2E
        wtail_f = dt * jnp.exp(tot - cum)
        ew_x, af_x = expand(jnp.concatenate([jnp.exp(cum), wtail_f], axis=0), jnp.exp(tot), exf_ref)
        ecum_x = ew_x[0:Q]
        wtail_x = ew_x[Q:2 * Q]
        lt = rowq > colq
        gt = rowq < colq
        lane_g = lax.broadcasted_iota(jnp.int32, (Q, GROUP_W), 1) // SSD_HEAD_DIM

        for g in range(SSD_GROUPS):
            gc = slice(g * GROUP_W, (g + 1) * GROUP_W)
            xg = xbc_ref[0, :, gc]
            ws, xs = [], []
            for j in range(4):
                h = 4 * g + j
                hb = H + h
                arg = jnp.where(lt, cum2[:, h:h + 1] - adj_t[h:h + 1, :],
                                jnp.where(gt, rev2[:, hb:hb + 1] - adj_t[hb:hb + 1, :], dg_t[h:h + 1, :]))
                ws.append((gms[g] * jnp.exp2(arg)).astype(BF16))
                xs.append(jnp.where(lane_g == j, xg, jnp.zeros_like(xg)))
            y = _dot(jnp.concatenate(ws, axis=1), jnp.concatenate(xs, axis=0))
            yas_ref[pl.ds(r0, Q), gc] = y + css[g] * ecum_x[:, gc]
        ssd_state_update(0, wtail_x, af_x)

        if emit_state:
            @pl.when(s == nc - 1)
            def _emit_fwd():
                emit_ssd_state(0)
                or_ref[0, 0, 0] = sr_ref[0]

    @pl.when(phase2)
    def _phase2():
        for pr in range(RET_HEADS // 2):
            qp = q_ref[0, :, pr * LANES:(pr + 1) * LANES]
            pc = slice(pr * PAIR_W, (pr + 1) * PAIR_W)
            y2 = yar_ref[pl.ds(r0, Q), pc] + _dot(qp, ret_state_blockdiag(1, pr)) * eb_ref[pr]
            for hh in range(2):
                h = 2 * pr + hh
                hc = slice(h * LANES, (h + 1) * LANES)
                y = y2[:, hh * LANES:(hh + 1) * LANES]
                mu = jnp.mean(y, axis=-1, keepdims=True)
                yc = y - mu
                var = jnp.mean(yc * yc, axis=-1, keepdims=True)
                yn = yc * lax.rsqrt(var + EPS) * gs_ref[0, :, hc].astype(F32)
                mix_ref[0, :, SSD_WIDTH + h * LANES:SSD_WIDTH + (h + 1) * LANES] = yn.astype(BF16)
        ret_state_update(1, tb_ref)

        dt = dt_ref[0]
        cum = cum_ref[pl.ds(r0, Q), :]
        rev = cum[Q - 1:Q, :] - cum + dt * nega_ref[...]
        first = rev[0:1, :]
        wtail_b = dt * jnp.exp(first - rev)
        ew_x, ab_x = expand(jnp.concatenate([jnp.exp(rev), wtail_b], axis=0), jnp.exp(first), exb_ref)
        erev_x = ew_x[0:Q]
        wtail_x = ew_x[Q:2 * Q]

        parts = []
        ssq = None
        for g in range(SSD_GROUPS):
            cg = xbc_ref[0, :, SSD_WIDTH + 512 + g * SSD_STATE:SSD_WIDTH + 512 + (g + 1) * SSD_STATE]
            gc = slice(g * GROUP_W, (g + 1) * GROUP_W)
            cs = _dot(cg, ss_ref[1, :, gc].astype(BF16))
            y = (yas_ref[pl.ds(r0, Q), gc] + cs * erev_x[:, gc]
                 + dsk_ref[:, gc] * xbc_ref[0, :, gc].astype(F32))
            y = y * zs_ref[0, :, gc].astype(F32)
            parts.append(y)
            ssq = y * y if ssq is None else ssq + y * y
        inv = lax.rsqrt(jnp.sum(ssq, axis=-1, keepdims=True) * (1.0 / SSD_WIDTH) + EPS)
        for g, y in enumerate(parts):
            gc = slice(g * GROUP_W, (g + 1) * GROUP_W)
            mix_ref[0, :, gc] = (y * inv).astype(BF16)
        ssd_state_update(1, wtail_x, ab_x)

        if emit_state:
            @pl.when(s == 2 * nc - 1)
            def _emit_bwd():
                emit_ssd_state(1)
                or_ref[0, 0, 1] = sr_ref[1]


def _mixer_call(zs, xbc, dt, q, k, v, gs, nega, dsk, lamx, s0_ssd, s0_ret, emit_state):
    nb, L, _ = zs.shape
    nc = L // CHUNK
    Q = CHUNK
    has_s0 = s0_ssd is not None

    def cidx(s):
        return jnp.where(s < nc, s, 2 * nc - 1 - s)

    def chunk_map(b, s):
        return (b, cidx(s), 0)

    def phase2_map(b, s):
        return (b, jnp.where(s < nc, nc - 1, 2 * nc - 1 - s), 0)

    const2 = lambda b, s: (0, 0)
    in_specs = [
        pl.BlockSpec((1, Q, SSD_WIDTH), phase2_map),
        pl.BlockSpec((1, Q, CONV_CH), chunk_map),
        pl.BlockSpec((1, Q, DT_PAD), chunk_map),
        pl.BlockSpec((1, Q, RET_QK_WIDTH), chunk_map),
        pl.BlockSpec((1, Q, RET_QK_WIDTH), chunk_map),
        pl.BlockSpec((1, Q, RET_V_WIDTH), chunk_map),
        pl.BlockSpec((1, Q, RET_V_WIDTH), phase2_map),
        pl.BlockSpec((1, DT_PAD), const2),
        pl.BlockSpec((1, SSD_WIDTH), const2),
        pl.BlockSpec((2 * RET_HEADS, LANES), const2),
    ]
    args = [zs, xbc, dt, q, k, v, gs, nega, dsk, lamx]
    state_map = lambda b, s: (b, 0, 0, 0, 0, 0)
    ssd_state_block = (1, 1, 2, SSD_HEADS, SSD_HEAD_DIM, SSD_STATE)
    ret_state_block = (1, 1, 2, RET_HEADS, RET_QK_DIM, RET_V_DIM)
    if has_s0:
        in_specs += [pl.BlockSpec(ssd_state_block, state_map), pl.BlockSpec(ret_state_block, state_map)]
        args += [s0_ssd, s0_ret]
    out_specs = [pl.BlockSpec((1, Q, MIX_WIDTH), phase2_map)]
    out_shape = [jax.ShapeDtypeStruct((nb, L, MIX_WIDTH), BF16)]
    if emit_state:
        out_specs += [pl.BlockSpec(ssd_state_block, state_map), pl.BlockSpec(ret_state_block, state_map)]
        out_shape += [jax.ShapeDtypeStruct((nb,) + ssd_state_block[1:], F32),
                      jax.ShapeDtypeStruct((nb,) + ret_state_block[1:], F32)]
    n_pairs = RET_HEADS // 2
    scratch = [
        pltpu.VMEM((L, SSD_WIDTH), F32),
        pltpu.VMEM((L, RET_V_WIDTH), F32),
        pltpu.VMEM((2, SSD_STATE, SSD_WIDTH), F32),
        pltpu.VMEM((2, RET_HEADS, RET_QK_DIM, RET_V_DIM), F32),
        pltpu.VMEM((n_pairs, Q, 2 * Q), F32),
        pltpu.VMEM((n_pairs, Q, PAIR_W), F32),
        pltpu.VMEM((n_pairs, Q, PAIR_W), F32),
        pltpu.VMEM((n_pairs, Q, PAIR_W), F32),
        pltpu.VMEM((n_pairs, Q, PAIR_W), F32),
        pltpu.VMEM((2 * RET_HEADS, LANES), F32),
        pltpu.VMEM((LANES, SSD_WIDTH), BF16),
        pltpu.VMEM((LANES, SSD_WIDTH), BF16),
        pltpu.VMEM((L, DT_PAD), F32),
    ]
    outs = pl.pallas_call(
        functools.partial(_mixer_kernel, nc=nc, has_s0=has_s0, emit_state=emit_state),
        grid=(nb, 2 * nc),
        in_specs=in_specs,
        out_specs=out_specs,
        out_shape=out_shape,
        scratch_shapes=scratch,
        compiler_params=pltpu.CompilerParams(
            dimension_semantics=("arbitrary", "arbitrary"), vmem_limit_bytes=VMEM_LIMIT),
        name="mixer_sample" if has_s0 else "mixer_prompt",
    )(*args)
    return outs


def _out_proj_kernel(mix_ref, w_ref, x_ref, mod_ref, npw_ref, y_ref):
    out = _dot(mix_ref[0], w_ref[...])
    ms = jnp.mean(out * out, axis=-1, keepdims=True)
    o = out * lax.rsqrt(ms + EPS) * npw_ref[...]
    gate = mod_ref[0, :, 2 * D_MODEL:3 * D_MODEL]
    y_ref[0] = x_ref[0] + gate * o


def _out_proj_call(mix, w_out_bf, x, mod, npw):
    nb, L, _ = x.shape
    tl = min(L, 512)
    per_seq_mod = mod.shape[0] > 1
    mod_map = (lambda s, t: (s, 0, 0)) if per_seq_mod else (lambda s, t: (0, 0, 0))
    return pl.pallas_call(
        _out_proj_kernel,
        grid=(nb, L // tl),
        in_specs=[
            pl.BlockSpec((1, tl, MIX_WIDTH), lambda s, t: (s, t, 0)),
            pl.BlockSpec((MIX_WIDTH, D_MODEL), lambda s, t: (0, 0)),
            pl.BlockSpec((1, tl, D_MODEL), lambda s, t: (s, t, 0)),
            pl.BlockSpec((1, 1, 3 * D_MODEL), mod_map),
            pl.BlockSpec((1, D_MODEL), lambda s, t: (0, 0)),
        ],
        out_specs=pl.BlockSpec((1, tl, D_MODEL), lambda s, t: (s, t, 0)),
        out_shape=jax.ShapeDtypeStruct((nb, L, D_MODEL), F32),
        compiler_params=pltpu.CompilerParams(
            dimension_semantics=("parallel", "parallel"), vmem_limit_bytes=VMEM_LIMIT),
        name="out_proj",
    )(mix, w_out_bf, x, mod, npw)


def _rope_tables(L):
    rows = L // GRID_W
    row = jnp.repeat(jnp.arange(rows, dtype=F32), GRID_W)
    col = jnp.tile(jnp.arange(GRID_W, dtype=F32), rows)
    half = RET_QK_DIM // 2
    inv = ROPE_BASE ** (-jnp.arange(0, half, 2, dtype=F32) / half)
    ang_r = row[:, None] * inv
    ang_c = col[:, None] * inv
    cos_h = jnp.concatenate([jnp.cos(ang_r), jnp.cos(ang_r), jnp.cos(ang_c), jnp.cos(ang_c)], axis=-1)
    sin_h = jnp.concatenate([-jnp.sin(ang_r), jnp.sin(ang_r), -jnp.sin(ang_c), jnp.sin(ang_c)], axis=-1)
    return jnp.tile(cos_h, (1, 2)), jnp.tile(sin_h, (1, 2))


def kernel(x_prompt, x_sample, state_ssd, state_ret, c, c_ctx, w_mod, b_mod, norm_pre_w, norm_post_w,
           w_in, conv_w, conv_b, ssd_A_log, ssd_dt_bias, ssd_D, ssd_norm_w, ret_decay, ret_norm_w, w_out):
    nb_p = x_prompt.shape[0]
    nb_s = x_sample.shape[0]
    l = 0

    n_dt = 2 * SSD_HEADS
    w_packed = _pack_call(jnp.swapaxes(w_in[l], 0, 1))
    mix_norm_w = jnp.concatenate([ssd_norm_w[l], ret_norm_w[l]]).reshape(MIX_WIDTH, 1)
    w_out_bf = _scale_rows_call(w_out[l], mix_norm_w)
    dtb = jnp.pad(ssd_dt_bias[l].reshape(1, n_dt), ((0, 0), (0, DT_PAD - n_dt)))
    nega = jnp.pad(-jnp.exp(ssd_A_log[l].reshape(1, n_dt)), ((0, 0), (0, DT_PAD - n_dt)))
    dsk = jnp.repeat(ssd_D[l], SSD_HEAD_DIM).reshape(1, SSD_WIDTH)
    lamx = jnp.broadcast_to(-jnp.exp(ret_decay[l].reshape(2 * RET_HEADS, 1)), (2 * RET_HEADS, LANES))
    npre = norm_pre_w[l].reshape(1, D_MODEL)
    npost = norm_post_w[l].reshape(1, D_MODEL)
    cb = conv_b[l].reshape(1, CONV_CH)
    cos_t, sin_t = _rope_tables(x_sample.shape[1])

    cond = jnp.concatenate([c, c_ctx[None, :], jnp.zeros((16 - nb_s - 1, D_MODEL), F32)], axis=0)
    mod = _mod_call(cond, w_mod[l], b_mod[l].reshape(1, 3 * D_MODEL))
    mod_s = mod[:nb_s].reshape(nb_s, 1, 3 * D_MODEL)
    mod_p = mod[nb_s:nb_s + 1].reshape(1, 1, 3 * D_MODEL)

    pp = _in_proj_call(x_prompt, mod_p, npre, w_packed, conv_w[l], cb, dtb, None, None)
    mix_p, st_ssd_t, st_ret = _mixer_call(*pp, nega, dsk, lamx, None, None, True)
    y_p = _out_proj_call(mix_p, w_out_bf, x_prompt, mod_p, npost)

    ps = _in_proj_call(x_sample, mod_s, npre, w_packed, conv_w[l], cb, dtb, cos_t, sin_t)
    (mix_s,) = _mixer_call(*ps, nega, dsk, lamx, jnp.swapaxes(state_ssd, -1, -2), state_ret, False)
    y_s = _out_proj_call(mix_s, w_out_bf, x_sample, mod_s, npost)
    return (y_p, y_s, jnp.swapaxes(st_ssd_t, -1, -2), st_ret)
```

```python
import functools

import jax
import jax.numpy as jnp
from jax import lax
from jax.experimental import pallas as pl
from jax.experimental.pallas import tpu as pltpu

F32 = jnp.float32
BF16 = jnp.bfloat16

D_MODEL = 1024
CHUNK = 128
GRID_W = 64
EPS = 1e-6
SSD_WIDTH = 1024
SSD_HEAD_DIM = 64
SSD_HEADS = 16
SSD_GROUPS = 4
SSD_STATE = 128
CONV_CH = SSD_WIDTH + 2 * SSD_GROUPS * SSD_STATE
RET_HEADS = 8
RET_QK_DIM = 64
RET_V_DIM = 128
RET_QK_WIDTH = RET_HEADS * RET_QK_DIM
RET_V_WIDTH = RET_HEADS * RET_V_DIM
MIX_WIDTH = SSD_WIDTH + RET_V_WIDTH
ROPE_BASE = 10000.0
LANES = 128
DT_PAD = LANES
GROUP_W = SSD_WIDTH // SSD_GROUPS
PAIR_W = 2 * RET_V_DIM

OFF_Z = 0
OFF_XBC = OFF_Z + SSD_WIDTH
OFF_Q = OFF_XBC + CONV_CH
OFF_K = OFF_Q + RET_QK_WIDTH
OFF_V = OFF_K + RET_QK_WIDTH
OFF_G = OFF_V + RET_V_WIDTH
OFF_DT = OFF_G + RET_V_WIDTH
IN_COLS_PACKED = OFF_DT + DT_PAD

LOG2E = 1.4426950408889634
VMEM_LIMIT = 56 * 1024 * 1024


def _silu(x):
    return x * (1.0 / (1.0 + jnp.exp(-x)))


def _dot(a, b):
    return jnp.dot(a, b, preferred_element_type=F32)


def _dot_nt(a, b):
    return lax.dot_general(a, b, (((1,), (1,)), ((), ())), preferred_element_type=F32)


def _split3(x):
    hi = x.astype(BF16)
    r1 = x - hi.astype(F32)
    mid = r1.astype(BF16)
    lo = (r1 - mid.astype(F32)).astype(BF16)
    return hi, mid, lo


PACK_COLS = 512
PACK_STEPS = -(-IN_COLS_PACKED // PACK_COLS)


def _pack_kernel(wt_ref, out_ref):
    t = wt_ref[...].T
    lane = lax.broadcasted_iota(jnp.int32, t.shape, 1)
    is_dt = pl.program_id(0) == PACK_STEPS - 1
    t = jnp.where(jnp.logical_and(is_dt, lane >= 2 * SSD_HEADS), 0.0, t)
    out_ref[...] = t.astype(BF16)


def _pack_call(w_in_t):
    n_head = (OFF_Q - OFF_Z) // PACK_COLS
    src_dt = SSD_WIDTH + CONV_CH
    src_tail = src_dt + 2 * SSD_HEADS

    def src_row(i):
        t8 = jnp.where(i < n_head, i * (PACK_COLS // 8),
                       jnp.where(i < PACK_STEPS - 1, src_tail // 8 + (i - n_head) * (PACK_COLS // 8),
                                 src_dt // 8))
        return t8 * 8

    return pl.pallas_call(
        _pack_kernel,
        grid=(PACK_STEPS,),
        in_specs=[pl.BlockSpec((pl.Element(PACK_COLS), pl.Element(D_MODEL)), lambda i: (src_row(i), 0))],
        out_specs=pl.BlockSpec((D_MODEL, PACK_COLS), lambda i: (0, i)),
        out_shape=jax.ShapeDtypeStruct((D_MODEL, IN_COLS_PACKED), BF16),
        name="pack_w_in",
    )(w_in_t)


def _scale_rows_kernel(w_ref, s_ref, out_ref):
    out_ref[...] = (w_ref[...] * s_ref[...]).astype(BF16)


def _scale_rows_call(w, row_scale):
    rows, cols = w.shape
    tr = 512
    return pl.pallas_call(
        _scale_rows_kernel,
        grid=(rows // tr,),
        in_specs=[pl.BlockSpec((tr, cols), lambda i: (i, 0)), pl.BlockSpec((tr, 1), lambda i: (i, 0))],
        out_specs=pl.BlockSpec((tr, cols), lambda i: (i, 0)),
        out_shape=jax.ShapeDtypeStruct((rows, cols), BF16),
        name="fold_norm_w_out",
    )(w, row_scale)


def _mod_kernel(cond_ref, w_ref, b_ref, out_ref):
    a = _silu(cond_ref[...]).astype(BF16)
    out_ref[...] = _dot(a, w_ref[...].astype(BF16)) + b_ref[...]


def _mod_call(cond, w_mod, b_mod):
    rows = cond.shape[0]
    tn = 1024
    return pl.pallas_call(
        _mod_kernel,
        grid=(3 * D_MODEL // tn,),
        in_specs=[
            pl.BlockSpec((rows, D_MODEL), lambda j: (0, 0)),
            pl.BlockSpec((D_MODEL, tn), lambda j: (0, j)),
            pl.BlockSpec((1, tn), lambda j: (0, j)),
        ],
        out_specs=pl.BlockSpec((rows, tn), lambda j: (0, j)),
        out_shape=jax.ShapeDtypeStruct((rows, 3 * D_MODEL), F32),
        name="mod",
    )(cond, w_mod, b_mod)


def _modnorm(x, mod_ref, npw_ref):
    ms = jnp.mean(x * x, axis=-1, keepdims=True)
    xn = x * lax.rsqrt(ms + EPS) * npw_ref[...]
    shift = mod_ref[0, :, 0:D_MODEL]
    scale = mod_ref[0, :, D_MODEL:2 * D_MODEL]
    return (xn * (1.0 + scale) + shift).astype(BF16)


def _in_proj_kernel(*refs, tl, n_tiles, rope):
    if rope:
        (x_ref, xp_ref, xn_ref, mod_ref, npw_ref, w_ref, cw_ref, cb_ref, dtb_ref,
         cos_ref, sin_ref, zs_ref, xbc_ref, dt_ref, q_ref, k_ref, v_ref, gs_ref, h_ref) = refs
    else:
        (x_ref, xp_ref, xn_ref, mod_ref, npw_ref, w_ref, cw_ref, cb_ref, dtb_ref,
         zs_ref, xbc_ref, dt_ref, q_ref, k_ref, v_ref, gs_ref, h_ref) = refs
    t = pl.program_id(1)
    h_ref[...] = _modnorm(x_ref[0], mod_ref, npw_ref)
    piece = 256

    if n_tiles > 1:
        halo = jnp.concatenate([xp_ref[0], xn_ref[0]], axis=0)
        hh = _modnorm(halo, mod_ref, npw_ref)
        pv = jnp.where(t > 0, 1.0, 0.0).astype(F32)
        nv = jnp.where(t < n_tiles - 1, 1.0, 0.0).astype(F32)
    row = lax.broadcasted_iota(jnp.int32, (tl, piece), 0)

    def sec_xbc(c0):
        wcols = w_ref[:, OFF_XBC + c0:OFF_XBC + c0 + piece]
        acc = _dot(h_ref[...], wcols)
        up = pltpu.roll(acc, 1, axis=0)
        dn = pltpu.roll(acc, tl - 1, axis=0)
        if n_tiles > 1:
            uh = _dot(hh, wcols)
            up = jnp.where(row == 0, uh[7:8, :] * pv, up)
            dn = jnp.where(row == tl - 1, uh[8:9, :] * nv, dn)
        else:
            up = jnp.where(row == 0, 0.0, up)
            dn = jnp.where(row == tl - 1, 0.0, dn)
        y = (cw_ref[0:1, c0:c0 + piece] * up + cw_ref[1:2, c0:c0 + piece] * acc
             + cw_ref[2:3, c0:c0 + piece] * dn + cb_ref[:, c0:c0 + piece])
        xbc_ref[0, :, c0:c0 + piece] = _silu(y).astype(BF16)

    lane = lax.broadcasted_iota(jnp.int32, (tl, LANES), 1)
    first_half = (lane % 32) < 16

    def sec_qk(off, ref, scl, c0):
        acc = _dot(h_ref[...], w_ref[:, off + c0:off + c0 + piece]) * scl
        if rope:
            for s0 in range(0, piece, LANES):
                xs = acc[:, s0:s0 + LANES]
                partner = jnp.where(first_half, pltpu.roll(xs, LANES - 16, axis=1),
                                    pltpu.roll(xs, 16, axis=1))
                ref[0, :, c0 + s0:c0 + s0 + LANES] = (
                    xs * cos_ref[...] + partner * sin_ref[...]).astype(BF16)
        else:
            ref[0, :, c0:c0 + piece] = acc.astype(BF16)

    def sec_q(c0):
        sec_qk(OFF_Q, q_ref, 1.0, c0)

    def sec_k(c0):
        sec_qk(OFF_K, k_ref, RET_QK_DIM ** -0.5, c0)

    def sec_z(c0):
        acc = _dot(h_ref[...], w_ref[:, OFF_Z + c0:OFF_Z + c0 + piece])
        zs_ref[0, :, c0:c0 + piece] = _silu(acc).astype(BF16)

    def sec_g(c0):
        acc = _dot(h_ref[...], w_ref[:, OFF_G + c0:OFF_G + c0 + piece])
        gs_ref[0, :, c0:c0 + piece] = _silu(acc).astype(BF16)

    def sec_v(c0):
        acc = _dot(h_ref[...], w_ref[:, OFF_V + c0:OFF_V + c0 + piece])
        v_ref[0, :, c0:c0 + piece] = acc.astype(BF16)

    def sec_dt(c0):
        acc = _dot(h_ref[...], w_ref[:, OFF_DT:OFF_DT + DT_PAD]) + dtb_ref[...]
        dt_ref[0] = jnp.maximum(acc, 0.0) + jnp.log(1.0 + jnp.exp(-jnp.abs(acc)))

    def pieces(fn, width):
        return [(fn, c0) for c0 in range(0, width, piece)]

    heavy = (pieces(sec_z, SSD_WIDTH) + pieces(sec_g, RET_V_WIDTH) + pieces(sec_xbc, CONV_CH)
             + pieces(sec_q, RET_QK_WIDTH) + pieces(sec_k, RET_QK_WIDTH))
    light = pieces(sec_v, RET_V_WIDTH) + [(sec_dt, 0)]
    stride = len(heavy) // len(light)
    order = []
    for i, item in enumerate(heavy):
        order.append(item)
        if i % stride == stride - 1 and light:
            order.append(light.pop(0))
    order += light
    for fn, c0 in order:
        fn(c0)


def _in_proj_call(x, mod, npw, w_packed, conv_w, conv_b, dtb, cos_t, sin_t):
    nb, L, _ = x.shape
    tl = min(L, 512)
    n_tiles = L // tl
    g8 = tl // 8
    rope = cos_t is not None
    per_seq_mod = mod.shape[0] > 1
    mod_map = (lambda s, t: (s, 0, 0)) if per_seq_mod else (lambda s, t: (0, 0, 0))
    const2 = lambda s, t: (0, 0)
    in_specs = [
        pl.BlockSpec((1, tl, D_MODEL), lambda s, t: (s, t, 0)),
        pl.BlockSpec((1, 8, D_MODEL), lambda s, t: (s, jnp.maximum(t * g8 - 1, 0), 0)),
        pl.BlockSpec((1, 8, D_MODEL), lambda s, t: (s, jnp.minimum((t + 1) * g8, L // 8 - 1), 0)),
        pl.BlockSpec((1, 1, 3 * D_MODEL), mod_map),
        pl.BlockSpec((1, D_MODEL), const2),
        pl.BlockSpec((D_MODEL, IN_COLS_PACKED), const2, pipeline_mode=pl.Buffered(1)),
        pl.BlockSpec((3, CONV_CH), const2),
        pl.BlockSpec((1, CONV_CH), const2),
        pl.BlockSpec((1, DT_PAD), const2),
    ]
    args = [x, x, x, mod, npw, w_packed, conv_w, conv_b, dtb]
    if rope:
        in_specs += [pl.BlockSpec((tl, LANES), lambda s, t: (t, 0)),
                     pl.BlockSpec((tl, LANES), lambda s, t: (t, 0))]
        args += [cos_t, sin_t]
    widths = (SSD_WIDTH, CONV_CH, DT_PAD, RET_QK_WIDTH, RET_QK_WIDTH, RET_V_WIDTH, RET_V_WIDTH)
    dtypes = (BF16, BF16, F32, BF16, BF16, BF16, BF16)
    out_specs = [pl.BlockSpec((1, tl, w), lambda s, t: (s, t, 0)) for w in widths]
    out_shape = [jax.ShapeDtypeStruct((nb, L, w), d) for w, d in zip(widths, dtypes)]
    return pl.pallas_call(
        functools.partial(_in_proj_kernel, tl=tl, n_tiles=n_tiles, rope=rope),
        grid=(nb, n_tiles),
        in_specs=in_specs,
        out_specs=out_specs,
        out_shape=out_shape,
        scratch_shapes=[pltpu.VMEM((tl, D_MODEL), BF16)],
        compiler_params=pltpu.CompilerParams(
            dimension_semantics=("parallel", "parallel"), vmem_limit_bytes=VMEM_LIMIT),
        name="in_proj_rope" if rope else "in_proj",
    )(*args)


def _mixer_kernel(*refs, nb, nc, has_s0, emit_state):
    Q = CHUNK
    H = SSD_HEADS
    it = iter(refs)
    zs_ref, xbc_ref, dt_ref, q_ref, k_ref, v_ref, gs_ref = (next(it) for _ in range(7))
    nega_ref, dsk_ref, lam_ref = (next(it) for _ in range(3))
    wout_ref, xres_ref, modl_ref, npost_ref = (next(it) for _ in range(4))
    if has_s0:
        s0s_ref, s0r_ref = next(it), next(it)
    y_ref = next(it)
    if emit_state:
        os_ref, or_ref = next(it), next(it)
    (yas_ref, yar_ref, ss_ref, sr_ref, dsum_ref, ef_ref, eb_ref, tf_ref, tb_ref,
     ar_ref, exf_ref, exb_ref, cum_ref, mixs_ref) = (next(it) for _ in range(14))

    i = pl.program_id(0)
    n_real = nb * 2 * nc
    is_real = i < n_real
    ic = jnp.minimum(i, n_real - 1)
    s = ic % (2 * nc)
    phase2 = s >= nc
    c = jnp.where(phase2, 2 * nc - 1 - s, s)
    r0 = pl.multiple_of(c * Q, Q)
    slot_w = 1
    slot_r = 0
    last_slot = slot_r

    rowq = lax.broadcasted_iota(jnp.int32, (Q, Q), 0)
    colq = lax.broadcasted_iota(jnp.int32, (Q, Q), 1)

    def out_proj_matmul(slot):
        return _dot(mixs_ref[slot], wout_ref[...])

    def out_proj_finish(out):
        ms = jnp.mean(out * out, axis=-1, keepdims=True)
        o = out * lax.rsqrt(ms + EPS) * npost_ref[...]
        y_ref[0] = xres_ref[0] + modl_ref[0, :, 2 * D_MODEL:3 * D_MODEL] * o

    @pl.when(i == 0)
    def _init_tables():
        mixs_ref[...] = jnp.zeros_like(mixs_ref)
        diff = (rowq - colq).astype(F32)
        rowf = lax.broadcasted_iota(jnp.int32, (Q, LANES), 0).astype(F32)
        colf = lax.broadcasted_iota(jnp.int32, (RET_QK_DIM, Q), 1).astype(F32)
        for h in range(RET_HEADS):
            pr, hh = divmod(h, 2)
            lf = lam_ref[h:h + 1, :]
            lb = lam_ref[RET_HEADS + h:RET_HEADS + h + 1, :]
            e = jnp.where(rowq >= colq, lf * diff, lb * (-diff))
            dsum_ref[pr, :, hh * Q:(hh + 1) * Q] = jnp.exp(e) * jnp.where(rowq == colq, 2.0, 1.0)
            hc = slice(hh * LANES, (hh + 1) * LANES)
            ef_ref[pr, :, hc] = jnp.exp(lf * (rowf + 1.0))
            eb_ref[pr, :, hc] = jnp.exp(lb * (Q - rowf))
            kr = slice(hh * RET_QK_DIM, (hh + 1) * RET_QK_DIM)
            tf_ref[pr, kr, :] = jnp.exp(lf * (Q - 1.0 - colf))
            tb_ref[pr, kr, :] = jnp.exp(lb * colf)
        ar_ref[...] = jnp.exp(lam_ref[...] * float(Q))
        er = lax.broadcasted_iota(jnp.int32, (LANES, SSD_WIDTH), 0)
        ec = lax.broadcasted_iota(jnp.int32, (LANES, SSD_WIDTH), 1) // SSD_HEAD_DIM
        exf_ref[...] = jnp.where(er == ec, 1.0, 0.0).astype(BF16)
        exb_ref[...] = jnp.where(er == ec + H, 1.0, 0.0).astype(BF16)

    @pl.when(is_real & (s == 0))
    def _init_state():
        if has_s0:
            for d in range(2):
                for p in range(SSD_HEADS // 2):
                    pair_t = jnp.concatenate(
                        [s0s_ref[0, 0, d, 2 * p], s0s_ref[0, 0, d, 2 * p + 1]], axis=0)
                    ss_ref[d, :, p * LANES:(p + 1) * LANES] = pair_t.T
            sr_ref[...] = s0r_ref[0, 0]
        else:
            ss_ref[...] = jnp.zeros_like(ss_ref)
            sr_ref[...] = jnp.zeros_like(sr_ref)

    def expand(w, a_row, ex_ref):
        a3 = jnp.concatenate(_split3(jnp.broadcast_to(a_row, (16, LANES))), axis=0)
        out = _dot(jnp.concatenate([w.astype(BF16), a3], axis=0), ex_ref[...])
        n = w.shape[0]
        return out[0:n], out[n:n + 1] + out[n + 16:n + 17] + out[n + 32:n + 33]

    def ssd_state_update(d, wtail_x, a_x):
        for g in range(SSD_GROUPS):
            bg = xbc_ref[0, :, SSD_WIDTH + g * SSD_STATE:SSD_WIDTH + (g + 1) * SSD_STATE]
            bgt = bg.T
            gc = slice(g * GROUP_W, (g + 1) * GROUP_W)
            xt = xbc_ref[0, :, gc] * wtail_x[:, gc].astype(BF16)
            ss_ref[d, :, gc] = ss_ref[d, :, gc] * a_x[:, gc] + _dot(bgt, xt)

    def ret_state_update(d, tail_ref):
        for pr in range(RET_HEADS // 2):
            kt = k_ref[0, :, pr * LANES:(pr + 1) * LANES].T
            pc = slice(pr * PAIR_W, (pr + 1) * PAIR_W)
            ktt = (kt.astype(F32) * tail_ref[pr]).astype(BF16)
            ds = _dot(ktt, v_ref[0, :, pc])
            for hh in range(2):
                h = 2 * pr + hh
                a = ar_ref[d * RET_HEADS + h:d * RET_HEADS + h + 1, :]
                sr_ref[d, h] = (sr_ref[d, h] * a
                                + ds[hh * RET_QK_DIM:(hh + 1) * RET_QK_DIM, hh * LANES:(hh + 1) * LANES])

    def emit_ssd_state(d):
        for p in range(SSD_HEADS // 2):
            pair_t = ss_ref[d, :, p * LANES:(p + 1) * LANES].T
            os_ref[0, 0, d, 2 * p] = pair_t[0:SSD_HEAD_DIM]
            os_ref[0, 0, d, 2 * p + 1] = pair_t[SSD_HEAD_DIM:2 * SSD_HEAD_DIM]

    def ret_state_blockdiag(d, pr):
        z = jnp.zeros((RET_QK_DIM, RET_V_DIM), F32)
        top = jnp.concatenate([sr_ref[d, 2 * pr], z], axis=1)
        bot = jnp.concatenate([z, sr_ref[d, 2 * pr + 1]], axis=1)
        return jnp.concatenate([top, bot], axis=0).astype(BF16)

    def phase1(with_out_proj):
        dt = dt_ref[0]
        la = dt * nega_ref[...]
        tl_bf = jnp.where(rowq >= colq, 1.0, 0.0).astype(BF16)
        c3 = _dot(tl_bf, jnp.concatenate(_split3(la), axis=1))
        cum = c3[:, 0:LANES] + c3[:, LANES:2 * LANES] + c3[:, 2 * LANES:3 * LANES]
        cum_ref[pl.ds(r0, Q), :] = cum
        tot = cum[Q - 1:Q, :]
        rev = tot - cum + la

        sub_k = lax.broadcasted_iota(jnp.int32, (LANES, Q), 0) < RET_QK_DIM
        lane_v = lax.broadcasted_iota(jnp.int32, (Q, PAIR_W), 1) < RET_V_DIM
        for pr in range(RET_HEADS // 2):
            qp = q_ref[0, :, pr * LANES:(pr + 1) * LANES]
            kt = k_ref[0, :, pr * LANES:(pr + 1) * LANES].T
            zk = jnp.zeros_like(kt)
            kbd = jnp.concatenate([jnp.where(sub_k, kt, zk), jnp.where(sub_k, zk, kt)], axis=1)
            a2 = _dot(qp, kbd)
            ad = (a2 * dsum_ref[pr]).astype(BF16)
            pc = slice(pr * PAIR_W, (pr + 1) * PAIR_W)
            vp = v_ref[0, :, pc]
            zv = jnp.zeros_like(vp)
            vbd = jnp.concatenate([jnp.where(lane_v, vp, zv), jnp.where(lane_v, zv, vp)], axis=0)
            y = _dot(ad, vbd) + _dot(qp, ret_state_blockdiag(0, pr)) * ef_ref[pr]
            yar_ref[pl.ds(r0, Q), pc] = y
        ret_state_update(0, tf_ref)

        gms, css = [], []
        for g in range(SSD_GROUPS):
            cg = xbc_ref[0, :, SSD_WIDTH + 512 + g * SSD_STATE:SSD_WIDTH + 512 + (g + 1) * SSD_STATE]
            bg = xbc_ref[0, :, SSD_WIDTH + g * SSD_STATE:SSD_WIDTH + (g + 1) * SSD_STATE]
            gms.append(_dot_nt(cg, bg))
            css.append(_dot(cg, ss_ref[0, :, g * GROUP_W:(g + 1) * GROUP_W].astype(BF16)))
        if with_out_proj:
            proj = out_proj_matmul(last_slot)

        dt_t = dt.T
        ldt_t = jnp.log(dt_t)
        sub = lax.broadcasted_iota(jnp.int32, (LANES, Q), 0)
        adj_t = (jnp.where(sub < H, cum.T, rev.T) - ldt_t) * LOG2E
        dg_t = jnp.log(dt_t[0:H, :] + dt_t[H:2 * H, :]) * LOG2E
        cum2 = cum * LOG2E
        rev2 = rev * LOG2E
        wtail_f = dt * jnp.exp(tot - cum)
        ew_x, af_x = expand(jnp.concatenate([jnp.exp(cum), wtail_f], axis=0), jnp.exp(tot), exf_ref)
        ecum_x = ew_x[0:Q]
        wtail_x = ew_x[Q:2 * Q]
        lt = rowq > colq
        gt = rowq < colq
        lane_g = lax.broadcasted_iota(jnp.int32, (Q, GROUP_W), 1) // SSD_HEAD_DIM

        for g in range(SSD_GROUPS):
            gc = slice(g * GROUP_W, (g + 1) * GROUP_W)
            xg = xbc_ref[0, :, gc]
            ws, xs = [], []
            for j in range(4):
                h = 4 * g + j
                hb = H + h
                arg = jnp.where(lt, cum2[:, h:h + 1] - adj_t[h:h + 1, :],
                                jnp.where(gt, rev2[:, hb:hb + 1] - adj_t[hb:hb + 1, :], dg_t[h:h + 1, :]))
                ws.append((gms[g] * jnp.exp2(arg)).astype(BF16))
                xs.append(jnp.where(lane_g == j, xg, jnp.zeros_like(xg)))
            y = _dot(jnp.concatenate(ws, axis=1), jnp.concatenate(xs, axis=0))
            yas_ref[pl.ds(r0, Q), gc] = y + css[g] * ecum_x[:, gc]
        ssd_state_update(0, wtail_x, af_x)
        if with_out_proj:
            out_proj_finish(proj)

        if emit_state:
            @pl.when(s == nc - 1)
            def _emit_fwd():
                emit_ssd_state(0)
                or_ref[0, 0, 0] = sr_ref[0]

    @pl.when(is_real & (s == 0))
    def _phase1_first():
        phase1(True)

    if nc > 1:
        @pl.when(is_real & (s > 0) & (s < nc))
        def _phase1_rest():
            phase1(False)

    @pl.when(jnp.logical_not(is_real))
    def _drain():
        out_proj_finish(out_proj_matmul(last_slot))

    @pl.when(is_real & phase2)
    def _phase2():
        for pr in range(RET_HEADS // 2):
            qp = q_ref[0, :, pr * LANES:(pr + 1) * LANES]
            pc = slice(pr * PAIR_W, (pr + 1) * PAIR_W)
            y2 = yar_ref[pl.ds(r0, Q), pc] + _dot(qp, ret_state_blockdiag(1, pr)) * eb_ref[pr]
            for hh in range(2):
                h = 2 * pr + hh
                hc = slice(h * LANES, (h + 1) * LANES)
                y = y2[:, hh * LANES:(hh + 1) * LANES]
                mu = jnp.mean(y, axis=-1, keepdims=True)
                yc = y - mu
                var = jnp.mean(yc * yc, axis=-1, keepdims=True)
                yn = yc * lax.rsqrt(var + EPS) * gs_ref[0, :, hc].astype(F32)
                mixs_ref[slot_w, :, SSD_WIDTH + h * LANES:SSD_WIDTH + (h + 1) * LANES] = yn.astype(BF16)
        ret_state_update(1, tb_ref)

        dt = dt_ref[0]
        cum = cum_ref[pl.ds(r0, Q), :]
        rev = cum[Q - 1:Q, :] - cum + dt * nega_ref[...]
        first = rev[0:1, :]
        wtail_b = dt * jnp.exp(first - rev)
        ew_x, ab_x = expand(jnp.concatenate([jnp.exp(rev), wtail_b], axis=0), jnp.exp(first), exb_ref)
        erev_x = ew_x[0:Q]
        wtail_x = ew_x[Q:2 * Q]

        parts = []
        ssq = None
        for g in range(SSD_GROUPS):
            cg = xbc_ref[0, :, SSD_WIDTH + 512 + g * SSD_STATE:SSD_WIDTH + 512 + (g + 1) * SSD_STATE]
            gc = slice(g * GROUP_W, (g + 1) * GROUP_W)
            cs = _dot(cg, ss_ref[1, :, gc].astype(BF16))
            y = (yas_ref[pl.ds(r0, Q), gc] + cs * erev_x[:, gc]
                 + dsk_ref[:, gc] * xbc_ref[0, :, gc].astype(F32))
            y = y * zs_ref[0, :, gc].astype(F32)
            parts.append(y)
            ssq = y * y if ssq is None else ssq + y * y
        proj = out_proj_matmul(slot_r)
        inv = lax.rsqrt(jnp.sum(ssq, axis=-1, keepdims=True) * (1.0 / SSD_WIDTH) + EPS)
        for g, y in enumerate(parts):
            gc = slice(g * GROUP_W, (g + 1) * GROUP_W)
            mixs_ref[slot_w, :, gc] = (y * inv).astype(BF16)
        ssd_state_update(1, wtail_x, ab_x)
        out_proj_finish(proj)
        mixs_ref[slot_r] = mixs_ref[slot_w]

        if emit_state:
            @pl.when(s == 2 * nc - 1)
            def _emit_bwd():
                emit_ssd_state(1)
                or_ref[0, 0, 1] = sr_ref[1]


def _mixer_call(zs, xbc, dt, q, k, v, gs, nega, dsk, lamx, w_out_bf, x, mod, npost, s0_ssd, s0_ret,
                emit_state):
    nb, L, _ = zs.shape
    nc = L // CHUNK
    Q = CHUNK
    has_s0 = s0_ssd is not None
    steps = 2 * nc
    n_real = nb * steps
    per_seq_mod = mod.shape[0] > 1

    def cur(i):
        ic = jnp.minimum(i, n_real - 1)
        return ic // steps, ic % steps

    def chunk_map(i):
        b, s = cur(i)
        return (b, jnp.where(s < nc, s, steps - 1 - s), 0)

    def phase2_map(i):
        b, s = cur(i)
        return (b, jnp.where(s < nc, nc - 1, steps - 1 - s), 0)

    def lag(i):
        j = jnp.maximum(i - 1, 0)
        return j // steps, j % steps

    def lag_map(i):
        bl, sl = lag(i)
        return (bl, jnp.where(sl < nc, nc - 1, steps - 1 - sl), 0)

    def lag_mod_map(i):
        bl, _ = lag(i)
        return (bl if per_seq_mod else 0, 0, 0)

    const2 = lambda i: (0, 0)
    in_specs = [
        pl.BlockSpec((1, Q, SSD_WIDTH), phase2_map),
        pl.BlockSpec((1, Q, CONV_CH), chunk_map),
        pl.BlockSpec((1, Q, DT_PAD), chunk_map),
        pl.BlockSpec((1, Q, RET_QK_WIDTH), chunk_map),
        pl.BlockSpec((1, Q, RET_QK_WIDTH), chunk_map),
        pl.BlockSpec((1, Q, RET_V_WIDTH), chunk_map),
        pl.BlockSpec((1, Q, RET_V_WIDTH), phase2_map),
        pl.BlockSpec((1, DT_PAD), const2),
        pl.BlockSpec((1, SSD_WIDTH), const2),
        pl.BlockSpec((2 * RET_HEADS, LANES), const2),
        pl.BlockSpec((MIX_WIDTH, D_MODEL), const2, pipeline_mode=pl.Buffered(1)),
        pl.BlockSpec((1, Q, D_MODEL), lag_map),
        pl.BlockSpec((1, 1, 3 * D_MODEL), lag_mod_map),
        pl.BlockSpec((1, D_MODEL), const2),
    ]
    args = [zs, xbc, dt, q, k, v, gs, nega, dsk, lamx, w_out_bf, x, mod, npost]
    state_map = lambda i: (cur(i)[0], 0, 0, 0, 0, 0)
    ssd_state_block = (1, 1, 2, SSD_HEADS, SSD_HEAD_DIM, SSD_STATE)
    ret_state_block = (1, 1, 2, RET_HEADS, RET_QK_DIM, RET_V_DIM)
    if has_s0:
        in_specs += [pl.BlockSpec(ssd_state_block, state_map), pl.BlockSpec(ret_state_block, state_map)]
        args += [s0_ssd, s0_ret]
    out_specs = [pl.BlockSpec((1, Q, D_MODEL), lag_map)]
    out_shape = [jax.ShapeDtypeStruct((nb, L, D_MODEL), F32)]
    if emit_state:
        out_specs += [pl.BlockSpec(ssd_state_block, state_map), pl.BlockSpec(ret_state_block, state_map)]
        out_shape += [jax.ShapeDtypeStruct((nb,) + ssd_state_block[1:], F32),
                      jax.ShapeDtypeStruct((nb,) + ret_state_block[1:], F32)]
    n_pairs = RET_HEADS // 2
    scratch = [
        pltpu.VMEM((L, SSD_WIDTH), F32),
        pltpu.VMEM((L, RET_V_WIDTH), F32),
        pltpu.VMEM((2, SSD_STATE, SSD_WIDTH), F32),
        pltpu.VMEM((2, RET_HEADS, RET_QK_DIM, RET_V_DIM), F32),
        pltpu.VMEM((n_pairs, Q, 2 * Q), F32),
        pltpu.VMEM((n_pairs, Q, PAIR_W), F32),
        pltpu.VMEM((n_pairs, Q, PAIR_W), F32),
        pltpu.VMEM((n_pairs, LANES, Q), F32),
        pltpu.VMEM((n_pairs, LANES, Q), F32),
        pltpu.VMEM((2 * RET_HEADS, LANES), F32),
        pltpu.VMEM((LANES, SSD_WIDTH), BF16),
        pltpu.VMEM((LANES, SSD_WIDTH), BF16),
        pltpu.VMEM((L, DT_PAD), F32),
        pltpu.VMEM((2, Q, MIX_WIDTH), BF16),
    ]
    outs = pl.pallas_call(
        functools.partial(_mixer_kernel, nb=nb, nc=nc, has_s0=has_s0, emit_state=emit_state),
        grid=(n_real + 1,),
        in_specs=in_specs,
        out_specs=out_specs,
        out_shape=out_shape,
        scratch_shapes=scratch,
        compiler_params=pltpu.CompilerParams(
            dimension_semantics=("arbitrary",), vmem_limit_bytes=VMEM_LIMIT),
        name="mixer_sample" if has_s0 else "mixer_prompt",
    )(*args)
    return outs


def _rope_tables(L):
    rows = L // GRID_W
    row = jnp.repeat(jnp.arange(rows, dtype=F32), GRID_W)
    col = jnp.tile(jnp.arange(GRID_W, dtype=F32), rows)
    half = RET_QK_DIM // 2
    inv = ROPE_BASE ** (-jnp.arange(0, half, 2, dtype=F32) / half)
    ang_r = row[:, None] * inv
    ang_c = col[:, None] * inv
    cos_h = jnp.concatenate([jnp.cos(ang_r), jnp.cos(ang_r), jnp.cos(ang_c), jnp.cos(ang_c)], axis=-1)
    sin_h = jnp.concatenate([-jnp.sin(ang_r), jnp.sin(ang_r), -jnp.sin(ang_c), jnp.sin(ang_c)], axis=-1)
    return jnp.tile(cos_h, (1, 2)), jnp.tile(sin_h, (1, 2))


def kernel(x_prompt, x_sample, state_ssd, state_ret, c, c_ctx, w_mod, b_mod, norm_pre_w, norm_post_w,
           w_in, conv_w, conv_b, ssd_A_log, ssd_dt_bias, ssd_D, ssd_norm_w, ret_decay, ret_norm_w, w_out):
    nb_p = x_prompt.shape[0]
    nb_s = x_sample.shape[0]
    l = 0

    n_dt = 2 * SSD_HEADS
    w_packed = _pack_call(jnp.swapaxes(w_in[l], 0, 1))
    mix_norm_w = jnp.concatenate([ssd_norm_w[l], ret_norm_w[l]]).reshape(MIX_WIDTH, 1)
    w_out_bf = _scale_rows_call(w_out[l], mix_norm_w)
    dtb = jnp.pad(ssd_dt_bias[l].reshape(1, n_dt), ((0, 0), (0, DT_PAD - n_dt)))
    nega = jnp.pad(-jnp.exp(ssd_A_log[l].reshape(1, n_dt)), ((0, 0), (0, DT_PAD - n_dt)))
    dsk = jnp.repeat(ssd_D[l], SSD_HEAD_DIM).reshape(1, SSD_WIDTH)
    lamx = jnp.broadcast_to(-jnp.exp(ret_decay[l].reshape(2 * RET_HEADS, 1)), (2 * RET_HEADS, LANES))
    npre = norm_pre_w[l].reshape(1, D_MODEL)
    npost = norm_post_w[l].reshape(1, D_MODEL)
    cb = conv_b[l].reshape(1, CONV_CH)
    cos_t, sin_t = _rope_tables(x_sample.shape[1])

    cond = jnp.concatenate([c, c_ctx[None, :], jnp.zeros((16 - nb_s - 1, D_MODEL), F32)], axis=0)
    mod = _mod_call(cond, w_mod[l], b_mod[l].reshape(1, 3 * D_MODEL))
    mod_s = mod[:nb_s].reshape(nb_s, 1, 3 * D_MODEL)
    mod_p = mod[nb_s:nb_s + 1].reshape(1, 1, 3 * D_MODEL)

    pp = _in_proj_call(x_prompt, mod_p, npre, w_packed, conv_w[l], cb, dtb, None, None)
    y_p, st_ssd_t, st_ret = _mixer_call(*pp, nega, dsk, lamx, w_out_bf, x_prompt, mod_p, npost,
                                        None, None, True)

    ps = _in_proj_call(x_sample, mod_s, npre, w_packed, conv_w[l], cb, dtb, cos_t, sin_t)
    (y_s,) = _mixer_call(*ps, nega, dsk, lamx, w_out_bf, x_sample, mod_s, npost,
                         jnp.swapaxes(state_ssd, -1, -2), state_ret, False)
    return (y_p, y_s, jnp.swapaxes(st_ssd_t, -1, -2), st_ret)
```

```python
import functools
import types

import jax
import jax.numpy as jnp
from jax import lax
from jax.experimental import pallas as pl
from jax.experimental.pallas import tpu as pltpu

F32 = jnp.float32
BF16 = jnp.bfloat16

D_MODEL = 1024
CHUNK = 128
GRID_W = 64
EPS = 1e-6
SSD_WIDTH = 1024
SSD_HEAD_DIM = 64
SSD_HEADS = 16
SSD_GROUPS = 4
SSD_STATE = 128
CONV_CH = SSD_WIDTH + 2 * SSD_GROUPS * SSD_STATE
RET_HEADS = 8
RET_QK_DIM = 64
RET_V_DIM = 128
RET_QK_WIDTH = RET_HEADS * RET_QK_DIM
RET_V_WIDTH = RET_HEADS * RET_V_DIM
MIX_WIDTH = SSD_WIDTH + RET_V_WIDTH
ROPE_BASE = 10000.0
LANES = 128
DT_PAD = LANES
GROUP_W = SSD_WIDTH // SSD_GROUPS
PAIR_W = 2 * RET_V_DIM
STREAMS = 2

OFF_Z = 0
OFF_XBC = OFF_Z + SSD_WIDTH
OFF_Q = OFF_XBC + CONV_CH
OFF_K = OFF_Q + RET_QK_WIDTH
OFF_V = OFF_K + RET_QK_WIDTH
OFF_G = OFF_V + RET_V_WIDTH
OFF_DT = OFF_G + RET_V_WIDTH
IN_COLS_PACKED = OFF_DT + DT_PAD

LOG2E = 1.4426950408889634
VMEM_LIMIT = 56 * 1024 * 1024


def _silu(x):
    return x * (1.0 / (1.0 + jnp.exp(-x)))


def _dot(a, b):
    return jnp.dot(a, b, preferred_element_type=F32)


def _dot_nt(a, b):
    return lax.dot_general(a, b, (((1,), (1,)), ((), ())), preferred_element_type=F32)


def _split3(x):
    hi = x.astype(BF16)
    r1 = x - hi.astype(F32)
    mid = r1.astype(BF16)
    lo = (r1 - mid.astype(F32)).astype(BF16)
    return hi, mid, lo


PACK_COLS = 512
PACK_STEPS = -(-IN_COLS_PACKED // PACK_COLS)


def _pack_kernel(wt_ref, out_ref):
    t = wt_ref[...].T
    lane = lax.broadcasted_iota(jnp.int32, t.shape, 1)
    is_dt = pl.program_id(0) == PACK_STEPS - 1
    t = jnp.where(jnp.logical_and(is_dt, lane >= 2 * SSD_HEADS), 0.0, t)
    out_ref[...] = t.astype(BF16)


def _pack_call(w_in_t):
    n_head = (OFF_Q - OFF_Z) // PACK_COLS
    src_dt = SSD_WIDTH + CONV_CH
    src_tail = src_dt + 2 * SSD_HEADS

    def src_row(i):
        t8 = jnp.where(i < n_head, i * (PACK_COLS // 8),
                       jnp.where(i < PACK_STEPS - 1, src_tail // 8 + (i - n_head) * (PACK_COLS // 8),
                                 src_dt // 8))
        return t8 * 8

    return pl.pallas_call(
        _pack_kernel,
        grid=(PACK_STEPS,),
        in_specs=[pl.BlockSpec((pl.Element(PACK_COLS), pl.Element(D_MODEL)), lambda i: (src_row(i), 0))],
        out_specs=pl.BlockSpec((D_MODEL, PACK_COLS), lambda i: (0, i)),
        out_shape=jax.ShapeDtypeStruct((D_MODEL, IN_COLS_PACKED), BF16),
        name="pack_w_in",
    )(w_in_t)


def _scale_rows_kernel(w_ref, s_ref, out_ref):
    out_ref[...] = (w_ref[...] * s_ref[...]).astype(BF16)


def _scale_rows_call(w, row_scale):
    rows, cols = w.shape
    tr = 512
    return pl.pallas_call(
        _scale_rows_kernel,
        grid=(rows // tr,),
        in_specs=[pl.BlockSpec((tr, cols), lambda i: (i, 0)), pl.BlockSpec((tr, 1), lambda i: (i, 0))],
        out_specs=pl.BlockSpec((tr, cols), lambda i: (i, 0)),
        out_shape=jax.ShapeDtypeStruct((rows, cols), BF16),
        name="fold_norm_w_out",
    )(w, row_scale)


def _mod_kernel(cond_ref, w_ref, b_ref, out_ref):
    a = _silu(cond_ref[...]).astype(BF16)
    out_ref[...] = _dot(a, w_ref[...].astype(BF16)) + b_ref[...]


def _mod_call(cond, w_mod, b_mod):
    rows = cond.shape[0]
    tn = 1024
    return pl.pallas_call(
        _mod_kernel,
        grid=(3 * D_MODEL // tn,),
        in_specs=[
            pl.BlockSpec((rows, D_MODEL), lambda j: (0, 0)),
            pl.BlockSpec((D_MODEL, tn), lambda j: (0, j)),
            pl.BlockSpec((1, tn), lambda j: (0, j)),
        ],
        out_specs=pl.BlockSpec((rows, tn), lambda j: (0, j)),
        out_shape=jax.ShapeDtypeStruct((rows, 3 * D_MODEL), F32),
        name="mod",
    )(cond, w_mod, b_mod)


def _modnorm(x, mod_ref, npw_ref):
    ms = jnp.mean(x * x, axis=-1, keepdims=True)
    xn = x * lax.rsqrt(ms + EPS) * npw_ref[...]
    shift = mod_ref[0, :, 0:D_MODEL]
    scale = mod_ref[0, :, D_MODEL:2 * D_MODEL]
    return (xn * (1.0 + scale) + shift).astype(BF16)


def _in_proj_kernel(*refs, tl, n_tiles, rope):
    if rope:
        (x_ref, xp_ref, xn_ref, mod_ref, npw_ref, w_ref, cw_ref, cb_ref, dtb_ref,
         cos_ref, sin_ref, zs_ref, xbc_ref, dt_ref, q_ref, k_ref, v_ref, gs_ref, h_ref) = refs
    else:
        (x_ref, xp_ref, xn_ref, mod_ref, npw_ref, w_ref, cw_ref, cb_ref, dtb_ref,
         zs_ref, xbc_ref, dt_ref, q_ref, k_ref, v_ref, gs_ref, h_ref) = refs
    t = pl.program_id(1)
    h_ref[...] = _modnorm(x_ref[0], mod_ref, npw_ref)
    piece = 256

    if n_tiles > 1:
        halo = jnp.concatenate([xp_ref[0], xn_ref[0]], axis=0)
        hh = _modnorm(halo, mod_ref, npw_ref)
        pv = jnp.where(t > 0, 1.0, 0.0).astype(F32)
        nv = jnp.where(t < n_tiles - 1, 1.0, 0.0).astype(F32)
    row = lax.broadcasted_iota(jnp.int32, (tl, piece), 0)

    def sec_xbc(c0):
        wcols = w_ref[:, OFF_XBC + c0:OFF_XBC + c0 + piece]
        acc = _dot(h_ref[...], wcols)
        up = pltpu.roll(acc, 1, axis=0)
        dn = pltpu.roll(acc, tl - 1, axis=0)
        if n_tiles > 1:
            uh = _dot(hh, wcols)
            up = jnp.where(row == 0, uh[7:8, :] * pv, up)
            dn = jnp.where(row == tl - 1, uh[8:9, :] * nv, dn)
        else:
            up = jnp.where(row == 0, 0.0, up)
            dn = jnp.where(row == tl - 1, 0.0, dn)
        y = (cw_ref[0:1, c0:c0 + piece] * up + cw_ref[1:2, c0:c0 + piece] * acc
             + cw_ref[2:3, c0:c0 + piece] * dn + cb_ref[:, c0:c0 + piece])
        xbc_ref[0, :, c0:c0 + piece] = _silu(y).astype(BF16)

    lane = lax.broadcasted_iota(jnp.int32, (tl, LANES), 1)
    first_half = (lane % 32) < 16

    def sec_qk(off, ref, scl, c0):
        acc = _dot(h_ref[...], w_ref[:, off + c0:off + c0 + piece]) * scl
        if rope:
            for s0 in range(0, piece, LANES):
                xs = acc[:, s0:s0 + LANES]
                partner = jnp.where(first_half, pltpu.roll(xs, LANES - 16, axis=1),
                                    pltpu.roll(xs, 16, axis=1))
                ref[0, :, c0 + s0:c0 + s0 + LANES] = (
                    xs * cos_ref[...] + partner * sin_ref[...]).astype(BF16)
        else:
            ref[0, :, c0:c0 + piece] = acc.astype(BF16)

    def sec_q(c0):
        sec_qk(OFF_Q, q_ref, 1.0, c0)

    def sec_k(c0):
        sec_qk(OFF_K, k_ref, RET_QK_DIM ** -0.5, c0)

    def sec_z(c0):
        acc = _dot(h_ref[...], w_ref[:, OFF_Z + c0:OFF_Z + c0 + piece])
        zs_ref[0, :, c0:c0 + piece] = _silu(acc).astype(BF16)

    def sec_g(c0):
        acc = _dot(h_ref[...], w_ref[:, OFF_G + c0:OFF_G + c0 + piece])
        gs_ref[0, :, c0:c0 + piece] = _silu(acc).astype(BF16)

    def sec_v(c0):
        acc = _dot(h_ref[...], w_ref[:, OFF_V + c0:OFF_V + c0 + piece])
        v_ref[0, :, c0:c0 + piece] = acc.astype(BF16)

    def sec_dt(c0):
        acc = _dot(h_ref[...], w_ref[:, OFF_DT:OFF_DT + DT_PAD]) + dtb_ref[...]
        dt_ref[0] = jnp.maximum(acc, 0.0) + jnp.log(1.0 + jnp.exp(-jnp.abs(acc)))

    def pieces(fn, width):
        return [(fn, c0) for c0 in range(0, width, piece)]

    heavy = (pieces(sec_z, SSD_WIDTH) + pieces(sec_g, RET_V_WIDTH) + pieces(sec_xbc, CONV_CH)
             + pieces(sec_q, RET_QK_WIDTH) + pieces(sec_k, RET_QK_WIDTH))
    light = pieces(sec_v, RET_V_WIDTH) + [(sec_dt, 0)]
    stride = len(heavy) // len(light)
    order = []
    for i, item in enumerate(heavy):
        order.append(item)
        if i % stride == stride - 1 and light:
            order.append(light.pop(0))
    order += light
    for fn, c0 in order:
        fn(c0)


def _in_proj_call(x, mod, npw, w_packed, conv_w, conv_b, dtb, cos_t, sin_t):
    nb, L, _ = x.shape
    tl = min(L, 512)
    n_tiles = L // tl
    g8 = tl // 8
    rope = cos_t is not None
    per_seq_mod = mod.shape[0] > 1
    mod_map = (lambda s, t: (s, 0, 0)) if per_seq_mod else (lambda s, t: (0, 0, 0))
    const2 = lambda s, t: (0, 0)
    in_specs = [
        pl.BlockSpec((1, tl, D_MODEL), lambda s, t: (s, t, 0)),
        pl.BlockSpec((1, 8, D_MODEL), lambda s, t: (s, jnp.maximum(t * g8 - 1, 0), 0)),
        pl.BlockSpec((1, 8, D_MODEL), lambda s, t: (s, jnp.minimum((t + 1) * g8, L // 8 - 1), 0)),
        pl.BlockSpec((1, 1, 3 * D_MODEL), mod_map),
        pl.BlockSpec((1, D_MODEL), const2),
        pl.BlockSpec((D_MODEL, IN_COLS_PACKED), const2, pipeline_mode=pl.Buffered(1)),
        pl.BlockSpec((3, CONV_CH), const2),
        pl.BlockSpec((1, CONV_CH), const2),
        pl.BlockSpec((1, DT_PAD), const2),
    ]
    args = [x, x, x, mod, npw, w_packed, conv_w, conv_b, dtb]
    if rope:
        in_specs += [pl.BlockSpec((tl, LANES), lambda s, t: (t, 0)),
                     pl.BlockSpec((tl, LANES), lambda s, t: (t, 0))]
        args += [cos_t, sin_t]
    widths = (SSD_WIDTH, CONV_CH, DT_PAD, RET_QK_WIDTH, RET_QK_WIDTH, RET_V_WIDTH, RET_V_WIDTH)
    dtypes = (BF16, BF16, F32, BF16, BF16, BF16, BF16)
    out_specs = [pl.BlockSpec((1, tl, w), lambda s, t: (s, t, 0)) for w in widths]
    out_shape = [jax.ShapeDtypeStruct((nb, L, w), d) for w, d in zip(widths, dtypes)]
    return pl.pallas_call(
        functools.partial(_in_proj_kernel, tl=tl, n_tiles=n_tiles, rope=rope),
        grid=(nb, n_tiles),
        in_specs=in_specs,
        out_specs=out_specs,
        out_shape=out_shape,
        scratch_shapes=[pltpu.VMEM((tl, D_MODEL), BF16)],
        compiler_params=pltpu.CompilerParams(
            dimension_semantics=("parallel", "parallel"), vmem_limit_bytes=VMEM_LIMIT),
        name="in_proj_rope" if rope else "in_proj",
    )(*args)


def _mixer_kernel(*refs, n_groups, nc, has_s0, emit_state, per_seq_mod):
    Q = CHUNK
    H = SSD_HEADS
    U = STREAMS
    it = iter(refs)
    zs_ref, xbc_ref, dt_ref, q_ref, k_ref, v_ref, gs_ref = (next(it) for _ in range(7))
    nega_ref, dsk_ref, lam_ref = (next(it) for _ in range(3))
    wout_ref, xres_ref, modl_ref, npost_ref = (next(it) for _ in range(4))
    if has_s0:
        s0s_ref, s0r_ref = next(it), next(it)
    y_ref = next(it)
    if emit_state:
        os_ref, or_ref = next(it), next(it)
    (yas_ref, yar_ref, ss_ref, sr_ref, dsum_ref, ef_ref, eb_ref, tf_ref, tb_ref,
     ar_ref, exf_ref, exb_ref, cum_ref, mixs_ref) = (next(it) for _ in range(14))

    i = pl.program_id(0)
    n_real = n_groups * 2 * nc
    is_real = i < n_real
    ic = jnp.minimum(i, n_real - 1)
    s = ic % (2 * nc)
    phase2 = s >= nc
    c = jnp.where(phase2, 2 * nc - 1 - s, s)
    r0 = pl.multiple_of(c * Q, Q)
    slot_w = 1
    slot_r = 0

    rowq = lax.broadcasted_iota(jnp.int32, (Q, Q), 0)
    colq = lax.broadcasted_iota(jnp.int32, (Q, Q), 1)

    def out_proj_matmul():
        return _dot(mixs_ref[slot_r], wout_ref[...])

    def out_proj_finish(out):
        ms = jnp.mean(out * out, axis=-1, keepdims=True)
        o = out * lax.rsqrt(ms + EPS) * npost_ref[...]
        for u in range(U):
            gate = modl_ref[u if per_seq_mod else 0, :, 2 * D_MODEL:3 * D_MODEL]
            y_ref[u] = xres_ref[u] + gate * o[u * Q:(u + 1) * Q]

    @pl.when(i == 0)
    def _init_tables():
        mixs_ref[...] = jnp.zeros_like(mixs_ref)
        diff = (rowq - colq).astype(F32)
        rowf = lax.broadcasted_iota(jnp.int32, (Q, LANES), 0).astype(F32)
        colf = lax.broadcasted_iota(jnp.int32, (RET_QK_DIM, Q), 1).astype(F32)
        for h in range(RET_HEADS):
            pr, hh = divmod(h, 2)
            lf = lam_ref[h:h + 1, :]
            lb = lam_ref[RET_HEADS + h:RET_HEADS + h + 1, :]
            e = jnp.where(rowq >= colq, lf * diff, lb * (-diff))
            dsum_ref[pr, :, hh * Q:(hh + 1) * Q] = jnp.exp(e) * jnp.where(rowq == colq, 2.0, 1.0)
            hc = slice(hh * LANES, (hh + 1) * LANES)
            ef_ref[pr, :, hc] = jnp.exp(lf * (rowf + 1.0))
            eb_ref[pr, :, hc] = jnp.exp(lb * (Q - rowf))
            kr = slice(hh * RET_QK_DIM, (hh + 1) * RET_QK_DIM)
            tf_ref[pr, kr, :] = jnp.exp(lf * (Q - 1.0 - colf))
            tb_ref[pr, kr, :] = jnp.exp(lb * colf)
        ar_ref[...] = jnp.exp(lam_ref[...] * float(Q))
        er = lax.broadcasted_iota(jnp.int32, (LANES, SSD_WIDTH), 0)
        ec = lax.broadcasted_iota(jnp.int32, (LANES, SSD_WIDTH), 1) // SSD_HEAD_DIM
        exf_ref[...] = jnp.where(er == ec, 1.0, 0.0).astype(BF16)
        exb_ref[...] = jnp.where(er == ec + H, 1.0, 0.0).astype(BF16)

    def expand(w, a_row, ex_ref):
        a3 = jnp.concatenate(_split3(jnp.broadcast_to(a_row, (16, LANES))), axis=0)
        out = _dot(jnp.concatenate([w.astype(BF16), a3], axis=0), ex_ref[...])
        n = w.shape[0]
        return out[0:n], out[n:n + 1] + out[n + 16:n + 17] + out[n + 32:n + 33]

    def make_stream(u):
        zs, xbc, dtr, qr, kr_, vr, gsr = (r.at[u] for r in (zs_ref, xbc_ref, dt_ref, q_ref, k_ref,
                                                            v_ref, gs_ref))
        yas, yar, ss, sr, cums = (r.at[u] for r in (yas_ref, yar_ref, ss_ref, sr_ref, cum_ref))
        mix_rows = slice(u * Q, (u + 1) * Q)

        def init_state():
            if has_s0:
                for d in range(2):
                    for p in range(SSD_HEADS // 2):
                        pair_t = jnp.concatenate(
                            [s0s_ref[u, 0, d, 2 * p], s0s_ref[u, 0, d, 2 * p + 1]], axis=0)
                        ss[d, :, p * LANES:(p + 1) * LANES] = pair_t.T
                sr[...] = s0r_ref[u, 0]
            else:
                ss[...] = jnp.zeros_like(ss)
                sr[...] = jnp.zeros_like(sr)

        def ssd_state_update(d, wtail_x, a_x):
            for g in range(SSD_GROUPS):
                bgt = xbc[:, SSD_WIDTH + g * SSD_STATE:SSD_WIDTH + (g + 1) * SSD_STATE].T
                gc = slice(g * GROUP_W, (g + 1) * GROUP_W)
                xt = xbc[:, gc] * wtail_x[:, gc].astype(BF16)
                ss[d, :, gc] = ss[d, :, gc] * a_x[:, gc] + _dot(bgt, xt)

        def k_transposed(pr):
            return kr_[:, pr * LANES:(pr + 1) * LANES].T

        def ret_state_update(d, tail_ref, kts):
            for pr, kt in enumerate(kts):
                pc = slice(pr * PAIR_W, (pr + 1) * PAIR_W)
                ktt = (kt.astype(F32) * tail_ref[pr]).astype(BF16)
                ds = _dot(ktt, vr[:, pc])
                for hh in range(2):
                    h = 2 * pr + hh
                    a = ar_ref[d * RET_HEADS + h:d * RET_HEADS + h + 1, :]
                    sr[d, h] = (sr[d, h] * a
                                + ds[hh * RET_QK_DIM:(hh + 1) * RET_QK_DIM, hh * LANES:(hh + 1) * LANES])

        def emit_states(d):
            for p in range(SSD_HEADS // 2):
                pair_t = ss[d, :, p * LANES:(p + 1) * LANES].T
                os_ref[u, 0, d, 2 * p] = pair_t[0:SSD_HEAD_DIM]
                os_ref[u, 0, d, 2 * p + 1] = pair_t[SSD_HEAD_DIM:2 * SSD_HEAD_DIM]
            or_ref[u, 0, d] = sr[d]

        def ret_state_blockdiag(d, pr):
            z = jnp.zeros((RET_QK_DIM, RET_V_DIM), F32)
            top = jnp.concatenate([sr[d, 2 * pr], z], axis=1)
            bot = jnp.concatenate([z, sr[d, 2 * pr + 1]], axis=1)
            return jnp.concatenate([top, bot], axis=0).astype(BF16)

        def phase1():
            dt = dtr[...]
            la = dt * nega_ref[...]
            tl_bf = jnp.where(rowq >= colq, 1.0, 0.0).astype(BF16)
            c3 = _dot(tl_bf, jnp.concatenate(_split3(la), axis=1))
            cum = c3[:, 0:LANES] + c3[:, LANES:2 * LANES] + c3[:, 2 * LANES:3 * LANES]
            cums[pl.ds(r0, Q), :] = cum
            tot = cum[Q - 1:Q, :]
            rev = tot - cum + la
            yield

            sub_k = lax.broadcasted_iota(jnp.int32, (LANES, Q), 0) < RET_QK_DIM
            lane_v = lax.broadcasted_iota(jnp.int32, (Q, PAIR_W), 1) < RET_V_DIM
            kts = [k_transposed(pr) for pr in range(RET_HEADS // 2)]
            for pr, kt in enumerate(kts):
                qp = qr[:, pr * LANES:(pr + 1) * LANES]
                zk = jnp.zeros_like(kt)
                kbd = jnp.concatenate([jnp.where(sub_k, kt, zk), jnp.where(sub_k, zk, kt)], axis=1)
                a2 = _dot(qp, kbd)
                ad = (a2 * dsum_ref[pr]).astype(BF16)
                pc = slice(pr * PAIR_W, (pr + 1) * PAIR_W)
                vp = vr[:, pc]
                zv = jnp.zeros_like(vp)
                vbd = jnp.concatenate([jnp.where(lane_v, vp, zv), jnp.where(lane_v, zv, vp)], axis=0)
                y = _dot(ad, vbd) + _dot(qp, ret_state_blockdiag(0, pr)) * ef_ref[pr]
                yar[pl.ds(r0, Q), pc] = y
            ret_state_update(0, tf_ref, kts)
            yield

            gms, css = [], []
            for g in range(SSD_GROUPS):
                cg = xbc[:, SSD_WIDTH + 512 + g * SSD_STATE:SSD_WIDTH + 512 + (g + 1) * SSD_STATE]
                bg = xbc[:, SSD_WIDTH + g * SSD_STATE:SSD_WIDTH + (g + 1) * SSD_STATE]
                gms.append(_dot_nt(cg, bg))
                css.append(_dot(cg, ss[0, :, g * GROUP_W:(g + 1) * GROUP_W].astype(BF16)))
            yield

            dt_t = dt.T
            ldt_t = jnp.log(dt_t)
            sub = lax.broadcasted_iota(jnp.int32, (LANES, Q), 0)
            adj_t = (jnp.where(sub < H, cum.T, rev.T) - ldt_t) * LOG2E
            dg_t = jnp.log(dt_t[0:H, :] + dt_t[H:2 * H, :]) * LOG2E
            cum2 = cum * LOG2E
            rev2 = rev * LOG2E
            wtail_f = dt * jnp.exp(tot - cum)
            ew_x, af_x = expand(jnp.concatenate([jnp.exp(cum), wtail_f], axis=0), jnp.exp(tot), exf_ref)
            ecum_x = ew_x[0:Q]
            wtail_x = ew_x[Q:2 * Q]
            lt = rowq > colq
            gt = rowq < colq
            lane_g = lax.broadcasted_iota(jnp.int32, (Q, GROUP_W), 1) // SSD_HEAD_DIM
            yield

            for g in range(SSD_GROUPS):
                gc = slice(g * GROUP_W, (g + 1) * GROUP_W)
                xg = xbc[:, gc]
                ws, xs = [], []
                for j in range(4):
                    h = 4 * g + j
                    hb = H + h
                    arg = jnp.where(lt, cum2[:, h:h + 1] - adj_t[h:h + 1, :],
                                    jnp.where(gt, rev2[:, hb:hb + 1] - adj_t[hb:hb + 1, :], dg_t[h:h + 1, :]))
                    ws.append((gms[g] * jnp.exp2(arg)).astype(BF16))
                    xs.append(jnp.where(lane_g == j, xg, jnp.zeros_like(xg)))
                y = _dot(jnp.concatenate(ws, axis=1), jnp.concatenate(xs, axis=0))
                yas[pl.ds(r0, Q), gc] = y + css[g] * ecum_x[:, gc]
            yield
            ssd_state_update(0, wtail_x, af_x)

        def phase2():
            for pr in range(RET_HEADS // 2):
                qp = qr[:, pr * LANES:(pr + 1) * LANES]
                pc = slice(pr * PAIR_W, (pr + 1) * PAIR_W)
                y2 = yar[pl.ds(r0, Q), pc] + _dot(qp, ret_state_blockdiag(1, pr)) * eb_ref[pr]
                for hh in range(2):
                    h = 2 * pr + hh
                    hc = slice(h * LANES, (h + 1) * LANES)
                    y = y2[:, hh * LANES:(hh + 1) * LANES]
                    mu = jnp.mean(y, axis=-1, keepdims=True)
                    yc = y - mu
                    var = jnp.mean(yc * yc, axis=-1, keepdims=True)
                    yn = yc * lax.rsqrt(var + EPS) * gsr[:, hc].astype(F32)
                    mixs_ref[slot_w, mix_rows, SSD_WIDTH + h * LANES:SSD_WIDTH + (h + 1) * LANES] = (
                        yn.astype(BF16))
            ret_state_update(1, tb_ref, [k_transposed(pr) for pr in range(RET_HEADS // 2)])
            yield

            dt = dtr[...]
            cum = cums[pl.ds(r0, Q), :]
            rev = cum[Q - 1:Q, :] - cum + dt * nega_ref[...]
            first = rev[0:1, :]
            wtail_b = dt * jnp.exp(first - rev)
            ew_x, ab_x = expand(jnp.concatenate([jnp.exp(rev), wtail_b], axis=0), jnp.exp(first), exb_ref)
            erev_x = ew_x[0:Q]
            wtail_x = ew_x[Q:2 * Q]

            parts = []
            ssq = None
            for g in range(SSD_GROUPS):
                cg = xbc[:, SSD_WIDTH + 512 + g * SSD_STATE:SSD_WIDTH + 512 + (g + 1) * SSD_STATE]
                gc = slice(g * GROUP_W, (g + 1) * GROUP_W)
                cs = _dot(cg, ss[1, :, gc].astype(BF16))
                y = (yas[pl.ds(r0, Q), gc] + cs * erev_x[:, gc]
                     + dsk_ref[:, gc] * xbc[:, gc].astype(F32))
                y = y * zs[:, gc].astype(F32)
                parts.append(y)
                ssq = y * y if ssq is None else ssq + y * y
            yield "project"

            inv = lax.rsqrt(jnp.sum(ssq, axis=-1, keepdims=True) * (1.0 / SSD_WIDTH) + EPS)
            for g, y in enumerate(parts):
                gc = slice(g * GROUP_W, (g + 1) * GROUP_W)
                mixs_ref[slot_w, mix_rows, gc] = (y * inv).astype(BF16)
            ssd_state_update(1, wtail_x, ab_x)

        return types.SimpleNamespace(init_state=init_state, emit_states=emit_states, phase1=phase1,
                                     phase2=phase2)

    streams = [make_stream(u) for u in range(U)]

    def run_interleaved(stage_gens, project_after=None):
        proj = None
        live = list(stage_gens)
        rnd = 0
        while live:
            marks = []
            nxt = []
            for gen in live:
                try:
                    marks.append(next(gen))
                    nxt.append(gen)
                except StopIteration:
                    pass
            if (project_after == rnd) or ("project" in marks):
                proj = out_proj_matmul()
            live = nxt
            rnd += 1
        return proj

    @pl.when(is_real & (s == 0))
    def _init_state():
        for st in streams:
            st.init_state()

    def phase1_all(with_out_proj):
        proj = run_interleaved([st.phase1() for st in streams], project_after=2 if with_out_proj else None)
        if with_out_proj:
            out_proj_finish(proj)
        if emit_state:
            @pl.when(s == nc - 1)
            def _emit_fwd():
                for st in streams:
                    st.emit_states(0)

    @pl.when(is_real & (s == 0))
    def _phase1_first():
        phase1_all(True)

    if nc > 1:
        @pl.when(is_real & (s > 0) & (s < nc))
        def _phase1_rest():
            phase1_all(False)

    @pl.when(jnp.logical_not(is_real))
    def _drain():
        out_proj_finish(out_proj_matmul())

    @pl.when(is_real & phase2)
    def _phase2():
        proj = run_interleaved([st.phase2() for st in streams])
        out_proj_finish(proj)
        mixs_ref[slot_r] = mixs_ref[slot_w]
        if emit_state:
            @pl.when(s == 2 * nc - 1)
            def _emit_bwd():
                for st in streams:
                    st.emit_states(1)


def _mixer_call(zs, xbc, dt, q, k, v, gs, nega, dsk, lamx, w_out_bf, x, mod, npost, s0_ssd, s0_ret,
                emit_state):
    nb, L, _ = zs.shape
    nc = L // CHUNK
    Q = CHUNK
    U = STREAMS
    assert nb % U == 0
    has_s0 = s0_ssd is not None
    n_groups = nb // U
    steps = 2 * nc
    n_real = n_groups * steps
    per_seq_mod = mod.shape[0] > 1

    def cur(i):
        ic = jnp.minimum(i, n_real - 1)
        return ic // steps, ic % steps

    def chunk_map(i):
        b, s = cur(i)
        return (b, jnp.where(s < nc, s, steps - 1 - s), 0)

    def phase2_map(i):
        b, s = cur(i)
        return (b, jnp.where(s < nc, nc - 1, steps - 1 - s), 0)

    def lag(i):
        j = jnp.maximum(i - 1, 0)
        return j // steps, j % steps

    def lag_map(i):
        bl, sl = lag(i)
        return (bl, jnp.where(sl < nc, nc - 1, steps - 1 - sl), 0)

    def lag_mod_map(i):
        bl, _ = lag(i)
        return (bl if per_seq_mod else 0, 0, 0)

    const2 = lambda i: (0, 0)
    in_specs = [
        pl.BlockSpec((U, Q, SSD_WIDTH), phase2_map),
        pl.BlockSpec((U, Q, CONV_CH), chunk_map),
        pl.BlockSpec((U, Q, DT_PAD), chunk_map),
        pl.BlockSpec((U, Q, RET_QK_WIDTH), chunk_map),
        pl.BlockSpec((U, Q, RET_QK_WIDTH), chunk_map),
        pl.BlockSpec((U, Q, RET_V_WIDTH), chunk_map),
        pl.BlockSpec((U, Q, RET_V_WIDTH), phase2_map),
        pl.BlockSpec((1, DT_PAD), const2),
        pl.BlockSpec((1, SSD_WIDTH), const2),
        pl.BlockSpec((2 * RET_HEADS, LANES), const2),
        pl.BlockSpec((MIX_WIDTH, D_MODEL), const2, pipeline_mode=pl.Buffered(1)),
        pl.BlockSpec((U, Q, D_MODEL), lag_map),
        pl.BlockSpec((U if per_seq_mod else 1, 1, 3 * D_MODEL), lag_mod_map),
        pl.BlockSpec((1, D_MODEL), const2),
    ]
    args = [zs, xbc, dt, q, k, v, gs, nega, dsk, lamx, w_out_bf, x, mod, npost]
    state_map = lambda i: (cur(i)[0], 0, 0, 0, 0, 0)
    ssd_state_block = (U, 1, 2, SSD_HEADS, SSD_HEAD_DIM, SSD_STATE)
    ret_state_block = (U, 1, 2, RET_HEADS, RET_QK_DIM, RET_V_DIM)
    if has_s0:
        in_specs += [pl.BlockSpec(ssd_state_block, state_map, pipeline_mode=pl.Buffered(1)),
                     pl.BlockSpec(ret_state_block, state_map, pipeline_mode=pl.Buffered(1))]
        args += [s0_ssd, s0_ret]
    out_specs = [pl.BlockSpec((U, Q, D_MODEL), lag_map)]
    out_shape = [jax.ShapeDtypeStruct((nb, L, D_MODEL), F32)]
    if emit_state:
        out_specs += [pl.BlockSpec(ssd_state_block, state_map), pl.BlockSpec(ret_state_block, state_map)]
        out_shape += [jax.ShapeDtypeStruct((nb,) + ssd_state_block[1:], F32),
                      jax.ShapeDtypeStruct((nb,) + ret_state_block[1:], F32)]
    n_pairs = RET_HEADS // 2
    scratch = [
        pltpu.VMEM((U, L, SSD_WIDTH), F32),
        pltpu.VMEM((U, L, RET_V_WIDTH), F32),
        pltpu.VMEM((U, 2, SSD_STATE, SSD_WIDTH), F32),
        pltpu.VMEM((U, 2, RET_HEADS, RET_QK_DIM, RET_V_DIM), F32),
        pltpu.VMEM((n_pairs, Q, 2 * Q), F32),
        pltpu.VMEM((n_pairs, Q, PAIR_W), F32),
        pltpu.VMEM((n_pairs, Q, PAIR_W), F32),
        pltpu.VMEM((n_pairs, LANES, Q), F32),
        pltpu.VMEM((n_pairs, LANES, Q), F32),
        pltpu.VMEM((2 * RET_HEADS, LANES), F32),
        pltpu.VMEM((LANES, SSD_WIDTH), BF16),
        pltpu.VMEM((LANES, SSD_WIDTH), BF16),
        pltpu.VMEM((U, L, DT_PAD), F32),
        pltpu.VMEM((2, U * Q, MIX_WIDTH), BF16),
    ]
    outs = pl.pallas_call(
        functools.partial(_mixer_kernel, n_groups=n_groups, nc=nc, has_s0=has_s0, emit_state=emit_state,
                          per_seq_mod=per_seq_mod),
        grid=(n_real + 1,),
        in_specs=in_specs,
        out_specs=out_specs,
        out_shape=out_shape,
        scratch_shapes=scratch,
        compiler_params=pltpu.CompilerParams(
            dimension_semantics=("arbitrary",), vmem_limit_bytes=VMEM_LIMIT),
        name="mixer_sample" if has_s0 else "mixer_prompt",
    )(*args)
    return outs


def _rope_tables(L):
    rows = L // GRID_W
    row = jnp.repeat(jnp.arange(rows, dtype=F32), GRID_W)
    col = jnp.tile(jnp.arange(GRID_W, dtype=F32), rows)
    half = RET_QK_DIM // 2
    inv = ROPE_BASE ** (-jnp.arange(0, half, 2, dtype=F32) / half)
    ang_r = row[:, None] * inv
    ang_c = col[:, None] * inv
    cos_h = jnp.concatenate([jnp.cos(ang_r), jnp.cos(ang_r), jnp.cos(ang_c), jnp.cos(ang_c)], axis=-1)
    sin_h = jnp.concatenate([-jnp.sin(ang_r), jnp.sin(ang_r), -jnp.sin(ang_c), jnp.sin(ang_c)], axis=-1)
    return jnp.tile(cos_h, (1, 2)), jnp.tile(sin_h, (1, 2))


def kernel(x_prompt, x_sample, state_ssd, state_ret, c, c_ctx, w_mod, b_mod, norm_pre_w, norm_post_w,
           w_in, conv_w, conv_b, ssd_A_log, ssd_dt_bias, ssd_D, ssd_norm_w, ret_decay, ret_norm_w, w_out):
    nb_s = x_sample.shape[0]
    l = 0

    n_dt = 2 * SSD_HEADS
    w_packed = _pack_call(jnp.swapaxes(w_in[l], 0, 1))
    mix_norm_w = jnp.concatenate([ssd_norm_w[l], ret_norm_w[l]]).reshape(MIX_WIDTH, 1)
    w_out_bf = _scale_rows_call(w_out[l], mix_norm_w)
    dtb = jnp.pad(ssd_dt_bias[l].reshape(1, n_dt), ((0, 0), (0, DT_PAD - n_dt)))
    nega = jnp.pad(-jnp.exp(ssd_A_log[l].reshape(1, n_dt)), ((0, 0), (0, DT_PAD - n_dt)))
    dsk = jnp.repeat(ssd_D[l], SSD_HEAD_DIM).reshape(1, SSD_WIDTH)
    lamx = jnp.broadcast_to(-jnp.exp(ret_decay[l].reshape(2 * RET_HEADS, 1)), (2 * RET_HEADS, LANES))
    npre = norm_pre_w[l].reshape(1, D_MODEL)
    npost = norm_post_w[l].reshape(1, D_MODEL)
    cb = conv_b[l].reshape(1, CONV_CH)
    cos_t, sin_t = _rope_tables(x_sample.shape[1])

    cond = jnp.concatenate([c, c_ctx[None, :], jnp.zeros((16 - nb_s - 1, D_MODEL), F32)], axis=0)
    mod = _mod_call(cond, w_mod[l], b_mod[l].reshape(1, 3 * D_MODEL))
    mod_s = mod[:nb_s].reshape(nb_s, 1, 3 * D_MODEL)
    mod_p = mod[nb_s:nb_s + 1].reshape(1, 1, 3 * D_MODEL)

    pp = _in_proj_call(x_prompt, mod_p, npre, w_packed, conv_w[l], cb, dtb, None, None)
    y_p, st_ssd_t, st_ret = _mixer_call(*pp, nega, dsk, lamx, w_out_bf, x_prompt, mod_p, npost,
                                        None, None, True)

    ps = _in_proj_call(x_sample, mod_s, npre, w_packed, conv_w[l], cb, dtb, cos_t, sin_t)
    (y_s,) = _mixer_call(*ps, nega, dsk, lamx, w_out_bf, x_sample, mod_s, npost,
                         jnp.swapaxes(state_ssd, -1, -2), state_ret, False)
    return (y_p, y_s, jnp.swapaxes(st_ssd_t, -1, -2), st_ret)
```

```python
import functools
import types

import jax
import jax.numpy as jnp
from jax import lax
from jax.experimental import pallas as pl
from jax.experimental.pallas import tpu as pltpu

F32 = jnp.float32
BF16 = jnp.bfloat16

D_MODEL = 1024
CHUNK = 128
GRID_W = 64
EPS = 1e-6
SSD_WIDTH = 1024
SSD_HEAD_DIM = 64
SSD_HEADS = 16
SSD_GROUPS = 4
SSD_STATE = 128
CONV_CH = SSD_WIDTH + 2 * SSD_GROUPS * SSD_STATE
RET_HEADS = 8
RET_QK_DIM = 64
RET_V_DIM = 128
RET_QK_WIDTH = RET_HEADS * RET_QK_DIM
RET_V_WIDTH = RET_HEADS * RET_V_DIM
MIX_WIDTH = SSD_WIDTH + RET_V_WIDTH
ROPE_BASE = 10000.0
LANES = 128
DT_PAD = LANES
GROUP_W = SSD_WIDTH // SSD_GROUPS
PAIR_W = 2 * RET_V_DIM
IN_PROJ_PIECE = 256
STREAMS = 2

OFF_Z = 0
OFF_XBC = OFF_Z + SSD_WIDTH
OFF_Q = OFF_XBC + CONV_CH
OFF_K = OFF_Q + RET_QK_WIDTH
OFF_V = OFF_K + RET_QK_WIDTH
OFF_G = OFF_V + RET_V_WIDTH
OFF_DT = OFF_G + RET_V_WIDTH
IN_COLS_PACKED = OFF_DT + DT_PAD

LOG2E = 1.4426950408889634
VMEM_LIMIT = 56 * 1024 * 1024


def _silu(x):
    return x * (1.0 / (1.0 + jnp.exp(-x)))


def _dot(a, b):
    return jnp.dot(a, b, preferred_element_type=F32)


def _dot_nt(a, b):
    return lax.dot_general(a, b, (((1,), (1,)), ((), ())), preferred_element_type=F32)


def _split3(x):
    hi = x.astype(BF16)
    r1 = x - hi.astype(F32)
    mid = r1.astype(BF16)
    lo = (r1 - mid.astype(F32)).astype(BF16)
    return hi, mid, lo


PACK_COLS = 512
PACK_STEPS = -(-IN_COLS_PACKED // PACK_COLS)


def _pack_kernel(wt_ref, out_ref):
    t = wt_ref[...].T
    lane = lax.broadcasted_iota(jnp.int32, t.shape, 1)
    is_dt = pl.program_id(0) == PACK_STEPS - 1
    t = jnp.where(jnp.logical_and(is_dt, lane >= 2 * SSD_HEADS), 0.0, t)
    out_ref[...] = t.astype(BF16)


def _pack_call(w_in_t):
    n_head = (OFF_Q - OFF_Z) // PACK_COLS
    src_dt = SSD_WIDTH + CONV_CH
    src_tail = src_dt + 2 * SSD_HEADS

    def src_row(i):
        t8 = jnp.where(i < n_head, i * (PACK_COLS // 8),
                       jnp.where(i < PACK_STEPS - 1, src_tail // 8 + (i - n_head) * (PACK_COLS // 8),
                                 src_dt // 8))
        return t8 * 8

    return pl.pallas_call(
        _pack_kernel,
        grid=(PACK_STEPS,),
        in_specs=[pl.BlockSpec((pl.Element(PACK_COLS), pl.Element(D_MODEL)), lambda i: (src_row(i), 0))],
        out_specs=pl.BlockSpec((D_MODEL, PACK_COLS), lambda i: (0, i)),
        out_shape=jax.ShapeDtypeStruct((D_MODEL, IN_COLS_PACKED), BF16),
        name="pack_w_in",
    )(w_in_t)


def _scale_rows_kernel(w_ref, s_ref, out_ref):
    out_ref[...] = (w_ref[...] * s_ref[...]).astype(BF16)


def _scale_rows_call(w, row_scale):
    rows, cols = w.shape
    tr = 512
    return pl.pallas_call(
        _scale_rows_kernel,
        grid=(rows // tr,),
        in_specs=[pl.BlockSpec((tr, cols), lambda i: (i, 0)), pl.BlockSpec((tr, 1), lambda i: (i, 0))],
        out_specs=pl.BlockSpec((tr, cols), lambda i: (i, 0)),
        out_shape=jax.ShapeDtypeStruct((rows, cols), BF16),
        name="fold_norm_w_out",
    )(w, row_scale)


def _mod_kernel(cond_ref, w_ref, b_ref, out_ref):
    a = _silu(cond_ref[...]).astype(BF16)
    out_ref[...] = _dot(a, w_ref[...].astype(BF16)) + b_ref[...]


def _mod_call(cond, w_mod, b_mod):
    rows = cond.shape[0]
    tn = 1024
    return pl.pallas_call(
        _mod_kernel,
        grid=(3 * D_MODEL // tn,),
        in_specs=[
            pl.BlockSpec((rows, D_MODEL), lambda j: (0, 0)),
            pl.BlockSpec((D_MODEL, tn), lambda j: (0, j)),
            pl.BlockSpec((1, tn), lambda j: (0, j)),
        ],
        out_specs=pl.BlockSpec((rows, tn), lambda j: (0, j)),
        out_shape=jax.ShapeDtypeStruct((rows, 3 * D_MODEL), F32),
        name="mod",
    )(cond, w_mod, b_mod)


def _modnorm(x, mod_ref, npw_ref):
    ms = jnp.mean(x * x, axis=-1, keepdims=True)
    shift = mod_ref[0, :, 0:D_MODEL]
    scale = mod_ref[0, :, D_MODEL:2 * D_MODEL]
    gain = npw_ref[...] * (1.0 + scale)
    return (x * lax.rsqrt(ms + EPS) * gain + shift).astype(BF16)


def _in_proj_kernel(*refs, tl, n_tiles, rope):
    if rope:
        (x_ref, xp_ref, xn_ref, mod_ref, npw_ref, w_ref, cw_ref, cb_ref, dtb_ref,
         cos_ref, sin_ref, zs_ref, xbc_ref, dt_ref, q_ref, k_ref, v_ref, gs_ref, h_ref) = refs
    else:
        (x_ref, xp_ref, xn_ref, mod_ref, npw_ref, w_ref, cw_ref, cb_ref, dtb_ref,
         zs_ref, xbc_ref, dt_ref, q_ref, k_ref, v_ref, gs_ref, h_ref) = refs
    t = pl.program_id(1)
    h_ref[...] = _modnorm(x_ref[0], mod_ref, npw_ref)
    piece = IN_PROJ_PIECE

    if n_tiles > 1:
        halo = jnp.concatenate([xp_ref[0], xn_ref[0]], axis=0)
        hh = _modnorm(halo, mod_ref, npw_ref)
        pv = jnp.where(t > 0, 1.0, 0.0).astype(F32)
        nv = jnp.where(t < n_tiles - 1, 1.0, 0.0).astype(F32)
    row = lax.broadcasted_iota(jnp.int32, (tl, piece), 0)

    def sec_xbc(c0):
        wcols = w_ref[:, OFF_XBC + c0:OFF_XBC + c0 + piece]
        acc = _dot(h_ref[...], wcols)
        up = pltpu.roll(acc, 1, axis=0)
        dn = pltpu.roll(acc, tl - 1, axis=0)
        if n_tiles > 1:
            uh = _dot(hh, wcols)
            up = jnp.where(row == 0, uh[7:8, :] * pv, up)
            dn = jnp.where(row == tl - 1, uh[8:9, :] * nv, dn)
        else:
            up = jnp.where(row == 0, 0.0, up)
            dn = jnp.where(row == tl - 1, 0.0, dn)
        y = (cw_ref[0:1, c0:c0 + piece] * up + cw_ref[1:2, c0:c0 + piece] * acc
             + cw_ref[2:3, c0:c0 + piece] * dn + cb_ref[:, c0:c0 + piece])
        xbc_ref[0, :, c0:c0 + piece] = _silu(y).astype(BF16)

    lane = lax.broadcasted_iota(jnp.int32, (tl, LANES), 1)
    first_half = (lane % 32) < 16

    def sec_qk(off, ref, scl, c0):
        acc = _dot(h_ref[...], w_ref[:, off + c0:off + c0 + piece]) * scl
        if rope:
            for s0 in range(0, piece, LANES):
                xs = acc[:, s0:s0 + LANES]
                partner = jnp.where(first_half, pltpu.roll(xs, LANES - 16, axis=1),
                                    pltpu.roll(xs, 16, axis=1))
                ref[0, :, c0 + s0:c0 + s0 + LANES] = (
                    xs * cos_ref[...] + partner * sin_ref[...]).astype(BF16)
        else:
            ref[0, :, c0:c0 + piece] = acc.astype(BF16)

    def sec_q(c0):
        sec_qk(OFF_Q, q_ref, 1.0, c0)

    def sec_k(c0):
        sec_qk(OFF_K, k_ref, RET_QK_DIM ** -0.5, c0)

    def sec_z(c0):
        acc = _dot(h_ref[...], w_ref[:, OFF_Z + c0:OFF_Z + c0 + piece])
        zs_ref[0, :, c0:c0 + piece] = _silu(acc).astype(BF16)

    def sec_g(c0):
        acc = _dot(h_ref[...], w_ref[:, OFF_G + c0:OFF_G + c0 + piece])
        gs_ref[0, :, c0:c0 + piece] = _silu(acc).astype(BF16)

    def sec_v(c0):
        acc = _dot(h_ref[...], w_ref[:, OFF_V + c0:OFF_V + c0 + piece])
        v_ref[0, :, c0:c0 + piece] = acc.astype(BF16)

    def sec_dt(c0):
        acc = _dot(h_ref[...], w_ref[:, OFF_DT:OFF_DT + DT_PAD]) + dtb_ref[...]
        dt_ref[0] = jnp.maximum(acc, 0.0) + jnp.log(1.0 + jnp.exp(-jnp.abs(acc)))

    def pieces(fn, width):
        return [(fn, c0) for c0 in range(0, width, piece)]

    conv = pieces(sec_xbc, CONV_CH)
    light = pieces(sec_v, RET_V_WIDTH) + [(sec_dt, 0)]
    order = pieces(sec_q, RET_QK_WIDTH) + pieces(sec_k, RET_QK_WIDTH)
    stride = -(-len(conv) // len(light))
    for i, item in enumerate(conv):
        order.append(item)
        if i % stride == stride - 1 and light:
            order.append(light.pop(0))
    order += light + pieces(sec_z, SSD_WIDTH) + pieces(sec_g, RET_V_WIDTH)
    for fn, c0 in order:
        fn(c0)


def _in_proj_call(x, mod, npw, w_packed, conv_w, conv_b, dtb, cos_t, sin_t):
    nb, L, _ = x.shape
    tl = min(L, 512)
    n_tiles = L // tl
    g8 = tl // 8
    rope = cos_t is not None
    per_seq_mod = mod.shape[0] > 1
    mod_map = (lambda s, t: (s, 0, 0)) if per_seq_mod else (lambda s, t: (0, 0, 0))
    const2 = lambda s, t: (0, 0)
    in_specs = [
        pl.BlockSpec((1, tl, D_MODEL), lambda s, t: (s, t, 0)),
        pl.BlockSpec((1, 8, D_MODEL), lambda s, t: (s, jnp.maximum(t * g8 - 1, 0), 0)),
        pl.BlockSpec((1, 8, D_MODEL), lambda s, t: (s, jnp.minimum((t + 1) * g8, L // 8 - 1), 0)),
        pl.BlockSpec((1, 1, 3 * D_MODEL), mod_map),
        pl.BlockSpec((1, D_MODEL), const2),
        pl.BlockSpec((D_MODEL, IN_COLS_PACKED), const2, pipeline_mode=pl.Buffered(1)),
        pl.BlockSpec((3, CONV_CH), const2),
        pl.BlockSpec((1, CONV_CH), const2),
        pl.BlockSpec((1, DT_PAD), const2),
    ]
    args = [x, x, x, mod, npw, w_packed, conv_w, conv_b, dtb]
    if rope:
        in_specs += [pl.BlockSpec((tl, LANES), lambda s, t: (t, 0)),
                     pl.BlockSpec((tl, LANES), lambda s, t: (t, 0))]
        args += [cos_t, sin_t]
    widths = (SSD_WIDTH, CONV_CH, DT_PAD, RET_QK_WIDTH, RET_QK_WIDTH, RET_V_WIDTH, RET_V_WIDTH)
    dtypes = (BF16, BF16, F32, BF16, BF16, BF16, BF16)
    out_specs = [pl.BlockSpec((1, tl, w), lambda s, t: (s, t, 0)) for w in widths]
    out_shape = [jax.ShapeDtypeStruct((nb, L, w), d) for w, d in zip(widths, dtypes)]
    return pl.pallas_call(
        functools.partial(_in_proj_kernel, tl=tl, n_tiles=n_tiles, rope=rope),
        grid=(nb, n_tiles),
        in_specs=in_specs,
        out_specs=out_specs,
        out_shape=out_shape,
        scratch_shapes=[pltpu.VMEM((tl, D_MODEL), BF16)],
        compiler_params=pltpu.CompilerParams(
            dimension_semantics=("parallel", "parallel"), vmem_limit_bytes=VMEM_LIMIT),
        name="in_proj_rope" if rope else "in_proj",
    )(*args)


def _mixer_kernel(*refs, n_streams, n_groups, nc, has_s0, emit_state, per_seq_mod):
    Q = CHUNK
    H = SSD_HEADS
    U = n_streams
    it = iter(refs)
    zs_ref, xbc_ref, dt_ref, q_ref, k_ref, v_ref, gs_ref = (next(it) for _ in range(7))
    nega_ref, dsk_ref, lam_ref = (next(it) for _ in range(3))
    wout_ref, xres_ref, modl_ref, npost_ref = (next(it) for _ in range(4))
    if has_s0:
        s0s_ref, s0r_ref = next(it), next(it)
    y_ref = next(it)
    if emit_state:
        os_ref, or_ref = next(it), next(it)
    (yas_ref, yar_ref, ss_ref, sr_ref, dsum_ref, ef_ref, eb_ref, tf_ref, tb_ref,
     ar_ref, exf_ref, exb_ref, cum_ref, mixs_ref) = (next(it) for _ in range(14))

    i = pl.program_id(0)
    n_real = n_groups * 2 * nc
    is_real = i < n_real
    ic = jnp.minimum(i, n_real - 1)
    s = ic % (2 * nc)
    phase2 = s >= nc
    c = jnp.where(phase2, 2 * nc - 1 - s, s)
    r0 = pl.multiple_of(c * Q, Q)
    slot_w = 1
    slot_r = 0

    rowq = lax.broadcasted_iota(jnp.int32, (Q, Q), 0)
    colq = lax.broadcasted_iota(jnp.int32, (Q, Q), 1)

    def out_proj_matmul():
        return _dot(mixs_ref[slot_r], wout_ref[...])

    def out_proj_finish(out):
        ms = jnp.mean(out * out, axis=-1, keepdims=True)
        o = out * lax.rsqrt(ms + EPS) * npost_ref[...]
        for u in range(U):
            gate = modl_ref[u if per_seq_mod else 0, :, 2 * D_MODEL:3 * D_MODEL]
            y_ref[u] = xres_ref[u] + gate * o[u * Q:(u + 1) * Q]

    @pl.when(i == 0)
    def _init_tables():
        mixs_ref[...] = jnp.zeros_like(mixs_ref)
        diff = (rowq - colq).astype(F32)
        rowf = lax.broadcasted_iota(jnp.int32, (Q, LANES), 0).astype(F32)
        colf = lax.broadcasted_iota(jnp.int32, (RET_QK_DIM, Q), 1).astype(F32)
        for h in range(RET_HEADS):
            pr, hh = divmod(h, 2)
            lf = lam_ref[h:h + 1, :]
            lb = lam_ref[RET_HEADS + h:RET_HEADS + h + 1, :]
            e = jnp.where(rowq >= colq, lf * diff, lb * (-diff))
            dsum_ref[pr, :, hh * Q:(hh + 1) * Q] = jnp.exp(e) * jnp.where(rowq == colq, 2.0, 1.0)
            hc = slice(hh * LANES, (hh + 1) * LANES)
            ef_ref[pr, :, hc] = jnp.exp(lf * (rowf + 1.0))
            eb_ref[pr, :, hc] = jnp.exp(lb * (Q - rowf))
            kr = slice(hh * RET_QK_DIM, (hh + 1) * RET_QK_DIM)
            tf_ref[pr, kr, :] = jnp.exp(lf * (Q - 1.0 - colf))
            tb_ref[pr, kr, :] = jnp.exp(lb * colf)
        ar_ref[...] = jnp.exp(lam_ref[...] * float(Q))
        er = lax.broadcasted_iota(jnp.int32, (LANES, SSD_WIDTH), 0)
        ec = lax.broadcasted_iota(jnp.int32, (LANES, SSD_WIDTH), 1) // SSD_HEAD_DIM
        exf_ref[...] = jnp.where(er == ec, 1.0, 0.0).astype(BF16)
        exb_ref[...] = jnp.where(er == ec + H, 1.0, 0.0).astype(BF16)

    def expand(w, a_row, ex_ref):
        a3 = jnp.concatenate(_split3(jnp.broadcast_to(a_row, (16, LANES))), axis=0)
        out = _dot(jnp.concatenate([w.astype(BF16), a3], axis=0), ex_ref[...])
        n = w.shape[0]
        return out[0:n], out[n:n + 1] + out[n + 16:n + 17] + out[n + 32:n + 33]

    def make_stream(u):
        zs, xbc, dtr, qr, kr_, vr, gsr = (r.at[u] for r in (zs_ref, xbc_ref, dt_ref, q_ref, k_ref,
                                                            v_ref, gs_ref))
        yas, yar, ss, sr, cums = (r.at[u] for r in (yas_ref, yar_ref, ss_ref, sr_ref, cum_ref))
        mix_rows = slice(u * Q, (u + 1) * Q)

        def init_state():
            if has_s0:
                for d in range(2):
                    for p in range(SSD_HEADS // 2):
                        pair_t = jnp.concatenate(
                            [s0s_ref[u, 0, d, 2 * p], s0s_ref[u, 0, d, 2 * p + 1]], axis=0)
                        ss[d, :, p * LANES:(p + 1) * LANES] = pair_t.T
                sr[...] = s0r_ref[u, 0]
            else:
                ss[...] = jnp.zeros_like(ss)
                sr[...] = jnp.zeros_like(sr)

        def ssd_state_update(d, wtail_x, a_x):
            for g in range(SSD_GROUPS):
                bgt = xbc[:, SSD_WIDTH + g * SSD_STATE:SSD_WIDTH + (g + 1) * SSD_STATE].T
                gc = slice(g * GROUP_W, (g + 1) * GROUP_W)
                xt = xbc[:, gc] * wtail_x[:, gc].astype(BF16)
                ss[d, :, gc] = ss[d, :, gc] * a_x[:, gc] + _dot(bgt, xt)

        def k_transposed(pr):
            return kr_[:, pr * LANES:(pr + 1) * LANES].T

        def ret_state_update(d, tail_ref, kts):
            for pr, kt in enumerate(kts):
                pc = slice(pr * PAIR_W, (pr + 1) * PAIR_W)
                ktt = (kt.astype(F32) * tail_ref[pr]).astype(BF16)
                ds = _dot(ktt, vr[:, pc])
                for hh in range(2):
                    h = 2 * pr + hh
                    a = ar_ref[d * RET_HEADS + h:d * RET_HEADS + h + 1, :]
                    sr[d, h] = (sr[d, h] * a
                                + ds[hh * RET_QK_DIM:(hh + 1) * RET_QK_DIM, hh * LANES:(hh + 1) * LANES])

        def emit_states(d):
            for p in range(SSD_HEADS // 2):
                pair_t = ss[d, :, p * LANES:(p + 1) * LANES].T
                os_ref[u, 0, d, 2 * p] = pair_t[0:SSD_HEAD_DIM]
                os_ref[u, 0, d, 2 * p + 1] = pair_t[SSD_HEAD_DIM:2 * SSD_HEAD_DIM]
            or_ref[u, 0, d] = sr[d]

        def ret_state_blockdiag(d, pr):
            z = jnp.zeros((RET_QK_DIM, RET_V_DIM), F32)
            top = jnp.concatenate([sr[d, 2 * pr], z], axis=1)
            bot = jnp.concatenate([z, sr[d, 2 * pr + 1]], axis=1)
            return jnp.concatenate([top, bot], axis=0).astype(BF16)

        def phase1():
            dt = dtr[...]
            la = dt * nega_ref[...]
            tl_bf = jnp.where(rowq >= colq, 1.0, 0.0).astype(BF16)
            c3 = _dot(tl_bf, jnp.concatenate(_split3(la), axis=1))
            cum = c3[:, 0:LANES] + c3[:, LANES:2 * LANES] + c3[:, 2 * LANES:3 * LANES]
            cums[pl.ds(r0, Q), :] = cum
            tot = cum[Q - 1:Q, :]
            rev = tot - cum + la
            yield

            sub_k = lax.broadcasted_iota(jnp.int32, (LANES, Q), 0) < RET_QK_DIM
            lane_v = lax.broadcasted_iota(jnp.int32, (Q, PAIR_W), 1) < RET_V_DIM
            kts = [k_transposed(pr) for pr in range(RET_HEADS // 2)]
            for pr, kt in enumerate(kts):
                qp = qr[:, pr * LANES:(pr + 1) * LANES]
                zk = jnp.zeros_like(kt)
                kbd = jnp.concatenate([jnp.where(sub_k, kt, zk), jnp.where(sub_k, zk, kt)], axis=1)
                a2 = _dot(qp, kbd)
                ad = (a2 * dsum_ref[pr]).astype(BF16)
                pc = slice(pr * PAIR_W, (pr + 1) * PAIR_W)
                vp = vr[:, pc]
                zv = jnp.zeros_like(vp)
                vbd = jnp.concatenate([jnp.where(lane_v, vp, zv), jnp.where(lane_v, zv, vp)], axis=0)
                y = _dot(ad, vbd) + _dot(qp, ret_state_blockdiag(0, pr)) * ef_ref[pr]
                yar[pl.ds(r0, Q), pc] = y
            ret_state_update(0, tf_ref, kts)
            yield

            gms, css = [], []
            for g in range(SSD_GROUPS):
                cg = xbc[:, SSD_WIDTH + 512 + g * SSD_STATE:SSD_WIDTH + 512 + (g + 1) * SSD_STATE]
                bg = xbc[:, SSD_WIDTH + g * SSD_STATE:SSD_WIDTH + (g + 1) * SSD_STATE]
                gms.append(_dot_nt(cg, bg))
                css.append(_dot(cg, ss[0, :, g * GROUP_W:(g + 1) * GROUP_W].astype(BF16)))
            yield

            dt_t = dt.T
            ldt_t = jnp.log(dt_t)
            sub = lax.broadcasted_iota(jnp.int32, (LANES, Q), 0)
            adj_t = (jnp.where(sub < H, cum.T, rev.T) - ldt_t) * LOG2E
            dg_t = jnp.log(dt_t[0:H, :] + dt_t[H:2 * H, :]) * LOG2E
            cum2 = cum * LOG2E
            rev2 = rev * LOG2E
            wtail_f = dt * jnp.exp(tot - cum)
            ew_x, af_x = expand(jnp.concatenate([jnp.exp(cum), wtail_f], axis=0), jnp.exp(tot), exf_ref)
            ecum_x = ew_x[0:Q]
            wtail_x = ew_x[Q:2 * Q]
            lt = rowq > colq
            gt = rowq < colq
            lane_g = lax.broadcasted_iota(jnp.int32, (Q, GROUP_W), 1) // SSD_HEAD_DIM
            yield

            for g in range(SSD_GROUPS):
                gc = slice(g * GROUP_W, (g + 1) * GROUP_W)
                xg = xbc[:, gc]
                ws, xs = [], []
                for j in range(4):
                    h = 4 * g + j
                    hb = H + h
                    arg = jnp.where(lt, cum2[:, h:h + 1] - adj_t[h:h + 1, :],
                                    jnp.where(gt, rev2[:, hb:hb + 1] - adj_t[hb:hb + 1, :], dg_t[h:h + 1, :]))
                    ws.append((gms[g] * jnp.exp2(arg)).astype(BF16))
                    xs.append(jnp.where(lane_g == j, xg, jnp.zeros_like(xg)))
                y = _dot(jnp.concatenate(ws, axis=1), jnp.concatenate(xs, axis=0))
                yas[pl.ds(r0, Q), gc] = y + css[g] * ecum_x[:, gc]
            yield
            ssd_state_update(0, wtail_x, af_x)

        def phase2():
            for pr in range(RET_HEADS // 2):
                qp = qr[:, pr * LANES:(pr + 1) * LANES]
                pc = slice(pr * PAIR_W, (pr + 1) * PAIR_W)
                y2 = yar[pl.ds(r0, Q), pc] + _dot(qp, ret_state_blockdiag(1, pr)) * eb_ref[pr]
                for hh in range(2):
                    h = 2 * pr + hh
                    hc = slice(h * LANES, (h + 1) * LANES)
                    y = y2[:, hh * LANES:(hh + 1) * LANES]
                    mu = jnp.mean(y, axis=-1, keepdims=True)
                    yc = y - mu
                    var = jnp.mean(yc * yc, axis=-1, keepdims=True)
                    yn = yc * lax.rsqrt(var + EPS) * gsr[:, hc].astype(F32)
                    mixs_ref[slot_w, mix_rows, SSD_WIDTH + h * LANES:SSD_WIDTH + (h + 1) * LANES] = (
                        yn.astype(BF16))
            ret_state_update(1, tb_ref, [k_transposed(pr) for pr in range(RET_HEADS // 2)])
            yield

            dt = dtr[...]
            cum = cums[pl.ds(r0, Q), :]
            rev = cum[Q - 1:Q, :] - cum + dt * nega_ref[...]
            first = rev[0:1, :]
            wtail_b = dt * jnp.exp(first - rev)
            ew_x, ab_x = expand(jnp.concatenate([jnp.exp(rev), wtail_b], axis=0), jnp.exp(first), exb_ref)
            erev_x = ew_x[0:Q]
            wtail_x = ew_x[Q:2 * Q]

            parts = []
            ssq = None
            for g in range(SSD_GROUPS):
                cg = xbc[:, SSD_WIDTH + 512 + g * SSD_STATE:SSD_WIDTH + 512 + (g + 1) * SSD_STATE]
                gc = slice(g * GROUP_W, (g + 1) * GROUP_W)
                cs = _dot(cg, ss[1, :, gc].astype(BF16))
                y = (yas[pl.ds(r0, Q), gc] + cs * erev_x[:, gc]
                     + dsk_ref[:, gc] * xbc[:, gc].astype(F32))
                y = y * zs[:, gc].astype(F32)
                parts.append(y)
                ssq = y * y if ssq is None else ssq + y * y
            yield "project"

            inv = lax.rsqrt(jnp.sum(ssq, axis=-1, keepdims=True) * (1.0 / SSD_WIDTH) + EPS)
            for g, y in enumerate(parts):
                gc = slice(g * GROUP_W, (g + 1) * GROUP_W)
                mixs_ref[slot_w, mix_rows, gc] = (y * inv).astype(BF16)
            ssd_state_update(1, wtail_x, ab_x)

        return types.SimpleNamespace(init_state=init_state, emit_states=emit_states, phase1=phase1,
                                     phase2=phase2)

    streams = [make_stream(u) for u in range(U)]

    def run_interleaved(stage_gens, project_after=None):
        proj = None
        live = list(stage_gens)
        rnd = 0
        while live:
            marks = []
            nxt = []
            for gen in live:
                try:
                    marks.append(next(gen))
                    nxt.append(gen)
                except StopIteration:
                    pass
            if (project_after == rnd) or ("project" in marks):
                proj = out_proj_matmul()
            live = nxt
            rnd += 1
        return proj

    @pl.when(is_real & (s == 0))
    def _init_state():
        for st in streams:
            st.init_state()

    def phase1_all(with_out_proj):
        proj = run_interleaved([st.phase1() for st in streams], project_after=2 if with_out_proj else None)
        if with_out_proj:
            out_proj_finish(proj)
        if emit_state:
            @pl.when(s == nc - 1)
            def _emit_fwd():
                for st in streams:
                    st.emit_states(0)

    @pl.when(is_real & (s == 0))
    def _phase1_first():
        phase1_all(True)

    if nc > 1:
        @pl.when(is_real & (s > 0) & (s < nc))
        def _phase1_rest():
            phase1_all(False)

    @pl.when(jnp.logical_not(is_real))
    def _drain():
        out_proj_finish(out_proj_matmul())

    @pl.when(is_real & phase2)
    def _phase2():
        proj = run_interleaved([st.phase2() for st in streams])
        out_proj_finish(proj)
        mixs_ref[slot_r] = mixs_ref[slot_w]
        if emit_state:
            @pl.when(s == 2 * nc - 1)
            def _emit_bwd():
                for st in streams:
                    st.emit_states(1)


def _mixer_call(zs, xbc, dt, q, k, v, gs, nega, dsk, lamx, w_out_bf, x, mod, npost, s0_ssd, s0_ret,
                emit_state):
    nb, L, _ = zs.shape
    nc = L // CHUNK
    Q = CHUNK
    U = STREAMS
    assert nb % U == 0
    has_s0 = s0_ssd is not None
    n_groups = nb // U
    steps = 2 * nc
    n_real = n_groups * steps
    per_seq_mod = mod.shape[0] > 1

    def cur(i):
        ic = jnp.minimum(i, n_real - 1)
        return ic // steps, ic % steps

    def chunk_map(i):
        b, s = cur(i)
        return (b, jnp.where(s < nc, s, steps - 1 - s), 0)

    def phase2_map(i):
        b, s = cur(i)
        return (b, jnp.where(s < nc, nc - 1, steps - 1 - s), 0)

    def lag(i):
        j = jnp.maximum(i - 1, 0)
        return j // steps, j % steps

    def lag_map(i):
        bl, sl = lag(i)
        return (bl, jnp.where(sl < nc, nc - 1, steps - 1 - sl), 0)

    def lag_mod_map(i):
        bl, _ = lag(i)
        return (bl if per_seq_mod else 0, 0, 0)

    const2 = lambda i: (0, 0)
    in_specs = [
        pl.BlockSpec((U, Q, SSD_WIDTH), phase2_map),
        pl.BlockSpec((U, Q, CONV_CH), chunk_map),
        pl.BlockSpec((U, Q, DT_PAD), chunk_map),
        pl.BlockSpec((U, Q, RET_QK_WIDTH), chunk_map),
        pl.BlockSpec((U, Q, RET_QK_WIDTH), chunk_map),
        pl.BlockSpec((U, Q, RET_V_WIDTH), chunk_map),
        pl.BlockSpec((U, Q, RET_V_WIDTH), phase2_map),
        pl.BlockSpec((1, DT_PAD), const2),
        pl.BlockSpec((1, SSD_WIDTH), const2),
        pl.BlockSpec((2 * RET_HEADS, LANES), const2),
        pl.BlockSpec((MIX_WIDTH, D_MODEL), const2, pipeline_mode=pl.Buffered(1)),
        pl.BlockSpec((U, Q, D_MODEL), lag_map),
        pl.BlockSpec((U if per_seq_mod else 1, 1, 3 * D_MODEL), lag_mod_map),
        pl.BlockSpec((1, D_MODEL), const2),
    ]
    args = [zs, xbc, dt, q, k, v, gs, nega, dsk, lamx, w_out_bf, x, mod, npost]
    state_map = lambda i: (cur(i)[0], 0, 0, 0, 0, 0)
    ssd_state_block = (U, 1, 2, SSD_HEADS, SSD_HEAD_DIM, SSD_STATE)
    ret_state_block = (U, 1, 2, RET_HEADS, RET_QK_DIM, RET_V_DIM)
    if has_s0:
        in_specs += [pl.BlockSpec(ssd_state_block, state_map, pipeline_mode=pl.Buffered(1)),
                     pl.BlockSpec(ret_state_block, state_map, pipeline_mode=pl.Buffered(1))]
        args += [s0_ssd, s0_ret]
    out_specs = [pl.BlockSpec((U, Q, D_MODEL), lag_map)]
    out_shape = [jax.ShapeDtypeStruct((nb, L, D_MODEL), F32)]
    if emit_state:
        out_specs += [pl.BlockSpec(ssd_state_block, state_map), pl.BlockSpec(ret_state_block, state_map)]
        out_shape += [jax.ShapeDtypeStruct((nb,) + ssd_state_block[1:], F32),
                      jax.ShapeDtypeStruct((nb,) + ret_state_block[1:], F32)]
    n_pairs = RET_HEADS // 2
    scratch = [
        pltpu.VMEM((U, L, SSD_WIDTH), F32),
        pltpu.VMEM((U, L, RET_V_WIDTH), F32),
        pltpu.VMEM((U, 2, SSD_STATE, SSD_WIDTH), F32),
        pltpu.VMEM((U, 2, RET_HEADS, RET_QK_DIM, RET_V_DIM), F32),
        pltpu.VMEM((n_pairs, Q, 2 * Q), F32),
        pltpu.VMEM((n_pairs, Q, PAIR_W), F32),
        pltpu.VMEM((n_pairs, Q, PAIR_W), F32),
        pltpu.VMEM((n_pairs, LANES, Q), F32),
        pltpu.VMEM((n_pairs, LANES, Q), F32),
        pltpu.VMEM((2 * RET_HEADS, LANES), F32),
        pltpu.VMEM((LANES, SSD_WIDTH), BF16),
        pltpu.VMEM((LANES, SSD_WIDTH), BF16),
        pltpu.VMEM((U, L, DT_PAD), F32),
        pltpu.VMEM((2, U * Q, MIX_WIDTH), BF16),
    ]
    outs = pl.pallas_call(
        functools.partial(_mixer_kernel, n_streams=U, n_groups=n_groups, nc=nc, has_s0=has_s0,
                          emit_state=emit_state,
                          per_seq_mod=per_seq_mod),
        grid=(n_real + 1,),
        in_specs=in_specs,
        out_specs=out_specs,
        out_shape=out_shape,
        scratch_shapes=scratch,
        compiler_params=pltpu.CompilerParams(
            dimension_semantics=("arbitrary",), vmem_limit_bytes=VMEM_LIMIT),
        name="mixer_sample" if has_s0 else "mixer_prompt",
    )(*args)
    return outs


def _rope_tables(L):
    rows = L // GRID_W
    row = jnp.repeat(jnp.arange(rows, dtype=F32), GRID_W)
    col = jnp.tile(jnp.arange(GRID_W, dtype=F32), rows)
    half = RET_QK_DIM // 2
    inv = ROPE_BASE ** (-jnp.arange(0, half, 2, dtype=F32) / half)
    ang_r = row[:, None] * inv
    ang_c = col[:, None] * inv
    cos_h = jnp.concatenate([jnp.cos(ang_r), jnp.cos(ang_r), jnp.cos(ang_c), jnp.cos(ang_c)], axis=-1)
    sin_h = jnp.concatenate([-jnp.sin(ang_r), jnp.sin(ang_r), -jnp.sin(ang_c), jnp.sin(ang_c)], axis=-1)
    return jnp.tile(cos_h, (1, 2)), jnp.tile(sin_h, (1, 2))


def kernel(x_prompt, x_sample, state_ssd, state_ret, c, c_ctx, w_mod, b_mod, norm_pre_w, norm_post_w,
           w_in, conv_w, conv_b, ssd_A_log, ssd_dt_bias, ssd_D, ssd_norm_w, ret_decay, ret_norm_w, w_out):
    nb_s = x_sample.shape[0]
    l = 0

    n_dt = 2 * SSD_HEADS
    w_packed = _pack_call(jnp.swapaxes(w_in[l], 0, 1))
    mix_norm_w = jnp.concatenate([ssd_norm_w[l], ret_norm_w[l]]).reshape(MIX_WIDTH, 1)
    w_out_bf = _scale_rows_call(w_out[l], mix_norm_w)
    dtb = jnp.pad(ssd_dt_bias[l].reshape(1, n_dt), ((0, 0), (0, DT_PAD - n_dt)))
    nega = jnp.pad(-jnp.exp(ssd_A_log[l].reshape(1, n_dt)), ((0, 0), (0, DT_PAD - n_dt)))
    dsk = jnp.repeat(ssd_D[l], SSD_HEAD_DIM).reshape(1, SSD_WIDTH)
    lamx = jnp.broadcast_to(-jnp.exp(ret_decay[l].reshape(2 * RET_HEADS, 1)), (2 * RET_HEADS, LANES))
    npre = norm_pre_w[l].reshape(1, D_MODEL)
    npost = norm_post_w[l].reshape(1, D_MODEL)
    cb = conv_b[l].reshape(1, CONV_CH)
    cos_t, sin_t = _rope_tables(x_sample.shape[1])

    cond = jnp.concatenate([c, c_ctx[None, :], jnp.zeros((16 - nb_s - 1, D_MODEL), F32)], axis=0)
    mod = _mod_call(cond, w_mod[l], b_mod[l].reshape(1, 3 * D_MODEL))
    mod_s = mod[:nb_s].reshape(nb_s, 1, 3 * D_MODEL)
    mod_p = mod[nb_s:nb_s + 1].reshape(1, 1, 3 * D_MODEL)

    pp = _in_proj_call(x_prompt, mod_p, npre, w_packed, conv_w[l], cb, dtb, None, None)
    y_p, st_ssd_t, st_ret = _mixer_call(*pp, nega, dsk, lamx, w_out_bf, x_prompt, mod_p, npost,
                                        None, None, True)

    ps = _in_proj_call(x_sample, mod_s, npre, w_packed, conv_w[l], cb, dtb, cos_t, sin_t)
    (y_s,) = _mixer_call(*ps, nega, dsk, lamx, w_out_bf, x_sample, mod_s, npost,
                         jnp.swapaxes(state_ssd, -1, -2), state_ret, False)
    return (y_p, y_s, jnp.swapaxes(st_ssd_t, -1, -2), st_ret)
```

```python
import functools
import types

import jax
import jax.numpy as jnp
from jax import lax
from jax.experimental import pallas as pl
from jax.experimental.pallas import tpu as pltpu

F32 = jnp.float32
BF16 = jnp.bfloat16

D_MODEL = 1024
CHUNK = 128
GRID_W = 64
EPS = 1e-6
SSD_WIDTH = 1024
SSD_HEAD_DIM = 64
SSD_HEADS = 16
SSD_GROUPS = 4
SSD_STATE = 128
CONV_CH = SSD_WIDTH + 2 * SSD_GROUPS * SSD_STATE
RET_HEADS = 8
RET_QK_DIM = 64
RET_V_DIM = 128
RET_QK_WIDTH = RET_HEADS * RET_QK_DIM
RET_V_WIDTH = RET_HEADS * RET_V_DIM
MIX_WIDTH = SSD_WIDTH + RET_V_WIDTH
ROPE_BASE = 10000.0
LANES = 128
DT_PAD = LANES
GROUP_W = SSD_WIDTH // SSD_GROUPS
PAIR_W = 2 * RET_V_DIM
IN_PROJ_PIECE = 256
IN_PROJ_ROWS = 1024
STREAMS = 2

OFF_Z = 0
OFF_XBC = OFF_Z + SSD_WIDTH
OFF_Q = OFF_XBC + CONV_CH
OFF_K = OFF_Q + RET_QK_WIDTH
OFF_V = OFF_K + RET_QK_WIDTH
OFF_G = OFF_V + RET_V_WIDTH
OFF_DT = OFF_G + RET_V_WIDTH
IN_COLS_PACKED = OFF_DT + DT_PAD

LOG2E = 1.4426950408889634
VMEM_LIMIT = 56 * 1024 * 1024


def _silu(x):
    return x * (1.0 / (1.0 + jnp.exp(-x)))


def _dot(a, b):
    return jnp.dot(a, b, preferred_element_type=F32)


def _dot_nt(a, b):
    return lax.dot_general(a, b, (((1,), (1,)), ((), ())), preferred_element_type=F32)


def _split3(x):
    hi = x.astype(BF16)
    r1 = x - hi.astype(F32)
    mid = r1.astype(BF16)
    lo = (r1 - mid.astype(F32)).astype(BF16)
    return hi, mid, lo


PACK_COLS = 512
PACK_STEPS = -(-IN_COLS_PACKED // PACK_COLS)


def _pack_kernel(wt_ref, out_ref):
    t = wt_ref[...].T
    lane = lax.broadcasted_iota(jnp.int32, t.shape, 1)
    is_dt = pl.program_id(0) == PACK_STEPS - 1
    t = jnp.where(jnp.logical_and(is_dt, lane >= 2 * SSD_HEADS), 0.0, t)
    out_ref[...] = t.astype(BF16)


def _pack_call(w_in_t):
    n_head = (OFF_Q - OFF_Z) // PACK_COLS
    src_dt = SSD_WIDTH + CONV_CH
    src_tail = src_dt + 2 * SSD_HEADS

    def src_row(i):
        t8 = jnp.where(i < n_head, i * (PACK_COLS // 8),
                       jnp.where(i < PACK_STEPS - 1, src_tail // 8 + (i - n_head) * (PACK_COLS // 8),
                                 src_dt // 8))
        return t8 * 8

    return pl.pallas_call(
        _pack_kernel,
        grid=(PACK_STEPS,),
        in_specs=[pl.BlockSpec((pl.Element(PACK_COLS), pl.Element(D_MODEL)), lambda i: (src_row(i), 0))],
        out_specs=pl.BlockSpec((D_MODEL, PACK_COLS), lambda i: (0, i)),
        out_shape=jax.ShapeDtypeStruct((D_MODEL, IN_COLS_PACKED), BF16),
        name="pack_w_in",
    )(w_in_t)


def _scale_rows_kernel(w_ref, s_ref, out_ref):
    out_ref[...] = (w_ref[...] * s_ref[...]).astype(BF16)


def _scale_rows_call(w, row_scale):
    rows, cols = w.shape
    tr = 512
    return pl.pallas_call(
        _scale_rows_kernel,
        grid=(rows // tr,),
        in_specs=[pl.BlockSpec((tr, cols), lambda i: (i, 0)), pl.BlockSpec((tr, 1), lambda i: (i, 0))],
        out_specs=pl.BlockSpec((tr, cols), lambda i: (i, 0)),
        out_shape=jax.ShapeDtypeStruct((rows, cols), BF16),
        name="fold_norm_w_out",
    )(w, row_scale)


def _mod_kernel(cond_ref, w_ref, b_ref, out_ref):
    a = _silu(cond_ref[...]).astype(BF16)
    out_ref[...] = _dot(a, w_ref[...].astype(BF16)) + b_ref[...]


def _mod_call(cond, w_mod, b_mod):
    rows = cond.shape[0]
    tn = 1024
    return pl.pallas_call(
        _mod_kernel,
        grid=(3 * D_MODEL // tn,),
        in_specs=[
            pl.BlockSpec((rows, D_MODEL), lambda j: (0, 0)),
            pl.BlockSpec((D_MODEL, tn), lambda j: (0, j)),
            pl.BlockSpec((1, tn), lambda j: (0, j)),
        ],
        out_specs=pl.BlockSpec((rows, tn), lambda j: (0, j)),
        out_shape=jax.ShapeDtypeStruct((rows, 3 * D_MODEL), F32),
        name="mod",
    )(cond, w_mod, b_mod)


def _modnorm(x, mod_ref, npw_ref):
    ms = jnp.mean(x * x, axis=-1, keepdims=True)
    shift = mod_ref[0, :, 0:D_MODEL]
    scale = mod_ref[0, :, D_MODEL:2 * D_MODEL]
    gain = npw_ref[...] * (1.0 + scale)
    return (x * lax.rsqrt(ms + EPS) * gain + shift).astype(BF16)


def _in_proj_kernel(*refs, seq_len, n_seq, rope):
    if rope:
        (x_ref, mod_ref, npw_ref, w_ref, cw_ref, cb_ref, dtb_ref,
         cos_ref, sin_ref, zs_ref, xbc_ref, dt_ref, q_ref, k_ref, v_ref, gs_ref, h_ref) = refs
    else:
        (x_ref, mod_ref, npw_ref, w_ref, cw_ref, cb_ref, dtb_ref,
         zs_ref, xbc_ref, dt_ref, q_ref, k_ref, v_ref, gs_ref, h_ref) = refs
    rows = n_seq * seq_len
    h_ref[...] = _modnorm(x_ref[...].reshape(rows, D_MODEL), mod_ref, npw_ref)
    piece = IN_PROJ_PIECE

    def put(ref, cols, val):
        for s in range(n_seq):
            ref[s, :, cols] = val[s * seq_len:(s + 1) * seq_len]

    row_in_seq = lax.broadcasted_iota(jnp.int32, (rows, piece), 0) % seq_len
    seq_first = row_in_seq == 0
    seq_last = row_in_seq == seq_len - 1

    def sec_xbc(c0):
        cols = slice(c0, c0 + piece)
        acc = _dot(h_ref[...], w_ref[:, OFF_XBC + c0:OFF_XBC + c0 + piece])
        up = jnp.where(seq_first, 0.0, pltpu.roll(acc, 1, axis=0))
        dn = jnp.where(seq_last, 0.0, pltpu.roll(acc, rows - 1, axis=0))
        y = cw_ref[0:1, cols] * up + cw_ref[1:2, cols] * acc + cw_ref[2:3, cols] * dn + cb_ref[:, cols]
        put(xbc_ref, cols, _silu(y).astype(BF16))

    lane = lax.broadcasted_iota(jnp.int32, (rows, LANES), 1)
    first_half = (lane % 32) < 16

    def sec_qk(off, ref, scl, c0):
        acc = _dot(h_ref[...], w_ref[:, off + c0:off + c0 + piece]) * scl
        if rope:
            for s0 in range(0, piece, LANES):
                xs = acc[:, s0:s0 + LANES]
                partner = jnp.where(first_half, pltpu.roll(xs, LANES - 16, axis=1),
                                    pltpu.roll(xs, 16, axis=1))
                put(ref, slice(c0 + s0, c0 + s0 + LANES),
                    (xs * cos_ref[...] + partner * sin_ref[...]).astype(BF16))
        else:
            put(ref, slice(c0, c0 + piece), acc.astype(BF16))

    def sec_q(c0):
        sec_qk(OFF_Q, q_ref, 1.0, c0)

    def sec_k(c0):
        sec_qk(OFF_K, k_ref, RET_QK_DIM ** -0.5, c0)

    def sec_z(c0):
        acc = _dot(h_ref[...], w_ref[:, OFF_Z + c0:OFF_Z + c0 + piece])
        put(zs_ref, slice(c0, c0 + piece), _silu(acc).astype(BF16))

    def sec_g(c0):
        acc = _dot(h_ref[...], w_ref[:, OFF_G + c0:OFF_G + c0 + piece])
        put(gs_ref, slice(c0, c0 + piece), _silu(acc).astype(BF16))

    def sec_v(c0):
        acc = _dot(h_ref[...], w_ref[:, OFF_V + c0:OFF_V + c0 + piece])
        put(v_ref, slice(c0, c0 + piece), acc.astype(BF16))

    def sec_dt(c0):
        acc = _dot(h_ref[...], w_ref[:, OFF_DT:OFF_DT + DT_PAD]) + dtb_ref[...]
        put(dt_ref, slice(0, DT_PAD), jnp.maximum(acc, 0.0) + jnp.log(1.0 + jnp.exp(-jnp.abs(acc))))

    def pieces(fn, width):
        return [(fn, c0) for c0 in range(0, width, piece)]

    conv = pieces(sec_xbc, CONV_CH)
    light = pieces(sec_v, RET_V_WIDTH) + [(sec_dt, 0)]
    order = pieces(sec_q, RET_QK_WIDTH) + pieces(sec_k, RET_QK_WIDTH)
    stride = -(-len(conv) // len(light))
    for i, item in enumerate(conv):
        order.append(item)
        if i % stride == stride - 1 and light:
            order.append(light.pop(0))
    order += light + pieces(sec_z, SSD_WIDTH) + pieces(sec_g, RET_V_WIDTH)
    for fn, c0 in order:
        fn(c0)


def _in_proj_call(x, mod, npw, w_packed, conv_w, conv_b, dtb, cos_t, sin_t):
    nb, L, _ = x.shape
    rope = cos_t is not None
    per_seq_mod = mod.shape[0] > 1
    n_seq = 1 if (per_seq_mod or rope) else max(1, IN_PROJ_ROWS // L)
    assert L <= IN_PROJ_ROWS and nb % n_seq == 0
    rows = n_seq * L
    mod_map = (lambda s: (s, 0, 0)) if per_seq_mod else (lambda s: (0, 0, 0))
    const2 = lambda s: (0, 0)
    in_specs = [
        pl.BlockSpec((n_seq, L, D_MODEL), lambda s: (s, 0, 0)),
        pl.BlockSpec((1, 1, 3 * D_MODEL), mod_map),
        pl.BlockSpec((1, D_MODEL), const2),
        pl.BlockSpec((D_MODEL, IN_COLS_PACKED), const2, pipeline_mode=pl.Buffered(1)),
        pl.BlockSpec((3, CONV_CH), const2),
        pl.BlockSpec((1, CONV_CH), const2),
        pl.BlockSpec((1, DT_PAD), const2),
    ]
    args = [x, mod, npw, w_packed, conv_w, conv_b, dtb]
    if rope:
        in_specs += [pl.BlockSpec((L, LANES), const2), pl.BlockSpec((L, LANES), const2)]
        args += [cos_t, sin_t]
    widths = (SSD_WIDTH, CONV_CH, DT_PAD, RET_QK_WIDTH, RET_QK_WIDTH, RET_V_WIDTH, RET_V_WIDTH)
    dtypes = (BF16, BF16, F32, BF16, BF16, BF16, BF16)
    out_specs = [pl.BlockSpec((n_seq, L, w), lambda s: (s, 0, 0)) for w in widths]
    out_shape = [jax.ShapeDtypeStruct((nb, L, w), d) for w, d in zip(widths, dtypes)]
    return pl.pallas_call(
        functools.partial(_in_proj_kernel, seq_len=L, n_seq=n_seq, rope=rope),
        grid=(nb // n_seq,),
        in_specs=in_specs,
        out_specs=out_specs,
        out_shape=out_shape,
        scratch_shapes=[pltpu.VMEM((rows, D_MODEL), BF16)],
        compiler_params=pltpu.CompilerParams(
            dimension_semantics=("parallel",), vmem_limit_bytes=VMEM_LIMIT),
        name="in_proj_rope" if rope else "in_proj",
    )(*args)


def _mixer_kernel(*refs, n_streams, n_groups, nc, has_s0, emit_state, per_seq_mod):
    Q = CHUNK
    H = SSD_HEADS
    U = n_streams
    it = iter(refs)
    zs_ref, xbc_ref, dt_ref, q_ref, k_ref, v_ref, gs_ref = (next(it) for _ in range(7))
    nega_ref, dsk_ref, lam_ref = (next(it) for _ in range(3))
    wout_ref, xres_ref, modl_ref, npost_ref = (next(it) for _ in range(4))
    if has_s0:
        s0s_ref, s0r_ref = next(it), next(it)
    y_ref = next(it)
    if emit_state:
        os_ref, or_ref = next(it), next(it)
    (yas_ref, yar_ref, ss_ref, sr_ref, dsum_ref, ef_ref, eb_ref, tf_ref, tb_ref,
     ar_ref, exf_ref, exb_ref, cum_ref, mixs_ref) = (next(it) for _ in range(14))

    i = pl.program_id(0)
    n_real = n_groups * 2 * nc
    is_real = i < n_real
    ic = jnp.minimum(i, n_real - 1)
    s = ic % (2 * nc)
    phase2 = s >= nc
    c = jnp.where(phase2, 2 * nc - 1 - s, s)
    r0 = pl.multiple_of(c * Q, Q)
    slot_w = 1
    slot_r = 0

    rowq = lax.broadcasted_iota(jnp.int32, (Q, Q), 0)
    colq = lax.broadcasted_iota(jnp.int32, (Q, Q), 1)

    def out_proj_matmul():
        return _dot(mixs_ref[slot_r], wout_ref[...])

    def out_proj_finish(out):
        ms = jnp.mean(out * out, axis=-1, keepdims=True)
        o = out * lax.rsqrt(ms + EPS) * npost_ref[...]
        for u in range(U):
            gate = modl_ref[u if per_seq_mod else 0, :, 2 * D_MODEL:3 * D_MODEL]
            y_ref[u] = xres_ref[u] + gate * o[u * Q:(u + 1) * Q]

    @pl.when(i == 0)
    def _init_tables():
        mixs_ref[...] = jnp.zeros_like(mixs_ref)
        diff = (rowq - colq).astype(F32)
        rowf = lax.broadcasted_iota(jnp.int32, (Q, LANES), 0).astype(F32)
        colf = lax.broadcasted_iota(jnp.int32, (RET_QK_DIM, Q), 1).astype(F32)
        for h in range(RET_HEADS):
            pr, hh = divmod(h, 2)
            lf = lam_ref[h:h + 1, :]
            lb = lam_ref[RET_HEADS + h:RET_HEADS + h + 1, :]
            e = jnp.where(rowq >= colq, lf * diff, lb * (-diff))
            dsum_ref[pr, :, hh * Q:(hh + 1) * Q] = jnp.exp(e) * jnp.where(rowq == colq, 2.0, 1.0)
            hc = slice(hh * LANES, (hh + 1) * LANES)
            ef_ref[pr, :, hc] = jnp.exp(lf * (rowf + 1.0))
            eb_ref[pr, :, hc] = jnp.exp(lb * (Q - rowf))
            kr = slice(hh * RET_QK_DIM, (hh + 1) * RET_QK_DIM)
            tf_ref[pr, kr, :] = jnp.exp(lf * (Q - 1.0 - colf))
            tb_ref[pr, kr, :] = jnp.exp(lb * colf)
        ar_ref[...] = jnp.exp(lam_ref[...] * float(Q))
        er = lax.broadcasted_iota(jnp.int32, (LANES, SSD_WIDTH), 0)
        ec = lax.broadcasted_iota(jnp.int32, (LANES, SSD_WIDTH), 1) // SSD_HEAD_DIM
        exf_ref[...] = jnp.where(er == ec, 1.0, 0.0).astype(BF16)
        exb_ref[...] = jnp.where(er == ec + H, 1.0, 0.0).astype(BF16)

    def expand(w, a_row, ex_ref):
        a3 = jnp.concatenate(_split3(jnp.broadcast_to(a_row, (16, LANES))), axis=0)
        out = _dot(jnp.concatenate([w.astype(BF16), a3], axis=0), ex_ref[...])
        n = w.shape[0]
        return out[0:n], out[n:n + 1] + out[n + 16:n + 17] + out[n + 32:n + 33]

    def make_stream(u):
        zs, xbc, dtr, qr, kr_, vr, gsr = (r.at[u] for r in (zs_ref, xbc_ref, dt_ref, q_ref, k_ref,
                                                            v_ref, gs_ref))
        yas, yar, ss, sr, cums = (r.at[u] for r in (yas_ref, yar_ref, ss_ref, sr_ref, cum_ref))
        mix_rows = slice(u * Q, (u + 1) * Q)

        def init_state():
            if has_s0:
                for d in range(2):
                    for p in range(SSD_HEADS // 2):
                        pair_t = jnp.concatenate(
                            [s0s_ref[u, 0, d, 2 * p], s0s_ref[u, 0, d, 2 * p + 1]], axis=0)
                        ss[d, :, p * LANES:(p + 1) * LANES] = pair_t.T
                sr[...] = s0r_ref[u, 0]
            else:
                ss[...] = jnp.zeros_like(ss)
                sr[...] = jnp.zeros_like(sr)

        def ssd_state_update(d, wtail_x, a_x):
            for g in range(SSD_GROUPS):
                bgt = xbc[:, SSD_WIDTH + g * SSD_STATE:SSD_WIDTH + (g + 1) * SSD_STATE].T
                gc = slice(g * GROUP_W, (g + 1) * GROUP_W)
                xt = xbc[:, gc] * wtail_x[:, gc].astype(BF16)
                ss[d, :, gc] = ss[d, :, gc] * a_x[:, gc] + _dot(bgt, xt)

        def k_transposed(pr):
            return kr_[:, pr * LANES:(pr + 1) * LANES].T

        def ret_state_update(d, tail_ref, kts):
            for pr, kt in enumerate(kts):
                pc = slice(pr * PAIR_W, (pr + 1) * PAIR_W)
                ktt = (kt.astype(F32) * tail_ref[pr]).astype(BF16)
                ds = _dot(ktt, vr[:, pc])
                for hh in range(2):
                    h = 2 * pr + hh
                    a = ar_ref[d * RET_HEADS + h:d * RET_HEADS + h + 1, :]
                    sr[d, h] = (sr[d, h] * a
                                + ds[hh * RET_QK_DIM:(hh + 1) * RET_QK_DIM, hh * LANES:(hh + 1) * LANES])

        def emit_states(d):
            for p in range(SSD_HEADS // 2):
                pair_t = ss[d, :, p * LANES:(p + 1) * LANES].T
                os_ref[u, 0, d, 2 * p] = pair_t[0:SSD_HEAD_DIM]
                os_ref[u, 0, d, 2 * p + 1] = pair_t[SSD_HEAD_DIM:2 * SSD_HEAD_DIM]
            or_ref[u, 0, d] = sr[d]

        def ret_state_blockdiag(d, pr):
            z = jnp.zeros((RET_QK_DIM, RET_V_DIM), F32)
            top = jnp.concatenate([sr[d, 2 * pr], z], axis=1)
            bot = jnp.concatenate([z, sr[d, 2 * pr + 1]], axis=1)
            return jnp.concatenate([top, bot], axis=0).astype(BF16)

        def phase1():
            dt = dtr[...]
            la = dt * nega_ref[...]
            tl_bf = jnp.where(rowq >= colq, 1.0, 0.0).astype(BF16)
            c3 = _dot(tl_bf, jnp.concatenate(_split3(la), axis=1))
            cum = c3[:, 0:LANES] + c3[:, LANES:2 * LANES] + c3[:, 2 * LANES:3 * LANES]
            cums[pl.ds(r0, Q), :] = cum
            tot = cum[Q - 1:Q, :]
            rev = tot - cum + la
            yield

            sub_k = lax.broadcasted_iota(jnp.int32, (LANES, Q), 0) < RET_QK_DIM
            lane_v = lax.broadcasted_iota(jnp.int32, (Q, PAIR_W), 1) < RET_V_DIM
            kts = [k_transposed(pr) for pr in range(RET_HEADS // 2)]
            for pr, kt in enumerate(kts):
                qp = qr[:, pr * LANES:(pr + 1) * LANES]
                zk = jnp.zeros_like(kt)
                kbd = jnp.concatenate([jnp.where(sub_k, kt, zk), jnp.where(sub_k, zk, kt)], axis=1)
                a2 = _dot(qp, kbd)
                ad = (a2 * dsum_ref[pr]).astype(BF16)
                pc = slice(pr * PAIR_W, (pr + 1) * PAIR_W)
                vp = vr[:, pc]
                zv = jnp.zeros_like(vp)
                vbd = jnp.concatenate([jnp.where(lane_v, vp, zv), jnp.where(lane_v, zv, vp)], axis=0)
                y = _dot(ad, vbd) + _dot(qp, ret_state_blockdiag(0, pr)) * ef_ref[pr]
                yar[pl.ds(r0, Q), pc] = y
            ret_state_update(0, tf_ref, kts)
            yield

            gms, css = [], []
            for g in range(SSD_GROUPS):
                cg = xbc[:, SSD_WIDTH + 512 + g * SSD_STATE:SSD_WIDTH + 512 + (g + 1) * SSD_STATE]
                bg = xbc[:, SSD_WIDTH + g * SSD_STATE:SSD_WIDTH + (g + 1) * SSD_STATE]
                gms.append(_dot_nt(cg, bg))
                css.append(_dot(cg, ss[0, :, g * GROUP_W:(g + 1) * GROUP_W].astype(BF16)))
            yield

            dt_t = dt.T
            ldt_t = jnp.log(dt_t)
            sub = lax.broadcasted_iota(jnp.int32, (LANES, Q), 0)
            adj_t = (jnp.where(sub < H, cum.T, rev.T) - ldt_t) * LOG2E
            dg_t = jnp.log(dt_t[0:H, :] + dt_t[H:2 * H, :]) * LOG2E
            cum2 = cum * LOG2E
            rev2 = rev * LOG2E
            wtail_f = dt * jnp.exp(tot - cum)
            ew_x, af_x = expand(jnp.concatenate([jnp.exp(cum), wtail_f], axis=0), jnp.exp(tot), exf_ref)
            ecum_x = ew_x[0:Q]
            wtail_x = ew_x[Q:2 * Q]
            lt = rowq > colq
            gt = rowq < colq
            lane_g = lax.broadcasted_iota(jnp.int32, (Q, GROUP_W), 1) // SSD_HEAD_DIM
            yield

            for g in range(SSD_GROUPS):
                gc = slice(g * GROUP_W, (g + 1) * GROUP_W)
                xg = xbc[:, gc]
                ws, xs = [], []
                for j in range(4):
                    h = 4 * g + j
                    hb = H + h
                    arg = jnp.where(lt, cum2[:, h:h + 1] - adj_t[h:h + 1, :],
                                    jnp.where(gt, rev2[:, hb:hb + 1] - adj_t[hb:hb + 1, :], dg_t[h:h + 1, :]))
                    ws.append((gms[g] * jnp.exp2(arg)).astype(BF16))
                    xs.append(jnp.where(lane_g == j, xg, jnp.zeros_like(xg)))
                y = _dot(jnp.concatenate(ws, axis=1), jnp.concatenate(xs, axis=0))
                yas[pl.ds(r0, Q), gc] = y + css[g] * ecum_x[:, gc]
            yield
            ssd_state_update(0, wtail_x, af_x)

        def phase2():
            for pr in range(RET_HEADS // 2):
                qp = qr[:, pr * LANES:(pr + 1) * LANES]
                pc = slice(pr * PAIR_W, (pr + 1) * PAIR_W)
                y2 = yar[pl.ds(r0, Q), pc] + _dot(qp, ret_state_blockdiag(1, pr)) * eb_ref[pr]
                for hh in range(2):
                    h = 2 * pr + hh
                    hc = slice(h * LANES, (h + 1) * LANES)
                    y = y2[:, hh * LANES:(hh + 1) * LANES]
                    mu = jnp.mean(y, axis=-1, keepdims=True)
                    yc = y - mu
                    var = jnp.mean(yc * yc, axis=-1, keepdims=True)
                    yn = yc * lax.rsqrt(var + EPS) * gsr[:, hc].astype(F32)
                    mixs_ref[slot_w, mix_rows, SSD_WIDTH + h * LANES:SSD_WIDTH + (h + 1) * LANES] = (
                        yn.astype(BF16))
            ret_state_update(1, tb_ref, [k_transposed(pr) for pr in range(RET_HEADS // 2)])
            yield

            dt = dtr[...]
            cum = cums[pl.ds(r0, Q), :]
            rev = cum[Q - 1:Q, :] - cum + dt * nega_ref[...]
            first = rev[0:1, :]
            wtail_b = dt * jnp.exp(first - rev)
            ew_x, ab_x = expand(jnp.concatenate([jnp.exp(rev), wtail_b], axis=0), jnp.exp(first), exb_ref)
            erev_x = ew_x[0:Q]
            wtail_x = ew_x[Q:2 * Q]

            parts = []
            ssq = None
            for g in range(SSD_GROUPS):
                cg = xbc[:, SSD_WIDTH + 512 + g * SSD_STATE:SSD_WIDTH + 512 + (g + 1) * SSD_STATE]
                gc = slice(g * GROUP_W, (g + 1) * GROUP_W)
                cs = _dot(cg, ss[1, :, gc].astype(BF16))
                y = (yas[pl.ds(r0, Q), gc] + cs * erev_x[:, gc]
                     + dsk_ref[:, gc] * xbc[:, gc].astype(F32))
                y = y * zs[:, gc].astype(F32)
                parts.append(y)
                ssq = y * y if ssq is None else ssq + y * y
            yield "project"

            inv = lax.rsqrt(jnp.sum(ssq, axis=-1, keepdims=True) * (1.0 / SSD_WIDTH) + EPS)
            for g, y in enumerate(parts):
                gc = slice(g * GROUP_W, (g + 1) * GROUP_W)
                mixs_ref[slot_w, mix_rows, gc] = (y * inv).astype(BF16)
            ssd_state_update(1, wtail_x, ab_x)

        return types.SimpleNamespace(init_state=init_state, emit_states=emit_states, phase1=phase1,
                                     phase2=phase2)

    streams = [make_stream(u) for u in range(U)]

    def run_interleaved(stage_gens, project_after=None):
        proj = None
        live = list(stage_gens)
        rnd = 0
        while live:
            marks = []
            nxt = []
            for gen in live:
                try:
                    marks.append(next(gen))
                    nxt.append(gen)
                except StopIteration:
                    pass
            if (project_after == rnd) or ("project" in marks):
                proj = out_proj_matmul()
            live = nxt
            rnd += 1
        return proj

    @pl.when(is_real & (s == 0))
    def _init_state():
        for st in streams:
            st.init_state()

    def phase1_all(with_out_proj):
        proj = run_interleaved([st.phase1() for st in streams], project_after=2 if with_out_proj else None)
        if with_out_proj:
            out_proj_finish(proj)
        if emit_state:
            @pl.when(s == nc - 1)
            def _emit_fwd():
                for st in streams:
                    st.emit_states(0)

    @pl.when(is_real & (s == 0))
    def _phase1_first():
        phase1_all(True)

    if nc > 1:
        @pl.when(is_real & (s > 0) & (s < nc))
        def _phase1_rest():
            phase1_all(False)

    @pl.when(jnp.logical_not(is_real))
    def _drain():
        out_proj_finish(out_proj_matmul())

    @pl.when(is_real & phase2)
    def _phase2():
        proj = run_interleaved([st.phase2() for st in streams])
        out_proj_finish(proj)
        mixs_ref[slot_r] = mixs_ref[slot_w]
        if emit_state:
            @pl.when(s == 2 * nc - 1)
            def _emit_bwd():
                for st in streams:
                    st.emit_states(1)


def _mixer_call(zs, xbc, dt, q, k, v, gs, nega, dsk, lamx, w_out_bf, x, mod, npost, s0_ssd, s0_ret,
                emit_state):
    nb, L, _ = zs.shape
    nc = L // CHUNK
    Q = CHUNK
    U = STREAMS
    assert nb % U == 0
    has_s0 = s0_ssd is not None
    n_groups = nb // U
    steps = 2 * nc
    n_real = n_groups * steps
    per_seq_mod = mod.shape[0] > 1

    def cur(i):
        ic = jnp.minimum(i, n_real - 1)
        return ic // steps, ic % steps

    def chunk_map(i):
        b, s = cur(i)
        return (b, jnp.where(s < nc, s, steps - 1 - s), 0)

    def phase2_map(i):
        b, s = cur(i)
        return (b, jnp.where(s < nc, nc - 1, steps - 1 - s), 0)

    def lag(i):
        j = jnp.maximum(i - 1, 0)
        return j // steps, j % steps

    def lag_map(i):
        bl, sl = lag(i)
        return (bl, jnp.where(sl < nc, nc - 1, steps - 1 - sl), 0)

    def lag_mod_map(i):
        bl, _ = lag(i)
        return (bl if per_seq_mod else 0, 0, 0)

    const2 = lambda i: (0, 0)
    in_specs = [
        pl.BlockSpec((U, Q, SSD_WIDTH), phase2_map),
        pl.BlockSpec((U, Q, CONV_CH), chunk_map),
        pl.BlockSpec((U, Q, DT_PAD), chunk_map),
        pl.BlockSpec((U, Q, RET_QK_WIDTH), chunk_map),
        pl.BlockSpec((U, Q, RET_QK_WIDTH), chunk_map),
        pl.BlockSpec((U, Q, RET_V_WIDTH), chunk_map),
        pl.BlockSpec((U, Q, RET_V_WIDTH), phase2_map),
        pl.BlockSpec((1, DT_PAD), const2),
        pl.BlockSpec((1, SSD_WIDTH), const2),
        pl.BlockSpec((2 * RET_HEADS, LANES), const2),
        pl.BlockSpec((MIX_WIDTH, D_MODEL), const2, pipeline_mode=pl.Buffered(1)),
        pl.BlockSpec((U, Q, D_MODEL), lag_map),
        pl.BlockSpec((U if per_seq_mod else 1, 1, 3 * D_MODEL), lag_mod_map),
        pl.BlockSpec((1, D_MODEL), const2),
    ]
    args = [zs, xbc, dt, q, k, v, gs, nega, dsk, lamx, w_out_bf, x, mod, npost]
    state_map = lambda i: (cur(i)[0], 0, 0, 0, 0, 0)
    ssd_state_block = (U, 1, 2, SSD_HEADS, SSD_HEAD_DIM, SSD_STATE)
    ret_state_block = (U, 1, 2, RET_HEADS, RET_QK_DIM, RET_V_DIM)
    if has_s0:
        in_specs += [pl.BlockSpec(ssd_state_block, state_map, pipeline_mode=pl.Buffered(1)),
                     pl.BlockSpec(ret_state_block, state_map, pipeline_mode=pl.Buffered(1))]
        args += [s0_ssd, s0_ret]
    out_specs = [pl.BlockSpec((U, Q, D_MODEL), lag_map)]
    out_shape = [jax.ShapeDtypeStruct((nb, L, D_MODEL), F32)]
    if emit_state:
        out_specs += [pl.BlockSpec(ssd_state_block, state_map), pl.BlockSpec(ret_state_block, state_map)]
        out_shape += [jax.ShapeDtypeStruct((nb,) + ssd_state_block[1:], F32),
                      jax.ShapeDtypeStruct((nb,) + ret_state_block[1:], F32)]
    n_pairs = RET_HEADS // 2
    scratch = [
        pltpu.VMEM((U, L, SSD_WIDTH), F32),
        pltpu.VMEM((U, L, RET_V_WIDTH), F32),
        pltpu.VMEM((U, 2, SSD_STATE, SSD_WIDTH), F32),
        pltpu.VMEM((U, 2, RET_HEADS, RET_QK_DIM, RET_V_DIM), F32),
        pltpu.VMEM((n_pairs, Q, 2 * Q), F32),
        pltpu.VMEM((n_pairs, Q, PAIR_W), F32),
        pltpu.VMEM((n_pairs, Q, PAIR_W), F32),
        pltpu.VMEM((n_pairs, LANES, Q), F32),
        pltpu.VMEM((n_pairs, LANES, Q), F32),
        pltpu.VMEM((2 * RET_HEADS, LANES), F32),
        pltpu.VMEM((LANES, SSD_WIDTH), BF16),
        pltpu.VMEM((LANES, SSD_WIDTH), BF16),
        pltpu.VMEM((U, L, DT_PAD), F32),
        pltpu.VMEM((2, U * Q, MIX_WIDTH), BF16),
    ]
    outs = pl.pallas_call(
        functools.partial(_mixer_kernel, n_streams=U, n_groups=n_groups, nc=nc, has_s0=has_s0,
                          emit_state=emit_state,
                          per_seq_mod=per_seq_mod),
        grid=(n_real + 1,),
        in_specs=in_specs,
        out_specs=out_specs,
        out_shape=out_shape,
        scratch_shapes=scratch,
        compiler_params=pltpu.CompilerParams(
            dimension_semantics=("arbitrary",), vmem_limit_bytes=VMEM_LIMIT),
        name="mixer_sample" if has_s0 else "mixer_prompt",
    )(*args)
    return outs


def _rope_tables(L):
    rows = L // GRID_W
    row = jnp.repeat(jnp.arange(rows, dtype=F32), GRID_W)
    col = jnp.tile(jnp.arange(GRID_W, dtype=F32), rows)
    half = RET_QK_DIM // 2
    inv = ROPE_BASE ** (-jnp.arange(0, half, 2, dtype=F32) / half)
    ang_r = row[:, None] * inv
    ang_c = col[:, None] * inv
    cos_h = jnp.concatenate([jnp.cos(ang_r), jnp.cos(ang_r), jnp.cos(ang_c), jnp.cos(ang_c)], axis=-1)
    sin_h = jnp.concatenate([-jnp.sin(ang_r), jnp.sin(ang_r), -jnp.sin(ang_c), jnp.sin(ang_c)], axis=-1)
    return jnp.tile(cos_h, (1, 2)), jnp.tile(sin_h, (1, 2))


def kernel(x_prompt, x_sample, state_ssd, state_ret, c, c_ctx, w_mod, b_mod, norm_pre_w, norm_post_w,
           w_in, conv_w, conv_b, ssd_A_log, ssd_dt_bias, ssd_D, ssd_norm_w, ret_decay, ret_norm_w, w_out):
    nb_s = x_sample.shape[0]
    l = 0

    n_dt = 2 * SSD_HEADS
    w_packed = _pack_call(jnp.swapaxes(w_in[l], 0, 1))
    mix_norm_w = jnp.concatenate([ssd_norm_w[l], ret_norm_w[l]]).reshape(MIX_WIDTH, 1)
    w_out_bf = _scale_rows_call(w_out[l], mix_norm_w)
    dtb = jnp.pad(ssd_dt_bias[l].reshape(1, n_dt), ((0, 0), (0, DT_PAD - n_dt)))
    nega = jnp.pad(-jnp.exp(ssd_A_log[l].reshape(1, n_dt)), ((0, 0), (0, DT_PAD - n_dt)))
    dsk = jnp.repeat(ssd_D[l], SSD_HEAD_DIM).reshape(1, SSD_WIDTH)
    lamx = jnp.broadcast_to(-jnp.exp(ret_decay[l].reshape(2 * RET_HEADS, 1)), (2 * RET_HEADS, LANES))
    npre = norm_pre_w[l].reshape(1, D_MODEL)
    npost = norm_post_w[l].reshape(1, D_MODEL)
    cb = conv_b[l].reshape(1, CONV_CH)
    cos_t, sin_t = _rope_tables(x_sample.shape[1])

    cond = jnp.concatenate([c, c_ctx[None, :], jnp.zeros((16 - nb_s - 1, D_MODEL), F32)], axis=0)
    mod = _mod_call(cond, w_mod[l], b_mod[l].reshape(1, 3 * D_MODEL))
    mod_s = mod[:nb_s].reshape(nb_s, 1, 3 * D_MODEL)
    mod_p = mod[nb_s:nb_s + 1].reshape(1, 1, 3 * D_MODEL)

    pp = _in_proj_call(x_prompt, mod_p, npre, w_packed, conv_w[l], cb, dtb, None, None)
    y_p, st_ssd_t, st_ret = _mixer_call(*pp, nega, dsk, lamx, w_out_bf, x_prompt, mod_p, npost,
                                        None, None, True)

    ps = _in_proj_call(x_sample, mod_s, npre, w_packed, conv_w[l], cb, dtb, cos_t, sin_t)
    (y_s,) = _mixer_call(*ps, nega, dsk, lamx, w_out_bf, x_sample, mod_s, npost,
                         jnp.swapaxes(state_ssd, -1, -2), state_ret, False)
    return (y_p, y_s, jnp.swapaxes(st_ssd_t, -1, -2), st_ret)
```

```python
import functools
import types

import jax
import jax.numpy as jnp
from jax import lax
from jax.experimental import pallas as pl
from jax.experimental.pallas import tpu as pltpu

F32 = jnp.float32
BF16 = jnp.bfloat16

D_MODEL = 1024
CHUNK = 128
GRID_W = 64
EPS = 1e-6
SSD_WIDTH = 1024
SSD_HEAD_DIM = 64
SSD_HEADS = 16
SSD_GROUPS = 4
SSD_STATE = 128
CONV_CH = SSD_WIDTH + 2 * SSD_GROUPS * SSD_STATE
RET_HEADS = 8
RET_QK_DIM = 64
RET_V_DIM = 128
RET_QK_WIDTH = RET_HEADS * RET_QK_DIM
RET_V_WIDTH = RET_HEADS * RET_V_DIM
MIX_WIDTH = SSD_WIDTH + RET_V_WIDTH
ROPE_BASE = 10000.0
LANES = 128
DT_PAD = LANES
GROUP_W = SSD_WIDTH // SSD_GROUPS
PAIR_W = 2 * RET_V_DIM
IN_PROJ_PIECE = 256
IN_PROJ_ROWS = 1024
STREAMS = 2

OFF_Z = 0
OFF_XBC = OFF_Z + SSD_WIDTH
OFF_Q = OFF_XBC + CONV_CH
OFF_K = OFF_Q + RET_QK_WIDTH
OFF_V = OFF_K + RET_QK_WIDTH
OFF_G = OFF_V + RET_V_WIDTH
OFF_DT = OFF_G + RET_V_WIDTH
IN_COLS_PACKED = OFF_DT + DT_PAD

LOG2E = 1.4426950408889634
VMEM_LIMIT = 56 * 1024 * 1024


def _silu(x):
    return x * (1.0 / (1.0 + jnp.exp(-x)))


def _dot(a, b):
    return jnp.dot(a, b, preferred_element_type=F32)


def _dot_nt(a, b):
    return lax.dot_general(a, b, (((1,), (1,)), ((), ())), preferred_element_type=F32)


def _split3(x):
    hi = x.astype(BF16)
    r1 = x - hi.astype(F32)
    mid = r1.astype(BF16)
    lo = (r1 - mid.astype(F32)).astype(BF16)
    return hi, mid, lo


PACK_COLS = 512
PACK_STEPS = -(-IN_COLS_PACKED // PACK_COLS)


def _pack_kernel(wt_ref, out_ref):
    t = wt_ref[...].T
    lane = lax.broadcasted_iota(jnp.int32, t.shape, 1)
    is_dt = pl.program_id(0) == PACK_STEPS - 1
    t = jnp.where(jnp.logical_and(is_dt, lane >= 2 * SSD_HEADS), 0.0, t)
    out_ref[...] = t.astype(BF16)


def _pack_call(w_in_t):
    n_head = (OFF_Q - OFF_Z) // PACK_COLS
    src_dt = SSD_WIDTH + CONV_CH
    src_tail = src_dt + 2 * SSD_HEADS

    def src_row(i):
        t8 = jnp.where(i < n_head, i * (PACK_COLS // 8),
                       jnp.where(i < PACK_STEPS - 1, src_tail // 8 + (i - n_head) * (PACK_COLS // 8),
                                 src_dt // 8))
        return t8 * 8

    return pl.pallas_call(
        _pack_kernel,
        grid=(PACK_STEPS,),
        in_specs=[pl.BlockSpec((pl.Element(PACK_COLS), pl.Element(D_MODEL)), lambda i: (src_row(i), 0))],
        out_specs=pl.BlockSpec((D_MODEL, PACK_COLS), lambda i: (0, i)),
        out_shape=jax.ShapeDtypeStruct((D_MODEL, IN_COLS_PACKED), BF16),
        name="pack_w_in",
    )(w_in_t)


def _scale_rows_kernel(w_ref, s_ref, out_ref):
    out_ref[...] = (w_ref[...] * s_ref[...]).astype(BF16)


def _scale_rows_call(w, row_scale):
    rows, cols = w.shape
    tr = 512
    return pl.pallas_call(
        _scale_rows_kernel,
        grid=(rows // tr,),
        in_specs=[pl.BlockSpec((tr, cols), lambda i: (i, 0)), pl.BlockSpec((tr, 1), lambda i: (i, 0))],
        out_specs=pl.BlockSpec((tr, cols), lambda i: (i, 0)),
        out_shape=jax.ShapeDtypeStruct((rows, cols), BF16),
        name="fold_norm_w_out",
    )(w, row_scale)


def _mod_kernel(cond_ref, w_ref, b_ref, out_ref):
    a = _silu(cond_ref[...]).astype(BF16)
    out_ref[...] = _dot(a, w_ref[...].astype(BF16)) + b_ref[...]


def _mod_call(cond, w_mod, b_mod):
    rows = cond.shape[0]
    tn = 1024
    return pl.pallas_call(
        _mod_kernel,
        grid=(3 * D_MODEL // tn,),
        in_specs=[
            pl.BlockSpec((rows, D_MODEL), lambda j: (0, 0)),
            pl.BlockSpec((D_MODEL, tn), lambda j: (0, j)),
            pl.BlockSpec((1, tn), lambda j: (0, j)),
        ],
        out_specs=pl.BlockSpec((rows, tn), lambda j: (0, j)),
        out_shape=jax.ShapeDtypeStruct((rows, 3 * D_MODEL), F32),
        name="mod",
    )(cond, w_mod, b_mod)


def _modnorm(x, mod_ref, npw_ref):
    ms = jnp.mean(x * x, axis=-1, keepdims=True)
    shift = mod_ref[0, :, 0:D_MODEL]
    scale = mod_ref[0, :, D_MODEL:2 * D_MODEL]
    gain = npw_ref[...] * (1.0 + scale)
    return (x * lax.rsqrt(ms + EPS) * gain + shift).astype(BF16)


def _in_proj_kernel(*refs, seq_len, rope):
    if rope:
        (x_ref, mod_ref, npw_ref, w_ref, cw_ref, cb_ref, dtb_ref,
         cos_ref, sin_ref, zs_ref, xbc_ref, dt_ref, q_ref, k_ref, v_ref, gs_ref, h_ref) = refs
    else:
        (x_ref, mod_ref, npw_ref, w_ref, cw_ref, cb_ref, dtb_ref,
         zs_ref, xbc_ref, dt_ref, q_ref, k_ref, v_ref, gs_ref, h_ref) = refs
    rows = seq_len
    h_ref[...] = _modnorm(x_ref[0], mod_ref, npw_ref)
    piece = IN_PROJ_PIECE

    def put(ref, cols, val):
        ref[0, :, cols] = val

    row = lax.broadcasted_iota(jnp.int32, (rows, piece), 0)
    seq_first = row == 0
    seq_last = row == rows - 1

    def sec_xbc(c0):
        cols = slice(c0, c0 + piece)
        acc = _dot(h_ref[...], w_ref[:, OFF_XBC + c0:OFF_XBC + c0 + piece])
        up = jnp.where(seq_first, 0.0, pltpu.roll(acc, 1, axis=0))
        dn = jnp.where(seq_last, 0.0, pltpu.roll(acc, rows - 1, axis=0))
        y = cw_ref[0:1, cols] * up + cw_ref[1:2, cols] * acc + cw_ref[2:3, cols] * dn + cb_ref[:, cols]
        put(xbc_ref, cols, _silu(y).astype(BF16))

    lane = lax.broadcasted_iota(jnp.int32, (rows, LANES), 1)
    first_half = (lane % 32) < 16

    def sec_qk(off, ref, scl, c0):
        acc = _dot(h_ref[...], w_ref[:, off + c0:off + c0 + piece]) * scl
        if rope:
            for s0 in range(0, piece, LANES):
                xs = acc[:, s0:s0 + LANES]
                partner = jnp.where(first_half, pltpu.roll(xs, LANES - 16, axis=1),
                                    pltpu.roll(xs, 16, axis=1))
                put(ref, slice(c0 + s0, c0 + s0 + LANES),
                    (xs * cos_ref[...] + partner * sin_ref[...]).astype(BF16))
        else:
            put(ref, slice(c0, c0 + piece), acc.astype(BF16))

    def sec_q(c0):
        sec_qk(OFF_Q, q_ref, 1.0, c0)

    def sec_k(c0):
        sec_qk(OFF_K, k_ref, RET_QK_DIM ** -0.5, c0)

    def sec_z(c0):
        acc = _dot(h_ref[...], w_ref[:, OFF_Z + c0:OFF_Z + c0 + piece])
        put(zs_ref, slice(c0, c0 + piece), _silu(acc).astype(BF16))

    def sec_g(c0):
        acc = _dot(h_ref[...], w_ref[:, OFF_G + c0:OFF_G + c0 + piece])
        put(gs_ref, slice(c0, c0 + piece), _silu(acc).astype(BF16))

    def sec_v(c0):
        acc = _dot(h_ref[...], w_ref[:, OFF_V + c0:OFF_V + c0 + piece])
        put(v_ref, slice(c0, c0 + piece), acc.astype(BF16))

    def sec_dt(c0):
        acc = _dot(h_ref[...], w_ref[:, OFF_DT:OFF_DT + DT_PAD]) + dtb_ref[...]
        put(dt_ref, slice(0, DT_PAD), jnp.maximum(acc, 0.0) + jnp.log(1.0 + jnp.exp(-jnp.abs(acc))))

    def pieces(fn, width):
        return [(fn, c0) for c0 in range(0, width, piece)]

    conv = pieces(sec_xbc, CONV_CH)
    light = pieces(sec_v, RET_V_WIDTH) + [(sec_dt, 0)]
    order = pieces(sec_q, RET_QK_WIDTH) + pieces(sec_k, RET_QK_WIDTH)
    stride = -(-len(conv) // len(light))
    for i, item in enumerate(conv):
        order.append(item)
        if i % stride == stride - 1 and light:
            order.append(light.pop(0))
    order += light + pieces(sec_z, SSD_WIDTH) + pieces(sec_g, RET_V_WIDTH)
    for fn, c0 in order:
        fn(c0)


def _in_proj_call(x, mod, npw, w_packed, conv_w, conv_b, dtb, cos_t, sin_t):
    nb, L, _ = x.shape
    rope = cos_t is not None
    per_seq_mod = mod.shape[0] > 1
    assert L <= IN_PROJ_ROWS
    mod_map = (lambda s: (s, 0, 0)) if per_seq_mod else (lambda s: (0, 0, 0))
    const2 = lambda s: (0, 0)
    in_specs = [
        pl.BlockSpec((1, L, D_MODEL), lambda s: (s, 0, 0)),
        pl.BlockSpec((1, 1, 3 * D_MODEL), mod_map),
        pl.BlockSpec((1, D_MODEL), const2),
        pl.BlockSpec((D_MODEL, IN_COLS_PACKED), const2, pipeline_mode=pl.Buffered(1)),
        pl.BlockSpec((3, CONV_CH), const2),
        pl.BlockSpec((1, CONV_CH), const2),
        pl.BlockSpec((1, DT_PAD), const2),
    ]
    args = [x, mod, npw, w_packed, conv_w, conv_b, dtb]
    if rope:
        in_specs += [pl.BlockSpec((L, LANES), const2), pl.BlockSpec((L, LANES), const2)]
        args += [cos_t, sin_t]
    widths = (SSD_WIDTH, CONV_CH, DT_PAD, RET_QK_WIDTH, RET_QK_WIDTH, RET_V_WIDTH, RET_V_WIDTH)
    dtypes = (BF16, BF16, F32, BF16, BF16, BF16, BF16)
    out_specs = [pl.BlockSpec((1, L, w), lambda s: (s, 0, 0)) for w in widths]
    out_shape = [jax.ShapeDtypeStruct((nb, L, w), d) for w, d in zip(widths, dtypes)]
    return pl.pallas_call(
        functools.partial(_in_proj_kernel, seq_len=L, rope=rope),
        grid=(nb,),
        in_specs=in_specs,
        out_specs=out_specs,
        out_shape=out_shape,
        scratch_shapes=[pltpu.VMEM((L, D_MODEL), BF16)],
        compiler_params=pltpu.CompilerParams(
            dimension_semantics=("parallel",), vmem_limit_bytes=VMEM_LIMIT),
        name="in_proj_rope" if rope else "in_proj",
    )(*args)


def _mixer_kernel(*refs, n_streams, n_groups, nc, has_s0, emit_state, per_seq_mod):
    Q = CHUNK
    H = SSD_HEADS
    U = n_streams
    it = iter(refs)
    zs_ref, xbc_ref, dt_ref, q_ref, k_ref, v_ref, gs_ref = (next(it) for _ in range(7))
    nega_ref, dsk_ref, lam_ref = (next(it) for _ in range(3))
    wout_ref, xres_ref, modl_ref, npost_ref = (next(it) for _ in range(4))
    if has_s0:
        s0s_ref, s0r_ref = next(it), next(it)
    y_ref = next(it)
    if emit_state:
        os_ref, or_ref = next(it), next(it)
    (yas_ref, yar_ref, ss_ref, sr_ref, dsum_ref, ef_ref, eb_ref, tf_ref, tb_ref,
     ar_ref, exf_ref, exb_ref, cum_ref, mixs_ref) = (next(it) for _ in range(14))

    i = pl.program_id(0)
    n_real = n_groups * 2 * nc
    is_real = i < n_real
    ic = jnp.minimum(i, n_real - 1)
    s = ic % (2 * nc)
    phase2 = s >= nc
    c = jnp.where(phase2, 2 * nc - 1 - s, s)
    r0 = pl.multiple_of(c * Q, Q)
    slot_w = 1
    slot_r = 0

    rowq = lax.broadcasted_iota(jnp.int32, (Q, Q), 0)
    colq = lax.broadcasted_iota(jnp.int32, (Q, Q), 1)

    def out_proj_matmul():
        return _dot(mixs_ref[slot_r], wout_ref[...])

    def out_proj_finish(out):
        ms = jnp.mean(out * out, axis=-1, keepdims=True)
        o = out * lax.rsqrt(ms + EPS) * npost_ref[...]
        for u in range(U):
            gate = modl_ref[u if per_seq_mod else 0, :, 2 * D_MODEL:3 * D_MODEL]
            y_ref[u] = xres_ref[u] + gate * o[u * Q:(u + 1) * Q]

    @pl.when(i == 0)
    def _init_tables():
        mixs_ref[...] = jnp.zeros_like(mixs_ref)
        diff = (rowq - colq).astype(F32)
        rowf = lax.broadcasted_iota(jnp.int32, (Q, LANES), 0).astype(F32)
        colf = lax.broadcasted_iota(jnp.int32, (RET_QK_DIM, Q), 1).astype(F32)
        for h in range(RET_HEADS):
            pr, hh = divmod(h, 2)
            lf = lam_ref[h:h + 1, :]
            lb = lam_ref[RET_HEADS + h:RET_HEADS + h + 1, :]
            e = jnp.where(rowq >= colq, lf * diff, lb * (-diff))
            dsum_ref[pr, :, hh * Q:(hh + 1) * Q] = jnp.exp(e) * jnp.where(rowq == colq, 2.0, 1.0)
            hc = slice(hh * LANES, (hh + 1) * LANES)
            ef_ref[pr, :, hc] = jnp.exp(lf * (rowf + 1.0))
            eb_ref[pr, :, hc] = jnp.exp(lb * (Q - rowf))
            kr = slice(hh * RET_QK_DIM, (hh + 1) * RET_QK_DIM)
            tf_ref[pr, kr, :] = jnp.exp(lf * (Q - 1.0 - colf))
            tb_ref[pr, kr, :] = jnp.exp(lb * colf)
        ar_ref[...] = jnp.exp(lam_ref[...] * float(Q))
        er = lax.broadcasted_iota(jnp.int32, (LANES, SSD_WIDTH), 0)
        ec = lax.broadcasted_iota(jnp.int32, (LANES, SSD_WIDTH), 1) // SSD_HEAD_DIM
        exf_ref[...] = jnp.where(er == ec, 1.0, 0.0).astype(BF16)
        exb_ref[...] = jnp.where(er == ec + H, 1.0, 0.0).astype(BF16)

    def expand(w, a_row, ex_ref):
        a3 = jnp.concatenate(_split3(jnp.broadcast_to(a_row, (16, LANES))), axis=0)
        out = _dot(jnp.concatenate([w.astype(BF16), a3], axis=0), ex_ref[...])
        n = w.shape[0]
        return out[0:n], out[n:n + 1] + out[n + 16:n + 17] + out[n + 32:n + 33]

    def make_stream(u):
        zs, xbc, dtr, qr, kr_, vr, gsr = (r.at[u] for r in (zs_ref, xbc_ref, dt_ref, q_ref, k_ref,
                                                            v_ref, gs_ref))
        yas, yar, ss, sr, cums = (r.at[u] for r in (yas_ref, yar_ref, ss_ref, sr_ref, cum_ref))
        mix_rows = slice(u * Q, (u + 1) * Q)

        def init_state():
            if has_s0:
                for d in range(2):
                    for p in range(SSD_HEADS // 2):
                        pair_t = jnp.concatenate(
                            [s0s_ref[u, 0, d, 2 * p], s0s_ref[u, 0, d, 2 * p + 1]], axis=0)
                        ss[d, :, p * LANES:(p + 1) * LANES] = pair_t.T
                sr[...] = s0r_ref[u, 0]
            else:
                ss[...] = jnp.zeros_like(ss)
                sr[...] = jnp.zeros_like(sr)

        def ssd_state_update(d, wtail_x, a_x):
            for g in range(SSD_GROUPS):
                bgt = xbc[:, SSD_WIDTH + g * SSD_STATE:SSD_WIDTH + (g + 1) * SSD_STATE].T
                gc = slice(g * GROUP_W, (g + 1) * GROUP_W)
                xt = xbc[:, gc] * wtail_x[:, gc].astype(BF16)
                ss[d, :, gc] = ss[d, :, gc] * a_x[:, gc] + _dot(bgt, xt)

        def k_transposed(pr):
            return kr_[:, pr * LANES:(pr + 1) * LANES].T

        def ret_state_update(d, tail_ref, kts):
            for pr, kt in enumerate(kts):
                pc = slice(pr * PAIR_W, (pr + 1) * PAIR_W)
                ktt = (kt.astype(F32) * tail_ref[pr]).astype(BF16)
                ds = _dot(ktt, vr[:, pc])
                for hh in range(2):
                    h = 2 * pr + hh
                    a = ar_ref[d * RET_HEADS + h:d * RET_HEADS + h + 1, :]
                    sr[d, h] = (sr[d, h] * a
                                + ds[hh * RET_QK_DIM:(hh + 1) * RET_QK_DIM, hh * LANES:(hh + 1) * LANES])

        def emit_states(d):
            for p in range(SSD_HEADS // 2):
                pair_t = ss[d, :, p * LANES:(p + 1) * LANES].T
                os_ref[u, 0, d, 2 * p] = pair_t[0:SSD_HEAD_DIM]
                os_ref[u, 0, d, 2 * p + 1] = pair_t[SSD_HEAD_DIM:2 * SSD_HEAD_DIM]
            or_ref[u, 0, d] = sr[d]

        def ret_state_blockdiag(d, pr):
            z = jnp.zeros((RET_QK_DIM, RET_V_DIM), F32)
            top = jnp.concatenate([sr[d, 2 * pr], z], axis=1)
            bot = jnp.concatenate([z, sr[d, 2 * pr + 1]], axis=1)
            return jnp.concatenate([top, bot], axis=0).astype(BF16)

        def phase1():
            dt = dtr[...]
            la = dt * nega_ref[...]
            tl_bf = jnp.where(rowq >= colq, 1.0, 0.0).astype(BF16)
            c3 = _dot(tl_bf, jnp.concatenate(_split3(la), axis=1))
            cum = c3[:, 0:LANES] + c3[:, LANES:2 * LANES] + c3[:, 2 * LANES:3 * LANES]
            cums[pl.ds(r0, Q), :] = cum
            tot = cum[Q - 1:Q, :]
            rev = tot - cum + la
            yield

            sub_k = lax.broadcasted_iota(jnp.int32, (LANES, Q), 0) < RET_QK_DIM
            lane_v = lax.broadcasted_iota(jnp.int32, (Q, PAIR_W), 1) < RET_V_DIM
            kts = [k_transposed(pr) for pr in range(RET_HEADS // 2)]
            for pr, kt in enumerate(kts):
                qp = qr[:, pr * LANES:(pr + 1) * LANES]
                zk = jnp.zeros_like(kt)
                kbd = jnp.concatenate([jnp.where(sub_k, kt, zk), jnp.where(sub_k, zk, kt)], axis=1)
                a2 = _dot(qp, kbd)
                ad = (a2 * dsum_ref[pr]).astype(BF16)
                pc = slice(pr * PAIR_W, (pr + 1) * PAIR_W)
                vp = vr[:, pc]
                zv = jnp.zeros_like(vp)
                vbd = jnp.concatenate([jnp.where(lane_v, vp, zv), jnp.where(lane_v, zv, vp)], axis=0)
                y = _dot(ad, vbd) + _dot(qp, ret_state_blockdiag(0, pr)) * ef_ref[pr]
                yar[pl.ds(r0, Q), pc] = y
            ret_state_update(0, tf_ref, kts)
            yield

            gms, css = [], []
            for g in range(SSD_GROUPS):
                cg = xbc[:, SSD_WIDTH + 512 + g * SSD_STATE:SSD_WIDTH + 512 + (g + 1) * SSD_STATE]
                bg = xbc[:, SSD_WIDTH + g * SSD_STATE:SSD_WIDTH + (g + 1) * SSD_STATE]
                gms.append(_dot_nt(cg, bg))
                css.append(_dot(cg, ss[0, :, g * GROUP_W:(g + 1) * GROUP_W].astype(BF16)))
            yield

            dt_t = dt.T
            ldt_t = jnp.log(dt_t)
            sub = lax.broadcasted_iota(jnp.int32, (LANES, Q), 0)
            adj_t = (jnp.where(sub < H, cum.T, rev.T) - ldt_t) * LOG2E
            dg_t = jnp.log(dt_t[0:H, :] + dt_t[H:2 * H, :]) * LOG2E
            cum2 = cum * LOG2E
            rev2 = rev * LOG2E
            wtail_f = dt * jnp.exp(tot - cum)
            ew_x, af_x = expand(jnp.concatenate([jnp.exp(cum), wtail_f], axis=0), jnp.exp(tot), exf_ref)
            ecum_x = ew_x[0:Q]
            wtail_x = ew_x[Q:2 * Q]
            lt = rowq > colq
            gt = rowq < colq
            lane_g = lax.broadcasted_iota(jnp.int32, (Q, GROUP_W), 1) // SSD_HEAD_DIM
            yield

            for g in range(SSD_GROUPS):
                gc = slice(g * GROUP_W, (g + 1) * GROUP_W)
                xg = xbc[:, gc]
                ws, xs = [], []
                for j in range(4):
                    h = 4 * g + j
                    hb = H + h
                    arg = jnp.where(lt, cum2[:, h:h + 1] - adj_t[h:h + 1, :],
                                    jnp.where(gt, rev2[:, hb:hb + 1] - adj_t[hb:hb + 1, :], dg_t[h:h + 1, :]))
                    ws.append((gms[g] * jnp.exp2(arg)).astype(BF16))
                    xs.append(jnp.where(lane_g == j, xg, jnp.zeros_like(xg)))
                y = _dot(jnp.concatenate(ws, axis=1), jnp.concatenate(xs, axis=0))
                yas[pl.ds(r0, Q), gc] = y + css[g] * ecum_x[:, gc]
            yield
            ssd_state_update(0, wtail_x, af_x)

        def phase2():
            for pr in range(RET_HEADS // 2):
                qp = qr[:, pr * LANES:(pr + 1) * LANES]
                pc = slice(pr * PAIR_W, (pr + 1) * PAIR_W)
                y2 = yar[pl.ds(r0, Q), pc] + _dot(qp, ret_state_blockdiag(1, pr)) * eb_ref[pr]
                for hh in range(2):
                    h = 2 * pr + hh
                    hc = slice(h * LANES, (h + 1) * LANES)
                    y = y2[:, hh * LANES:(hh + 1) * LANES]
                    mu = jnp.mean(y, axis=-1, keepdims=True)
                    yc = y - mu
                    var = jnp.mean(yc * yc, axis=-1, keepdims=True)
                    yn = yc * lax.rsqrt(var + EPS) * gsr[:, hc].astype(F32)
                    mixs_ref[slot_w, mix_rows, SSD_WIDTH + h * LANES:SSD_WIDTH + (h + 1) * LANES] = (
                        yn.astype(BF16))
            ret_state_update(1, tb_ref, [k_transposed(pr) for pr in range(RET_HEADS // 2)])
            yield

            dt = dtr[...]
            cum = cums[pl.ds(r0, Q), :]
            rev = cum[Q - 1:Q, :] - cum + dt * nega_ref[...]
            first = rev[0:1, :]
            wtail_b = dt * jnp.exp(first - rev)
            ew_x, ab_x = expand(jnp.concatenate([jnp.exp(rev), wtail_b], axis=0), jnp.exp(first), exb_ref)
            erev_x = ew_x[0:Q]
            wtail_x = ew_x[Q:2 * Q]

            parts = []
            ssq = None
            for g in range(SSD_GROUPS):
                cg = xbc[:, SSD_WIDTH + 512 + g * SSD_STATE:SSD_WIDTH + 512 + (g + 1) * SSD_STATE]
                gc = slice(g * GROUP_W, (g + 1) * GROUP_W)
                cs = _dot(cg, ss[1, :, gc].astype(BF16))
                y = (yas[pl.ds(r0, Q), gc] + cs * erev_x[:, gc]
                     + dsk_ref[:, gc] * xbc[:, gc].astype(F32))
                y = y * zs[:, gc].astype(F32)
                parts.append(y)
                ssq = y * y if ssq is None else ssq + y * y
            yield

            inv = lax.rsqrt(jnp.sum(ssq, axis=-1, keepdims=True) * (1.0 / SSD_WIDTH) + EPS)
            for g, y in enumerate(parts):
                gc = slice(g * GROUP_W, (g + 1) * GROUP_W)
                mixs_ref[slot_w, mix_rows, gc] = (y * inv).astype(BF16)
            ssd_state_update(1, wtail_x, ab_x)

        return types.SimpleNamespace(init_state=init_state, emit_states=emit_states, phase1=phase1,
                                     phase2=phase2)

    streams = [make_stream(u) for u in range(U)]

    def run_interleaved(stage_gens, project_after=None):
        proj = None
        live = list(stage_gens)
        rnd = 0
        while live:
            nxt = []
            for gen in live:
                try:
                    next(gen)
                    nxt.append(gen)
                except StopIteration:
                    pass
            if project_after == rnd:
                proj = out_proj_matmul()
            live = nxt
            rnd += 1
        return proj

    @pl.when(is_real & (s == 0))
    def _init_state():
        for st in streams:
            st.init_state()

    def phase1_all(with_out_proj):
        proj = run_interleaved([st.phase1() for st in streams], project_after=2 if with_out_proj else None)
        if with_out_proj:
            out_proj_finish(proj)

    @pl.when(is_real & (s == 0))
    def _phase1_first():
        phase1_all(True)

    if nc > 1:
        @pl.when(is_real & (s > 0) & (s < nc))
        def _phase1_rest():
            phase1_all(False)

    @pl.when(jnp.logical_not(is_real))
    def _drain():
        out_proj_finish(out_proj_matmul())

    def phase2_all(with_out_proj):
        proj = run_interleaved([st.phase2() for st in streams], project_after=1 if with_out_proj else None)
        if with_out_proj:
            out_proj_finish(proj)
        mixs_ref[slot_r] = mixs_ref[slot_w]

    @pl.when(is_real & (s == nc))
    def _phase2_first():
        phase2_all(False)

    if nc > 1:
        @pl.when(is_real & (s > nc))
        def _phase2_rest():
            phase2_all(True)

    if emit_state:
        @pl.when(is_real & (s == nc - 1))
        def _emit_fwd():
            for st in streams:
                st.emit_states(0)

        @pl.when(is_real & (s == 2 * nc - 1))
        def _emit_bwd():
            for st in streams:
                st.emit_states(1)


def _mixer_call(zs, xbc, dt, q, k, v, gs, nega, dsk, lamx, w_out_bf, x, mod, npost, s0_ssd, s0_ret,
                emit_state):
    nb, L, _ = zs.shape
    nc = L // CHUNK
    Q = CHUNK
    U = STREAMS
    assert nb % U == 0
    has_s0 = s0_ssd is not None
    n_groups = nb // U
    steps = 2 * nc
    n_real = n_groups * steps
    per_seq_mod = mod.shape[0] > 1

    def cur(i):
        ic = jnp.minimum(i, n_real - 1)
        return ic // steps, ic % steps

    def chunk_map(i):
        b, s = cur(i)
        return (b, jnp.where(s < nc, s, steps - 1 - s), 0)

    def phase2_map(i):
        b, s = cur(i)
        return (b, jnp.where(s < nc, nc - 1, steps - 1 - s), 0)

    def lag(i):
        j = jnp.maximum(i - 1, 0)
        return j // steps, j % steps

    def lag_map(i):
        bl, sl = lag(i)
        return (bl, jnp.where(sl < nc, nc - 1, steps - 1 - sl), 0)

    def lag_mod_map(i):
        bl, _ = lag(i)
        return (bl if per_seq_mod else 0, 0, 0)

    const2 = lambda i: (0, 0)
    in_specs = [
        pl.BlockSpec((U, Q, SSD_WIDTH), phase2_map),
        pl.BlockSpec((U, Q, CONV_CH), chunk_map),
        pl.BlockSpec((U, Q, DT_PAD), chunk_map),
        pl.BlockSpec((U, Q, RET_QK_WIDTH), chunk_map),
        pl.BlockSpec((U, Q, RET_QK_WIDTH), chunk_map),
        pl.BlockSpec((U, Q, RET_V_WIDTH), chunk_map),
        pl.BlockSpec((U, Q, RET_V_WIDTH), phase2_map),
        pl.BlockSpec((1, DT_PAD), const2),
        pl.BlockSpec((1, SSD_WIDTH), const2),
        pl.BlockSpec((2 * RET_HEADS, LANES), const2),
        pl.BlockSpec((MIX_WIDTH, D_MODEL), const2, pipeline_mode=pl.Buffered(1)),
        pl.BlockSpec((U, Q, D_MODEL), lag_map),
        pl.BlockSpec((U if per_seq_mod else 1, 1, 3 * D_MODEL), lag_mod_map),
        pl.BlockSpec((1, D_MODEL), const2),
    ]
    args = [zs, xbc, dt, q, k, v, gs, nega, dsk, lamx, w_out_bf, x, mod, npost]
    state_map = lambda i: (cur(i)[0], 0, 0, 0, 0, 0)
    ssd_state_block = (U, 1, 2, SSD_HEADS, SSD_HEAD_DIM, SSD_STATE)
    ret_state_block = (U, 1, 2, RET_HEADS, RET_QK_DIM, RET_V_DIM)
    if has_s0:
        in_specs += [pl.BlockSpec(ssd_state_block, state_map, pipeline_mode=pl.Buffered(1)),
                     pl.BlockSpec(ret_state_block, state_map, pipeline_mode=pl.Buffered(1))]
        args += [s0_ssd, s0_ret]
    out_specs = [pl.BlockSpec((U, Q, D_MODEL), lag_map)]
    out_shape = [jax.ShapeDtypeStruct((nb, L, D_MODEL), F32)]
    if emit_state:
        out_specs += [pl.BlockSpec(ssd_state_block, state_map), pl.BlockSpec(ret_state_block, state_map)]
        out_shape += [jax.ShapeDtypeStruct((nb,) + ssd_state_block[1:], F32),
                      jax.ShapeDtypeStruct((nb,) + ret_state_block[1:], F32)]
    n_pairs = RET_HEADS // 2
    scratch = [
        pltpu.VMEM((U, L, SSD_WIDTH), F32),
        pltpu.VMEM((U, L, RET_V_WIDTH), F32),
        pltpu.VMEM((U, 2, SSD_STATE, SSD_WIDTH), F32),
        pltpu.VMEM((U, 2, RET_HEADS, RET_QK_DIM, RET_V_DIM), F32),
        pltpu.VMEM((n_pairs, Q, 2 * Q), F32),
        pltpu.VMEM((n_pairs, Q, PAIR_W), F32),
        pltpu.VMEM((n_pairs, Q, PAIR_W), F32),
        pltpu.VMEM((n_pairs, LANES, Q), F32),
        pltpu.VMEM((n_pairs, LANES, Q), F32),
        pltpu.VMEM((2 * RET_HEADS, LANES), F32),
        pltpu.VMEM((LANES, SSD_WIDTH), BF16),
        pltpu.VMEM((LANES, SSD_WIDTH), BF16),
        pltpu.VMEM((U, L, DT_PAD), F32),
        pltpu.VMEM((2, U * Q, MIX_WIDTH), BF16),
    ]
    outs = pl.pallas_call(
        functools.partial(_mixer_kernel, n_streams=U, n_groups=n_groups, nc=nc, has_s0=has_s0,
                          emit_state=emit_state,
                          per_seq_mod=per_seq_mod),
        grid=(n_real + 1,),
        in_specs=in_specs,
        out_specs=out_specs,
        out_shape=out_shape,
        scratch_shapes=scratch,
        compiler_params=pltpu.CompilerParams(
            dimension_semantics=("arbitrary",), vmem_limit_bytes=VMEM_LIMIT),
        name="mixer_sample" if has_s0 else "mixer_prompt",
    )(*args)
    return outs


def _rope_tables(L):
    rows = L // GRID_W
    row = jnp.repeat(jnp.arange(rows, dtype=F32), GRID_W)
    col = jnp.tile(jnp.arange(GRID_W, dtype=F32), rows)
    half = RET_QK_DIM // 2
    inv = ROPE_BASE ** (-jnp.arange(0, half, 2, dtype=F32) / half)
    ang_r = row[:, None] * inv
    ang_c = col[:, None] * inv
    cos_h = jnp.concatenate([jnp.cos(ang_r), jnp.cos(ang_r), jnp.cos(ang_c), jnp.cos(ang_c)], axis=-1)
    sin_h = jnp.concatenate([-jnp.sin(ang_r), jnp.sin(ang_r), -jnp.sin(ang_c), jnp.sin(ang_c)], axis=-1)
    return jnp.tile(cos_h, (1, 2)), jnp.tile(sin_h, (1, 2))


def kernel(x_prompt, x_sample, state_ssd, state_ret, c, c_ctx, w_mod, b_mod, norm_pre_w, norm_post_w,
           w_in, conv_w, conv_b, ssd_A_log, ssd_dt_bias, ssd_D, ssd_norm_w, ret_decay, ret_norm_w, w_out):
    nb_s = x_sample.shape[0]
    l = 0

    n_dt = 2 * SSD_HEADS
    w_packed = _pack_call(jnp.swapaxes(w_in[l], 0, 1))
    mix_norm_w = jnp.concatenate([ssd_norm_w[l], ret_norm_w[l]]).reshape(MIX_WIDTH, 1)
    w_out_bf = _scale_rows_call(w_out[l], mix_norm_w)
    dtb = jnp.pad(ssd_dt_bias[l].reshape(1, n_dt), ((0, 0), (0, DT_PAD - n_dt)))
    nega = jnp.pad(-jnp.exp(ssd_A_log[l].reshape(1, n_dt)), ((0, 0), (0, DT_PAD - n_dt)))
    dsk = jnp.repeat(ssd_D[l], SSD_HEAD_DIM).reshape(1, SSD_WIDTH)
    lamx = jnp.broadcast_to(-jnp.exp(ret_decay[l].reshape(2 * RET_HEADS, 1)), (2 * RET_HEADS, LANES))
    npre = norm_pre_w[l].reshape(1, D_MODEL)
    npost = norm_post_w[l].reshape(1, D_MODEL)
    cb = conv_b[l].reshape(1, CONV_CH)
    cos_t, sin_t = _rope_tables(x_sample.shape[1])

    cond = jnp.concatenate([c, c_ctx[None, :], jnp.zeros((16 - nb_s - 1, D_MODEL), F32)], axis=0)
    mod = _mod_call(cond, w_mod[l], b_mod[l].reshape(1, 3 * D_MODEL))
    mod_s = mod[:nb_s].reshape(nb_s, 1, 3 * D_MODEL)
    mod_p = mod[nb_s:nb_s + 1].reshape(1, 1, 3 * D_MODEL)

    pp = _in_proj_call(x_prompt, mod_p, npre, w_packed, conv_w[l], cb, dtb, None, None)
    y_p, st_ssd_t, st_ret = _mixer_call(*pp, nega, dsk, lamx, w_out_bf, x_prompt, mod_p, npost,
                                        None, None, True)

    ps = _in_proj_call(x_sample, mod_s, npre, w_packed, conv_w[l], cb, dtb, cos_t, sin_t)
    (y_s,) = _mixer_call(*ps, nega, dsk, lamx, w_out_bf, x_sample, mod_s, npost,
                         jnp.swapaxes(state_ssd, -1, -2), state_ret, False)
    return (y_p, y_s, jnp.swapaxes(st_ssd_t, -1, -2), st_ret)
```

```python
import functools
import types

import jax
import jax.numpy as jnp
from jax import lax
from jax.experimental import pallas as pl
from jax.experimental.pallas import tpu as pltpu

F32 = jnp.float32
BF16 = jnp.bfloat16

D_MODEL = 1024
CHUNK = 128
GRID_W = 64
EPS = 1e-6
SSD_WIDTH = 1024
SSD_HEAD_DIM = 64
SSD_HEADS = 16
SSD_GROUPS = 4
SSD_STATE = 128
CONV_CH = SSD_WIDTH + 2 * SSD_GROUPS * SSD_STATE
RET_HEADS = 8
RET_QK_DIM = 64
RET_V_DIM = 128
RET_QK_WIDTH = RET_HEADS * RET_QK_DIM
RET_V_WIDTH = RET_HEADS * RET_V_DIM
MIX_WIDTH = SSD_WIDTH + RET_V_WIDTH
ROPE_BASE = 10000.0
LANES = 128
DT_PAD = LANES
GROUP_W = SSD_WIDTH // SSD_GROUPS
PAIR_W = 2 * RET_V_DIM
IN_PROJ_PIECE = 256
IN_PROJ_ROWS = 1024
STREAMS = 2
BF16_ROWS = 16
XBC_B = SSD_WIDTH
XBC_C = XBC_B + SSD_GROUPS * SSD_STATE
COND_ROWS = 16
PROJECT_POINT = "project"

OFF_Z = 0
OFF_XBC = OFF_Z + SSD_WIDTH
OFF_Q = OFF_XBC + CONV_CH
OFF_K = OFF_Q + RET_QK_WIDTH
OFF_V = OFF_K + RET_QK_WIDTH
OFF_G = OFF_V + RET_V_WIDTH
OFF_DT = OFF_G + RET_V_WIDTH
IN_COLS_PACKED = OFF_DT + DT_PAD

LOG2E = 1.4426950408889634
VMEM_LIMIT = 56 * 1024 * 1024


def _silu(x):
    return x * (1.0 / (1.0 + jnp.exp(-x)))


def _dot(a, b):
    return jnp.dot(a, b, preferred_element_type=F32)


def _dot_nt(a, b):
    return lax.dot_general(a, b, (((1,), (1,)), ((), ())), preferred_element_type=F32)


def _split3(x):
    hi = x.astype(BF16)
    r1 = x - hi.astype(F32)
    mid = r1.astype(BF16)
    lo = (r1 - mid.astype(F32)).astype(BF16)
    return hi, mid, lo


PACK_COLS = 1024
PACK_STEPS = -(-IN_COLS_PACKED // PACK_COLS)


def _pack_kernel(wt_ref, out_ref):
    t = wt_ref[...].T
    lane = lax.broadcasted_iota(jnp.int32, t.shape, 1)
    is_dt = pl.program_id(0) == PACK_STEPS - 1
    t = jnp.where(jnp.logical_and(is_dt, lane >= 2 * SSD_HEADS), 0.0, t)
    out_ref[...] = t.astype(BF16)


def _pack_call(w_in_t):
    n_head = (OFF_Q - OFF_Z) // PACK_COLS
    src_dt = SSD_WIDTH + CONV_CH
    src_tail = src_dt + 2 * SSD_HEADS

    def src_row(i):
        t8 = jnp.where(i < n_head, i * (PACK_COLS // 8),
                       jnp.where(i < PACK_STEPS - 1, src_tail // 8 + (i - n_head) * (PACK_COLS // 8),
                                 src_dt // 8))
        return t8 * 8

    return pl.pallas_call(
        _pack_kernel,
        grid=(PACK_STEPS,),
        in_specs=[pl.BlockSpec((pl.Element(PACK_COLS), pl.Element(D_MODEL)), lambda i: (src_row(i), 0))],
        out_specs=pl.BlockSpec((D_MODEL, PACK_COLS), lambda i: (0, i)),
        out_shape=jax.ShapeDtypeStruct((D_MODEL, IN_COLS_PACKED), BF16),
        name="pack_w_in",
    )(w_in_t)


def _scale_rows_kernel(w_ref, s_ref, out_ref):
    out_ref[...] = (w_ref[...] * s_ref[...]).astype(BF16)


def _scale_rows_call(w, row_scale):
    rows, cols = w.shape
    tr = 512
    return pl.pallas_call(
        _scale_rows_kernel,
        grid=(rows // tr,),
        in_specs=[pl.BlockSpec((tr, cols), lambda i: (i, 0)), pl.BlockSpec((tr, 1), lambda i: (i, 0))],
        out_specs=pl.BlockSpec((tr, cols), lambda i: (i, 0)),
        out_shape=jax.ShapeDtypeStruct((rows, cols), BF16),
        name="fold_norm_w_out",
    )(w, row_scale)


def _mod_kernel(cond_ref, w_ref, b_ref, out_ref):
    a = _silu(cond_ref[...]).astype(BF16)
    out_ref[...] = _dot(a, w_ref[...].astype(BF16)) + b_ref[...]


def _mod_call(cond, w_mod, b_mod):
    rows = cond.shape[0]
    tn = 512
    return pl.pallas_call(
        _mod_kernel,
        grid=(3 * D_MODEL // tn,),
        in_specs=[
            pl.BlockSpec((rows, D_MODEL), lambda j: (0, 0)),
            pl.BlockSpec((D_MODEL, tn), lambda j: (0, j)),
            pl.BlockSpec((1, tn), lambda j: (0, j)),
        ],
        out_specs=pl.BlockSpec((rows, tn), lambda j: (0, j)),
        out_shape=jax.ShapeDtypeStruct((rows, 3 * D_MODEL), F32),
        name="mod",
    )(cond, w_mod, b_mod)


def _modnorm(x, mod_ref, npw_ref):
    ms = jnp.mean(x * x, axis=-1, keepdims=True)
    shift = mod_ref[0, :, 0:D_MODEL]
    scale = mod_ref[0, :, D_MODEL:2 * D_MODEL]
    gain = npw_ref[...] * (1.0 + scale)
    return (x * lax.rsqrt(ms + EPS) * gain + shift).astype(BF16)


def _in_proj_kernel(*refs, seq_len, rope):
    if rope:
        (x_ref, mod_ref, npw_ref, w_ref, cw_ref, cb_ref, dtb_ref,
         cos_ref, sin_ref, zs_ref, xbc_ref, dt_ref, q_ref, k_ref, v_ref, gs_ref, h_ref) = refs
    else:
        (x_ref, mod_ref, npw_ref, w_ref, cw_ref, cb_ref, dtb_ref,
         zs_ref, xbc_ref, dt_ref, q_ref, k_ref, v_ref, gs_ref, h_ref) = refs
    rows = seq_len
    h_ref[...] = _modnorm(x_ref[0], mod_ref, npw_ref)
    piece = IN_PROJ_PIECE

    def put(ref, cols, val):
        ref[0, :, cols] = val

    row = lax.broadcasted_iota(jnp.int32, (rows, piece), 0)
    seq_first = row == 0
    seq_last = row == rows - 1

    def sec_xbc(c0):
        cols = slice(c0, c0 + piece)
        acc = _dot(h_ref[...], w_ref[:, OFF_XBC + c0:OFF_XBC + c0 + piece])
        up = jnp.where(seq_first, 0.0, pltpu.roll(acc, 1, axis=0))
        dn = jnp.where(seq_last, 0.0, pltpu.roll(acc, rows - 1, axis=0))
        y = cw_ref[0:1, cols] * up + cw_ref[1:2, cols] * acc + cw_ref[2:3, cols] * dn + cb_ref[:, cols]
        put(xbc_ref, cols, _silu(y).astype(BF16))

    lane = lax.broadcasted_iota(jnp.int32, (rows, LANES), 1)
    first_half = (lane % 32) < 16

    def sec_qk(off, ref, scl, c0):
        acc = _dot(h_ref[...], w_ref[:, off + c0:off + c0 + piece]) * scl
        if rope:
            for s0 in range(0, piece, LANES):
                xs = acc[:, s0:s0 + LANES]
                partner = jnp.where(first_half, pltpu.roll(xs, LANES - 16, axis=1),
                                    pltpu.roll(xs, 16, axis=1))
                put(ref, slice(c0 + s0, c0 + s0 + LANES),
                    (xs * cos_ref[...] + partner * sin_ref[...]).astype(BF16))
        else:
            put(ref, slice(c0, c0 + piece), acc.astype(BF16))

    def sec_q(c0):
        sec_qk(OFF_Q, q_ref, 1.0, c0)

    def sec_k(c0):
        sec_qk(OFF_K, k_ref, RET_QK_DIM ** -0.5, c0)

    def sec_z(c0):
        acc = _dot(h_ref[...], w_ref[:, OFF_Z + c0:OFF_Z + c0 + piece])
        put(zs_ref, slice(c0, c0 + piece), _silu(acc).astype(BF16))

    def sec_g(c0):
        acc = _dot(h_ref[...], w_ref[:, OFF_G + c0:OFF_G + c0 + piece])
        put(gs_ref, slice(c0, c0 + piece), _silu(acc).astype(BF16))

    def sec_v(c0):
        acc = _dot(h_ref[...], w_ref[:, OFF_V + c0:OFF_V + c0 + piece])
        put(v_ref, slice(c0, c0 + piece), acc.astype(BF16))

    def sec_dt(c0):
        acc = _dot(h_ref[...], w_ref[:, OFF_DT:OFF_DT + DT_PAD]) + dtb_ref[...]
        put(dt_ref, slice(0, DT_PAD), jnp.maximum(acc, 0.0) + jnp.log(1.0 + jnp.exp(-jnp.abs(acc))))

    def pieces(fn, width):
        return [(fn, c0) for c0 in range(0, width, piece)]

    conv = pieces(sec_xbc, CONV_CH)
    light = pieces(sec_v, RET_V_WIDTH) + [(sec_dt, 0)]
    order = pieces(sec_q, RET_QK_WIDTH) + pieces(sec_k, RET_QK_WIDTH)
    stride = -(-len(conv) // len(light))
    for i, item in enumerate(conv):
        order.append(item)
        if i % stride == stride - 1 and light:
            order.append(light.pop(0))
    order += light + pieces(sec_z, SSD_WIDTH) + pieces(sec_g, RET_V_WIDTH)
    for fn, c0 in order:
        fn(c0)


def _in_proj_call(x, mod, npw, w_packed, conv_w, conv_b, dtb, cos_t, sin_t):
    nb, L, _ = x.shape
    rope = cos_t is not None
    per_seq_mod = mod.shape[0] > 1
    assert L <= IN_PROJ_ROWS
    mod_map = (lambda s: (s, 0, 0)) if per_seq_mod else (lambda s: (0, 0, 0))
    const2 = lambda s: (0, 0)
    in_specs = [
        pl.BlockSpec((1, L, D_MODEL), lambda s: (s, 0, 0)),
        pl.BlockSpec((1, 1, 3 * D_MODEL), mod_map),
        pl.BlockSpec((1, D_MODEL), const2),
        pl.BlockSpec((D_MODEL, IN_COLS_PACKED), const2, pipeline_mode=pl.Buffered(1)),
        pl.BlockSpec((3, CONV_CH), const2),
        pl.BlockSpec((1, CONV_CH), const2),
        pl.BlockSpec((1, DT_PAD), const2),
    ]
    args = [x, mod, npw, w_packed, conv_w, conv_b, dtb]
    if rope:
        in_specs += [pl.BlockSpec((L, LANES), const2), pl.BlockSpec((L, LANES), const2)]
        args += [cos_t, sin_t]
    widths = (SSD_WIDTH, CONV_CH, DT_PAD, RET_QK_WIDTH, RET_QK_WIDTH, RET_V_WIDTH, RET_V_WIDTH)
    dtypes = (BF16, BF16, F32, BF16, BF16, BF16, BF16)
    out_specs = [pl.BlockSpec((1, L, w), lambda s: (s, 0, 0)) for w in widths]
    out_shape = [jax.ShapeDtypeStruct((nb, L, w), d) for w, d in zip(widths, dtypes)]
    return pl.pallas_call(
        functools.partial(_in_proj_kernel, seq_len=L, rope=rope),
        grid=(nb,),
        in_specs=in_specs,
        out_specs=out_specs,
        out_shape=out_shape,
        scratch_shapes=[pltpu.VMEM((L, D_MODEL), BF16)],
        compiler_params=pltpu.CompilerParams(
            dimension_semantics=("parallel",), vmem_limit_bytes=VMEM_LIMIT),
        name="in_proj_rope" if rope else "in_proj",
    )(*args)


def _mixer_kernel(*refs, n_streams, n_groups, nc, has_s0, emit_state, per_seq_mod):
    Q = CHUNK
    H = SSD_HEADS
    U = n_streams
    it = iter(refs)
    zs_ref, xbc_ref, dt_ref, q_ref, k_ref, v_ref, gs_ref = (next(it) for _ in range(7))
    nega_ref, dsk_ref, lam_ref = (next(it) for _ in range(3))
    wout_ref, xres_ref, modl_ref, npost_ref = (next(it) for _ in range(4))
    if has_s0:
        s0s_ref, s0r_ref = next(it), next(it)
    y_ref = next(it)
    if emit_state:
        os_ref, or_ref = next(it), next(it)
    (yas_ref, yar_ref, ss_ref, sr_ref, dsum_ref, ef_ref, eb_ref, tf_ref, tb_ref,
     ar_ref, exf_ref, exb_ref, cum_ref, mixs_ref) = (next(it) for _ in range(14))

    i = pl.program_id(0)
    n_real = n_groups * 2 * nc
    is_real = i < n_real
    ic = jnp.minimum(i, n_real - 1)
    s = ic % (2 * nc)
    phase2 = s >= nc
    c = jnp.where(phase2, 2 * nc - 1 - s, s)
    r0 = pl.multiple_of(c * Q, Q)
    slot_w = 1
    slot_r = 0

    rowq = lax.broadcasted_iota(jnp.int32, (Q, Q), 0)
    colq = lax.broadcasted_iota(jnp.int32, (Q, Q), 1)

    def out_proj_matmul():
        return _dot(mixs_ref[slot_r], wout_ref[...])

    def out_proj_finish(out):
        ms = jnp.mean(out * out, axis=-1, keepdims=True)
        o = out * lax.rsqrt(ms + EPS) * npost_ref[...]
        for u in range(U):
            gate = modl_ref[u if per_seq_mod else 0, :, 2 * D_MODEL:3 * D_MODEL]
            y_ref[u] = xres_ref[u] + gate * o[u * Q:(u + 1) * Q]

    @pl.when(i == 0)
    def _init_tables():
        mixs_ref[...] = jnp.zeros_like(mixs_ref)
        diff = (rowq - colq).astype(F32)
        rowf = lax.broadcasted_iota(jnp.int32, (Q, LANES), 0).astype(F32)
        colf = lax.broadcasted_iota(jnp.int32, (RET_QK_DIM, Q), 1).astype(F32)
        for h in range(RET_HEADS):
            pr, hh = divmod(h, 2)
            lf = lam_ref[h:h + 1, :]
            lb = lam_ref[RET_HEADS + h:RET_HEADS + h + 1, :]
            e = jnp.where(rowq >= colq, lf * diff, lb * (-diff))
            dsum_ref[pr, :, hh * Q:(hh + 1) * Q] = jnp.exp(e) * jnp.where(rowq == colq, 2.0, 1.0)
            hc = slice(hh * LANES, (hh + 1) * LANES)
            ef_ref[pr, :, hc] = jnp.exp(lf * (rowf + 1.0))
            eb_ref[pr, :, hc] = jnp.exp(lb * (Q - rowf))
            kr = slice(hh * RET_QK_DIM, (hh + 1) * RET_QK_DIM)
            tf_ref[pr, kr, :] = jnp.exp(lf * (Q - 1.0 - colf))
            tb_ref[pr, kr, :] = jnp.exp(lb * colf)
        ar_ref[...] = jnp.exp(lam_ref[...] * float(Q))
        er = lax.broadcasted_iota(jnp.int32, (LANES, SSD_WIDTH), 0)
        ec = lax.broadcasted_iota(jnp.int32, (LANES, SSD_WIDTH), 1) // SSD_HEAD_DIM
        exf_ref[...] = jnp.where(er == ec, 1.0, 0.0).astype(BF16)
        exb_ref[...] = jnp.where(er == ec + H, 1.0, 0.0).astype(BF16)

    def expand(w, a_row, ex_ref):
        r = BF16_ROWS
        a3 = jnp.concatenate(_split3(jnp.broadcast_to(a_row, (r, LANES))), axis=0)
        out = _dot(jnp.concatenate([w.astype(BF16), a3], axis=0), ex_ref[...])
        n = w.shape[0]
        return out[0:n], out[n:n + 1] + out[n + r:n + r + 1] + out[n + 2 * r:n + 2 * r + 1]

    def make_stream(u):
        zs, xbc, dtr, qr, kr_, vr, gsr = (r.at[u] for r in (zs_ref, xbc_ref, dt_ref, q_ref, k_ref,
                                                            v_ref, gs_ref))
        yas, yar, ss, sr, cums = (r.at[u] for r in (yas_ref, yar_ref, ss_ref, sr_ref, cum_ref))
        mix_rows = slice(u * Q, (u + 1) * Q)

        def init_state():
            if has_s0:
                for d in range(2):
                    for p in range(SSD_HEADS // 2):
                        pair_t = jnp.concatenate(
                            [s0s_ref[u, 0, d, 2 * p], s0s_ref[u, 0, d, 2 * p + 1]], axis=0)
                        ss[d, :, p * LANES:(p + 1) * LANES] = pair_t.T
                sr[...] = s0r_ref[u, 0]
            else:
                ss[...] = jnp.zeros_like(ss)
                sr[...] = jnp.zeros_like(sr)

        def ssd_state_update(d, wtail_x, a_x):
            for g in range(SSD_GROUPS):
                bgt = xbc[:, XBC_B + g * SSD_STATE:XBC_B + (g + 1) * SSD_STATE].T
                gc = slice(g * GROUP_W, (g + 1) * GROUP_W)
                xt = xbc[:, gc] * wtail_x[:, gc].astype(BF16)
                ss[d, :, gc] = ss[d, :, gc] * a_x[:, gc] + _dot(bgt, xt)

        def k_transposed(pr):
            return kr_[:, pr * LANES:(pr + 1) * LANES].T

        def ret_state_update(d, tail_ref, kts):
            for pr, kt in enumerate(kts):
                pc = slice(pr * PAIR_W, (pr + 1) * PAIR_W)
                ktt = (kt.astype(F32) * tail_ref[pr]).astype(BF16)
                ds = _dot(ktt, vr[:, pc])
                for hh in range(2):
                    h = 2 * pr + hh
                    a = ar_ref[d * RET_HEADS + h:d * RET_HEADS + h + 1, :]
                    sr[d, h] = (sr[d, h] * a
                                + ds[hh * RET_QK_DIM:(hh + 1) * RET_QK_DIM, hh * LANES:(hh + 1) * LANES])

        def emit_states(d):
            for p in range(SSD_HEADS // 2):
                pair_t = ss[d, :, p * LANES:(p + 1) * LANES].T
                os_ref[u, 0, d, 2 * p] = pair_t[0:SSD_HEAD_DIM]
                os_ref[u, 0, d, 2 * p + 1] = pair_t[SSD_HEAD_DIM:2 * SSD_HEAD_DIM]
            or_ref[u, 0, d] = sr[d]

        def ret_state_blockdiag(d, pr):
            z = jnp.zeros((RET_QK_DIM, RET_V_DIM), F32)
            top = jnp.concatenate([sr[d, 2 * pr], z], axis=1)
            bot = jnp.concatenate([z, sr[d, 2 * pr + 1]], axis=1)
            return jnp.concatenate([top, bot], axis=0).astype(BF16)

        def phase1():
            dt = dtr[...]
            la = dt * nega_ref[...]
            tl_bf = jnp.where(rowq >= colq, 1.0, 0.0).astype(BF16)
            c3 = _dot(tl_bf, jnp.concatenate(_split3(la), axis=1))
            cum = c3[:, 0:LANES] + c3[:, LANES:2 * LANES] + c3[:, 2 * LANES:3 * LANES]
            cums[pl.ds(r0, Q), :] = cum
            tot = cum[Q - 1:Q, :]
            rev = tot - cum + la
            yield

            sub_k = lax.broadcasted_iota(jnp.int32, (LANES, Q), 0) < RET_QK_DIM
            lane_v = lax.broadcasted_iota(jnp.int32, (Q, PAIR_W), 1) < RET_V_DIM
            kts = [k_transposed(pr) for pr in range(RET_HEADS // 2)]
            for pr, kt in enumerate(kts):
                qp = qr[:, pr * LANES:(pr + 1) * LANES]
                zk = jnp.zeros_like(kt)
                kbd = jnp.concatenate([jnp.where(sub_k, kt, zk), jnp.where(sub_k, zk, kt)], axis=1)
                a2 = _dot(qp, kbd)
                ad = (a2 * dsum_ref[pr]).astype(BF16)
                pc = slice(pr * PAIR_W, (pr + 1) * PAIR_W)
                vp = vr[:, pc]
                zv = jnp.zeros_like(vp)
                vbd = jnp.concatenate([jnp.where(lane_v, vp, zv), jnp.where(lane_v, zv, vp)], axis=0)
                y = _dot(ad, vbd) + _dot(qp, ret_state_blockdiag(0, pr)) * ef_ref[pr]
                yar[pl.ds(r0, Q), pc] = y
            ret_state_update(0, tf_ref, kts)
            yield

            gms, css = [], []
            for g in range(SSD_GROUPS):
                cg = xbc[:, XBC_C + g * SSD_STATE:XBC_C + (g + 1) * SSD_STATE]
                bg = xbc[:, XBC_B + g * SSD_STATE:XBC_B + (g + 1) * SSD_STATE]
                gms.append(_dot_nt(cg, bg))
                css.append(_dot(cg, ss[0, :, g * GROUP_W:(g + 1) * GROUP_W].astype(BF16)))
            yield PROJECT_POINT

            dt_t = dt.T
            ldt_t = jnp.log(dt_t)
            sub = lax.broadcasted_iota(jnp.int32, (LANES, Q), 0)
            adj_t = (jnp.where(sub < H, cum.T, rev.T) - ldt_t) * LOG2E
            dg_t = jnp.log(dt_t[0:H, :] + dt_t[H:2 * H, :]) * LOG2E
            cum2 = cum * LOG2E
            rev2 = rev * LOG2E
            wtail_f = dt * jnp.exp(tot - cum)
            ew_x, af_x = expand(jnp.concatenate([jnp.exp(cum), wtail_f], axis=0), jnp.exp(tot), exf_ref)
            ecum_x = ew_x[0:Q]
            wtail_x = ew_x[Q:2 * Q]
            lt = rowq > colq
            gt = rowq < colq
            lane_g = lax.broadcasted_iota(jnp.int32, (Q, GROUP_W), 1) // SSD_HEAD_DIM
            yield

            for g in range(SSD_GROUPS):
                gc = slice(g * GROUP_W, (g + 1) * GROUP_W)
                xg = xbc[:, gc]
                ws, xs = [], []
                for j in range(4):
                    h = 4 * g + j
                    hb = H + h
                    arg = jnp.where(lt, cum2[:, h:h + 1] - adj_t[h:h + 1, :],
                                    jnp.where(gt, rev2[:, hb:hb + 1] - adj_t[hb:hb + 1, :], dg_t[h:h + 1, :]))
                    ws.append((gms[g] * jnp.exp2(arg)).astype(BF16))
                    xs.append(jnp.where(lane_g == j, xg, jnp.zeros_like(xg)))
                y = _dot(jnp.concatenate(ws, axis=1), jnp.concatenate(xs, axis=0))
                yas[pl.ds(r0, Q), gc] = y + css[g] * ecum_x[:, gc]
            yield
            ssd_state_update(0, wtail_x, af_x)

        def phase2():
            for pr in range(RET_HEADS // 2):
                qp = qr[:, pr * LANES:(pr + 1) * LANES]
                pc = slice(pr * PAIR_W, (pr + 1) * PAIR_W)
                y2 = yar[pl.ds(r0, Q), pc] + _dot(qp, ret_state_blockdiag(1, pr)) * eb_ref[pr]
                for hh in range(2):
                    h = 2 * pr + hh
                    hc = slice(h * LANES, (h + 1) * LANES)
                    y = y2[:, hh * LANES:(hh + 1) * LANES]
                    mu = jnp.mean(y, axis=-1, keepdims=True)
                    yc = y - mu
                    var = jnp.mean(yc * yc, axis=-1, keepdims=True)
                    yn = yc * lax.rsqrt(var + EPS) * gsr[:, hc].astype(F32)
                    mixs_ref[slot_w, mix_rows, SSD_WIDTH + h * LANES:SSD_WIDTH + (h + 1) * LANES] = (
                        yn.astype(BF16))
            ret_state_update(1, tb_ref, [k_transposed(pr) for pr in range(RET_HEADS // 2)])
            yield

            dt = dtr[...]
            cum = cums[pl.ds(r0, Q), :]
            rev = cum[Q - 1:Q, :] - cum + dt * nega_ref[...]
            first = rev[0:1, :]
            wtail_b = dt * jnp.exp(first - rev)
            ew_x, ab_x = expand(jnp.concatenate([jnp.exp(rev), wtail_b], axis=0), jnp.exp(first), exb_ref)
            erev_x = ew_x[0:Q]
            wtail_x = ew_x[Q:2 * Q]

            parts = []
            ssq = None
            for g in range(SSD_GROUPS):
                cg = xbc[:, XBC_C + g * SSD_STATE:XBC_C + (g + 1) * SSD_STATE]
                gc = slice(g * GROUP_W, (g + 1) * GROUP_W)
                cs = _dot(cg, ss[1, :, gc].astype(BF16))
                y = (yas[pl.ds(r0, Q), gc] + cs * erev_x[:, gc]
                     + dsk_ref[:, gc] * xbc[:, gc].astype(F32))
                y = y * zs[:, gc].astype(F32)
                parts.append(y)
                ssq = y * y if ssq is None else ssq + y * y
            yield PROJECT_POINT

            inv = lax.rsqrt(jnp.sum(ssq, axis=-1, keepdims=True) * (1.0 / SSD_WIDTH) + EPS)
            for g, y in enumerate(parts):
                gc = slice(g * GROUP_W, (g + 1) * GROUP_W)
                mixs_ref[slot_w, mix_rows, gc] = (y * inv).astype(BF16)
            ssd_state_update(1, wtail_x, ab_x)

        return types.SimpleNamespace(init_state=init_state, emit_states=emit_states, phase1=phase1,
                                     phase2=phase2)

    streams = [make_stream(u) for u in range(U)]

    def run_interleaved(stage_gens, with_out_proj):
        proj = None
        live = list(stage_gens)
        while live:
            marks, nxt = [], []
            for gen in live:
                try:
                    marks.append(next(gen))
                    nxt.append(gen)
                except StopIteration:
                    pass
            if with_out_proj and PROJECT_POINT in marks:
                proj = out_proj_matmul()
            live = nxt
        return proj

    @pl.when(is_real & (s == 0))
    def _init_state():
        for st in streams:
            st.init_state()

    def phase1_all(with_out_proj):
        proj = run_interleaved([st.phase1() for st in streams], with_out_proj)
        if with_out_proj:
            out_proj_finish(proj)

    @pl.when(is_real & (s == 0))
    def _phase1_first():
        phase1_all(True)

    if nc > 1:
        @pl.when(is_real & (s > 0) & (s < nc))
        def _phase1_rest():
            phase1_all(False)

    @pl.when(jnp.logical_not(is_real))
    def _drain():
        out_proj_finish(out_proj_matmul())

    def phase2_all(with_out_proj):
        proj = run_interleaved([st.phase2() for st in streams], with_out_proj)
        if with_out_proj:
            out_proj_finish(proj)
        mixs_ref[slot_r] = mixs_ref[slot_w]

    @pl.when(is_real & (s == nc))
    def _phase2_first():
        phase2_all(False)

    if nc > 1:
        @pl.when(is_real & (s > nc))
        def _phase2_rest():
            phase2_all(True)

    if emit_state:
        @pl.when(is_real & (s == nc - 1))
        def _emit_fwd():
            for st in streams:
                st.emit_states(0)

        @pl.when(is_real & (s == 2 * nc - 1))
        def _emit_bwd():
            for st in streams:
                st.emit_states(1)


def _mixer_call(zs, xbc, dt, q, k, v, gs, nega, dsk, lamx, w_out_bf, x, mod, npost, s0_ssd, s0_ret,
                emit_state):
    nb, L, _ = zs.shape
    nc = L // CHUNK
    Q = CHUNK
    U = STREAMS
    assert nb % U == 0
    has_s0 = s0_ssd is not None
    n_groups = nb // U
    steps = 2 * nc
    n_real = n_groups * steps
    per_seq_mod = mod.shape[0] > 1

    def cur(i):
        ic = jnp.minimum(i, n_real - 1)
        return ic // steps, ic % steps

    def chunk_map(i):
        b, s = cur(i)
        return (b, jnp.where(s < nc, s, steps - 1 - s), 0)

    def phase2_map(i):
        b, s = cur(i)
        return (b, jnp.where(s < nc, nc - 1, steps - 1 - s), 0)

    def lag(i):
        j = jnp.maximum(i - 1, 0)
        return j // steps, j % steps

    def lag_map(i):
        bl, sl = lag(i)
        return (bl, jnp.where(sl < nc, nc - 1, steps - 1 - sl), 0)

    def lag_mod_map(i):
        bl, _ = lag(i)
        return (bl if per_seq_mod else 0, 0, 0)

    const2 = lambda i: (0, 0)
    in_specs = [
        pl.BlockSpec((U, Q, SSD_WIDTH), phase2_map),
        pl.BlockSpec((U, Q, CONV_CH), chunk_map),
        pl.BlockSpec((U, Q, DT_PAD), chunk_map),
        pl.BlockSpec((U, Q, RET_QK_WIDTH), chunk_map),
        pl.BlockSpec((U, Q, RET_QK_WIDTH), chunk_map),
        pl.BlockSpec((U, Q, RET_V_WIDTH), chunk_map),
        pl.BlockSpec((U, Q, RET_V_WIDTH), phase2_map),
        pl.BlockSpec((1, DT_PAD), const2),
        pl.BlockSpec((1, SSD_WIDTH), const2),
        pl.BlockSpec((2 * RET_HEADS, LANES), const2),
        pl.BlockSpec((MIX_WIDTH, D_MODEL), const2, pipeline_mode=pl.Buffered(1)),
        pl.BlockSpec((U, Q, D_MODEL), lag_map),
        pl.BlockSpec((U if per_seq_mod else 1, 1, 3 * D_MODEL), lag_mod_map),
        pl.BlockSpec((1, D_MODEL), const2),
    ]
    args = [zs, xbc, dt, q, k, v, gs, nega, dsk, lamx, w_out_bf, x, mod, npost]
    state_map = lambda i: (cur(i)[0], 0, 0, 0, 0, 0)
    ssd_state_block = (U, 1, 2, SSD_HEADS, SSD_HEAD_DIM, SSD_STATE)
    ret_state_block = (U, 1, 2, RET_HEADS, RET_QK_DIM, RET_V_DIM)
    if has_s0:
        in_specs += [pl.BlockSpec(ssd_state_block, state_map, pipeline_mode=pl.Buffered(1)),
                     pl.BlockSpec(ret_state_block, state_map, pipeline_mode=pl.Buffered(1))]
        args += [s0_ssd, s0_ret]
    out_specs = [pl.BlockSpec((U, Q, D_MODEL), lag_map)]
    out_shape = [jax.ShapeDtypeStruct((nb, L, D_MODEL), F32)]
    if emit_state:
        out_specs += [pl.BlockSpec(ssd_state_block, state_map), pl.BlockSpec(ret_state_block, state_map)]
        out_shape += [jax.ShapeDtypeStruct((nb,) + ssd_state_block[1:], F32),
                      jax.ShapeDtypeStruct((nb,) + ret_state_block[1:], F32)]
    n_pairs = RET_HEADS // 2
    scratch = [
        pltpu.VMEM((U, L, SSD_WIDTH), F32),
        pltpu.VMEM((U, L, RET_V_WIDTH), F32),
        pltpu.VMEM((U, 2, SSD_STATE, SSD_WIDTH), F32),
        pltpu.VMEM((U, 2, RET_HEADS, RET_QK_DIM, RET_V_DIM), F32),
        pltpu.VMEM((n_pairs, Q, 2 * Q), F32),
        pltpu.VMEM((n_pairs, Q, PAIR_W), F32),
        pltpu.VMEM((n_pairs, Q, PAIR_W), F32),
        pltpu.VMEM((n_pairs, LANES, Q), F32),
        pltpu.VMEM((n_pairs, LANES, Q), F32),
        pltpu.VMEM((2 * RET_HEADS, LANES), F32),
        pltpu.VMEM((LANES, SSD_WIDTH), BF16),
        pltpu.VMEM((LANES, SSD_WIDTH), BF16),
        pltpu.VMEM((U, L, DT_PAD), F32),
        pltpu.VMEM((2, U * Q, MIX_WIDTH), BF16),
    ]
    outs = pl.pallas_call(
        functools.partial(_mixer_kernel, n_streams=U, n_groups=n_groups, nc=nc, has_s0=has_s0,
                          emit_state=emit_state,
                          per_seq_mod=per_seq_mod),
        grid=(n_real + 1,),
        in_specs=in_specs,
        out_specs=out_specs,
        out_shape=out_shape,
        scratch_shapes=scratch,
        compiler_params=pltpu.CompilerParams(
            dimension_semantics=("arbitrary",), vmem_limit_bytes=VMEM_LIMIT),
        name="mixer_sample" if has_s0 else "mixer_prompt",
    )(*args)
    return outs


def _rope_tables(L):
    rows = L // GRID_W
    row = jnp.repeat(jnp.arange(rows, dtype=F32), GRID_W)
    col = jnp.tile(jnp.arange(GRID_W, dtype=F32), rows)
    half = RET_QK_DIM // 2
    inv = ROPE_BASE ** (-jnp.arange(0, half, 2, dtype=F32) / half)
    ang_r = row[:, None] * inv
    ang_c = col[:, None] * inv
    cos_h = jnp.concatenate([jnp.cos(ang_r), jnp.cos(ang_r), jnp.cos(ang_c), jnp.cos(ang_c)], axis=-1)
    sin_h = jnp.concatenate([-jnp.sin(ang_r), jnp.sin(ang_r), -jnp.sin(ang_c), jnp.sin(ang_c)], axis=-1)
    return jnp.tile(cos_h, (1, 2)), jnp.tile(sin_h, (1, 2))


def kernel(x_prompt, x_sample, state_ssd, state_ret, c, c_ctx, w_mod, b_mod, norm_pre_w, norm_post_w,
           w_in, conv_w, conv_b, ssd_A_log, ssd_dt_bias, ssd_D, ssd_norm_w, ret_decay, ret_norm_w, w_out):
    nb_s = x_sample.shape[0]
    l = 0

    n_dt = 2 * SSD_HEADS
    w_packed = _pack_call(jnp.swapaxes(w_in[l], 0, 1))
    mix_norm_w = jnp.concatenate([ssd_norm_w[l], ret_norm_w[l]]).reshape(MIX_WIDTH, 1)
    w_out_bf = _scale_rows_call(w_out[l], mix_norm_w)
    dtb = jnp.pad(ssd_dt_bias[l].reshape(1, n_dt), ((0, 0), (0, DT_PAD - n_dt)))
    nega = jnp.pad(-jnp.exp(ssd_A_log[l].reshape(1, n_dt)), ((0, 0), (0, DT_PAD - n_dt)))
    dsk = jnp.repeat(ssd_D[l], SSD_HEAD_DIM).reshape(1, SSD_WIDTH)
    lamx = jnp.broadcast_to(-jnp.exp(ret_decay[l].reshape(2 * RET_HEADS, 1)), (2 * RET_HEADS, LANES))
    npre = norm_pre_w[l].reshape(1, D_MODEL)
    npost = norm_post_w[l].reshape(1, D_MODEL)
    cb = conv_b[l].reshape(1, CONV_CH)
    cos_t, sin_t = _rope_tables(x_sample.shape[1])

    cond = jnp.concatenate([c, c_ctx[None, :], jnp.zeros((COND_ROWS - nb_s - 1, D_MODEL), F32)], axis=0)
    mod = _mod_call(cond, w_mod[l], b_mod[l].reshape(1, 3 * D_MODEL))
    mod_s = mod[:nb_s].reshape(nb_s, 1, 3 * D_MODEL)
    mod_p = mod[nb_s:nb_s + 1].reshape(1, 1, 3 * D_MODEL)

    pp = _in_proj_call(x_prompt, mod_p, npre, w_packed, conv_w[l], cb, dtb, None, None)
    y_p, st_ssd_t, st_ret = _mixer_call(*pp, nega, dsk, lamx, w_out_bf, x_prompt, mod_p, npost,
                                        None, None, True)

    ps = _in_proj_call(x_sample, mod_s, npre, w_packed, conv_w[l], cb, dtb, cos_t, sin_t)
    (y_s,) = _mixer_call(*ps, nega, dsk, lamx, w_out_bf, x_sample, mod_s, npost,
                         jnp.swapaxes(state_ssd, -1, -2), state_ret, False)
    return (y_p, y_s, jnp.swapaxes(st_ssd_t, -1, -2), st_ret)
```

```python
import functools
import types

import jax
import jax.numpy as jnp
from jax import lax
from jax.experimental import pallas as pl
from jax.experimental.pallas import tpu as pltpu

F32 = jnp.float32
BF16 = jnp.bfloat16

D_MODEL = 1024
CHUNK = 128
GRID_W = 64
EPS = 1e-6
SSD_WIDTH = 1024
SSD_HEAD_DIM = 64
SSD_HEADS = 16
SSD_GROUPS = 4
SSD_STATE = 128
CONV_CH = SSD_WIDTH + 2 * SSD_GROUPS * SSD_STATE
RET_HEADS = 8
RET_QK_DIM = 64
RET_V_DIM = 128
RET_QK_WIDTH = RET_HEADS * RET_QK_DIM
RET_V_WIDTH = RET_HEADS * RET_V_DIM
MIX_WIDTH = SSD_WIDTH + RET_V_WIDTH
ROPE_BASE = 10000.0
LANES = 128
DT_PAD = LANES
GROUP_W = SSD_WIDTH // SSD_GROUPS
PAIR_W = 2 * RET_V_DIM
IN_PROJ_PIECE = 256
IN_PROJ_ROWS = 1024
STREAMS = 2
BF16_ROWS = 16
XBC_B = SSD_WIDTH
XBC_C = XBC_B + SSD_GROUPS * SSD_STATE
COND_ROWS = 16
PROJECT_POINT = "project"

OFF_Z = 0
OFF_XBC = OFF_Z + SSD_WIDTH
OFF_Q = OFF_XBC + CONV_CH
OFF_K = OFF_Q + RET_QK_WIDTH
OFF_V = OFF_K + RET_QK_WIDTH
OFF_G = OFF_V + RET_V_WIDTH
OFF_DT = OFF_G + RET_V_WIDTH
IN_COLS_PACKED = OFF_DT + DT_PAD

LOG2E = 1.4426950408889634
VMEM_LIMIT = 56 * 1024 * 1024


def _silu(x):
    return x * (1.0 / (1.0 + jnp.exp(-x)))


def _dot(a, b):
    return jnp.dot(a, b, preferred_element_type=F32)


def _dot_nt(a, b):
    return lax.dot_general(a, b, (((1,), (1,)), ((), ())), preferred_element_type=F32)


def _split3(x):
    hi = x.astype(BF16)
    r1 = x - hi.astype(F32)
    mid = r1.astype(BF16)
    lo = (r1 - mid.astype(F32)).astype(BF16)
    return hi, mid, lo


PACK_COLS = 1024
PACK_STEPS = -(-IN_COLS_PACKED // PACK_COLS)


def _pack_kernel(wt_ref, out_ref):
    t = wt_ref[...].T
    lane = lax.broadcasted_iota(jnp.int32, t.shape, 1)
    is_dt = pl.program_id(0) == PACK_STEPS - 1
    t = jnp.where(jnp.logical_and(is_dt, lane >= 2 * SSD_HEADS), 0.0, t)
    out_ref[...] = t.astype(BF16)


def _pack_call(w_in_t):
    n_head = (OFF_Q - OFF_Z) // PACK_COLS
    src_dt = SSD_WIDTH + CONV_CH
    src_tail = src_dt + 2 * SSD_HEADS

    def src_row(i):
        t8 = jnp.where(i < n_head, i * (PACK_COLS // 8),
                       jnp.where(i < PACK_STEPS - 1, src_tail // 8 + (i - n_head) * (PACK_COLS // 8),
                                 src_dt // 8))
        return t8 * 8

    return pl.pallas_call(
        _pack_kernel,
        grid=(PACK_STEPS,),
        in_specs=[pl.BlockSpec((pl.Element(PACK_COLS), pl.Element(D_MODEL)), lambda i: (src_row(i), 0))],
        out_specs=pl.BlockSpec((D_MODEL, PACK_COLS), lambda i: (0, i)),
        out_shape=jax.ShapeDtypeStruct((D_MODEL, IN_COLS_PACKED), BF16),
        name="pack_w_in",
    )(w_in_t)


def _scale_rows_kernel(w_ref, s_ref, out_ref):
    out_ref[...] = (w_ref[...] * s_ref[...]).astype(BF16)


def _scale_rows_call(w, row_scale):
    rows, cols = w.shape
    tr = 512
    return pl.pallas_call(
        _scale_rows_kernel,
        grid=(rows // tr,),
        in_specs=[pl.BlockSpec((tr, cols), lambda i: (i, 0)), pl.BlockSpec((tr, 1), lambda i: (i, 0))],
        out_specs=pl.BlockSpec((tr, cols), lambda i: (i, 0)),
        out_shape=jax.ShapeDtypeStruct((rows, cols), BF16),
        name="fold_norm_w_out",
    )(w, row_scale)


def _mod_kernel(cond_ref, w_ref, b_ref, out_ref):
    a = _silu(cond_ref[...]).astype(BF16)
    out_ref[...] = _dot(a, w_ref[...].astype(BF16)) + b_ref[...]


def _mod_call(cond, w_mod, b_mod):
    rows = cond.shape[0]
    tn = 1024
    return pl.pallas_call(
        _mod_kernel,
        grid=(3 * D_MODEL // tn,),
        in_specs=[
            pl.BlockSpec((rows, D_MODEL), lambda j: (0, 0)),
            pl.BlockSpec((D_MODEL, tn), lambda j: (0, j)),
            pl.BlockSpec((1, tn), lambda j: (0, j)),
        ],
        out_specs=pl.BlockSpec((rows, tn), lambda j: (0, j)),
        out_shape=jax.ShapeDtypeStruct((rows, 3 * D_MODEL), F32),
        name="mod",
    )(cond, w_mod, b_mod)


def _modnorm(x, mod_ref, npw_ref):
    ms = jnp.mean(x * x, axis=-1, keepdims=True)
    shift = mod_ref[0, :, 0:D_MODEL]
    scale = mod_ref[0, :, D_MODEL:2 * D_MODEL]
    gain = npw_ref[...] * (1.0 + scale)
    return (x * lax.rsqrt(ms + EPS) * gain + shift).astype(BF16)


def _in_proj_kernel(*refs, seq_len, rope):
    if rope:
        (x_ref, mod_ref, npw_ref, w_ref, cw_ref, cb_ref, dtb_ref,
         cos_ref, sin_ref, zs_ref, xbc_ref, dt_ref, q_ref, k_ref, v_ref, gs_ref, h_ref) = refs
    else:
        (x_ref, mod_ref, npw_ref, w_ref, cw_ref, cb_ref, dtb_ref,
         zs_ref, xbc_ref, dt_ref, q_ref, k_ref, v_ref, gs_ref, h_ref) = refs
    rows = seq_len
    h_ref[...] = _modnorm(x_ref[0], mod_ref, npw_ref)
    piece = IN_PROJ_PIECE

    def put(ref, cols, val):
        ref[0, :, cols] = val

    row = lax.broadcasted_iota(jnp.int32, (rows, piece), 0)
    seq_first = row == 0
    seq_last = row == rows - 1

    def sec_xbc(c0):
        cols = slice(c0, c0 + piece)
        acc = _dot(h_ref[...], w_ref[:, OFF_XBC + c0:OFF_XBC + c0 + piece])
        up = jnp.where(seq_first, 0.0, pltpu.roll(acc, 1, axis=0))
        dn = jnp.where(seq_last, 0.0, pltpu.roll(acc, rows - 1, axis=0))
        y = cw_ref[0:1, cols] * up + cw_ref[1:2, cols] * acc + cw_ref[2:3, cols] * dn + cb_ref[:, cols]
        put(xbc_ref, cols, _silu(y).astype(BF16))

    lane = lax.broadcasted_iota(jnp.int32, (rows, LANES), 1)
    first_half = (lane % 32) < 16

    def sec_qk(off, ref, scl, c0):
        acc = _dot(h_ref[...], w_ref[:, off + c0:off + c0 + piece]) * scl
        if rope:
            for s0 in range(0, piece, LANES):
                xs = acc[:, s0:s0 + LANES]
                partner = jnp.where(first_half, pltpu.roll(xs, LANES - 16, axis=1),
                                    pltpu.roll(xs, 16, axis=1))
                put(ref, slice(c0 + s0, c0 + s0 + LANES),
                    (xs * cos_ref[...] + partner * sin_ref[...]).astype(BF16))
        else:
            put(ref, slice(c0, c0 + piece), acc.astype(BF16))

    def sec_q(c0):
        sec_qk(OFF_Q, q_ref, 1.0, c0)

    def sec_k(c0):
        sec_qk(OFF_K, k_ref, RET_QK_DIM ** -0.5, c0)

    def sec_z(c0):
        acc = _dot(h_ref[...], w_ref[:, OFF_Z + c0:OFF_Z + c0 + piece])
        put(zs_ref, slice(c0, c0 + piece), _silu(acc).astype(BF16))

    def sec_g(c0):
        acc = _dot(h_ref[...], w_ref[:, OFF_G + c0:OFF_G + c0 + piece])
        put(gs_ref, slice(c0, c0 + piece), _silu(acc).astype(BF16))

    def sec_v(c0):
        acc = _dot(h_ref[...], w_ref[:, OFF_V + c0:OFF_V + c0 + piece])
        put(v_ref, slice(c0, c0 + piece), acc.astype(BF16))

    def sec_dt(c0):
        acc = _dot(h_ref[...], w_ref[:, OFF_DT:OFF_DT + DT_PAD]) + dtb_ref[...]
        put(dt_ref, slice(0, DT_PAD), jnp.maximum(acc, 0.0) + jnp.log(1.0 + jnp.exp(-jnp.abs(acc))))

    def pieces(fn, width):
        return [(fn, c0) for c0 in range(0, width, piece)]

    conv = pieces(sec_xbc, CONV_CH)
    light = pieces(sec_v, RET_V_WIDTH) + [(sec_dt, 0)]
    order = pieces(sec_q, RET_QK_WIDTH) + pieces(sec_k, RET_QK_WIDTH)
    stride = -(-len(conv) // len(light))
    for i, item in enumerate(conv):
        order.append(item)
        if i % stride == stride - 1 and light:
            order.append(light.pop(0))
    order += light + pieces(sec_z, SSD_WIDTH) + pieces(sec_g, RET_V_WIDTH)
    for fn, c0 in order:
        fn(c0)


def _in_proj_call(x, mod, npw, w_packed, conv_w, conv_b, dtb, cos_t, sin_t):
    nb, L, _ = x.shape
    rope = cos_t is not None
    per_seq_mod = mod.shape[0] > 1
    assert L <= IN_PROJ_ROWS
    mod_map = (lambda s: (s, 0, 0)) if per_seq_mod else (lambda s: (0, 0, 0))
    const2 = lambda s: (0, 0)
    in_specs = [
        pl.BlockSpec((1, L, D_MODEL), lambda s: (s, 0, 0)),
        pl.BlockSpec((1, 1, 3 * D_MODEL), mod_map),
        pl.BlockSpec((1, D_MODEL), const2),
        pl.BlockSpec((D_MODEL, IN_COLS_PACKED), const2, pipeline_mode=pl.Buffered(1)),
        pl.BlockSpec((3, CONV_CH), const2),
        pl.BlockSpec((1, CONV_CH), const2),
        pl.BlockSpec((1, DT_PAD), const2),
    ]
    args = [x, mod, npw, w_packed, conv_w, conv_b, dtb]
    if rope:
        in_specs += [pl.BlockSpec((L, LANES), const2), pl.BlockSpec((L, LANES), const2)]
        args += [cos_t, sin_t]
    widths = (SSD_WIDTH, CONV_CH, DT_PAD, RET_QK_WIDTH, RET_QK_WIDTH, RET_V_WIDTH, RET_V_WIDTH)
    dtypes = (BF16, BF16, F32, BF16, BF16, BF16, BF16)
    out_specs = [pl.BlockSpec((1, L, w), lambda s: (s, 0, 0)) for w in widths]
    out_shape = [jax.ShapeDtypeStruct((nb, L, w), d) for w, d in zip(widths, dtypes)]
    return pl.pallas_call(
        functools.partial(_in_proj_kernel, seq_len=L, rope=rope),
        grid=(nb,),
        in_specs=in_specs,
        out_specs=out_specs,
        out_shape=out_shape,
        scratch_shapes=[pltpu.VMEM((L, D_MODEL), BF16)],
        compiler_params=pltpu.CompilerParams(
            dimension_semantics=("parallel",), vmem_limit_bytes=VMEM_LIMIT),
        name="in_proj_rope" if rope else "in_proj",
    )(*args)


def _mixer_kernel(*refs, n_streams, n_groups, nc, has_s0, emit_state, per_seq_mod):
    Q = CHUNK
    H = SSD_HEADS
    U = n_streams
    it = iter(refs)
    zs_ref, xbc_ref, dt_ref, q_ref, k_ref, v_ref, gs_ref = (next(it) for _ in range(7))
    nega_ref, dsk_ref, lam_ref = (next(it) for _ in range(3))
    wout_ref, xres_ref, modl_ref, npost_ref = (next(it) for _ in range(4))
    if has_s0:
        s0s_ref, s0r_ref = next(it), next(it)
    y_ref = next(it)
    if emit_state:
        os_ref, or_ref = next(it), next(it)
    (yas_ref, yar_ref, ss_ref, sr_ref, dsum_ref, ef_ref, eb_ref, tf_ref, tb_ref,
     ar_ref, exf_ref, exb_ref, cum_ref, mixs_ref) = (next(it) for _ in range(14))

    i = pl.program_id(0)
    n_real = n_groups * 2 * nc
    is_real = i < n_real
    ic = jnp.minimum(i, n_real - 1)
    s = ic % (2 * nc)
    phase2 = s >= nc
    c = jnp.where(phase2, 2 * nc - 1 - s, s)
    r0 = pl.multiple_of(c * Q, Q)
    slot_w = 1
    slot_r = 0

    rowq = lax.broadcasted_iota(jnp.int32, (Q, Q), 0)
    colq = lax.broadcasted_iota(jnp.int32, (Q, Q), 1)

    def out_proj_matmul():
        return _dot(mixs_ref[slot_r], wout_ref[...])

    def out_proj_finish(out):
        ms = jnp.mean(out * out, axis=-1, keepdims=True)
        o = out * lax.rsqrt(ms + EPS) * npost_ref[...]
        for u in range(U):
            gate = modl_ref[u if per_seq_mod else 0, :, 2 * D_MODEL:3 * D_MODEL]
            y_ref[u] = xres_ref[u] + gate * o[u * Q:(u + 1) * Q]

    @pl.when(i == 0)
    def _init_tables():
        mixs_ref[...] = jnp.zeros_like(mixs_ref)
        diff = (rowq - colq).astype(F32)
        rowf = lax.broadcasted_iota(jnp.int32, (Q, LANES), 0).astype(F32)
        colf = lax.broadcasted_iota(jnp.int32, (RET_QK_DIM, Q), 1).astype(F32)
        for h in range(RET_HEADS):
            pr, hh = divmod(h, 2)
            lf = lam_ref[h:h + 1, :]
            lb = lam_ref[RET_HEADS + h:RET_HEADS + h + 1, :]
            e = jnp.where(rowq >= colq, lf * diff, lb * (-diff))
            dsum_ref[pr, :, hh * Q:(hh + 1) * Q] = jnp.exp(e) * jnp.where(rowq == colq, 2.0, 1.0)
            hc = slice(hh * LANES, (hh + 1) * LANES)
            ef_ref[pr, :, hc] = jnp.exp(lf * (rowf + 1.0))
            eb_ref[pr, :, hc] = jnp.exp(lb * (Q - rowf))
            kr = slice(hh * RET_QK_DIM, (hh + 1) * RET_QK_DIM)
            tf_ref[pr, kr, :] = jnp.exp(lf * (Q - 1.0 - colf))
            tb_ref[pr, kr, :] = jnp.exp(lb * colf)
        ar_ref[...] = jnp.exp(lam_ref[...] * float(Q))
        er = lax.broadcasted_iota(jnp.int32, (LANES, SSD_WIDTH), 0)
        ec = lax.broadcasted_iota(jnp.int32, (LANES, SSD_WIDTH), 1) // SSD_HEAD_DIM
        exf_ref[...] = jnp.where(er == ec, 1.0, 0.0).astype(BF16)
        exb_ref[...] = jnp.where(er == ec + H, 1.0, 0.0).astype(BF16)

    def expand(w, a_row, ex_ref):
        r = BF16_ROWS
        a3 = jnp.concatenate(_split3(jnp.broadcast_to(a_row, (r, LANES))), axis=0)
        out = _dot(jnp.concatenate([w.astype(BF16), a3], axis=0), ex_ref[...])
        n = w.shape[0]
        return out[0:n], out[n:n + 1] + out[n + r:n + r + 1] + out[n + 2 * r:n + 2 * r + 1]

    def make_stream(u):
        zs, xbc, dtr, qr, kr_, vr, gsr = (r.at[u] for r in (zs_ref, xbc_ref, dt_ref, q_ref, k_ref,
                                                            v_ref, gs_ref))
        yas, yar, ss, sr, cums = (r.at[u] for r in (yas_ref, yar_ref, ss_ref, sr_ref, cum_ref))
        mix_rows = slice(u * Q, (u + 1) * Q)

        def init_state():
            if has_s0:
                for d in range(2):
                    for p in range(SSD_HEADS // 2):
                        pair_t = jnp.concatenate(
                            [s0s_ref[u, 0, d, 2 * p], s0s_ref[u, 0, d, 2 * p + 1]], axis=0)
                        ss[d, :, p * LANES:(p + 1) * LANES] = pair_t.T
                sr[...] = s0r_ref[u, 0]
            else:
                ss[...] = jnp.zeros_like(ss)
                sr[...] = jnp.zeros_like(sr)

        def ssd_state_update(d, wtail_x, a_x):
            for g in range(SSD_GROUPS):
                bgt = xbc[:, XBC_B + g * SSD_STATE:XBC_B + (g + 1) * SSD_STATE].T
                gc = slice(g * GROUP_W, (g + 1) * GROUP_W)
                xt = xbc[:, gc] * wtail_x[:, gc].astype(BF16)
                ss[d, :, gc] = ss[d, :, gc] * a_x[:, gc] + _dot(bgt, xt)

        def k_transposed(pr):
            return kr_[:, pr * LANES:(pr + 1) * LANES].T

        def ret_state_update(d, tail_ref, kts):
            for pr, kt in enumerate(kts):
                pc = slice(pr * PAIR_W, (pr + 1) * PAIR_W)
                ktt = (kt.astype(F32) * tail_ref[pr]).astype(BF16)
                ds = _dot(ktt, vr[:, pc])
                for hh in range(2):
                    h = 2 * pr + hh
                    a = ar_ref[d * RET_HEADS + h:d * RET_HEADS + h + 1, :]
                    sr[d, h] = (sr[d, h] * a
                                + ds[hh * RET_QK_DIM:(hh + 1) * RET_QK_DIM, hh * LANES:(hh + 1) * LANES])

        def emit_states(d):
            for p in range(SSD_HEADS // 2):
                pair_t = ss[d, :, p * LANES:(p + 1) * LANES].T
                os_ref[u, 0, d, 2 * p] = pair_t[0:SSD_HEAD_DIM]
                os_ref[u, 0, d, 2 * p + 1] = pair_t[SSD_HEAD_DIM:2 * SSD_HEAD_DIM]
            or_ref[u, 0, d] = sr[d]

        def ret_state_blockdiag(d, pr):
            z = jnp.zeros((RET_QK_DIM, RET_V_DIM), F32)
            top = jnp.concatenate([sr[d, 2 * pr], z], axis=1)
            bot = jnp.concatenate([z, sr[d, 2 * pr + 1]], axis=1)
            return jnp.concatenate([top, bot], axis=0).astype(BF16)

        def phase1():
            dt = dtr[...]
            la = dt * nega_ref[...]
            tl_bf = jnp.where(rowq >= colq, 1.0, 0.0).astype(BF16)
            c3 = _dot(tl_bf, jnp.concatenate(_split3(la), axis=1))
            cum = c3[:, 0:LANES] + c3[:, LANES:2 * LANES] + c3[:, 2 * LANES:3 * LANES]
            cums[pl.ds(r0, Q), :] = cum
            tot = cum[Q - 1:Q, :]
            rev = tot - cum + la
            yield

            sub_k = lax.broadcasted_iota(jnp.int32, (LANES, Q), 0) < RET_QK_DIM
            lane_v = lax.broadcasted_iota(jnp.int32, (Q, PAIR_W), 1) < RET_V_DIM
            kts = [k_transposed(pr) for pr in range(RET_HEADS // 2)]
            for pr, kt in enumerate(kts):
                qp = qr[:, pr * LANES:(pr + 1) * LANES]
                zk = jnp.zeros_like(kt)
                kbd = jnp.concatenate([jnp.where(sub_k, kt, zk), jnp.where(sub_k, zk, kt)], axis=1)
                a2 = _dot(qp, kbd)
                ad = (a2 * dsum_ref[pr]).astype(BF16)
                pc = slice(pr * PAIR_W, (pr + 1) * PAIR_W)
                vp = vr[:, pc]
                zv = jnp.zeros_like(vp)
                vbd = jnp.concatenate([jnp.where(lane_v, vp, zv), jnp.where(lane_v, zv, vp)], axis=0)
                y = _dot(ad, vbd) + _dot(qp, ret_state_blockdiag(0, pr)) * ef_ref[pr]
                yar[pl.ds(r0, Q), pc] = y
            ret_state_update(0, tf_ref, kts)
            yield

            gms, css = [], []
            for g in range(SSD_GROUPS):
                cg = xbc[:, XBC_C + g * SSD_STATE:XBC_C + (g + 1) * SSD_STATE]
                bg = xbc[:, XBC_B + g * SSD_STATE:XBC_B + (g + 1) * SSD_STATE]
                gms.append(_dot_nt(cg, bg))
                css.append(_dot(cg, ss[0, :, g * GROUP_W:(g + 1) * GROUP_W].astype(BF16)))
            yield PROJECT_POINT

            dt_t = dt.T
            ldt_t = jnp.log(dt_t)
            sub = lax.broadcasted_iota(jnp.int32, (LANES, Q), 0)
            adj_t = (jnp.where(sub < H, cum.T, rev.T) - ldt_t) * LOG2E
            dg_t = jnp.log(dt_t[0:H, :] + dt_t[H:2 * H, :]) * LOG2E
            cum2 = cum * LOG2E
            rev2 = rev * LOG2E
            wtail_f = dt * jnp.exp(tot - cum)
            ew_x, af_x = expand(jnp.concatenate([jnp.exp(cum), wtail_f], axis=0), jnp.exp(tot), exf_ref)
            ecum_x = ew_x[0:Q]
            wtail_x = ew_x[Q:2 * Q]
            lt = rowq > colq
            gt = rowq < colq
            lane_g = lax.broadcasted_iota(jnp.int32, (Q, GROUP_W), 1) // SSD_HEAD_DIM
            yield

            for g in range(SSD_GROUPS):
                gc = slice(g * GROUP_W, (g + 1) * GROUP_W)
                xg = xbc[:, gc]
                ws, xs = [], []
                for j in range(4):
                    h = 4 * g + j
                    hb = H + h
                    arg = jnp.where(lt, cum2[:, h:h + 1] - adj_t[h:h + 1, :],
                                    jnp.where(gt, rev2[:, hb:hb + 1] - adj_t[hb:hb + 1, :], dg_t[h:h + 1, :]))
                    ws.append((gms[g] * jnp.exp2(arg)).astype(BF16))
                    xs.append(jnp.where(lane_g == j, xg, jnp.zeros_like(xg)))
                y = _dot(jnp.concatenate(ws, axis=1), jnp.concatenate(xs, axis=0))
                yas[pl.ds(r0, Q), gc] = y + css[g] * ecum_x[:, gc]
            yield
            ssd_state_update(0, wtail_x, af_x)

        def phase2():
            for pr in range(RET_HEADS // 2):
                qp = qr[:, pr * LANES:(pr + 1) * LANES]
                pc = slice(pr * PAIR_W, (pr + 1) * PAIR_W)
                y2 = yar[pl.ds(r0, Q), pc] + _dot(qp, ret_state_blockdiag(1, pr)) * eb_ref[pr]
                for hh in range(2):
                    h = 2 * pr + hh
                    hc = slice(h * LANES, (h + 1) * LANES)
                    y = y2[:, hh * LANES:(hh + 1) * LANES]
                    mu = jnp.mean(y, axis=-1, keepdims=True)
                    yc = y - mu
                    var = jnp.mean(yc * yc, axis=-1, keepdims=True)
                    yn = yc * lax.rsqrt(var + EPS) * gsr[:, hc].astype(F32)
                    mixs_ref[slot_w, mix_rows, SSD_WIDTH + h * LANES:SSD_WIDTH + (h + 1) * LANES] = (
                        yn.astype(BF16))
            ret_state_update(1, tb_ref, [k_transposed(pr) for pr in range(RET_HEADS // 2)])
            yield

            dt = dtr[...]
            cum = cums[pl.ds(r0, Q), :]
            rev = cum[Q - 1:Q, :] - cum + dt * nega_ref[...]
            first = rev[0:1, :]
            wtail_b = dt * jnp.exp(first - rev)
            ew_x, ab_x = expand(jnp.concatenate([jnp.exp(rev), wtail_b], axis=0), jnp.exp(first), exb_ref)
            erev_x = ew_x[0:Q]
            wtail_x = ew_x[Q:2 * Q]

            parts = []
            ssq = None
            for g in range(SSD_GROUPS):
                cg = xbc[:, XBC_C + g * SSD_STATE:XBC_C + (g + 1) * SSD_STATE]
                gc = slice(g * GROUP_W, (g + 1) * GROUP_W)
                cs = _dot(cg, ss[1, :, gc].astype(BF16))
                y = (yas[pl.ds(r0, Q), gc] + cs * erev_x[:, gc]
                     + dsk_ref[:, gc] * xbc[:, gc].astype(F32))
                y = y * zs[:, gc].astype(F32)
                parts.append(y)
                ssq = y * y if ssq is None else ssq + y * y
            yield PROJECT_POINT

            inv = lax.rsqrt(jnp.sum(ssq, axis=-1, keepdims=True) * (1.0 / SSD_WIDTH) + EPS)
            for g, y in enumerate(parts):
                gc = slice(g * GROUP_W, (g + 1) * GROUP_W)
                mixs_ref[slot_w, mix_rows, gc] = (y * inv).astype(BF16)
            ssd_state_update(1, wtail_x, ab_x)

        return types.SimpleNamespace(init_state=init_state, emit_states=emit_states, phase1=phase1,
                                     phase2=phase2)

    streams = [make_stream(u) for u in range(U)]

    def run_interleaved(stage_gens, with_out_proj):
        proj = None
        live = list(stage_gens)
        while live:
            marks, nxt = [], []
            for gen in live:
                try:
                    marks.append(next(gen))
                    nxt.append(gen)
                except StopIteration:
                    pass
            if with_out_proj and PROJECT_POINT in marks:
                proj = out_proj_matmul()
            live = nxt
        return proj

    @pl.when(is_real & (s == 0))
    def _init_state():
        for st in streams:
            st.init_state()

    @pl.when(is_real & jnp.logical_not(phase2))
    def _forward_step():
        run_interleaved([st.phase1() for st in streams], False)

    @pl.when(is_real & phase2)
    def _backward_step():
        proj = run_interleaved([st.phase2() for st in streams], True)
        out_proj_finish(proj)
        mixs_ref[slot_r] = mixs_ref[slot_w]

    @pl.when(jnp.logical_not(is_real))
    def _drain():
        out_proj_finish(out_proj_matmul())

    if emit_state:
        @pl.when(is_real & (s == nc - 1))
        def _emit_fwd():
            for st in streams:
                st.emit_states(0)

        @pl.when(is_real & (s == 2 * nc - 1))
        def _emit_bwd():
            for st in streams:
                st.emit_states(1)


def _mixer_call(zs, xbc, dt, q, k, v, gs, nega, dsk, lamx, w_out_bf, x, mod, npost, s0_ssd, s0_ret,
                emit_state):
    nb, L, _ = zs.shape
    nc = L // CHUNK
    Q = CHUNK
    U = STREAMS
    assert nb % U == 0
    has_s0 = s0_ssd is not None
    n_groups = nb // U
    steps = 2 * nc
    n_real = n_groups * steps
    per_seq_mod = mod.shape[0] > 1

    def cur(i):
        ic = jnp.minimum(i, n_real - 1)
        return ic // steps, ic % steps

    def chunk_map(i):
        b, s = cur(i)
        return (b, jnp.where(s < nc, s, steps - 1 - s), 0)

    def phase2_map(i):
        b, s = cur(i)
        return (b, jnp.where(s < nc, nc - 1, steps - 1 - s), 0)

    def lag(i):
        b, s = i // steps, i % steps
        early = s <= nc
        return (jnp.where(early, jnp.maximum(b - 1, 0), b),
                jnp.where(early, jnp.where(b == 0, nc - 1, 0), steps - s))

    def lag_map(i):
        bl, cl = lag(i)
        return (bl, cl, 0)

    def lag_mod_map(i):
        bl, _ = lag(i)
        return (bl if per_seq_mod else 0, 0, 0)

    const2 = lambda i: (0, 0)
    in_specs = [
        pl.BlockSpec((U, Q, SSD_WIDTH), phase2_map),
        pl.BlockSpec((U, Q, CONV_CH), chunk_map),
        pl.BlockSpec((U, Q, DT_PAD), chunk_map),
        pl.BlockSpec((U, Q, RET_QK_WIDTH), chunk_map),
        pl.BlockSpec((U, Q, RET_QK_WIDTH), chunk_map),
        pl.BlockSpec((U, Q, RET_V_WIDTH), chunk_map),
        pl.BlockSpec((U, Q, RET_V_WIDTH), phase2_map),
        pl.BlockSpec((1, DT_PAD), const2),
        pl.BlockSpec((1, SSD_WIDTH), const2),
        pl.BlockSpec((2 * RET_HEADS, LANES), const2),
        pl.BlockSpec((MIX_WIDTH, D_MODEL), const2, pipeline_mode=pl.Buffered(1)),
        pl.BlockSpec((U, Q, D_MODEL), lag_map),
        pl.BlockSpec((U if per_seq_mod else 1, 1, 3 * D_MODEL), lag_mod_map),
        pl.BlockSpec((1, D_MODEL), const2),
    ]
    args = [zs, xbc, dt, q, k, v, gs, nega, dsk, lamx, w_out_bf, x, mod, npost]
    state_map = lambda i: (cur(i)[0], 0, 0, 0, 0, 0)
    ssd_state_block = (U, 1, 2, SSD_HEADS, SSD_HEAD_DIM, SSD_STATE)
    ret_state_block = (U, 1, 2, RET_HEADS, RET_QK_DIM, RET_V_DIM)
    if has_s0:
        in_specs += [pl.BlockSpec(ssd_state_block, state_map, pipeline_mode=pl.Buffered(1)),
                     pl.BlockSpec(ret_state_block, state_map, pipeline_mode=pl.Buffered(1))]
        args += [s0_ssd, s0_ret]
    out_specs = [pl.BlockSpec((U, Q, D_MODEL), lag_map)]
    out_shape = [jax.ShapeDtypeStruct((nb, L, D_MODEL), F32)]
    if emit_state:
        out_specs += [pl.BlockSpec(ssd_state_block, state_map), pl.BlockSpec(ret_state_block, state_map)]
        out_shape += [jax.ShapeDtypeStruct((nb,) + ssd_state_block[1:], F32),
                      jax.ShapeDtypeStruct((nb,) + ret_state_block[1:], F32)]
    n_pairs = RET_HEADS // 2
    scratch = [
        pltpu.VMEM((U, L, SSD_WIDTH), F32),
        pltpu.VMEM((U, L, RET_V_WIDTH), F32),
        pltpu.VMEM((U, 2, SSD_STATE, SSD_WIDTH), F32),
        pltpu.VMEM((U, 2, RET_HEADS, RET_QK_DIM, RET_V_DIM), F32),
        pltpu.VMEM((n_pairs, Q, 2 * Q), F32),
        pltpu.VMEM((n_pairs, Q, PAIR_W), F32),
        pltpu.VMEM((n_pairs, Q, PAIR_W), F32),
        pltpu.VMEM((n_pairs, LANES, Q), F32),
        pltpu.VMEM((n_pairs, LANES, Q), F32),
        pltpu.VMEM((2 * RET_HEADS, LANES), F32),
        pltpu.VMEM((LANES, SSD_WIDTH), BF16),
        pltpu.VMEM((LANES, SSD_WIDTH), BF16),
        pltpu.VMEM((U, L, DT_PAD), F32),
        pltpu.VMEM((2, U * Q, MIX_WIDTH), BF16),
    ]
    outs = pl.pallas_call(
        functools.partial(_mixer_kernel, n_streams=U, n_groups=n_groups, nc=nc, has_s0=has_s0,
                          emit_state=emit_state,
                          per_seq_mod=per_seq_mod),
        grid=(n_real + 1,),
        in_specs=in_specs,
        out_specs=out_specs,
        out_shape=out_shape,
        scratch_shapes=scratch,
        compiler_params=pltpu.CompilerParams(
            dimension_semantics=("arbitrary",), vmem_limit_bytes=VMEM_LIMIT),
        name="mixer_sample" if has_s0 else "mixer_prompt",
    )(*args)
    return outs


def _rope_tables(L):
    rows = L // GRID_W
    row = jnp.repeat(jnp.arange(rows, dtype=F32), GRID_W)
    col = jnp.tile(jnp.arange(GRID_W, dtype=F32), rows)
    half = RET_QK_DIM // 2
    inv = ROPE_BASE ** (-jnp.arange(0, half, 2, dtype=F32) / half)
    ang_r = row[:, None] * inv
    ang_c = col[:, None] * inv
    cos_h = jnp.concatenate([jnp.cos(ang_r), jnp.cos(ang_r), jnp.cos(ang_c), jnp.cos(ang_c)], axis=-1)
    sin_h = jnp.concatenate([-jnp.sin(ang_r), jnp.sin(ang_r), -jnp.sin(ang_c), jnp.sin(ang_c)], axis=-1)
    return jnp.tile(cos_h, (1, 2)), jnp.tile(sin_h, (1, 2))


def kernel(x_prompt, x_sample, state_ssd, state_ret, c, c_ctx, w_mod, b_mod, norm_pre_w, norm_post_w,
           w_in, conv_w, conv_b, ssd_A_log, ssd_dt_bias, ssd_D, ssd_norm_w, ret_decay, ret_norm_w, w_out):
    nb_s = x_sample.shape[0]
    l = 0

    n_dt = 2 * SSD_HEADS
    w_packed = _pack_call(jnp.swapaxes(w_in[l], 0, 1))
    mix_norm_w = jnp.concatenate([ssd_norm_w[l], ret_norm_w[l]]).reshape(MIX_WIDTH, 1)
    w_out_bf = _scale_rows_call(w_out[l], mix_norm_w)
    dtb = jnp.pad(ssd_dt_bias[l].reshape(1, n_dt), ((0, 0), (0, DT_PAD - n_dt)))
    nega = jnp.pad(-jnp.exp(ssd_A_log[l].reshape(1, n_dt)), ((0, 0), (0, DT_PAD - n_dt)))
    dsk = jnp.repeat(ssd_D[l], SSD_HEAD_DIM).reshape(1, SSD_WIDTH)
    lamx = jnp.broadcast_to(-jnp.exp(ret_decay[l].reshape(2 * RET_HEADS, 1)), (2 * RET_HEADS, LANES))
    npre = norm_pre_w[l].reshape(1, D_MODEL)
    npost = norm_post_w[l].reshape(1, D_MODEL)
    cb = conv_b[l].reshape(1, CONV_CH)
    cos_t, sin_t = _rope_tables(x_sample.shape[1])

    cond = jnp.concatenate([c, c_ctx[None, :], jnp.zeros((COND_ROWS - nb_s - 1, D_MODEL), F32)], axis=0)
    mod = _mod_call(cond, w_mod[l], b_mod[l].reshape(1, 3 * D_MODEL))
    mod_s = mod[:nb_s].reshape(nb_s, 1, 3 * D_MODEL)
    mod_p = mod[nb_s:nb_s + 1].reshape(1, 1, 3 * D_MODEL)

    pp = _in_proj_call(x_prompt, mod_p, npre, w_packed, conv_w[l], cb, dtb, None, None)
    y_p, st_ssd_t, st_ret = _mixer_call(*pp, nega, dsk, lamx, w_out_bf, x_prompt, mod_p, npost,
                                        None, None, True)

    ps = _in_proj_call(x_sample, mod_s, npre, w_packed, conv_w[l], cb, dtb, cos_t, sin_t)
    (y_s,) = _mixer_call(*ps, nega, dsk, lamx, w_out_bf, x_sample, mod_s, npost,
                         jnp.swapaxes(state_ssd, -1, -2), state_ret, False)
    return (y_p, y_s, jnp.swapaxes(st_ssd_t, -1, -2), st_ret)
```

```python
import functools
import types

import jax
import jax.numpy as jnp
import numpy as np
from jax import lax
from jax.experimental import pallas as pl
from jax.experimental.pallas import tpu as pltpu

F32 = jnp.float32
BF16 = jnp.bfloat16

D_MODEL = 1024
CHUNK = 128
GRID_W = 64
EPS = 1e-6
SSD_WIDTH = 1024
SSD_HEAD_DIM = 64
SSD_HEADS = 16
SSD_GROUPS = 4
SSD_STATE = 128
CONV_CH = SSD_WIDTH + 2 * SSD_GROUPS * SSD_STATE
RET_HEADS = 8
RET_QK_DIM = 64
RET_V_DIM = 128
RET_QK_WIDTH = RET_HEADS * RET_QK_DIM
RET_V_WIDTH = RET_HEADS * RET_V_DIM
MIX_WIDTH = SSD_WIDTH + RET_V_WIDTH
ROPE_BASE = 10000.0
LANES = 128
DT_PAD = LANES
GROUP_W = SSD_WIDTH // SSD_GROUPS
PAIR_W = 2 * RET_V_DIM
IN_PROJ_PIECE = 256
IN_PROJ_ROWS = 1024
STREAMS = 2
BF16_ROWS = 16
XBC_B = SSD_WIDTH
XBC_C = XBC_B + SSD_GROUPS * SSD_STATE
COND_ROWS = 16
PROJECT_POINT = "project"

OFF_Z = 0
OFF_XBC = OFF_Z + SSD_WIDTH
OFF_Q = OFF_XBC + CONV_CH
OFF_K = OFF_Q + RET_QK_WIDTH
OFF_V = OFF_K + RET_QK_WIDTH
OFF_G = OFF_V + RET_V_WIDTH
OFF_DT = OFF_G + RET_V_WIDTH
IN_COLS_PACKED = OFF_DT + DT_PAD

LOG2E = 1.4426950408889634
VMEM_LIMIT = 56 * 1024 * 1024


def _silu(x):
    return x * (1.0 / (1.0 + jnp.exp(-x)))


def _dot(a, b):
    return jnp.dot(a, b, preferred_element_type=F32)


def _dot_nt(a, b):
    return lax.dot_general(a, b, (((1,), (1,)), ((), ())), preferred_element_type=F32)


def _split3(x):
    hi = x.astype(BF16)
    r1 = x - hi.astype(F32)
    mid = r1.astype(BF16)
    lo = (r1 - mid.astype(F32)).astype(BF16)
    return hi, mid, lo


PACK_COLS = 1024
PACK_STEPS = -(-IN_COLS_PACKED // PACK_COLS)


def _pack_kernel(wt_ref, out_ref):
    t = wt_ref[...].T
    lane = lax.broadcasted_iota(jnp.int32, t.shape, 1)
    is_dt = pl.program_id(0) == PACK_STEPS - 1
    t = jnp.where(jnp.logical_and(is_dt, lane >= 2 * SSD_HEADS), 0.0, t)
    out_ref[...] = t.astype(BF16)


def _pack_call(w_in_t):
    n_head = (OFF_Q - OFF_Z) // PACK_COLS
    src_dt = SSD_WIDTH + CONV_CH
    src_tail = src_dt + 2 * SSD_HEADS

    def src_row(i):
        t8 = jnp.where(i < n_head, i * (PACK_COLS // 8),
                       jnp.where(i < PACK_STEPS - 1, src_tail // 8 + (i - n_head) * (PACK_COLS // 8),
                                 src_dt // 8))
        return t8 * 8

    return pl.pallas_call(
        _pack_kernel,
        grid=(PACK_STEPS,),
        in_specs=[pl.BlockSpec((pl.Element(PACK_COLS), pl.Element(D_MODEL)), lambda i: (src_row(i), 0))],
        out_specs=pl.BlockSpec((D_MODEL, PACK_COLS), lambda i: (0, i)),
        out_shape=jax.ShapeDtypeStruct((D_MODEL, IN_COLS_PACKED), BF16),
        name="pack_w_in",
    )(w_in_t)


def _scale_rows_kernel(w_ref, s_ref, out_ref):
    out_ref[...] = (w_ref[...] * s_ref[...]).astype(BF16)


def _scale_rows_call(w, row_scale):
    rows, cols = w.shape
    tr = 512
    return pl.pallas_call(
        _scale_rows_kernel,
        grid=(rows // tr,),
        in_specs=[pl.BlockSpec((tr, cols), lambda i: (i, 0)), pl.BlockSpec((tr, 1), lambda i: (i, 0))],
        out_specs=pl.BlockSpec((tr, cols), lambda i: (i, 0)),
        out_shape=jax.ShapeDtypeStruct((rows, cols), BF16),
        name="fold_norm_w_out",
    )(w, row_scale)


def _mod_kernel(cond_ref, w_ref, b_ref, out_ref):
    a = _silu(cond_ref[...]).astype(BF16)
    out_ref[...] = _dot(a, w_ref[...].astype(BF16)) + b_ref[...]


def _mod_call(cond, w_mod, b_mod):
    rows = cond.shape[0]
    tn = 1024
    return pl.pallas_call(
        _mod_kernel,
        grid=(3 * D_MODEL // tn,),
        in_specs=[
            pl.BlockSpec((rows, D_MODEL), lambda j: (0, 0)),
            pl.BlockSpec((D_MODEL, tn), lambda j: (0, j)),
            pl.BlockSpec((1, tn), lambda j: (0, j)),
        ],
        out_specs=pl.BlockSpec((rows, tn), lambda j: (0, j)),
        out_shape=jax.ShapeDtypeStruct((rows, 3 * D_MODEL), F32),
        name="mod",
    )(cond, w_mod, b_mod)


def _modnorm(x, mod_ref, npw_ref):
    ms = jnp.mean(x * x, axis=-1, keepdims=True)
    shift = mod_ref[0, :, 0:D_MODEL]
    scale = mod_ref[0, :, D_MODEL:2 * D_MODEL]
    gain = npw_ref[...] * (1.0 + scale)
    return (x * lax.rsqrt(ms + EPS) * gain + shift).astype(BF16)


def _in_proj_kernel(*refs, seq_len, rope):
    if rope:
        (x_ref, mod_ref, npw_ref, w_ref, cw_ref, cb_ref, dtb_ref,
         cos_ref, sin_ref, zs_ref, xbc_ref, dt_ref, q_ref, k_ref, v_ref, gs_ref, h_ref) = refs
    else:
        (x_ref, mod_ref, npw_ref, w_ref, cw_ref, cb_ref, dtb_ref,
         zs_ref, xbc_ref, dt_ref, q_ref, k_ref, v_ref, gs_ref, h_ref) = refs
    rows = seq_len
    h_ref[...] = _modnorm(x_ref[0], mod_ref, npw_ref)
    piece = IN_PROJ_PIECE

    def put(ref, cols, val):
        ref[0, :, cols] = val

    row = lax.broadcasted_iota(jnp.int32, (rows, piece), 0)
    seq_first = row == 0
    seq_last = row == rows - 1

    def sec_xbc(c0):
        cols = slice(c0, c0 + piece)
        acc = _dot(h_ref[...], w_ref[:, OFF_XBC + c0:OFF_XBC + c0 + piece])
        up = jnp.where(seq_first, 0.0, pltpu.roll(acc, 1, axis=0))
        dn = jnp.where(seq_last, 0.0, pltpu.roll(acc, rows - 1, axis=0))
        y = cw_ref[0:1, cols] * up + cw_ref[1:2, cols] * acc + cw_ref[2:3, cols] * dn + cb_ref[:, cols]
        put(xbc_ref, cols, _silu(y).astype(BF16))

    lane = lax.broadcasted_iota(jnp.int32, (rows, LANES), 1)
    first_half = (lane % 32) < 16

    def sec_qk(off, ref, scl, c0):
        acc = _dot(h_ref[...], w_ref[:, off + c0:off + c0 + piece]) * scl
        if rope:
            for s0 in range(0, piece, LANES):
                xs = acc[:, s0:s0 + LANES]
                partner = jnp.where(first_half, pltpu.roll(xs, LANES - 16, axis=1),
                                    pltpu.roll(xs, 16, axis=1))
                put(ref, slice(c0 + s0, c0 + s0 + LANES),
                    (xs * cos_ref[...] + partner * sin_ref[...]).astype(BF16))
        else:
            put(ref, slice(c0, c0 + piece), acc.astype(BF16))

    def sec_q(c0):
        sec_qk(OFF_Q, q_ref, 1.0, c0)

    def sec_k(c0):
        sec_qk(OFF_K, k_ref, RET_QK_DIM ** -0.5, c0)

    def sec_z(c0):
        acc = _dot(h_ref[...], w_ref[:, OFF_Z + c0:OFF_Z + c0 + piece])
        put(zs_ref, slice(c0, c0 + piece), _silu(acc).astype(BF16))

    def sec_g(c0):
        acc = _dot(h_ref[...], w_ref[:, OFF_G + c0:OFF_G + c0 + piece])
        put(gs_ref, slice(c0, c0 + piece), _silu(acc).astype(BF16))

    def sec_v(c0):
        acc = _dot(h_ref[...], w_ref[:, OFF_V + c0:OFF_V + c0 + piece])
        put(v_ref, slice(c0, c0 + piece), acc.astype(BF16))

    def sec_dt(c0):
        acc = _dot(h_ref[...], w_ref[:, OFF_DT:OFF_DT + DT_PAD]) + dtb_ref[...]
        put(dt_ref, slice(0, DT_PAD), jnp.maximum(acc, 0.0) + jnp.log(1.0 + jnp.exp(-jnp.abs(acc))))

    def pieces(fn, width):
        return [(fn, c0) for c0 in range(0, width, piece)]

    conv = pieces(sec_xbc, CONV_CH)
    z, g, v = pieces(sec_z, SSD_WIDTH), pieces(sec_g, RET_V_WIDTH), pieces(sec_v, RET_V_WIDTH)
    lighter = [sec for trio in zip(z, g, v) for sec in trio] + [(sec_dt, 0)]
    order = pieces(sec_q, RET_QK_WIDTH) + pieces(sec_k, RET_QK_WIDTH)
    for item in conv:
        order += [item, lighter.pop(0)]
    order += lighter
    for fn, c0 in order:
        fn(c0)


def _in_proj_call(x, mod, npw, w_packed, conv_w, conv_b, dtb, cos_t, sin_t):
    nb, L, _ = x.shape
    rope = cos_t is not None
    per_seq_mod = mod.shape[0] > 1
    assert L <= IN_PROJ_ROWS
    mod_map = (lambda s: (s, 0, 0)) if per_seq_mod else (lambda s: (0, 0, 0))
    const2 = lambda s: (0, 0)
    in_specs = [
        pl.BlockSpec((1, L, D_MODEL), lambda s: (s, 0, 0)),
        pl.BlockSpec((1, 1, 3 * D_MODEL), mod_map),
        pl.BlockSpec((1, D_MODEL), const2),
        pl.BlockSpec((D_MODEL, IN_COLS_PACKED), const2, pipeline_mode=pl.Buffered(1)),
        pl.BlockSpec((3, CONV_CH), const2),
        pl.BlockSpec((1, CONV_CH), const2),
        pl.BlockSpec((1, DT_PAD), const2),
    ]
    args = [x, mod, npw, w_packed, conv_w, conv_b, dtb]
    if rope:
        in_specs += [pl.BlockSpec((L, LANES), const2), pl.BlockSpec((L, LANES), const2)]
        args += [cos_t, sin_t]
    widths = (SSD_WIDTH, CONV_CH, DT_PAD, RET_QK_WIDTH, RET_QK_WIDTH, RET_V_WIDTH, RET_V_WIDTH)
    dtypes = (BF16, BF16, F32, BF16, BF16, BF16, BF16)
    out_specs = [pl.BlockSpec((1, L, w), lambda s: (s, 0, 0)) for w in widths]
    out_shape = [jax.ShapeDtypeStruct((nb, L, w), d) for w, d in zip(widths, dtypes)]
    return pl.pallas_call(
        functools.partial(_in_proj_kernel, seq_len=L, rope=rope),
        grid=(nb,),
        in_specs=in_specs,
        out_specs=out_specs,
        out_shape=out_shape,
        scratch_shapes=[pltpu.VMEM((L, D_MODEL), BF16)],
        compiler_params=pltpu.CompilerParams(
            dimension_semantics=("parallel",), vmem_limit_bytes=VMEM_LIMIT),
        name="in_proj_rope" if rope else "in_proj",
    )(*args)


def _mixer_kernel(*refs, n_streams, n_groups, nc, has_s0, emit_state, per_seq_mod):
    Q = CHUNK
    H = SSD_HEADS
    U = n_streams
    it = iter(refs)
    zs_ref, xbc_ref, dt_ref, q_ref, k_ref, v_ref, gs_ref = (next(it) for _ in range(7))
    nega_ref, dsk_ref, lam_ref = (next(it) for _ in range(3))
    wout_ref, xres_ref, modl_ref, npost_ref = (next(it) for _ in range(4))
    if has_s0:
        s0s_ref, s0r_ref = next(it), next(it)
    y_ref = next(it)
    if emit_state:
        os_ref, or_ref = next(it), next(it)
    (yas_ref, yar_ref, ss_ref, sr_ref, dsum_ref, ef_ref, eb_ref, tf_ref, tb_ref,
     ar_ref, exf_ref, exb_ref, cum_ref, mixs_ref) = (next(it) for _ in range(14))

    i = pl.program_id(0)
    n_real = n_groups * 2 * nc
    is_real = i < n_real
    ic = jnp.minimum(i, n_real - 1)
    s = ic % (2 * nc)
    phase2 = s >= nc
    c = jnp.where(phase2, 2 * nc - 1 - s, s)
    r0 = pl.multiple_of(c * Q, Q)
    slot_w = 1
    slot_r = 0

    rowq = lax.broadcasted_iota(jnp.int32, (Q, Q), 0)
    colq = lax.broadcasted_iota(jnp.int32, (Q, Q), 1)

    def out_proj_matmul():
        return _dot(mixs_ref[slot_r], wout_ref[...])

    def out_proj_finish(out):
        ms = jnp.mean(out * out, axis=-1, keepdims=True)
        o = out * lax.rsqrt(ms + EPS) * npost_ref[...]
        for u in range(U):
            gate = modl_ref[u if per_seq_mod else 0, :, 2 * D_MODEL:3 * D_MODEL]
            y_ref[u] = xres_ref[u] + gate * o[u * Q:(u + 1) * Q]

    @pl.when(i == 0)
    def _init_tables():
        mixs_ref[...] = jnp.zeros_like(mixs_ref)
        diff = (rowq - colq).astype(F32)
        rowf = lax.broadcasted_iota(jnp.int32, (Q, LANES), 0).astype(F32)
        colf = lax.broadcasted_iota(jnp.int32, (RET_QK_DIM, Q), 1).astype(F32)
        for h in range(RET_HEADS):
            pr, hh = divmod(h, 2)
            lf = lam_ref[h:h + 1, :]
            lb = lam_ref[RET_HEADS + h:RET_HEADS + h + 1, :]
            e = jnp.where(rowq >= colq, lf * diff, lb * (-diff))
            dsum_ref[pr, :, hh * Q:(hh + 1) * Q] = jnp.exp(e) * jnp.where(rowq == colq, 2.0, 1.0)
            hc = slice(hh * LANES, (hh + 1) * LANES)
            ef_ref[pr, :, hc] = jnp.exp(lf * (rowf + 1.0))
            eb_ref[pr, :, hc] = jnp.exp(lb * (Q - rowf))
            kr = slice(hh * RET_QK_DIM, (hh + 1) * RET_QK_DIM)
            tf_ref[pr, kr, :] = jnp.exp(lf * (Q - 1.0 - colf))
            tb_ref[pr, kr, :] = jnp.exp(lb * colf)
        ar_ref[...] = jnp.exp(lam_ref[...] * float(Q))
        er = lax.broadcasted_iota(jnp.int32, (LANES, SSD_WIDTH), 0)
        ec = lax.broadcasted_iota(jnp.int32, (LANES, SSD_WIDTH), 1) // SSD_HEAD_DIM
        exf_ref[...] = jnp.where(er == ec, 1.0, 0.0).astype(BF16)
        exb_ref[...] = jnp.where(er == ec + H, 1.0, 0.0).astype(BF16)

    def expand(w, a_row, ex_ref):
        r = BF16_ROWS
        a3 = jnp.concatenate(_split3(jnp.broadcast_to(a_row, (r, LANES))), axis=0)
        out = _dot(jnp.concatenate([w.astype(BF16), a3], axis=0), ex_ref[...])
        n = w.shape[0]
        return out[0:n], out[n:n + 1] + out[n + r:n + r + 1] + out[n + 2 * r:n + 2 * r + 1]

    def make_stream(u):
        zs, xbc, dtr, qr, kr_, vr, gsr = (r.at[u] for r in (zs_ref, xbc_ref, dt_ref, q_ref, k_ref,
                                                            v_ref, gs_ref))
        yas, yar, ss, sr, cums = (r.at[u] for r in (yas_ref, yar_ref, ss_ref, sr_ref, cum_ref))
        mix_rows = slice(u * Q, (u + 1) * Q)

        def init_state():
            if has_s0:
                for d in range(2):
                    for p in range(SSD_HEADS // 2):
                        pair_t = jnp.concatenate(
                            [s0s_ref[u, 0, d, 2 * p], s0s_ref[u, 0, d, 2 * p + 1]], axis=0)
                        ss[d, :, p * LANES:(p + 1) * LANES] = pair_t.T
                sr[...] = s0r_ref[u, 0]
            else:
                ss[...] = jnp.zeros_like(ss)
                sr[...] = jnp.zeros_like(sr)

        def ssd_state_update(d, wtail_x, a_x):
            for g in range(SSD_GROUPS):
                bgt = xbc[:, XBC_B + g * SSD_STATE:XBC_B + (g + 1) * SSD_STATE].T
                gc = slice(g * GROUP_W, (g + 1) * GROUP_W)
                xt = xbc[:, gc] * wtail_x[:, gc].astype(BF16)
                ss[d, :, gc] = ss[d, :, gc] * a_x[:, gc] + _dot(bgt, xt)

        def k_transposed(pr):
            return kr_[:, pr * LANES:(pr + 1) * LANES].T

        def ret_state_update(d, tail_ref, kts):
            for pr, kt in enumerate(kts):
                pc = slice(pr * PAIR_W, (pr + 1) * PAIR_W)
                ktt = (kt.astype(F32) * tail_ref[pr]).astype(BF16)
                ds = _dot(ktt, vr[:, pc])
                for hh in range(2):
                    h = 2 * pr + hh
                    a = ar_ref[d * RET_HEADS + h:d * RET_HEADS + h + 1, :]
                    sr[d, h] = (sr[d, h] * a
                                + ds[hh * RET_QK_DIM:(hh + 1) * RET_QK_DIM, hh * LANES:(hh + 1) * LANES])

        def emit_states(d):
            for p in range(SSD_HEADS // 2):
                pair_t = ss[d, :, p * LANES:(p + 1) * LANES].T
                os_ref[u, 0, d, 2 * p] = pair_t[0:SSD_HEAD_DIM]
                os_ref[u, 0, d, 2 * p + 1] = pair_t[SSD_HEAD_DIM:2 * SSD_HEAD_DIM]
            or_ref[u, 0, d] = sr[d]

        def ret_state_blockdiag(d, pr):
            z = jnp.zeros((RET_QK_DIM, RET_V_DIM), F32)
            top = jnp.concatenate([sr[d, 2 * pr], z], axis=1)
            bot = jnp.concatenate([z, sr[d, 2 * pr + 1]], axis=1)
            return jnp.concatenate([top, bot], axis=0).astype(BF16)

        def phase1():
            dt = dtr[...]
            la = dt * nega_ref[...]
            tl_bf = jnp.where(rowq >= colq, 1.0, 0.0).astype(BF16)
            c3 = _dot(tl_bf, jnp.concatenate(_split3(la), axis=1))
            cum = c3[:, 0:LANES] + c3[:, LANES:2 * LANES] + c3[:, 2 * LANES:3 * LANES]
            cums[pl.ds(r0, Q), :] = cum
            tot = cum[Q - 1:Q, :]
            rev = tot - cum + la
            yield

            sub_k = lax.broadcasted_iota(jnp.int32, (LANES, Q), 0) < RET_QK_DIM
            lane_v = lax.broadcasted_iota(jnp.int32, (Q, PAIR_W), 1) < RET_V_DIM
            kts = [k_transposed(pr) for pr in range(RET_HEADS // 2)]
            for pr, kt in enumerate(kts):
                qp = qr[:, pr * LANES:(pr + 1) * LANES]
                zk = jnp.zeros_like(kt)
                kbd = jnp.concatenate([jnp.where(sub_k, kt, zk), jnp.where(sub_k, zk, kt)], axis=1)
                a2 = _dot(qp, kbd)
                ad = (a2 * dsum_ref[pr]).astype(BF16)
                pc = slice(pr * PAIR_W, (pr + 1) * PAIR_W)
                vp = vr[:, pc]
                zv = jnp.zeros_like(vp)
                vbd = jnp.concatenate([jnp.where(lane_v, vp, zv), jnp.where(lane_v, zv, vp)], axis=0)
                y = _dot(ad, vbd) + _dot(qp, ret_state_blockdiag(0, pr)) * ef_ref[pr]
                yar[pl.ds(r0, Q), pc] = y
            ret_state_update(0, tf_ref, kts)
            yield

            gms, css = [], []
            for g in range(SSD_GROUPS):
                cg = xbc[:, XBC_C + g * SSD_STATE:XBC_C + (g + 1) * SSD_STATE]
                bg = xbc[:, XBC_B + g * SSD_STATE:XBC_B + (g + 1) * SSD_STATE]
                gms.append(_dot_nt(cg, bg))
                css.append(_dot(cg, ss[0, :, g * GROUP_W:(g + 1) * GROUP_W].astype(BF16)))
            yield PROJECT_POINT

            dt_t = dt.T
            ldt_t = jnp.log(dt_t)
            sub = lax.broadcasted_iota(jnp.int32, (LANES, Q), 0)
            adj_t = (jnp.where(sub < H, cum.T, rev.T) - ldt_t) * LOG2E
            dg_t = jnp.log(dt_t[0:H, :] + dt_t[H:2 * H, :]) * LOG2E
            cum2 = cum * LOG2E
            rev2 = rev * LOG2E
            wtail_f = dt * jnp.exp(tot - cum)
            ew_x, af_x = expand(jnp.concatenate([jnp.exp(cum), wtail_f], axis=0), jnp.exp(tot), exf_ref)
            ecum_x = ew_x[0:Q]
            wtail_x = ew_x[Q:2 * Q]
            lt = rowq > colq
            gt = rowq < colq
            lane_g = lax.broadcasted_iota(jnp.int32, (Q, GROUP_W), 1) // SSD_HEAD_DIM
            yield

            for g in range(SSD_GROUPS):
                gc = slice(g * GROUP_W, (g + 1) * GROUP_W)
                xg = xbc[:, gc]
                ws, xs = [], []
                for j in range(4):
                    h = 4 * g + j
                    hb = H + h
                    arg = jnp.where(lt, cum2[:, h:h + 1] - adj_t[h:h + 1, :],
                                    jnp.where(gt, rev2[:, hb:hb + 1] - adj_t[hb:hb + 1, :], dg_t[h:h + 1, :]))
                    ws.append((gms[g] * jnp.exp2(arg)).astype(BF16))
                    xs.append(jnp.where(lane_g == j, xg, jnp.zeros_like(xg)))
                y = _dot(jnp.concatenate(ws, axis=1), jnp.concatenate(xs, axis=0))
                yas[pl.ds(r0, Q), gc] = y + css[g] * ecum_x[:, gc]
            yield
            ssd_state_update(0, wtail_x, af_x)

        def phase2():
            for pr in range(RET_HEADS // 2):
                qp = qr[:, pr * LANES:(pr + 1) * LANES]
                pc = slice(pr * PAIR_W, (pr + 1) * PAIR_W)
                y2 = yar[pl.ds(r0, Q), pc] + _dot(qp, ret_state_blockdiag(1, pr)) * eb_ref[pr]
                for hh in range(2):
                    h = 2 * pr + hh
                    hc = slice(h * LANES, (h + 1) * LANES)
                    y = y2[:, hh * LANES:(hh + 1) * LANES]
                    mu = jnp.mean(y, axis=-1, keepdims=True)
                    yc = y - mu
                    var = jnp.mean(yc * yc, axis=-1, keepdims=True)
                    yn = yc * lax.rsqrt(var + EPS) * gsr[:, hc].astype(F32)
                    mixs_ref[slot_w, mix_rows, SSD_WIDTH + h * LANES:SSD_WIDTH + (h + 1) * LANES] = (
                        yn.astype(BF16))
            ret_state_update(1, tb_ref, [k_transposed(pr) for pr in range(RET_HEADS // 2)])
            yield

            dt = dtr[...]
            cum = cums[pl.ds(r0, Q), :]
            rev = cum[Q - 1:Q, :] - cum + dt * nega_ref[...]
            first = rev[0:1, :]
            wtail_b = dt * jnp.exp(first - rev)
            ew_x, ab_x = expand(jnp.concatenate([jnp.exp(rev), wtail_b], axis=0), jnp.exp(first), exb_ref)
            erev_x = ew_x[0:Q]
            wtail_x = ew_x[Q:2 * Q]

            parts = []
            ssq = None
            for g in range(SSD_GROUPS):
                cg = xbc[:, XBC_C + g * SSD_STATE:XBC_C + (g + 1) * SSD_STATE]
                gc = slice(g * GROUP_W, (g + 1) * GROUP_W)
                cs = _dot(cg, ss[1, :, gc].astype(BF16))
                y = (yas[pl.ds(r0, Q), gc] + cs * erev_x[:, gc]
                     + dsk_ref[:, gc] * xbc[:, gc].astype(F32))
                y = y * zs[:, gc].astype(F32)
                parts.append(y)
                ssq = y * y if ssq is None else ssq + y * y
            yield PROJECT_POINT

            inv = lax.rsqrt(jnp.sum(ssq, axis=-1, keepdims=True) * (1.0 / SSD_WIDTH) + EPS)
            for g, y in enumerate(parts):
                gc = slice(g * GROUP_W, (g + 1) * GROUP_W)
                mixs_ref[slot_w, mix_rows, gc] = (y * inv).astype(BF16)
            ssd_state_update(1, wtail_x, ab_x)

        return types.SimpleNamespace(init_state=init_state, emit_states=emit_states, phase1=phase1,
                                     phase2=phase2)

    streams = [make_stream(u) for u in range(U)]

    def run_interleaved(stage_gens, with_out_proj):
        proj = None
        live = list(stage_gens)
        while live:
            marks, nxt = [], []
            for gen in live:
                try:
                    marks.append(next(gen))
                    nxt.append(gen)
                except StopIteration:
                    pass
            if with_out_proj and PROJECT_POINT in marks:
                proj = out_proj_matmul()
            live = nxt
        return proj

    @pl.when(is_real & (s == 0))
    def _init_state():
        for st in streams:
            st.init_state()

    @pl.when(is_real & jnp.logical_not(phase2))
    def _forward_step():
        run_interleaved([st.phase1() for st in streams], False)

    @pl.when(is_real & phase2)
    def _backward_step():
        proj = run_interleaved([st.phase2() for st in streams], True)
        out_proj_finish(proj)
        mixs_ref[slot_r] = mixs_ref[slot_w]

    @pl.when(jnp.logical_not(is_real))
    def _drain():
        out_proj_finish(out_proj_matmul())

    if emit_state:
        @pl.when(is_real & (s == nc - 1))
        def _emit_fwd():
            for st in streams:
                st.emit_states(0)

        @pl.when(is_real & (s == 2 * nc - 1))
        def _emit_bwd():
            for st in streams:
                st.emit_states(1)


def _mixer_call(zs, xbc, dt, q, k, v, gs, nega, dsk, lamx, w_out_bf, x, mod, npost, s0_ssd, s0_ret,
                emit_state):
    nb, L, _ = zs.shape
    nc = L // CHUNK
    Q = CHUNK
    U = STREAMS
    assert nb % U == 0
    has_s0 = s0_ssd is not None
    n_groups = nb // U
    steps = 2 * nc
    n_real = n_groups * steps
    per_seq_mod = mod.shape[0] > 1

    def cur(i):
        ic = jnp.minimum(i, n_real - 1)
        return ic // steps, ic % steps

    def chunk_map(i):
        b, s = cur(i)
        return (b, jnp.where(s < nc, s, steps - 1 - s), 0)

    def phase2_map(i):
        b, s = cur(i)
        return (b, jnp.where(s < nc, nc - 1, steps - 1 - s), 0)

    def lag(i):
        b, s = i // steps, i % steps
        early = s <= nc
        return (jnp.where(early, jnp.maximum(b - 1, 0), b),
                jnp.where(early, jnp.where(b == 0, nc - 1, 0), steps - s))

    def lag_map(i):
        bl, cl = lag(i)
        return (bl, cl, 0)

    def lag_mod_map(i):
        bl, _ = lag(i)
        return (bl if per_seq_mod else 0, 0, 0)

    const2 = lambda i: (0, 0)
    in_specs = [
        pl.BlockSpec((U, Q, SSD_WIDTH), phase2_map),
        pl.BlockSpec((U, Q, CONV_CH), chunk_map),
        pl.BlockSpec((U, Q, DT_PAD), chunk_map),
        pl.BlockSpec((U, Q, RET_QK_WIDTH), chunk_map),
        pl.BlockSpec((U, Q, RET_QK_WIDTH), chunk_map),
        pl.BlockSpec((U, Q, RET_V_WIDTH), chunk_map),
        pl.BlockSpec((U, Q, RET_V_WIDTH), phase2_map),
        pl.BlockSpec((1, DT_PAD), const2),
        pl.BlockSpec((1, SSD_WIDTH), const2),
        pl.BlockSpec((2 * RET_HEADS, LANES), const2),
        pl.BlockSpec((MIX_WIDTH, D_MODEL), const2, pipeline_mode=pl.Buffered(1)),
        pl.BlockSpec((U, Q, D_MODEL), lag_map),
        pl.BlockSpec((U if per_seq_mod else 1, 1, 3 * D_MODEL), lag_mod_map),
        pl.BlockSpec((1, D_MODEL), const2),
    ]
    args = [zs, xbc, dt, q, k, v, gs, nega, dsk, lamx, w_out_bf, x, mod, npost]
    state_map = lambda i: (cur(i)[0], 0, 0, 0, 0, 0)
    ssd_state_block = (U, 1, 2, SSD_HEADS, SSD_HEAD_DIM, SSD_STATE)
    ret_state_block = (U, 1, 2, RET_HEADS, RET_QK_DIM, RET_V_DIM)
    if has_s0:
        in_specs += [pl.BlockSpec(ssd_state_block, state_map, pipeline_mode=pl.Buffered(1)),
                     pl.BlockSpec(ret_state_block, state_map, pipeline_mode=pl.Buffered(1))]
        args += [s0_ssd, s0_ret]
    out_specs = [pl.BlockSpec((U, Q, D_MODEL), lag_map)]
    out_shape = [jax.ShapeDtypeStruct((nb, L, D_MODEL), F32)]
    if emit_state:
        out_specs += [pl.BlockSpec(ssd_state_block, state_map), pl.BlockSpec(ret_state_block, state_map)]
        out_shape += [jax.ShapeDtypeStruct((nb,) + ssd_state_block[1:], F32),
                      jax.ShapeDtypeStruct((nb,) + ret_state_block[1:], F32)]
    n_pairs = RET_HEADS // 2
    scratch = [
        pltpu.VMEM((U, L, SSD_WIDTH), F32),
        pltpu.VMEM((U, L, RET_V_WIDTH), F32),
        pltpu.VMEM((U, 2, SSD_STATE, SSD_WIDTH), F32),
        pltpu.VMEM((U, 2, RET_HEADS, RET_QK_DIM, RET_V_DIM), F32),
        pltpu.VMEM((n_pairs, Q, 2 * Q), F32),
        pltpu.VMEM((n_pairs, Q, PAIR_W), F32),
        pltpu.VMEM((n_pairs, Q, PAIR_W), F32),
        pltpu.VMEM((n_pairs, LANES, Q), F32),
        pltpu.VMEM((n_pairs, LANES, Q), F32),
        pltpu.VMEM((2 * RET_HEADS, LANES), F32),
        pltpu.VMEM((LANES, SSD_WIDTH), BF16),
        pltpu.VMEM((LANES, SSD_WIDTH), BF16),
        pltpu.VMEM((U, L, DT_PAD), F32),
        pltpu.VMEM((2, U * Q, MIX_WIDTH), BF16),
    ]
    outs = pl.pallas_call(
        functools.partial(_mixer_kernel, n_streams=U, n_groups=n_groups, nc=nc, has_s0=has_s0,
                          emit_state=emit_state,
                          per_seq_mod=per_seq_mod),
        grid=(n_real + 1,),
        in_specs=in_specs,
        out_specs=out_specs,
        out_shape=out_shape,
        scratch_shapes=scratch,
        compiler_params=pltpu.CompilerParams(
            dimension_semantics=("arbitrary",), vmem_limit_bytes=VMEM_LIMIT),
        name="mixer_sample" if has_s0 else "mixer_prompt",
    )(*args)
    return outs


def _rope_tables(L):
    rows = L // GRID_W
    row = np.repeat(np.arange(rows, dtype=np.float64), GRID_W)
    col = np.tile(np.arange(GRID_W, dtype=np.float64), rows)
    half = RET_QK_DIM // 2
    inv = ROPE_BASE ** (-np.arange(0, half, 2, dtype=np.float64) / half)
    ang_r = row[:, None] * inv
    ang_c = col[:, None] * inv
    cos_h = np.concatenate([np.cos(ang_r), np.cos(ang_r), np.cos(ang_c), np.cos(ang_c)], axis=-1)
    sin_h = np.concatenate([-np.sin(ang_r), np.sin(ang_r), -np.sin(ang_c), np.sin(ang_c)], axis=-1)
    return (jnp.asarray(np.tile(cos_h, (1, 2)), dtype=F32), jnp.asarray(np.tile(sin_h, (1, 2)), dtype=F32))


def kernel(x_prompt, x_sample, state_ssd, state_ret, c, c_ctx, w_mod, b_mod, norm_pre_w, norm_post_w,
           w_in, conv_w, conv_b, ssd_A_log, ssd_dt_bias, ssd_D, ssd_norm_w, ret_decay, ret_norm_w, w_out):
    nb_s = x_sample.shape[0]
    l = 0

    n_dt = 2 * SSD_HEADS
    w_packed = _pack_call(jnp.swapaxes(w_in[l], 0, 1))
    mix_norm_w = jnp.concatenate([ssd_norm_w[l], ret_norm_w[l]]).reshape(MIX_WIDTH, 1)
    w_out_bf = _scale_rows_call(w_out[l], mix_norm_w)
    dtb = jnp.pad(ssd_dt_bias[l].reshape(1, n_dt), ((0, 0), (0, DT_PAD - n_dt)))
    nega = jnp.pad(-jnp.exp(ssd_A_log[l].reshape(1, n_dt)), ((0, 0), (0, DT_PAD - n_dt)))
    dsk = jnp.repeat(ssd_D[l], SSD_HEAD_DIM).reshape(1, SSD_WIDTH)
    lamx = jnp.broadcast_to(-jnp.exp(ret_decay[l].reshape(2 * RET_HEADS, 1)), (2 * RET_HEADS, LANES))
    npre = norm_pre_w[l].reshape(1, D_MODEL)
    npost = norm_post_w[l].reshape(1, D_MODEL)
    cb = conv_b[l].reshape(1, CONV_CH)
    cos_t, sin_t = _rope_tables(x_sample.shape[1])

    cond = jnp.concatenate([c, c_ctx[None, :], jnp.zeros((COND_ROWS - nb_s - 1, D_MODEL), F32)], axis=0)
    mod = _mod_call(cond, w_mod[l], b_mod[l].reshape(1, 3 * D_MODEL))
    mod_s = mod[:nb_s].reshape(nb_s, 1, 3 * D_MODEL)
    mod_p = mod[nb_s:nb_s + 1].reshape(1, 1, 3 * D_MODEL)

    pp = _in_proj_call(x_prompt, mod_p, npre, w_packed, conv_w[l], cb, dtb, None, None)
    y_p, st_ssd_t, st_ret = _mixer_call(*pp, nega, dsk, lamx, w_out_bf, x_prompt, mod_p, npost,
                                        None, None, True)

    ps = _in_proj_call(x_sample, mod_s, npre, w_packed, conv_w[l], cb, dtb, cos_t, sin_t)
    (y_s,) = _mixer_call(*ps, nega, dsk, lamx, w_out_bf, x_sample, mod_s, npost,
                         jnp.swapaxes(state_ssd, -1, -2), state_ret, False)
    return (y_p, y_s, jnp.swapaxes(st_ssd_t, -1, -2), st_ret)
```

```python
import functools
import types

import jax
import jax.numpy as jnp
import numpy as np
from jax import lax
from jax.experimental import pallas as pl
from jax.experimental.pallas import tpu as pltpu

F32 = jnp.float32
BF16 = jnp.bfloat16

D_MODEL = 1024
CHUNK = 128
GRID_W = 64
EPS = 1e-6
SSD_WIDTH = 1024
SSD_HEAD_DIM = 64
SSD_HEADS = 16
SSD_GROUPS = 4
SSD_STATE = 128
CONV_CH = SSD_WIDTH + 2 * SSD_GROUPS * SSD_STATE
RET_HEADS = 8
RET_QK_DIM = 64
RET_V_DIM = 128
RET_QK_WIDTH = RET_HEADS * RET_QK_DIM
RET_V_WIDTH = RET_HEADS * RET_V_DIM
MIX_WIDTH = SSD_WIDTH + RET_V_WIDTH
ROPE_BASE = 10000.0
LANES = 128
DT_PAD = LANES
GROUP_W = SSD_WIDTH // SSD_GROUPS
PAIR_W = 2 * RET_V_DIM
IN_PROJ_PIECE = 256
IN_PROJ_ROWS = 1024
STREAMS = 2
BF16_ROWS = 16
XBC_B = SSD_WIDTH
XBC_C = XBC_B + SSD_GROUPS * SSD_STATE
SUBLANES = 8
PROJECT_POINT = "project"

OFF_Z = 0
OFF_XBC = OFF_Z + SSD_WIDTH
OFF_Q = OFF_XBC + CONV_CH
OFF_K = OFF_Q + RET_QK_WIDTH
OFF_V = OFF_K + RET_QK_WIDTH
OFF_G = OFF_V + RET_V_WIDTH
OFF_DT = OFF_G + RET_V_WIDTH
IN_COLS_PACKED = OFF_DT + DT_PAD

LOG2E = 1.4426950408889634
VMEM_LIMIT = 56 * 1024 * 1024


def _silu(x):
    return x * (1.0 / (1.0 + jnp.exp(-x)))


def _dot(a, b):
    return jnp.dot(a, b, preferred_element_type=F32)


def _dot_nt(a, b):
    return lax.dot_general(a, b, (((1,), (1,)), ((), ())), preferred_element_type=F32)


def _split3(x):
    hi = x.astype(BF16)
    r1 = x - hi.astype(F32)
    mid = r1.astype(BF16)
    lo = (r1 - mid.astype(F32)).astype(BF16)
    return hi, mid, lo


PACK_COLS = 1024
PACK_STEPS = -(-IN_COLS_PACKED // PACK_COLS)


def _pack_kernel(wt_ref, out_ref):
    t = wt_ref[...].T
    lane = lax.broadcasted_iota(jnp.int32, t.shape, 1)
    is_dt = pl.program_id(0) == PACK_STEPS - 1
    t = jnp.where(jnp.logical_and(is_dt, lane >= 2 * SSD_HEADS), 0.0, t)
    out_ref[...] = t.astype(BF16)


def _pack_call(w_in_t):
    n_head = (OFF_Q - OFF_Z) // PACK_COLS
    src_dt = SSD_WIDTH + CONV_CH
    src_tail = src_dt + 2 * SSD_HEADS

    def src_row(i):
        t8 = jnp.where(i < n_head, i * (PACK_COLS // 8),
                       jnp.where(i < PACK_STEPS - 1, src_tail // 8 + (i - n_head) * (PACK_COLS // 8),
                                 src_dt // 8))
        return t8 * 8

    return pl.pallas_call(
        _pack_kernel,
        grid=(PACK_STEPS,),
        in_specs=[pl.BlockSpec((pl.Element(PACK_COLS), pl.Element(D_MODEL)), lambda i: (src_row(i), 0))],
        out_specs=pl.BlockSpec((D_MODEL, PACK_COLS), lambda i: (0, i)),
        out_shape=jax.ShapeDtypeStruct((D_MODEL, IN_COLS_PACKED), BF16),
        name="pack_w_in",
    )(w_in_t)


def _scale_rows_kernel(w_ref, s_ref, out_ref):
    out_ref[...] = (w_ref[...] * s_ref[...]).astype(BF16)


def _scale_rows_call(w, row_scale):
    rows, cols = w.shape
    tr = 512
    return pl.pallas_call(
        _scale_rows_kernel,
        grid=(rows // tr,),
        in_specs=[pl.BlockSpec((tr, cols), lambda i: (i, 0)), pl.BlockSpec((tr, 1), lambda i: (i, 0))],
        out_specs=pl.BlockSpec((tr, cols), lambda i: (i, 0)),
        out_shape=jax.ShapeDtypeStruct((rows, cols), BF16),
        name="fold_norm_w_out",
    )(w, row_scale)


def _mod_kernel(c_ref, cctx_ref, w_ref, b_ref, mod_s_ref, mod_p_ref):
    rows = c_ref.shape[0]
    ctx = jnp.broadcast_to(cctx_ref[...], (SUBLANES, D_MODEL))
    cond = jnp.concatenate([c_ref[...], ctx], axis=0)
    mod = _dot(_silu(cond).astype(BF16), w_ref[...].astype(BF16)) + b_ref[...]
    for r in range(rows):
        mod_s_ref[r] = mod[r:r + 1]
    mod_p_ref[0] = mod[rows:rows + 1]


def _mod_call(c, c_ctx, w_mod, b_mod):
    rows = c.shape[0]
    tn = 1024
    return pl.pallas_call(
        _mod_kernel,
        grid=(3 * D_MODEL // tn,),
        in_specs=[
            pl.BlockSpec((rows, D_MODEL), lambda j: (0, 0)),
            pl.BlockSpec((1, D_MODEL), lambda j: (0, 0)),
            pl.BlockSpec((D_MODEL, tn), lambda j: (0, j)),
            pl.BlockSpec((1, tn), lambda j: (0, j)),
        ],
        out_specs=[pl.BlockSpec((rows, 1, tn), lambda j: (0, 0, j)),
                   pl.BlockSpec((1, 1, tn), lambda j: (0, 0, j))],
        out_shape=[jax.ShapeDtypeStruct((rows, 1, 3 * D_MODEL), F32),
                   jax.ShapeDtypeStruct((1, 1, 3 * D_MODEL), F32)],
        name="mod",
    )(c, c_ctx, w_mod, b_mod)


def _modnorm(x, mod_ref, npw_ref):
    ms = jnp.mean(x * x, axis=-1, keepdims=True)
    shift = mod_ref[0, :, 0:D_MODEL]
    scale = mod_ref[0, :, D_MODEL:2 * D_MODEL]
    gain = npw_ref[...] * (1.0 + scale)
    return (x * lax.rsqrt(ms + EPS) * gain + shift).astype(BF16)


def _in_proj_kernel(*refs, seq_len, rope):
    if rope:
        (x_ref, mod_ref, npw_ref, w_ref, cw_ref, cb_ref, dtb_ref,
         cos_ref, sin_ref, zs_ref, xbc_ref, dt_ref, q_ref, k_ref, v_ref, gs_ref, h_ref) = refs
    else:
        (x_ref, mod_ref, npw_ref, w_ref, cw_ref, cb_ref, dtb_ref,
         zs_ref, xbc_ref, dt_ref, q_ref, k_ref, v_ref, gs_ref, h_ref) = refs
    rows = seq_len
    h_ref[...] = _modnorm(x_ref[0], mod_ref, npw_ref)
    piece = IN_PROJ_PIECE

    def put(ref, cols, val):
        ref[0, :, cols] = val

    row = lax.broadcasted_iota(jnp.int32, (rows, piece), 0)
    seq_first = row == 0
    seq_last = row == rows - 1

    def sec_xbc(c0):
        cols = slice(c0, c0 + piece)
        acc = _dot(h_ref[...], w_ref[:, OFF_XBC + c0:OFF_XBC + c0 + piece])
        up = jnp.where(seq_first, 0.0, pltpu.roll(acc, 1, axis=0))
        dn = jnp.where(seq_last, 0.0, pltpu.roll(acc, rows - 1, axis=0))
        y = cw_ref[0:1, cols] * up + cw_ref[1:2, cols] * acc + cw_ref[2:3, cols] * dn + cb_ref[:, cols]
        put(xbc_ref, cols, _silu(y).astype(BF16))

    lane = lax.broadcasted_iota(jnp.int32, (rows, LANES), 1)
    first_half = (lane % 32) < 16

    def sec_qk(off, ref, scl, c0):
        acc = _dot(h_ref[...], w_ref[:, off + c0:off + c0 + piece]) * scl
        if rope:
            for s0 in range(0, piece, LANES):
                xs = acc[:, s0:s0 + LANES]
                partner = jnp.where(first_half, pltpu.roll(xs, LANES - 16, axis=1),
                                    pltpu.roll(xs, 16, axis=1))
                put(ref, slice(c0 + s0, c0 + s0 + LANES),
                    (xs * cos_ref[...] + partner * sin_ref[...]).astype(BF16))
        else:
            put(ref, slice(c0, c0 + piece), acc.astype(BF16))

    def sec_q(c0):
        sec_qk(OFF_Q, q_ref, 1.0, c0)

    def sec_k(c0):
        sec_qk(OFF_K, k_ref, RET_QK_DIM ** -0.5, c0)

    def sec_z(c0):
        acc = _dot(h_ref[...], w_ref[:, OFF_Z + c0:OFF_Z + c0 + piece])
        put(zs_ref, slice(c0, c0 + piece), _silu(acc).astype(BF16))

    def sec_g(c0):
        acc = _dot(h_ref[...], w_ref[:, OFF_G + c0:OFF_G + c0 + piece])
        put(gs_ref, slice(c0, c0 + piece), _silu(acc).astype(BF16))

    def sec_v(c0):
        acc = _dot(h_ref[...], w_ref[:, OFF_V + c0:OFF_V + c0 + piece])
        put(v_ref, slice(c0, c0 + piece), acc.astype(BF16))

    def sec_dt(c0):
        acc = _dot(h_ref[...], w_ref[:, OFF_DT:OFF_DT + DT_PAD]) + dtb_ref[...]
        put(dt_ref, slice(0, DT_PAD), jnp.maximum(acc, 0.0) + jnp.log(1.0 + jnp.exp(-jnp.abs(acc))))

    def pieces(fn, width):
        return [(fn, c0) for c0 in range(0, width, piece)]

    conv = pieces(sec_xbc, CONV_CH)
    light = pieces(sec_v, RET_V_WIDTH) + [(sec_dt, 0)]
    order = pieces(sec_q, RET_QK_WIDTH) + pieces(sec_k, RET_QK_WIDTH)
    stride = -(-len(conv) // len(light))
    for i, item in enumerate(conv):
        order.append(item)
        if i % stride == stride - 1 and light:
            order.append(light.pop(0))
    order += light + pieces(sec_z, SSD_WIDTH) + pieces(sec_g, RET_V_WIDTH)
    for fn, c0 in order:
        fn(c0)


def _in_proj_call(x, mod, npw, w_packed, conv_w, conv_b, dtb, cos_t, sin_t):
    nb, L, _ = x.shape
    rope = cos_t is not None
    per_seq_mod = mod.shape[0] > 1
    assert L <= IN_PROJ_ROWS
    mod_map = (lambda s: (s, 0, 0)) if per_seq_mod else (lambda s: (0, 0, 0))
    const2 = lambda s: (0, 0)
    in_specs = [
        pl.BlockSpec((1, L, D_MODEL), lambda s: (s, 0, 0)),
        pl.BlockSpec((1, 1, 3 * D_MODEL), mod_map),
        pl.BlockSpec((1, D_MODEL), const2),
        pl.BlockSpec((D_MODEL, IN_COLS_PACKED), const2, pipeline_mode=pl.Buffered(1)),
        pl.BlockSpec((3, CONV_CH), const2),
        pl.BlockSpec((1, CONV_CH), const2),
        pl.BlockSpec((1, DT_PAD), const2),
    ]
    args = [x, mod, npw, w_packed, conv_w, conv_b, dtb]
    if rope:
        in_specs += [pl.BlockSpec((L, LANES), const2), pl.BlockSpec((L, LANES), const2)]
        args += [cos_t, sin_t]
    widths = (SSD_WIDTH, CONV_CH, DT_PAD, RET_QK_WIDTH, RET_QK_WIDTH, RET_V_WIDTH, RET_V_WIDTH)
    dtypes = (BF16, BF16, F32, BF16, BF16, BF16, BF16)
    out_specs = [pl.BlockSpec((1, L, w), lambda s: (s, 0, 0)) for w in widths]
    out_shape = [jax.ShapeDtypeStruct((nb, L, w), d) for w, d in zip(widths, dtypes)]
    return pl.pallas_call(
        functools.partial(_in_proj_kernel, seq_len=L, rope=rope),
        grid=(nb,),
        in_specs=in_specs,
        out_specs=out_specs,
        out_shape=out_shape,
        scratch_shapes=[pltpu.VMEM((L, D_MODEL), BF16)],
        compiler_params=pltpu.CompilerParams(
            dimension_semantics=("parallel",), vmem_limit_bytes=VMEM_LIMIT),
        name="in_proj_rope" if rope else "in_proj",
    )(*args)


def _mixer_kernel(*refs, n_streams, n_groups, nc, has_s0, emit_state, per_seq_mod):
    Q = CHUNK
    H = SSD_HEADS
    U = n_streams
    it = iter(refs)
    zs_ref, xbc_ref, dt_ref, q_ref, k_ref, v_ref, gs_ref = (next(it) for _ in range(7))
    nega_ref, dsk_ref, lam_ref = (next(it) for _ in range(3))
    wout_ref, xres_ref, modl_ref, npost_ref = (next(it) for _ in range(4))
    if has_s0:
        s0s_ref, s0r_ref = next(it), next(it)
    y_ref = next(it)
    if emit_state:
        os_ref, or_ref = next(it), next(it)
    (yas_ref, yar_ref, ss_ref, sr_ref, dsum_ref, ef_ref, eb_ref, tf_ref, tb_ref,
     ar_ref, exf_ref, exb_ref, cum_ref, mixs_ref) = (next(it) for _ in range(14))

    i = pl.program_id(0)
    n_real = n_groups * 2 * nc
    is_real = i < n_real
    ic = jnp.minimum(i, n_real - 1)
    s = ic % (2 * nc)
    phase2 = s >= nc
    c = jnp.where(phase2, 2 * nc - 1 - s, s)
    r0 = pl.multiple_of(c * Q, Q)
    slot_w = 1
    slot_r = 0

    rowq = lax.broadcasted_iota(jnp.int32, (Q, Q), 0)
    colq = lax.broadcasted_iota(jnp.int32, (Q, Q), 1)

    def out_proj_matmul():
        return _dot(mixs_ref[slot_r], wout_ref[...])

    def out_proj_finish(out):
        ms = jnp.mean(out * out, axis=-1, keepdims=True)
        o = out * lax.rsqrt(ms + EPS) * npost_ref[...]
        for u in range(U):
            gate = modl_ref[u if per_seq_mod else 0, :, 2 * D_MODEL:3 * D_MODEL]
            y_ref[u] = xres_ref[u] + gate * o[u * Q:(u + 1) * Q]

    @pl.when(i == 0)
    def _init_tables():
        mixs_ref[...] = jnp.zeros_like(mixs_ref)
        diff = (rowq - colq).astype(F32)
        rowf = lax.broadcasted_iota(jnp.int32, (Q, LANES), 0).astype(F32)
        colf = lax.broadcasted_iota(jnp.int32, (RET_QK_DIM, Q), 1).astype(F32)
        for h in range(RET_HEADS):
            pr, hh = divmod(h, 2)
            lf = lam_ref[h:h + 1, :]
            lb = lam_ref[RET_HEADS + h:RET_HEADS + h + 1, :]
            e = jnp.where(rowq >= colq, lf * diff, lb * (-diff))
            dsum_ref[pr, :, hh * Q:(hh + 1) * Q] = jnp.exp(e) * jnp.where(rowq == colq, 2.0, 1.0)
            hc = slice(hh * LANES, (hh + 1) * LANES)
            ef_ref[pr, :, hc] = jnp.exp(lf * (rowf + 1.0))
            eb_ref[pr, :, hc] = jnp.exp(lb * (Q - rowf))
            kr = slice(hh * RET_QK_DIM, (hh + 1) * RET_QK_DIM)
            tf_ref[pr, kr, :] = jnp.exp(lf * (Q - 1.0 - colf))
            tb_ref[pr, kr, :] = jnp.exp(lb * colf)
        ar_ref[...] = jnp.exp(lam_ref[...] * float(Q))
        er = lax.broadcasted_iota(jnp.int32, (LANES, SSD_WIDTH), 0)
        ec = lax.broadcasted_iota(jnp.int32, (LANES, SSD_WIDTH), 1) // SSD_HEAD_DIM
        exf_ref[...] = jnp.where(er == ec, 1.0, 0.0).astype(BF16)
        exb_ref[...] = jnp.where(er == ec + H, 1.0, 0.0).astype(BF16)

    def expand(w, a_row, ex_ref):
        r = BF16_ROWS
        a3 = jnp.concatenate(_split3(jnp.broadcast_to(a_row, (r, LANES))), axis=0)
        out = _dot(jnp.concatenate([w.astype(BF16), a3], axis=0), ex_ref[...])
        n = w.shape[0]
        return out[0:n], out[n:n + 1] + out[n + r:n + r + 1] + out[n + 2 * r:n + 2 * r + 1]

    def make_stream(u):
        zs, xbc, dtr, qr, kr_, vr, gsr = (r.at[u] for r in (zs_ref, xbc_ref, dt_ref, q_ref, k_ref,
                                                            v_ref, gs_ref))
        yas, yar, ss, sr, cums = (r.at[u] for r in (yas_ref, yar_ref, ss_ref, sr_ref, cum_ref))
        mix_rows = slice(u * Q, (u + 1) * Q)

        def init_state():
            if has_s0:
                for d in range(2):
                    for p in range(SSD_HEADS // 2):
                        pair_t = jnp.concatenate(
                            [s0s_ref[u, 0, d, 2 * p], s0s_ref[u, 0, d, 2 * p + 1]], axis=0)
                        ss[d, :, p * LANES:(p + 1) * LANES] = pair_t.T
                sr[...] = s0r_ref[u, 0]
            else:
                ss[...] = jnp.zeros_like(ss)
                sr[...] = jnp.zeros_like(sr)

        def ssd_state_update(d, wtail_x, a_x):
            for g in range(SSD_GROUPS):
                bgt = xbc[:, XBC_B + g * SSD_STATE:XBC_B + (g + 1) * SSD_STATE].T
                gc = slice(g * GROUP_W, (g + 1) * GROUP_W)
                xt = xbc[:, gc] * wtail_x[:, gc].astype(BF16)
                ss[d, :, gc] = ss[d, :, gc] * a_x[:, gc] + _dot(bgt, xt)

        def k_transposed(pr):
            return kr_[:, pr * LANES:(pr + 1) * LANES].T

        def ret_state_update(d, tail_ref, kts):
            for pr, kt in enumerate(kts):
                pc = slice(pr * PAIR_W, (pr + 1) * PAIR_W)
                ktt = (kt.astype(F32) * tail_ref[pr]).astype(BF16)
                ds = _dot(ktt, vr[:, pc])
                for hh in range(2):
                    h = 2 * pr + hh
                    a = ar_ref[d * RET_HEADS + h:d * RET_HEADS + h + 1, :]
                    sr[d, h] = (sr[d, h] * a
                                + ds[hh * RET_QK_DIM:(hh + 1) * RET_QK_DIM, hh * LANES:(hh + 1) * LANES])

        def emit_states(d):
            for p in range(SSD_HEADS // 2):
                pair_t = ss[d, :, p * LANES:(p + 1) * LANES].T
                os_ref[u, 0, d, 2 * p] = pair_t[0:SSD_HEAD_DIM]
                os_ref[u, 0, d, 2 * p + 1] = pair_t[SSD_HEAD_DIM:2 * SSD_HEAD_DIM]
            or_ref[u, 0, d] = sr[d]

        def ret_state_blockdiag(d, pr):
            z = jnp.zeros((RET_QK_DIM, RET_V_DIM), F32)
            top = jnp.concatenate([sr[d, 2 * pr], z], axis=1)
            bot = jnp.concatenate([z, sr[d, 2 * pr + 1]], axis=1)
            return jnp.concatenate([top, bot], axis=0).astype(BF16)

        def phase1():
            dt = dtr[...]
            la = dt * nega_ref[...]
            tl_bf = jnp.where(rowq >= colq, 1.0, 0.0).astype(BF16)
            c3 = _dot(tl_bf, jnp.concatenate(_split3(la), axis=1))
            cum = c3[:, 0:LANES] + c3[:, LANES:2 * LANES] + c3[:, 2 * LANES:3 * LANES]
            cums[pl.ds(r0, Q), :] = cum
            tot = cum[Q - 1:Q, :]
            rev = tot - cum + la
            yield

            sub_k = lax.broadcasted_iota(jnp.int32, (LANES, Q), 0) < RET_QK_DIM
            lane_v = lax.broadcasted_iota(jnp.int32, (Q, PAIR_W), 1) < RET_V_DIM
            kts = [k_transposed(pr) for pr in range(RET_HEADS // 2)]
            for pr, kt in enumerate(kts):
                qp = qr[:, pr * LANES:(pr + 1) * LANES]
                zk = jnp.zeros_like(kt)
                kbd = jnp.concatenate([jnp.where(sub_k, kt, zk), jnp.where(sub_k, zk, kt)], axis=1)
                a2 = _dot(qp, kbd)
                ad = (a2 * dsum_ref[pr]).astype(BF16)
                pc = slice(pr * PAIR_W, (pr + 1) * PAIR_W)
                vp = vr[:, pc]
                zv = jnp.zeros_like(vp)
                vbd = jnp.concatenate([jnp.where(lane_v, vp, zv), jnp.where(lane_v, zv, vp)], axis=0)
                y = _dot(ad, vbd) + _dot(qp, ret_state_blockdiag(0, pr)) * ef_ref[pr]
                yar[pl.ds(r0, Q), pc] = y
            ret_state_update(0, tf_ref, kts)
            yield

            gms, css = [], []
            for g in range(SSD_GROUPS):
                cg = xbc[:, XBC_C + g * SSD_STATE:XBC_C + (g + 1) * SSD_STATE]
                bg = xbc[:, XBC_B + g * SSD_STATE:XBC_B + (g + 1) * SSD_STATE]
                gms.append(_dot_nt(cg, bg))
                css.append(_dot(cg, ss[0, :, g * GROUP_W:(g + 1) * GROUP_W].astype(BF16)))
            yield PROJECT_POINT

            dt_t = dt.T
            ldt_t = jnp.log(dt_t)
            sub = lax.broadcasted_iota(jnp.int32, (LANES, Q), 0)
            adj_t = (jnp.where(sub < H, cum.T, rev.T) - ldt_t) * LOG2E
            dg_t = jnp.log(dt_t[0:H, :] + dt_t[H:2 * H, :]) * LOG2E
            cum2 = cum * LOG2E
            rev2 = rev * LOG2E
            wtail_f = dt * jnp.exp(tot - cum)
            ew_x, af_x = expand(jnp.concatenate([jnp.exp(cum), wtail_f], axis=0), jnp.exp(tot), exf_ref)
            ecum_x = ew_x[0:Q]
            wtail_x = ew_x[Q:2 * Q]
            lt = rowq > colq
            gt = rowq < colq
            lane_g = lax.broadcasted_iota(jnp.int32, (Q, GROUP_W), 1) // SSD_HEAD_DIM
            yield

            for g in range(SSD_GROUPS):
                gc = slice(g * GROUP_W, (g + 1) * GROUP_W)
                xg = xbc[:, gc]
                ws, xs = [], []
                for j in range(4):
                    h = 4 * g + j
                    hb = H + h
                    arg = jnp.where(lt, cum2[:, h:h + 1] - adj_t[h:h + 1, :],
                                    jnp.where(gt, rev2[:, hb:hb + 1] - adj_t[hb:hb + 1, :], dg_t[h:h + 1, :]))
                    ws.append((gms[g] * jnp.exp2(arg)).astype(BF16))
                    xs.append(jnp.where(lane_g == j, xg, jnp.zeros_like(xg)))
                y = _dot(jnp.concatenate(ws, axis=1), jnp.concatenate(xs, axis=0))
                yas[pl.ds(r0, Q), gc] = y + css[g] * ecum_x[:, gc]
            yield
            ssd_state_update(0, wtail_x, af_x)

        def phase2():
            for pr in range(RET_HEADS // 2):
                qp = qr[:, pr * LANES:(pr + 1) * LANES]
                pc = slice(pr * PAIR_W, (pr + 1) * PAIR_W)
                y2 = yar[pl.ds(r0, Q), pc] + _dot(qp, ret_state_blockdiag(1, pr)) * eb_ref[pr]
                for hh in range(2):
                    h = 2 * pr + hh
                    hc = slice(h * LANES, (h + 1) * LANES)
                    y = y2[:, hh * LANES:(hh + 1) * LANES]
                    mu = jnp.mean(y, axis=-1, keepdims=True)
                    yc = y - mu
                    var = jnp.mean(yc * yc, axis=-1, keepdims=True)
                    yn = yc * lax.rsqrt(var + EPS) * gsr[:, hc].astype(F32)
                    mixs_ref[slot_w, mix_rows, SSD_WIDTH + h * LANES:SSD_WIDTH + (h + 1) * LANES] = (
                        yn.astype(BF16))
            ret_state_update(1, tb_ref, [k_transposed(pr) for pr in range(RET_HEADS // 2)])
            yield

            dt = dtr[...]
            cum = cums[pl.ds(r0, Q), :]
            rev = cum[Q - 1:Q, :] - cum + dt * nega_ref[...]
            first = rev[0:1, :]
            wtail_b = dt * jnp.exp(first - rev)
            ew_x, ab_x = expand(jnp.concatenate([jnp.exp(rev), wtail_b], axis=0), jnp.exp(first), exb_ref)
            erev_x = ew_x[0:Q]
            wtail_x = ew_x[Q:2 * Q]

            parts = []
            ssq = None
            for g in range(SSD_GROUPS):
                cg = xbc[:, XBC_C + g * SSD_STATE:XBC_C + (g + 1) * SSD_STATE]
                gc = slice(g * GROUP_W, (g + 1) * GROUP_W)
                cs = _dot(cg, ss[1, :, gc].astype(BF16))
                y = (yas[pl.ds(r0, Q), gc] + cs * erev_x[:, gc]
                     + dsk_ref[:, gc] * xbc[:, gc].astype(F32))
                y = y * zs[:, gc].astype(F32)
                parts.append(y)
                ssq = y * y if ssq is None else ssq + y * y
            yield PROJECT_POINT

            inv = lax.rsqrt(jnp.sum(ssq, axis=-1, keepdims=True) * (1.0 / SSD_WIDTH) + EPS)
            for g, y in enumerate(parts):
                gc = slice(g * GROUP_W, (g + 1) * GROUP_W)
                mixs_ref[slot_w, mix_rows, gc] = (y * inv).astype(BF16)
            ssd_state_update(1, wtail_x, ab_x)

        return types.SimpleNamespace(init_state=init_state, emit_states=emit_states, phase1=phase1,
                                     phase2=phase2)

    streams = [make_stream(u) for u in range(U)]

    def run_interleaved(stage_gens, with_out_proj):
        proj = None
        live = list(stage_gens)
        while live:
            marks, nxt = [], []
            for gen in live:
                try:
                    marks.append(next(gen))
                    nxt.append(gen)
                except StopIteration:
                    pass
            if with_out_proj and PROJECT_POINT in marks:
                proj = out_proj_matmul()
            live = nxt
        return proj

    @pl.when(is_real & (s == 0))
    def _init_state():
        for st in streams:
            st.init_state()

    @pl.when(is_real & jnp.logical_not(phase2))
    def _forward_step():
        run_interleaved([st.phase1() for st in streams], False)

    @pl.when(is_real & phase2)
    def _backward_step():
        proj = run_interleaved([st.phase2() for st in streams], True)
        out_proj_finish(proj)
        mixs_ref[slot_r] = mixs_ref[slot_w]

    @pl.when(jnp.logical_not(is_real))
    def _drain():
        out_proj_finish(out_proj_matmul())

    if emit_state:
        @pl.when(is_real & (s == nc - 1))
        def _emit_fwd():
            for st in streams:
                st.emit_states(0)

        @pl.when(is_real & (s == 2 * nc - 1))
        def _emit_bwd():
            for st in streams:
                st.emit_states(1)


def _mixer_call(zs, xbc, dt, q, k, v, gs, nega, dsk, lamx, w_out_bf, x, mod, npost, s0_ssd, s0_ret,
                emit_state):
    nb, L, _ = zs.shape
    nc = L // CHUNK
    Q = CHUNK
    U = STREAMS
    assert nb % U == 0
    has_s0 = s0_ssd is not None
    n_groups = nb // U
    steps = 2 * nc
    n_real = n_groups * steps
    per_seq_mod = mod.shape[0] > 1

    def cur(i):
        ic = jnp.minimum(i, n_real - 1)
        return ic // steps, ic % steps

    def chunk_map(i):
        b, s = cur(i)
        return (b, jnp.where(s < nc, s, steps - 1 - s), 0)

    def phase2_map(i):
        b, s = cur(i)
        return (b, jnp.where(s < nc, nc - 1, steps - 1 - s), 0)

    def lag(i):
        b, s = i // steps, i % steps
        early = s <= nc
        return (jnp.where(early, jnp.maximum(b - 1, 0), b),
                jnp.where(early, jnp.where(b == 0, nc - 1, 0), steps - s))

    def lag_map(i):
        bl, cl = lag(i)
        return (bl, cl, 0)

    def lag_mod_map(i):
        bl, _ = lag(i)
        return (bl if per_seq_mod else 0, 0, 0)

    const2 = lambda i: (0, 0)
    in_specs = [
        pl.BlockSpec((U, Q, SSD_WIDTH), phase2_map),
        pl.BlockSpec((U, Q, CONV_CH), chunk_map),
        pl.BlockSpec((U, Q, DT_PAD), chunk_map),
        pl.BlockSpec((U, Q, RET_QK_WIDTH), chunk_map),
        pl.BlockSpec((U, Q, RET_QK_WIDTH), chunk_map),
        pl.BlockSpec((U, Q, RET_V_WIDTH), chunk_map),
        pl.BlockSpec((U, Q, RET_V_WIDTH), phase2_map),
        pl.BlockSpec((1, DT_PAD), const2),
        pl.BlockSpec((1, SSD_WIDTH), const2),
        pl.BlockSpec((2 * RET_HEADS, LANES), const2),
        pl.BlockSpec((MIX_WIDTH, D_MODEL), const2, pipeline_mode=pl.Buffered(1)),
        pl.BlockSpec((U, Q, D_MODEL), lag_map),
        pl.BlockSpec((U if per_seq_mod else 1, 1, 3 * D_MODEL), lag_mod_map),
        pl.BlockSpec((1, D_MODEL), const2),
    ]
    args = [zs, xbc, dt, q, k, v, gs, nega, dsk, lamx, w_out_bf, x, mod, npost]
    state_map = lambda i: (cur(i)[0], 0, 0, 0, 0, 0)
    ssd_state_block = (U, 1, 2, SSD_HEADS, SSD_HEAD_DIM, SSD_STATE)
    ret_state_block = (U, 1, 2, RET_HEADS, RET_QK_DIM, RET_V_DIM)
    if has_s0:
        in_specs += [pl.BlockSpec(ssd_state_block, state_map, pipeline_mode=pl.Buffered(1)),
                     pl.BlockSpec(ret_state_block, state_map, pipeline_mode=pl.Buffered(1))]
        args += [s0_ssd, s0_ret]
    out_specs = [pl.BlockSpec((U, Q, D_MODEL), lag_map)]
    out_shape = [jax.ShapeDtypeStruct((nb, L, D_MODEL), F32)]
    if emit_state:
        out_specs += [pl.BlockSpec(ssd_state_block, state_map), pl.BlockSpec(ret_state_block, state_map)]
        out_shape += [jax.ShapeDtypeStruct((nb,) + ssd_state_block[1:], F32),
                      jax.ShapeDtypeStruct((nb,) + ret_state_block[1:], F32)]
    n_pairs = RET_HEADS // 2
    scratch = [
        pltpu.VMEM((U, L, SSD_WIDTH), F32),
        pltpu.VMEM((U, L, RET_V_WIDTH), F32),
        pltpu.VMEM((U, 2, SSD_STATE, SSD_WIDTH), F32),
        pltpu.VMEM((U, 2, RET_HEADS, RET_QK_DIM, RET_V_DIM), F32),
        pltpu.VMEM((n_pairs, Q, 2 * Q), F32),
        pltpu.VMEM((n_pairs, Q, PAIR_W), F32),
        pltpu.VMEM((n_pairs, Q, PAIR_W), F32),
        pltpu.VMEM((n_pairs, LANES, Q), F32),
        pltpu.VMEM((n_pairs, LANES, Q), F32),
        pltpu.VMEM((2 * RET_HEADS, LANES), F32),
        pltpu.VMEM((LANES, SSD_WIDTH), BF16),
        pltpu.VMEM((LANES, SSD_WIDTH), BF16),
        pltpu.VMEM((U, L, DT_PAD), F32),
        pltpu.VMEM((2, U * Q, MIX_WIDTH), BF16),
    ]
    outs = pl.pallas_call(
        functools.partial(_mixer_kernel, n_streams=U, n_groups=n_groups, nc=nc, has_s0=has_s0,
                          emit_state=emit_state,
                          per_seq_mod=per_seq_mod),
        grid=(n_real + 1,),
        in_specs=in_specs,
        out_specs=out_specs,
        out_shape=out_shape,
        scratch_shapes=scratch,
        compiler_params=pltpu.CompilerParams(
            dimension_semantics=("arbitrary",), vmem_limit_bytes=VMEM_LIMIT),
        name="mixer_sample" if has_s0 else "mixer_prompt",
    )(*args)
    return outs


def _rope_tables(L):
    rows = L // GRID_W
    row = np.repeat(np.arange(rows, dtype=np.float64), GRID_W)
    col = np.tile(np.arange(GRID_W, dtype=np.float64), rows)
    half = RET_QK_DIM // 2
    inv = ROPE_BASE ** (-np.arange(0, half, 2, dtype=np.float64) / half)
    ang_r = row[:, None] * inv
    ang_c = col[:, None] * inv
    cos_h = np.concatenate([np.cos(ang_r), np.cos(ang_r), np.cos(ang_c), np.cos(ang_c)], axis=-1)
    sin_h = np.concatenate([-np.sin(ang_r), np.sin(ang_r), -np.sin(ang_c), np.sin(ang_c)], axis=-1)
    return (jnp.asarray(np.tile(cos_h, (1, 2)), dtype=F32), jnp.asarray(np.tile(sin_h, (1, 2)), dtype=F32))


def kernel(x_prompt, x_sample, state_ssd, state_ret, c, c_ctx, w_mod, b_mod, norm_pre_w, norm_post_w,
           w_in, conv_w, conv_b, ssd_A_log, ssd_dt_bias, ssd_D, ssd_norm_w, ret_decay, ret_norm_w, w_out):
    nb_s = x_sample.shape[0]
    l = 0

    n_dt = 2 * SSD_HEADS
    w_packed = _pack_call(jnp.swapaxes(w_in[l], 0, 1))
    mix_norm_w = jnp.concatenate([ssd_norm_w[l], ret_norm_w[l]]).reshape(MIX_WIDTH, 1)
    w_out_bf = _scale_rows_call(w_out[l], mix_norm_w)
    dtb = jnp.pad(ssd_dt_bias[l].reshape(1, n_dt), ((0, 0), (0, DT_PAD - n_dt)))
    nega = jnp.pad(-jnp.exp(ssd_A_log[l].reshape(1, n_dt)), ((0, 0), (0, DT_PAD - n_dt)))
    dsk = jnp.repeat(ssd_D[l], SSD_HEAD_DIM).reshape(1, SSD_WIDTH)
    lamx = jnp.broadcast_to(-jnp.exp(ret_decay[l].reshape(2 * RET_HEADS, 1)), (2 * RET_HEADS, LANES))
    npre = norm_pre_w[l].reshape(1, D_MODEL)
    npost = norm_post_w[l].reshape(1, D_MODEL)
    cb = conv_b[l].reshape(1, CONV_CH)
    cos_t, sin_t = _rope_tables(x_sample.shape[1])

    mod_s, mod_p = _mod_call(c, c_ctx.reshape(1, D_MODEL), w_mod[l], b_mod[l].reshape(1, 3 * D_MODEL))

    pp = _in_proj_call(x_prompt, mod_p, npre, w_packed, conv_w[l], cb, dtb, None, None)
    y_p, st_ssd_t, st_ret = _mixer_call(*pp, nega, dsk, lamx, w_out_bf, x_prompt, mod_p, npost,
                                        None, None, True)

    ps = _in_proj_call(x_sample, mod_s, npre, w_packed, conv_w[l], cb, dtb, cos_t, sin_t)
    (y_s,) = _mixer_call(*ps, nega, dsk, lamx, w_out_bf, x_sample, mod_s, npost,
                         jnp.swapaxes(state_ssd, -1, -2), state_ret, False)
    return (y_p, y_s, jnp.swapaxes(st_ssd_t, -1, -2), st_ret)
```

```python
import functools
import types

import jax
import jax.numpy as jnp
import numpy as np
from jax import lax
from jax.experimental import pallas as pl
from jax.experimental.pallas import tpu as pltpu

F32 = jnp.float32
BF16 = jnp.bfloat16

D_MODEL = 1024
CHUNK = 128
GRID_W = 64
EPS = 1e-6
SSD_WIDTH = 1024
SSD_HEAD_DIM = 64
SSD_HEADS = 16
SSD_GROUPS = 4
SSD_STATE = 128
CONV_CH = SSD_WIDTH + 2 * SSD_GROUPS * SSD_STATE
RET_HEADS = 8
RET_QK_DIM = 64
RET_V_DIM = 128
RET_QK_WIDTH = RET_HEADS * RET_QK_DIM
RET_V_WIDTH = RET_HEADS * RET_V_DIM
MIX_WIDTH = SSD_WIDTH + RET_V_WIDTH
ROPE_BASE = 10000.0
LANES = 128
DT_PAD = LANES
GROUP_W = SSD_WIDTH // SSD_GROUPS
PAIR_W = 2 * RET_V_DIM
IN_PROJ_PIECE = 256
IN_PROJ_ROWS = 1024
STREAMS = 2
BF16_ROWS = 16
XBC_B = SSD_WIDTH
XBC_C = XBC_B + SSD_GROUPS * SSD_STATE
SUBLANES = 8
PROJECT_POINT = "project"

SCAN_COLS = ((0, CONV_CH), (CONV_CH, RET_QK_WIDTH), (CONV_CH + RET_QK_WIDTH, RET_QK_WIDTH),
             (CONV_CH + 2 * RET_QK_WIDTH, RET_V_WIDTH))
SCAN_WIDTH = CONV_CH + 2 * RET_QK_WIDTH + RET_V_WIDTH
GATE_COLS = ((0, SSD_WIDTH), (SSD_WIDTH, RET_V_WIDTH))
GATE_WIDTH = SSD_WIDTH + RET_V_WIDTH

OFF_Z = 0
OFF_XBC = OFF_Z + SSD_WIDTH
OFF_Q = OFF_XBC + CONV_CH
OFF_K = OFF_Q + RET_QK_WIDTH
OFF_V = OFF_K + RET_QK_WIDTH
OFF_G = OFF_V + RET_V_WIDTH
OFF_DT = OFF_G + RET_V_WIDTH
IN_COLS_PACKED = OFF_DT + DT_PAD

LOG2E = 1.4426950408889634
VMEM_LIMIT = 56 * 1024 * 1024


def _silu(x):
    return x * (1.0 / (1.0 + jnp.exp(-x)))


def _dot(a, b):
    return jnp.dot(a, b, preferred_element_type=F32)


def _dot_nt(a, b):
    return lax.dot_general(a, b, (((1,), (1,)), ((), ())), preferred_element_type=F32)


def _split3(x):
    hi = x.astype(BF16)
    r1 = x - hi.astype(F32)
    mid = r1.astype(BF16)
    lo = (r1 - mid.astype(F32)).astype(BF16)
    return hi, mid, lo


PACK_COLS = 1024
PACK_STEPS = -(-IN_COLS_PACKED // PACK_COLS)


def _pack_kernel(wt_ref, out_ref):
    t = wt_ref[...].T
    lane = lax.broadcasted_iota(jnp.int32, t.shape, 1)
    is_dt = pl.program_id(0) == PACK_STEPS - 1
    t = jnp.where(jnp.logical_and(is_dt, lane >= 2 * SSD_HEADS), 0.0, t)
    out_ref[...] = t.astype(BF16)


def _pack_call(w_in_t):
    n_head = (OFF_Q - OFF_Z) // PACK_COLS
    src_dt = SSD_WIDTH + CONV_CH
    src_tail = src_dt + 2 * SSD_HEADS

    def src_row(i):
        t8 = jnp.where(i < n_head, i * (PACK_COLS // 8),
                       jnp.where(i < PACK_STEPS - 1, src_tail // 8 + (i - n_head) * (PACK_COLS // 8),
                                 src_dt // 8))
        return t8 * 8

    return pl.pallas_call(
        _pack_kernel,
        grid=(PACK_STEPS,),
        in_specs=[pl.BlockSpec((pl.Element(PACK_COLS), pl.Element(D_MODEL)), lambda i: (src_row(i), 0))],
        out_specs=pl.BlockSpec((D_MODEL, PACK_COLS), lambda i: (0, i)),
        out_shape=jax.ShapeDtypeStruct((D_MODEL, IN_COLS_PACKED), BF16),
        name="pack_w_in",
    )(w_in_t)


def _scale_rows_kernel(w_ref, s_ref, out_ref):
    out_ref[...] = (w_ref[...] * s_ref[...]).astype(BF16)


def _scale_rows_call(w, row_scale):
    rows, cols = w.shape
    tr = 512
    return pl.pallas_call(
        _scale_rows_kernel,
        grid=(rows // tr,),
        in_specs=[pl.BlockSpec((tr, cols), lambda i: (i, 0)), pl.BlockSpec((tr, 1), lambda i: (i, 0))],
        out_specs=pl.BlockSpec((tr, cols), lambda i: (i, 0)),
        out_shape=jax.ShapeDtypeStruct((rows, cols), BF16),
        name="fold_norm_w_out",
    )(w, row_scale)


def _mod_kernel(c_ref, cctx_ref, w_ref, b_ref, mod_s_ref, mod_p_ref):
    rows = c_ref.shape[0]
    ctx = jnp.broadcast_to(cctx_ref[...], (SUBLANES, D_MODEL))
    cond = jnp.concatenate([c_ref[...], ctx], axis=0)
    mod = _dot(_silu(cond).astype(BF16), w_ref[...].astype(BF16)) + b_ref[...]
    for r in range(rows):
        mod_s_ref[r] = mod[r:r + 1]
    mod_p_ref[0] = mod[rows:rows + 1]


def _mod_call(c, c_ctx, w_mod, b_mod):
    rows = c.shape[0]
    tn = 1024
    return pl.pallas_call(
        _mod_kernel,
        grid=(3 * D_MODEL // tn,),
        in_specs=[
            pl.BlockSpec((rows, D_MODEL), lambda j: (0, 0)),
            pl.BlockSpec((1, D_MODEL), lambda j: (0, 0)),
            pl.BlockSpec((D_MODEL, tn), lambda j: (0, j)),
            pl.BlockSpec((1, tn), lambda j: (0, j)),
        ],
        out_specs=[pl.BlockSpec((rows, 1, tn), lambda j: (0, 0, j)),
                   pl.BlockSpec((1, 1, tn), lambda j: (0, 0, j))],
        out_shape=[jax.ShapeDtypeStruct((rows, 1, 3 * D_MODEL), F32),
                   jax.ShapeDtypeStruct((1, 1, 3 * D_MODEL), F32)],
        name="mod",
    )(c, c_ctx, w_mod, b_mod)


def _modnorm(x, mod_ref, npw_ref):
    ms = jnp.mean(x * x, axis=-1, keepdims=True)
    shift = mod_ref[0, :, 0:D_MODEL]
    scale = mod_ref[0, :, D_MODEL:2 * D_MODEL]
    gain = npw_ref[...] * (1.0 + scale)
    return (x * lax.rsqrt(ms + EPS) * gain + shift).astype(BF16)


def _in_proj_kernel(*refs, seq_len, rope):
    if rope:
        (x_ref, mod_ref, npw_ref, w_ref, cw_ref, cb_ref, dtb_ref,
         cos_ref, sin_ref, scan_ref, gate_ref, dt_ref, h_ref) = refs
    else:
        (x_ref, mod_ref, npw_ref, w_ref, cw_ref, cb_ref, dtb_ref,
         scan_ref, gate_ref, dt_ref, h_ref) = refs
    xbc_ref, q_ref, k_ref, v_ref = (scan_ref.at[:, :, o:o + w] for o, w in SCAN_COLS)
    zs_ref, gs_ref = (gate_ref.at[:, :, o:o + w] for o, w in GATE_COLS)
    rows = seq_len
    h_ref[...] = _modnorm(x_ref[0], mod_ref, npw_ref)
    piece = IN_PROJ_PIECE

    def put(ref, cols, val):
        ref[0, :, cols] = val

    row = lax.broadcasted_iota(jnp.int32, (rows, piece), 0)
    seq_first = row == 0
    seq_last = row == rows - 1

    def sec_xbc(c0):
        cols = slice(c0, c0 + piece)
        acc = _dot(h_ref[...], w_ref[:, OFF_XBC + c0:OFF_XBC + c0 + piece])
        up = jnp.where(seq_first, 0.0, pltpu.roll(acc, 1, axis=0))
        dn = jnp.where(seq_last, 0.0, pltpu.roll(acc, rows - 1, axis=0))
        y = cw_ref[0:1, cols] * up + cw_ref[1:2, cols] * acc + cw_ref[2:3, cols] * dn + cb_ref[:, cols]
        put(xbc_ref, cols, _silu(y).astype(BF16))

    lane = lax.broadcasted_iota(jnp.int32, (rows, LANES), 1)
    first_half = (lane % 32) < 16

    def sec_qk(off, ref, scl, c0):
        acc = _dot(h_ref[...], w_ref[:, off + c0:off + c0 + piece]) * scl
        if rope:
            for s0 in range(0, piece, LANES):
                xs = acc[:, s0:s0 + LANES]
                partner = jnp.where(first_half, pltpu.roll(xs, LANES - 16, axis=1),
                                    pltpu.roll(xs, 16, axis=1))
                put(ref, slice(c0 + s0, c0 + s0 + LANES),
                    (xs * cos_ref[...] + partner * sin_ref[...]).astype(BF16))
        else:
            put(ref, slice(c0, c0 + piece), acc.astype(BF16))

    def sec_q(c0):
        sec_qk(OFF_Q, q_ref, 1.0, c0)

    def sec_k(c0):
        sec_qk(OFF_K, k_ref, RET_QK_DIM ** -0.5, c0)

    def sec_z(c0):
        acc = _dot(h_ref[...], w_ref[:, OFF_Z + c0:OFF_Z + c0 + piece])
        put(zs_ref, slice(c0, c0 + piece), _silu(acc).astype(BF16))

    def sec_g(c0):
        acc = _dot(h_ref[...], w_ref[:, OFF_G + c0:OFF_G + c0 + piece])
        put(gs_ref, slice(c0, c0 + piece), _silu(acc).astype(BF16))

    def sec_v(c0):
        acc = _dot(h_ref[...], w_ref[:, OFF_V + c0:OFF_V + c0 + piece])
        put(v_ref, slice(c0, c0 + piece), acc.astype(BF16))

    def sec_dt(c0):
        acc = _dot(h_ref[...], w_ref[:, OFF_DT:OFF_DT + DT_PAD]) + dtb_ref[...]
        put(dt_ref, slice(0, DT_PAD), jnp.maximum(acc, 0.0) + jnp.log(1.0 + jnp.exp(-jnp.abs(acc))))

    def pieces(fn, width):
        return [(fn, c0) for c0 in range(0, width, piece)]

    conv = pieces(sec_xbc, CONV_CH)
    light = pieces(sec_v, RET_V_WIDTH) + [(sec_dt, 0)]
    order = pieces(sec_q, RET_QK_WIDTH) + pieces(sec_k, RET_QK_WIDTH)
    stride = -(-len(conv) // len(light))
    for i, item in enumerate(conv):
        order.append(item)
        if i % stride == stride - 1 and light:
            order.append(light.pop(0))
    order += light + pieces(sec_z, SSD_WIDTH) + pieces(sec_g, RET_V_WIDTH)
    for fn, c0 in order:
        fn(c0)


def _in_proj_call(x, mod, npw, w_packed, conv_w, conv_b, dtb, cos_t, sin_t):
    nb, L, _ = x.shape
    rope = cos_t is not None
    per_seq_mod = mod.shape[0] > 1
    assert L <= IN_PROJ_ROWS
    mod_map = (lambda s: (s, 0, 0)) if per_seq_mod else (lambda s: (0, 0, 0))
    const2 = lambda s: (0, 0)
    in_specs = [
        pl.BlockSpec((1, L, D_MODEL), lambda s: (s, 0, 0)),
        pl.BlockSpec((1, 1, 3 * D_MODEL), mod_map),
        pl.BlockSpec((1, D_MODEL), const2),
        pl.BlockSpec((D_MODEL, IN_COLS_PACKED), const2, pipeline_mode=pl.Buffered(1)),
        pl.BlockSpec((3, CONV_CH), const2),
        pl.BlockSpec((1, CONV_CH), const2),
        pl.BlockSpec((1, DT_PAD), const2),
    ]
    args = [x, mod, npw, w_packed, conv_w, conv_b, dtb]
    if rope:
        in_specs += [pl.BlockSpec((L, LANES), const2), pl.BlockSpec((L, LANES), const2)]
        args += [cos_t, sin_t]
    widths = (SCAN_WIDTH, GATE_WIDTH, DT_PAD)
    dtypes = (BF16, BF16, F32)
    out_specs = [pl.BlockSpec((1, L, w), lambda s: (s, 0, 0)) for w in widths]
    out_shape = [jax.ShapeDtypeStruct((nb, L, w), d) for w, d in zip(widths, dtypes)]
    return pl.pallas_call(
        functools.partial(_in_proj_kernel, seq_len=L, rope=rope),
        grid=(nb,),
        in_specs=in_specs,
        out_specs=out_specs,
        out_shape=out_shape,
        scratch_shapes=[pltpu.VMEM((L, D_MODEL), BF16)],
        compiler_params=pltpu.CompilerParams(
            dimension_semantics=("parallel",), vmem_limit_bytes=VMEM_LIMIT),
        name="in_proj_rope" if rope else "in_proj",
    )(*args)


def _mixer_kernel(*refs, n_streams, n_groups, nc, has_s0, emit_state, per_seq_mod):
    Q = CHUNK
    H = SSD_HEADS
    U = n_streams
    it = iter(refs)
    scan_ref, gate_ref, dt_ref = (next(it) for _ in range(3))
    nega_ref, dsk_ref, lam_ref = (next(it) for _ in range(3))
    wout_ref, xres_ref, modl_ref, npost_ref = (next(it) for _ in range(4))
    if has_s0:
        s0s_ref, s0r_ref = next(it), next(it)
    y_ref = next(it)
    if emit_state:
        os_ref, or_ref = next(it), next(it)
    (yas_ref, yar_ref, ss_ref, sr_ref, dsum_ref, ef_ref, eb_ref, tf_ref, tb_ref,
     ar_ref, exf_ref, exb_ref, cum_ref, mixs_ref) = (next(it) for _ in range(14))

    i = pl.program_id(0)
    n_real = n_groups * 2 * nc
    is_real = i < n_real
    ic = jnp.minimum(i, n_real - 1)
    s = ic % (2 * nc)
    phase2 = s >= nc
    c = jnp.where(phase2, 2 * nc - 1 - s, s)
    r0 = pl.multiple_of(c * Q, Q)
    slot_w = 1
    slot_r = 0

    rowq = lax.broadcasted_iota(jnp.int32, (Q, Q), 0)
    colq = lax.broadcasted_iota(jnp.int32, (Q, Q), 1)

    def out_proj_matmul():
        return _dot(mixs_ref[slot_r], wout_ref[...])

    def out_proj_finish(out):
        ms = jnp.mean(out * out, axis=-1, keepdims=True)
        o = out * lax.rsqrt(ms + EPS) * npost_ref[...]
        for u in range(U):
            gate = modl_ref[u if per_seq_mod else 0, :, 2 * D_MODEL:3 * D_MODEL]
            y_ref[u] = xres_ref[u] + gate * o[u * Q:(u + 1) * Q]

    @pl.when(i == 0)
    def _init_tables():
        mixs_ref[...] = jnp.zeros_like(mixs_ref)
        diff = (rowq - colq).astype(F32)
        rowf = lax.broadcasted_iota(jnp.int32, (Q, LANES), 0).astype(F32)
        colf = lax.broadcasted_iota(jnp.int32, (RET_QK_DIM, Q), 1).astype(F32)
        for h in range(RET_HEADS):
            pr, hh = divmod(h, 2)
            lf = lam_ref[h:h + 1, :]
            lb = lam_ref[RET_HEADS + h:RET_HEADS + h + 1, :]
            e = jnp.where(rowq >= colq, lf * diff, lb * (-diff))
            dsum_ref[pr, :, hh * Q:(hh + 1) * Q] = jnp.exp(e) * jnp.where(rowq == colq, 2.0, 1.0)
            hc = slice(hh * LANES, (hh + 1) * LANES)
            ef_ref[pr, :, hc] = jnp.exp(lf * (rowf + 1.0))
            eb_ref[pr, :, hc] = jnp.exp(lb * (Q - rowf))
            kr = slice(hh * RET_QK_DIM, (hh + 1) * RET_QK_DIM)
            tf_ref[pr, kr, :] = jnp.exp(lf * (Q - 1.0 - colf))
            tb_ref[pr, kr, :] = jnp.exp(lb * colf)
        ar_ref[...] = jnp.exp(lam_ref[...] * float(Q))
        er = lax.broadcasted_iota(jnp.int32, (LANES, SSD_WIDTH), 0)
        ec = lax.broadcasted_iota(jnp.int32, (LANES, SSD_WIDTH), 1) // SSD_HEAD_DIM
        exf_ref[...] = jnp.where(er == ec, 1.0, 0.0).astype(BF16)
        exb_ref[...] = jnp.where(er == ec + H, 1.0, 0.0).astype(BF16)

    def expand(w, a_row, ex_ref):
        r = BF16_ROWS
        a3 = jnp.concatenate(_split3(jnp.broadcast_to(a_row, (r, LANES))), axis=0)
        out = _dot(jnp.concatenate([w.astype(BF16), a3], axis=0), ex_ref[...])
        n = w.shape[0]
        return out[0:n], out[n:n + 1] + out[n + r:n + r + 1] + out[n + 2 * r:n + 2 * r + 1]

    def make_stream(u):
        xbc, qr, kr_, vr = (scan_ref.at[u, :, o:o + w] for o, w in SCAN_COLS)
        zs, gsr = (gate_ref.at[u, :, o:o + w] for o, w in GATE_COLS)
        dtr = dt_ref.at[u]
        yas, yar, ss, sr, cums = (r.at[u] for r in (yas_ref, yar_ref, ss_ref, sr_ref, cum_ref))
        mix_rows = slice(u * Q, (u + 1) * Q)

        def init_state():
            if has_s0:
                for d in range(2):
                    for p in range(SSD_HEADS // 2):
                        pair_t = jnp.concatenate(
                            [s0s_ref[u, 0, d, 2 * p], s0s_ref[u, 0, d, 2 * p + 1]], axis=0)
                        ss[d, :, p * LANES:(p + 1) * LANES] = pair_t.T
                sr[...] = s0r_ref[u, 0]
            else:
                ss[...] = jnp.zeros_like(ss)
                sr[...] = jnp.zeros_like(sr)

        def ssd_state_update(d, wtail_x, a_x):
            for g in range(SSD_GROUPS):
                bgt = xbc[:, XBC_B + g * SSD_STATE:XBC_B + (g + 1) * SSD_STATE].T
                gc = slice(g * GROUP_W, (g + 1) * GROUP_W)
                xt = xbc[:, gc] * wtail_x[:, gc].astype(BF16)
                ss[d, :, gc] = ss[d, :, gc] * a_x[:, gc] + _dot(bgt, xt)

        def k_transposed(pr):
            return kr_[:, pr * LANES:(pr + 1) * LANES].T

        def ret_state_update(d, tail_ref, kts):
            for pr, kt in enumerate(kts):
                pc = slice(pr * PAIR_W, (pr + 1) * PAIR_W)
                ktt = (kt.astype(F32) * tail_ref[pr]).astype(BF16)
                ds = _dot(ktt, vr[:, pc])
                for hh in range(2):
                    h = 2 * pr + hh
                    a = ar_ref[d * RET_HEADS + h:d * RET_HEADS + h + 1, :]
                    sr[d, h] = (sr[d, h] * a
                                + ds[hh * RET_QK_DIM:(hh + 1) * RET_QK_DIM, hh * LANES:(hh + 1) * LANES])

        def emit_states(d):
            for p in range(SSD_HEADS // 2):
                pair_t = ss[d, :, p * LANES:(p + 1) * LANES].T
                os_ref[u, 0, d, 2 * p] = pair_t[0:SSD_HEAD_DIM]
                os_ref[u, 0, d, 2 * p + 1] = pair_t[SSD_HEAD_DIM:2 * SSD_HEAD_DIM]
            or_ref[u, 0, d] = sr[d]

        def ret_state_blockdiag(d, pr):
            z = jnp.zeros((RET_QK_DIM, RET_V_DIM), F32)
            top = jnp.concatenate([sr[d, 2 * pr], z], axis=1)
            bot = jnp.concatenate([z, sr[d, 2 * pr + 1]], axis=1)
            return jnp.concatenate([top, bot], axis=0).astype(BF16)

        def phase1():
            dt = dtr[...]
            la = dt * nega_ref[...]
            tl_bf = jnp.where(rowq >= colq, 1.0, 0.0).astype(BF16)
            c3 = _dot(tl_bf, jnp.concatenate(_split3(la), axis=1))
            cum = c3[:, 0:LANES] + c3[:, LANES:2 * LANES] + c3[:, 2 * LANES:3 * LANES]
            cums[pl.ds(r0, Q), :] = cum
            tot = cum[Q - 1:Q, :]
            rev = tot - cum + la
            yield

            sub_k = lax.broadcasted_iota(jnp.int32, (LANES, Q), 0) < RET_QK_DIM
            lane_v = lax.broadcasted_iota(jnp.int32, (Q, PAIR_W), 1) < RET_V_DIM
            kts = [k_transposed(pr) for pr in range(RET_HEADS // 2)]
            for pr, kt in enumerate(kts):
                qp = qr[:, pr * LANES:(pr + 1) * LANES]
                zk = jnp.zeros_like(kt)
                kbd = jnp.concatenate([jnp.where(sub_k, kt, zk), jnp.where(sub_k, zk, kt)], axis=1)
                a2 = _dot(qp, kbd)
                ad = (a2 * dsum_ref[pr]).astype(BF16)
                pc = slice(pr * PAIR_W, (pr + 1) * PAIR_W)
                vp = vr[:, pc]
                zv = jnp.zeros_like(vp)
                vbd = jnp.concatenate([jnp.where(lane_v, vp, zv), jnp.where(lane_v, zv, vp)], axis=0)
                y = _dot(ad, vbd) + _dot(qp, ret_state_blockdiag(0, pr)) * ef_ref[pr]
                yar[pl.ds(r0, Q), pc] = y
            ret_state_update(0, tf_ref, kts)
            yield

            gms, css = [], []
            for g in range(SSD_GROUPS):
                cg = xbc[:, XBC_C + g * SSD_STATE:XBC_C + (g + 1) * SSD_STATE]
                bg = xbc[:, XBC_B + g * SSD_STATE:XBC_B + (g + 1) * SSD_STATE]
                gms.append(_dot_nt(cg, bg))
                css.append(_dot(cg, ss[0, :, g * GROUP_W:(g + 1) * GROUP_W].astype(BF16)))
            yield PROJECT_POINT

            dt_t = dt.T
            ldt_t = jnp.log(dt_t)
            sub = lax.broadcasted_iota(jnp.int32, (LANES, Q), 0)
            adj_t = (jnp.where(sub < H, cum.T, rev.T) - ldt_t) * LOG2E
            dg_t = jnp.log(dt_t[0:H, :] + dt_t[H:2 * H, :]) * LOG2E
            cum2 = cum * LOG2E
            rev2 = rev * LOG2E
            wtail_f = dt * jnp.exp(tot - cum)
            ew_x, af_x = expand(jnp.concatenate([jnp.exp(cum), wtail_f], axis=0), jnp.exp(tot), exf_ref)
            ecum_x = ew_x[0:Q]
            wtail_x = ew_x[Q:2 * Q]
            lt = rowq > colq
            gt = rowq < colq
            lane_g = lax.broadcasted_iota(jnp.int32, (Q, GROUP_W), 1) // SSD_HEAD_DIM
            yield

            for g in range(SSD_GROUPS):
                gc = slice(g * GROUP_W, (g + 1) * GROUP_W)
                xg = xbc[:, gc]
                ws, xs = [], []
                for j in range(4):
                    h = 4 * g + j
                    hb = H + h
                    arg = jnp.where(lt, cum2[:, h:h + 1] - adj_t[h:h + 1, :],
                                    jnp.where(gt, rev2[:, hb:hb + 1] - adj_t[hb:hb + 1, :], dg_t[h:h + 1, :]))
                    ws.append((gms[g] * jnp.exp2(arg)).astype(BF16))
                    xs.append(jnp.where(lane_g == j, xg, jnp.zeros_like(xg)))
                y = _dot(jnp.concatenate(ws, axis=1), jnp.concatenate(xs, axis=0))
                yas[pl.ds(r0, Q), gc] = y + css[g] * ecum_x[:, gc]
            yield
            ssd_state_update(0, wtail_x, af_x)

        def phase2():
            for pr in range(RET_HEADS // 2):
                qp = qr[:, pr * LANES:(pr + 1) * LANES]
                pc = slice(pr * PAIR_W, (pr + 1) * PAIR_W)
                y2 = yar[pl.ds(r0, Q), pc] + _dot(qp, ret_state_blockdiag(1, pr)) * eb_ref[pr]
                for hh in range(2):
                    h = 2 * pr + hh
                    hc = slice(h * LANES, (h + 1) * LANES)
                    y = y2[:, hh * LANES:(hh + 1) * LANES]
                    mu = jnp.mean(y, axis=-1, keepdims=True)
                    yc = y - mu
                    var = jnp.mean(yc * yc, axis=-1, keepdims=True)
                    yn = yc * lax.rsqrt(var + EPS) * gsr[:, hc].astype(F32)
                    mixs_ref[slot_w, mix_rows, SSD_WIDTH + h * LANES:SSD_WIDTH + (h + 1) * LANES] = (
                        yn.astype(BF16))
            ret_state_update(1, tb_ref, [k_transposed(pr) for pr in range(RET_HEADS // 2)])
            yield

            dt = dtr[...]
            cum = cums[pl.ds(r0, Q), :]
            rev = cum[Q - 1:Q, :] - cum + dt * nega_ref[...]
            first = rev[0:1, :]
            wtail_b = dt * jnp.exp(first - rev)
            ew_x, ab_x = expand(jnp.concatenate([jnp.exp(rev), wtail_b], axis=0), jnp.exp(first), exb_ref)
            erev_x = ew_x[0:Q]
            wtail_x = ew_x[Q:2 * Q]

            parts = []
            ssq = None
            for g in range(SSD_GROUPS):
                cg = xbc[:, XBC_C + g * SSD_STATE:XBC_C + (g + 1) * SSD_STATE]
                gc = slice(g * GROUP_W, (g + 1) * GROUP_W)
                cs = _dot(cg, ss[1, :, gc].astype(BF16))
                y = (yas[pl.ds(r0, Q), gc] + cs * erev_x[:, gc]
                     + dsk_ref[:, gc] * xbc[:, gc].astype(F32))
                y = y * zs[:, gc].astype(F32)
                parts.append(y)
                ssq = y * y if ssq is None else ssq + y * y
            yield PROJECT_POINT

            inv = lax.rsqrt(jnp.sum(ssq, axis=-1, keepdims=True) * (1.0 / SSD_WIDTH) + EPS)
            for g, y in enumerate(parts):
                gc = slice(g * GROUP_W, (g + 1) * GROUP_W)
                mixs_ref[slot_w, mix_rows, gc] = (y * inv).astype(BF16)
            ssd_state_update(1, wtail_x, ab_x)

        return types.SimpleNamespace(init_state=init_state, emit_states=emit_states, phase1=phase1,
                                     phase2=phase2)

    streams = [make_stream(u) for u in range(U)]

    def run_interleaved(stage_gens, with_out_proj):
        proj = None
        live = list(stage_gens)
        while live:
            marks, nxt = [], []
            for gen in live:
                try:
                    marks.append(next(gen))
                    nxt.append(gen)
                except StopIteration:
                    pass
            if with_out_proj and PROJECT_POINT in marks:
                proj = out_proj_matmul()
            live = nxt
        return proj

    @pl.when(is_real & (s == 0))
    def _init_state():
        for st in streams:
            st.init_state()

    @pl.when(is_real & jnp.logical_not(phase2))
    def _forward_step():
        run_interleaved([st.phase1() for st in streams], False)

    @pl.when(is_real & phase2)
    def _backward_step():
        proj = run_interleaved([st.phase2() for st in streams], True)
        out_proj_finish(proj)
        mixs_ref[slot_r] = mixs_ref[slot_w]

    @pl.when(jnp.logical_not(is_real))
    def _drain():
        out_proj_finish(out_proj_matmul())

    if emit_state:
        @pl.when(is_real & (s == nc - 1))
        def _emit_fwd():
            for st in streams:
                st.emit_states(0)

        @pl.when(is_real & (s == 2 * nc - 1))
        def _emit_bwd():
            for st in streams:
                st.emit_states(1)


def _mixer_call(scan, gate, dt, nega, dsk, lamx, w_out_bf, x, mod, npost, s0_ssd, s0_ret, emit_state):
    nb, L, _ = scan.shape
    nc = L // CHUNK
    Q = CHUNK
    U = STREAMS
    assert nb % U == 0
    has_s0 = s0_ssd is not None
    n_groups = nb // U
    steps = 2 * nc
    n_real = n_groups * steps
    per_seq_mod = mod.shape[0] > 1

    def cur(i):
        ic = jnp.minimum(i, n_real - 1)
        return ic // steps, ic % steps

    def chunk_map(i):
        b, s = cur(i)
        return (b, jnp.where(s < nc, s, steps - 1 - s), 0)

    def phase2_map(i):
        b, s = cur(i)
        return (b, jnp.where(s < nc, nc - 1, steps - 1 - s), 0)

    def lag(i):
        b, s = i // steps, i % steps
        early = s <= nc
        return (jnp.where(early, jnp.maximum(b - 1, 0), b),
                jnp.where(early, jnp.where(b == 0, nc - 1, 0), steps - s))

    def lag_map(i):
        bl, cl = lag(i)
        return (bl, cl, 0)

    def lag_mod_map(i):
        bl, _ = lag(i)
        return (bl if per_seq_mod else 0, 0, 0)

    const2 = lambda i: (0, 0)
    in_specs = [
        pl.BlockSpec((U, Q, SCAN_WIDTH), chunk_map),
        pl.BlockSpec((U, Q, GATE_WIDTH), phase2_map),
        pl.BlockSpec((U, Q, DT_PAD), chunk_map),
        pl.BlockSpec((1, DT_PAD), const2),
        pl.BlockSpec((1, SSD_WIDTH), const2),
        pl.BlockSpec((2 * RET_HEADS, LANES), const2),
        pl.BlockSpec((MIX_WIDTH, D_MODEL), const2, pipeline_mode=pl.Buffered(1)),
        pl.BlockSpec((U, Q, D_MODEL), lag_map),
        pl.BlockSpec((U if per_seq_mod else 1, 1, 3 * D_MODEL), lag_mod_map),
        pl.BlockSpec((1, D_MODEL), const2),
    ]
    args = [scan, gate, dt, nega, dsk, lamx, w_out_bf, x, mod, npost]
    state_map = lambda i: (cur(i)[0], 0, 0, 0, 0, 0)
    ssd_state_block = (U, 1, 2, SSD_HEADS, SSD_HEAD_DIM, SSD_STATE)
    ret_state_block = (U, 1, 2, RET_HEADS, RET_QK_DIM, RET_V_DIM)
    if has_s0:
        in_specs += [pl.BlockSpec(ssd_state_block, state_map, pipeline_mode=pl.Buffered(1)),
                     pl.BlockSpec(ret_state_block, state_map, pipeline_mode=pl.Buffered(1))]
        args += [s0_ssd, s0_ret]
    out_specs = [pl.BlockSpec((U, Q, D_MODEL), lag_map)]
    out_shape = [jax.ShapeDtypeStruct((nb, L, D_MODEL), F32)]
    if emit_state:
        out_specs += [pl.BlockSpec(ssd_state_block, state_map), pl.BlockSpec(ret_state_block, state_map)]
        out_shape += [jax.ShapeDtypeStruct((nb,) + ssd_state_block[1:], F32),
                      jax.ShapeDtypeStruct((nb,) + ret_state_block[1:], F32)]
    n_pairs = RET_HEADS // 2
    scratch = [
        pltpu.VMEM((U, L, SSD_WIDTH), F32),
        pltpu.VMEM((U, L, RET_V_WIDTH), F32),
        pltpu.VMEM((U, 2, SSD_STATE, SSD_WIDTH), F32),
        pltpu.VMEM((U, 2, RET_HEADS, RET_QK_DIM, RET_V_DIM), F32),
        pltpu.VMEM((n_pairs, Q, 2 * Q), F32),
        pltpu.VMEM((n_pairs, Q, PAIR_W), F32),
        pltpu.VMEM((n_pairs, Q, PAIR_W), F32),
        pltpu.VMEM((n_pairs, LANES, Q), F32),
        pltpu.VMEM((n_pairs, LANES, Q), F32),
        pltpu.VMEM((2 * RET_HEADS, LANES), F32),
        pltpu.VMEM((LANES, SSD_WIDTH), BF16),
        pltpu.VMEM((LANES, SSD_WIDTH), BF16),
        pltpu.VMEM((U, L, DT_PAD), F32),
        pltpu.VMEM((2, U * Q, MIX_WIDTH), BF16),
    ]
    outs = pl.pallas_call(
        functools.partial(_mixer_kernel, n_streams=U, n_groups=n_groups, nc=nc, has_s0=has_s0,
                          emit_state=emit_state,
                          per_seq_mod=per_seq_mod),
        grid=(n_real + 1,),
        in_specs=in_specs,
        out_specs=out_specs,
        out_shape=out_shape,
        scratch_shapes=scratch,
        compiler_params=pltpu.CompilerParams(
            dimension_semantics=("arbitrary",), vmem_limit_bytes=VMEM_LIMIT),
        name="mixer_sample" if has_s0 else "mixer_prompt",
    )(*args)
    return outs


def _rope_tables(L):
    rows = L // GRID_W
    row = np.repeat(np.arange(rows, dtype=np.float64), GRID_W)
    col = np.tile(np.arange(GRID_W, dtype=np.float64), rows)
    half = RET_QK_DIM // 2
    inv = ROPE_BASE ** (-np.arange(0, half, 2, dtype=np.float64) / half)
    ang_r = row[:, None] * inv
    ang_c = col[:, None] * inv
    cos_h = np.concatenate([np.cos(ang_r), np.cos(ang_r), np.cos(ang_c), np.cos(ang_c)], axis=-1)
    sin_h = np.concatenate([-np.sin(ang_r), np.sin(ang_r), -np.sin(ang_c), np.sin(ang_c)], axis=-1)
    return (jnp.asarray(np.tile(cos_h, (1, 2)), dtype=F32), jnp.asarray(np.tile(sin_h, (1, 2)), dtype=F32))


def kernel(x_prompt, x_sample, state_ssd, state_ret, c, c_ctx, w_mod, b_mod, norm_pre_w, norm_post_w,
           w_in, conv_w, conv_b, ssd_A_log, ssd_dt_bias, ssd_D, ssd_norm_w, ret_decay, ret_norm_w, w_out):
    nb_s = x_sample.shape[0]
    l = 0

    n_dt = 2 * SSD_HEADS
    w_packed = _pack_call(jnp.swapaxes(w_in[l], 0, 1))
    mix_norm_w = jnp.concatenate([ssd_norm_w[l], ret_norm_w[l]]).reshape(MIX_WIDTH, 1)
    w_out_bf = _scale_rows_call(w_out[l], mix_norm_w)
    dtb = jnp.pad(ssd_dt_bias[l].reshape(1, n_dt), ((0, 0), (0, DT_PAD - n_dt)))
    nega = jnp.pad(-jnp.exp(ssd_A_log[l].reshape(1, n_dt)), ((0, 0), (0, DT_PAD - n_dt)))
    dsk = jnp.repeat(ssd_D[l], SSD_HEAD_DIM).reshape(1, SSD_WIDTH)
    lamx = jnp.broadcast_to(-jnp.exp(ret_decay[l].reshape(2 * RET_HEADS, 1)), (2 * RET_HEADS, LANES))
    npre = norm_pre_w[l].reshape(1, D_MODEL)
    npost = norm_post_w[l].reshape(1, D_MODEL)
    cb = conv_b[l].reshape(1, CONV_CH)
    cos_t, sin_t = _rope_tables(x_sample.shape[1])

    mod_s, mod_p = _mod_call(c, c_ctx.reshape(1, D_MODEL), w_mod[l], b_mod[l].reshape(1, 3 * D_MODEL))

    pp = _in_proj_call(x_prompt, mod_p, npre, w_packed, conv_w[l], cb, dtb, None, None)
    y_p, st_ssd_t, st_ret = _mixer_call(*pp, nega, dsk, lamx, w_out_bf, x_prompt, mod_p, npost,
                                        None, None, True)

    ps = _in_proj_call(x_sample, mod_s, npre, w_packed, conv_w[l], cb, dtb, cos_t, sin_t)
    (y_s,) = _mixer_call(*ps, nega, dsk, lamx, w_out_bf, x_sample, mod_s, npost,
                         jnp.swapaxes(state_ssd, -1, -2), state_ret, False)
    return (y_p, y_s, jnp.swapaxes(st_ssd_t, -1, -2), st_ret)
```

```python
import functools
import types

import jax
import jax.numpy as jnp
import numpy as np
from jax import lax
from jax.experimental import pallas as pl
from jax.experimental.pallas import tpu as pltpu

F32 = jnp.float32
BF16 = jnp.bfloat16

D_MODEL = 1024
CHUNK = 128
GRID_W = 64
EPS = 1e-6
SSD_WIDTH = 1024
SSD_HEAD_DIM = 64
SSD_HEADS = 16
SSD_GROUPS = 4
SSD_STATE = 128
CONV_CH = SSD_WIDTH + 2 * SSD_GROUPS * SSD_STATE
RET_HEADS = 8
RET_QK_DIM = 64
RET_V_DIM = 128
RET_QK_WIDTH = RET_HEADS * RET_QK_DIM
RET_V_WIDTH = RET_HEADS * RET_V_DIM
MIX_WIDTH = SSD_WIDTH + RET_V_WIDTH
ROPE_BASE = 10000.0
LANES = 128
DT_PAD = LANES
GROUP_W = SSD_WIDTH // SSD_GROUPS
PAIR_W = 2 * RET_V_DIM
IN_PROJ_PIECE = 256
IN_PROJ_ROWS = 1024
STREAMS = 2
BF16_ROWS = 16
XBC_B = SSD_WIDTH
XBC_C = XBC_B + SSD_GROUPS * SSD_STATE
SUBLANES = 8
PROJECT_POINT = "project"

OFF_Z = 0
OFF_XBC = OFF_Z + SSD_WIDTH
OFF_Q = OFF_XBC + CONV_CH
OFF_K = OFF_Q + RET_QK_WIDTH
OFF_V = OFF_K + RET_QK_WIDTH
OFF_G = OFF_V + RET_V_WIDTH
OFF_DT = OFF_G + RET_V_WIDTH
IN_COLS_PACKED = OFF_DT + DT_PAD

LOG2E = 1.4426950408889634
VMEM_LIMIT = 56 * 1024 * 1024


def _silu(x):
    return x * (1.0 / (1.0 + jnp.exp(-x)))


def _dot(a, b):
    return jnp.dot(a, b, preferred_element_type=F32)


def _dot_nt(a, b):
    return lax.dot_general(a, b, (((1,), (1,)), ((), ())), preferred_element_type=F32)


def _split3(x):
    hi = x.astype(BF16)
    r1 = x - hi.astype(F32)
    mid = r1.astype(BF16)
    lo = (r1 - mid.astype(F32)).astype(BF16)
    return hi, mid, lo


PACK_COLS = 1024
PACK_STEPS = -(-IN_COLS_PACKED // PACK_COLS)


def _pack_kernel(wt_ref, out_ref):
    t = wt_ref[...].T
    lane = lax.broadcasted_iota(jnp.int32, t.shape, 1)
    is_dt = pl.program_id(0) == PACK_STEPS - 1
    t = jnp.where(jnp.logical_and(is_dt, lane >= 2 * SSD_HEADS), 0.0, t)
    out_ref[...] = t.astype(BF16)


def _scale_rows_kernel(w_ref, s_ref, out_ref):
    out_ref[...] = (w_ref[...] * s_ref[...]).astype(BF16)


def _mod_kernel(c_ref, cctx_ref, w_ref, b_ref, mod_s_ref, mod_p_ref):
    rows = c_ref.shape[0]
    ctx = jnp.broadcast_to(cctx_ref[...], (SUBLANES, D_MODEL))
    cond = jnp.concatenate([c_ref[...], ctx], axis=0)
    mod = _dot(_silu(cond).astype(BF16), w_ref[...].astype(BF16)) + b_ref[...]
    for r in range(rows):
        mod_s_ref[r] = mod[r:r + 1]
    mod_p_ref[0] = mod[rows:rows + 1]


FOLD_ROWS = 512
FOLD_STEPS = MIX_WIDTH // FOLD_ROWS
MOD_COLS = 1024
MOD_STEPS = 3 * D_MODEL // MOD_COLS


def _prep_kernel(wt_ref, wout_ref, nw_ref, c_ref, cctx_ref, wmod_ref, bmod_ref,
                 packed_ref, wout_bf_ref, mod_s_ref, mod_p_ref):
    i = pl.program_id(0)
    _pack_kernel(wt_ref, packed_ref)

    @pl.when(i < FOLD_STEPS)
    def _fold():
        _scale_rows_kernel(wout_ref, nw_ref, wout_bf_ref)

    @pl.when(i < MOD_STEPS)
    def _mod():
        _mod_kernel(c_ref, cctx_ref, wmod_ref, bmod_ref, mod_s_ref, mod_p_ref)


def _prep_call(w_in_t, w_out, mix_norm_w, c, c_ctx, w_mod, b_mod):
    assert PACK_STEPS >= max(FOLD_STEPS, MOD_STEPS)
    rows = c.shape[0]
    n_head = (OFF_Q - OFF_Z) // PACK_COLS
    src_dt = SSD_WIDTH + CONV_CH
    src_tail = src_dt + 2 * SSD_HEADS

    def src_row(i):
        t8 = jnp.where(i < n_head, i * (PACK_COLS // 8),
                       jnp.where(i < PACK_STEPS - 1, src_tail // 8 + (i - n_head) * (PACK_COLS // 8),
                                 src_dt // 8))
        return t8 * 8

    fold_blk = lambda i: (jnp.minimum(i, FOLD_STEPS - 1), 0)
    mod_blk = lambda i: (0, jnp.minimum(i, MOD_STEPS - 1))
    mod_blk3 = lambda i: (0, 0, jnp.minimum(i, MOD_STEPS - 1))
    return pl.pallas_call(
        _prep_kernel,
        grid=(PACK_STEPS,),
        in_specs=[
            pl.BlockSpec((pl.Element(PACK_COLS), pl.Element(D_MODEL)), lambda i: (src_row(i), 0)),
            pl.BlockSpec((FOLD_ROWS, D_MODEL), fold_blk),
            pl.BlockSpec((FOLD_ROWS, 1), fold_blk),
            pl.BlockSpec((rows, D_MODEL), lambda i: (0, 0)),
            pl.BlockSpec((1, D_MODEL), lambda i: (0, 0)),
            pl.BlockSpec((D_MODEL, MOD_COLS), mod_blk),
            pl.BlockSpec((1, MOD_COLS), mod_blk),
        ],
        out_specs=[
            pl.BlockSpec((D_MODEL, PACK_COLS), lambda i: (0, i)),
            pl.BlockSpec((FOLD_ROWS, D_MODEL), fold_blk),
            pl.BlockSpec((rows, 1, MOD_COLS), mod_blk3),
            pl.BlockSpec((1, 1, MOD_COLS), mod_blk3),
        ],
        out_shape=[
            jax.ShapeDtypeStruct((D_MODEL, IN_COLS_PACKED), BF16),
            jax.ShapeDtypeStruct((MIX_WIDTH, D_MODEL), BF16),
            jax.ShapeDtypeStruct((rows, 1, 3 * D_MODEL), F32),
            jax.ShapeDtypeStruct((1, 1, 3 * D_MODEL), F32),
        ],
        compiler_params=pltpu.CompilerParams(
            dimension_semantics=("arbitrary",), vmem_limit_bytes=VMEM_LIMIT),
        name="prep_params",
    )(w_in_t, w_out, mix_norm_w, c, c_ctx, w_mod, b_mod)


def _modnorm(x, mod_ref, npw_ref):
    ms = jnp.mean(x * x, axis=-1, keepdims=True)
    shift = mod_ref[0, :, 0:D_MODEL]
    scale = mod_ref[0, :, D_MODEL:2 * D_MODEL]
    gain = npw_ref[...] * (1.0 + scale)
    return (x * lax.rsqrt(ms + EPS) * gain + shift).astype(BF16)


def _in_proj_kernel(*refs, seq_len, rope):
    if rope:
        (x_ref, mod_ref, npw_ref, w_ref, cw_ref, cb_ref, dtb_ref,
         cos_ref, sin_ref, zs_ref, xbc_ref, dt_ref, q_ref, k_ref, v_ref, gs_ref, h_ref) = refs
    else:
        (x_ref, mod_ref, npw_ref, w_ref, cw_ref, cb_ref, dtb_ref,
         zs_ref, xbc_ref, dt_ref, q_ref, k_ref, v_ref, gs_ref, h_ref) = refs
    rows = seq_len
    h_ref[...] = _modnorm(x_ref[0], mod_ref, npw_ref)
    piece = IN_PROJ_PIECE

    def put(ref, cols, val):
        ref[0, :, cols] = val

    row = lax.broadcasted_iota(jnp.int32, (rows, piece), 0)
    seq_first = row == 0
    seq_last = row == rows - 1

    def sec_xbc(c0):
        cols = slice(c0, c0 + piece)
        acc = _dot(h_ref[...], w_ref[:, OFF_XBC + c0:OFF_XBC + c0 + piece])
        up = jnp.where(seq_first, 0.0, pltpu.roll(acc, 1, axis=0))
        dn = jnp.where(seq_last, 0.0, pltpu.roll(acc, rows - 1, axis=0))
        y = cw_ref[0:1, cols] * up + cw_ref[1:2, cols] * acc + cw_ref[2:3, cols] * dn + cb_ref[:, cols]
        put(xbc_ref, cols, _silu(y).astype(BF16))

    lane = lax.broadcasted_iota(jnp.int32, (rows, LANES), 1)
    first_half = (lane % 32) < 16

    def sec_qk(off, ref, scl, c0):
        acc = _dot(h_ref[...], w_ref[:, off + c0:off + c0 + piece]) * scl
        if rope:
            for s0 in range(0, piece, LANES):
                xs = acc[:, s0:s0 + LANES]
                partner = jnp.where(first_half, pltpu.roll(xs, LANES - 16, axis=1),
                                    pltpu.roll(xs, 16, axis=1))
                put(ref, slice(c0 + s0, c0 + s0 + LANES),
                    (xs * cos_ref[...] + partner * sin_ref[...]).astype(BF16))
        else:
            put(ref, slice(c0, c0 + piece), acc.astype(BF16))

    def sec_q(c0):
        sec_qk(OFF_Q, q_ref, 1.0, c0)

    def sec_k(c0):
        sec_qk(OFF_K, k_ref, RET_QK_DIM ** -0.5, c0)

    def sec_z(c0):
        acc = _dot(h_ref[...], w_ref[:, OFF_Z + c0:OFF_Z + c0 + piece])
        put(zs_ref, slice(c0, c0 + piece), _silu(acc).astype(BF16))

    def sec_g(c0):
        acc = _dot(h_ref[...], w_ref[:, OFF_G + c0:OFF_G + c0 + piece])
        put(gs_ref, slice(c0, c0 + piece), _silu(acc).astype(BF16))

    def sec_v(c0):
        acc = _dot(h_ref[...], w_ref[:, OFF_V + c0:OFF_V + c0 + piece])
        put(v_ref, slice(c0, c0 + piece), acc.astype(BF16))

    def sec_dt(c0):
        acc = _dot(h_ref[...], w_ref[:, OFF_DT:OFF_DT + DT_PAD]) + dtb_ref[...]
        put(dt_ref, slice(0, DT_PAD), jnp.maximum(acc, 0.0) + jnp.log(1.0 + jnp.exp(-jnp.abs(acc))))

    def pieces(fn, width):
        return [(fn, c0) for c0 in range(0, width, piece)]

    conv = pieces(sec_xbc, CONV_CH)
    light = pieces(sec_v, RET_V_WIDTH) + [(sec_dt, 0)]
    order = pieces(sec_q, RET_QK_WIDTH) + pieces(sec_k, RET_QK_WIDTH)
    stride = -(-len(conv) // len(light))
    for i, item in enumerate(conv):
        order.append(item)
        if i % stride == stride - 1 and light:
            order.append(light.pop(0))
    order += light + pieces(sec_z, SSD_WIDTH) + pieces(sec_g, RET_V_WIDTH)
    for fn, c0 in order:
        fn(c0)


def _in_proj_call(x, mod, npw, w_packed, conv_w, conv_b, dtb, cos_t, sin_t):
    nb, L, _ = x.shape
    rope = cos_t is not None
    per_seq_mod = mod.shape[0] > 1
    assert L <= IN_PROJ_ROWS
    mod_map = (lambda s: (s, 0, 0)) if per_seq_mod else (lambda s: (0, 0, 0))
    const2 = lambda s: (0, 0)
    in_specs = [
        pl.BlockSpec((1, L, D_MODEL), lambda s: (s, 0, 0)),
        pl.BlockSpec((1, 1, 3 * D_MODEL), mod_map),
        pl.BlockSpec((1, D_MODEL), const2),
        pl.BlockSpec((D_MODEL, IN_COLS_PACKED), const2, pipeline_mode=pl.Buffered(1)),
        pl.BlockSpec((3, CONV_CH), const2),
        pl.BlockSpec((1, CONV_CH), const2),
        pl.BlockSpec((1, DT_PAD), const2),
    ]
    args = [x, mod, npw, w_packed, conv_w, conv_b, dtb]
    if rope:
        in_specs += [pl.BlockSpec((L, LANES), const2), pl.BlockSpec((L, LANES), const2)]
        args += [cos_t, sin_t]
    widths = (SSD_WIDTH, CONV_CH, DT_PAD, RET_QK_WIDTH, RET_QK_WIDTH, RET_V_WIDTH, RET_V_WIDTH)
    dtypes = (BF16, BF16, F32, BF16, BF16, BF16, BF16)
    out_specs = [pl.BlockSpec((1, L, w), lambda s: (s, 0, 0)) for w in widths]
    out_shape = [jax.ShapeDtypeStruct((nb, L, w), d) for w, d in zip(widths, dtypes)]
    return pl.pallas_call(
        functools.partial(_in_proj_kernel, seq_len=L, rope=rope),
        grid=(nb,),
        in_specs=in_specs,
        out_specs=out_specs,
        out_shape=out_shape,
        scratch_shapes=[pltpu.VMEM((L, D_MODEL), BF16)],
        compiler_params=pltpu.CompilerParams(
            dimension_semantics=("parallel",), vmem_limit_bytes=VMEM_LIMIT),
        name="in_proj_rope" if rope else "in_proj",
    )(*args)


def _mixer_kernel(*refs, n_streams, n_groups, nc, has_s0, emit_state, per_seq_mod):
    Q = CHUNK
    H = SSD_HEADS
    U = n_streams
    it = iter(refs)
    zs_ref, xbc_ref, dt_ref, q_ref, k_ref, v_ref, gs_ref = (next(it) for _ in range(7))
    nega_ref, dsk_ref, lam_ref = (next(it) for _ in range(3))
    wout_ref, xres_ref, modl_ref, npost_ref = (next(it) for _ in range(4))
    if has_s0:
        s0s_ref, s0r_ref = next(it), next(it)
    y_ref = next(it)
    if emit_state:
        os_ref, or_ref = next(it), next(it)
    (yas_ref, yar_ref, ss_ref, sr_ref, dsum_ref, ef_ref, eb_ref, tf_ref, tb_ref,
     ar_ref, exf_ref, exb_ref, cum_ref, mixs_ref) = (next(it) for _ in range(14))

    i = pl.program_id(0)
    n_real = n_groups * 2 * nc
    is_real = i < n_real
    ic = jnp.minimum(i, n_real - 1)
    s = ic % (2 * nc)
    phase2 = s >= nc
    c = jnp.where(phase2, 2 * nc - 1 - s, s)
    r0 = pl.multiple_of(c * Q, Q)
    slot_w = 1
    slot_r = 0

    rowq = lax.broadcasted_iota(jnp.int32, (Q, Q), 0)
    colq = lax.broadcasted_iota(jnp.int32, (Q, Q), 1)

    def out_proj_matmul():
        return _dot(mixs_ref[slot_r], wout_ref[...])

    def out_proj_finish(out):
        ms = jnp.mean(out * out, axis=-1, keepdims=True)
        o = out * lax.rsqrt(ms + EPS) * npost_ref[...]
        for u in range(U):
            gate = modl_ref[u if per_seq_mod else 0, :, 2 * D_MODEL:3 * D_MODEL]
            y_ref[u] = xres_ref[u] + gate * o[u * Q:(u + 1) * Q]

    @pl.when(i == 0)
    def _init_tables():
        mixs_ref[...] = jnp.zeros_like(mixs_ref)
        diff = (rowq - colq).astype(F32)
        rowf = lax.broadcasted_iota(jnp.int32, (Q, LANES), 0).astype(F32)
        colf = lax.broadcasted_iota(jnp.int32, (RET_QK_DIM, Q), 1).astype(F32)
        for h in range(RET_HEADS):
            pr, hh = divmod(h, 2)
            lf = lam_ref[h:h + 1, :]
            lb = lam_ref[RET_HEADS + h:RET_HEADS + h + 1, :]
            e = jnp.where(rowq >= colq, lf * diff, lb * (-diff))
            dsum_ref[pr, :, hh * Q:(hh + 1) * Q] = jnp.exp(e) * jnp.where(rowq == colq, 2.0, 1.0)
            hc = slice(hh * LANES, (hh + 1) * LANES)
            ef_ref[pr, :, hc] = jnp.exp(lf * (rowf + 1.0))
            eb_ref[pr, :, hc] = jnp.exp(lb * (Q - rowf))
            kr = slice(hh * RET_QK_DIM, (hh + 1) * RET_QK_DIM)
            tf_ref[pr, kr, :] = jnp.exp(lf * (Q - 1.0 - colf))
            tb_ref[pr, kr, :] = jnp.exp(lb * colf)
        ar_ref[...] = jnp.exp(lam_ref[...] * float(Q))
        er = lax.broadcasted_iota(jnp.int32, (LANES, SSD_WIDTH), 0)
        ec = lax.broadcasted_iota(jnp.int32, (LANES, SSD_WIDTH), 1) // SSD_HEAD_DIM
        exf_ref[...] = jnp.where(er == ec, 1.0, 0.0).astype(BF16)
        exb_ref[...] = jnp.where(er == ec + H, 1.0, 0.0).astype(BF16)

    def expand(w, a_row, ex_ref):
        r = BF16_ROWS
        a3 = jnp.concatenate(_split3(jnp.broadcast_to(a_row, (r, LANES))), axis=0)
        out = _dot(jnp.concatenate([w.astype(BF16), a3], axis=0), ex_ref[...])
        n = w.shape[0]
        return out[0:n], out[n:n + 1] + out[n + r:n + r + 1] + out[n + 2 * r:n + 2 * r + 1]

    def make_stream(u):
        zs, xbc, dtr, qr, kr_, vr, gsr = (r.at[u] for r in (zs_ref, xbc_ref, dt_ref, q_ref, k_ref,
                                                            v_ref, gs_ref))
        yas, yar, ss, sr, cums = (r.at[u] for r in (yas_ref, yar_ref, ss_ref, sr_ref, cum_ref))
        mix_rows = slice(u * Q, (u + 1) * Q)

        def init_state():
            if has_s0:
                for d in range(2):
                    for p in range(SSD_HEADS // 2):
                        pair_t = jnp.concatenate(
                            [s0s_ref[u, 0, d, 2 * p], s0s_ref[u, 0, d, 2 * p + 1]], axis=0)
                        ss[d, :, p * LANES:(p + 1) * LANES] = pair_t.T
                sr[...] = s0r_ref[u, 0]
            else:
                ss[...] = jnp.zeros_like(ss)
                sr[...] = jnp.zeros_like(sr)

        def ssd_state_update(d, wtail_x, a_x):
            for g in range(SSD_GROUPS):
                bgt = xbc[:, XBC_B + g * SSD_STATE:XBC_B + (g + 1) * SSD_STATE].T
                gc = slice(g * GROUP_W, (g + 1) * GROUP_W)
                xt = xbc[:, gc] * wtail_x[:, gc].astype(BF16)
                ss[d, :, gc] = ss[d, :, gc] * a_x[:, gc] + _dot(bgt, xt)

        def k_transposed(pr):
            return kr_[:, pr * LANES:(pr + 1) * LANES].T

        def ret_state_update(d, tail_ref, kts):
            for pr, kt in enumerate(kts):
                pc = slice(pr * PAIR_W, (pr + 1) * PAIR_W)
                ktt = (kt.astype(F32) * tail_ref[pr]).astype(BF16)
                ds = _dot(ktt, vr[:, pc])
                for hh in range(2):
                    h = 2 * pr + hh
                    a = ar_ref[d * RET_HEADS + h:d * RET_HEADS + h + 1, :]
                    sr[d, h] = (sr[d, h] * a
                                + ds[hh * RET_QK_DIM:(hh + 1) * RET_QK_DIM, hh * LANES:(hh + 1) * LANES])

        def emit_states(d):
            for p in range(SSD_HEADS // 2):
                pair_t = ss[d, :, p * LANES:(p + 1) * LANES].T
                os_ref[u, 0, d, 2 * p] = pair_t[0:SSD_HEAD_DIM]
                os_ref[u, 0, d, 2 * p + 1] = pair_t[SSD_HEAD_DIM:2 * SSD_HEAD_DIM]
            or_ref[u, 0, d] = sr[d]

        def ret_state_blockdiag(d, pr):
            z = jnp.zeros((RET_QK_DIM, RET_V_DIM), F32)
            top = jnp.concatenate([sr[d, 2 * pr], z], axis=1)
            bot = jnp.concatenate([z, sr[d, 2 * pr + 1]], axis=1)
            return jnp.concatenate([top, bot], axis=0).astype(BF16)

        def phase1():
            dt = dtr[...]
            la = dt * nega_ref[...]
            tl_bf = jnp.where(rowq >= colq, 1.0, 0.0).astype(BF16)
            c3 = _dot(tl_bf, jnp.concatenate(_split3(la), axis=1))
            cum = c3[:, 0:LANES] + c3[:, LANES:2 * LANES] + c3[:, 2 * LANES:3 * LANES]
            cums[pl.ds(r0, Q), :] = cum
            tot = cum[Q - 1:Q, :]
            rev = tot - cum + la
            yield

            sub_k = lax.broadcasted_iota(jnp.int32, (LANES, Q), 0) < RET_QK_DIM
            lane_v = lax.broadcasted_iota(jnp.int32, (Q, PAIR_W), 1) < RET_V_DIM
            kts = [k_transposed(pr) for pr in range(RET_HEADS // 2)]
            for pr, kt in enumerate(kts):
                qp = qr[:, pr * LANES:(pr + 1) * LANES]
                zk = jnp.zeros_like(kt)
                kbd = jnp.concatenate([jnp.where(sub_k, kt, zk), jnp.where(sub_k, zk, kt)], axis=1)
                a2 = _dot(qp, kbd)
                ad = (a2 * dsum_ref[pr]).astype(BF16)
                pc = slice(pr * PAIR_W, (pr + 1) * PAIR_W)
                vp = vr[:, pc]
                zv = jnp.zeros_like(vp)
                vbd = jnp.concatenate([jnp.where(lane_v, vp, zv), jnp.where(lane_v, zv, vp)], axis=0)
                y = _dot(ad, vbd) + _dot(qp, ret_state_blockdiag(0, pr)) * ef_ref[pr]
                yar[pl.ds(r0, Q), pc] = y
            ret_state_update(0, tf_ref, kts)
            yield

            gms, css = [], []
            for g in range(SSD_GROUPS):
                cg = xbc[:, XBC_C + g * SSD_STATE:XBC_C + (g + 1) * SSD_STATE]
                bg = xbc[:, XBC_B + g * SSD_STATE:XBC_B + (g + 1) * SSD_STATE]
                gms.append(_dot_nt(cg, bg))
                css.append(_dot(cg, ss[0, :, g * GROUP_W:(g + 1) * GROUP_W].astype(BF16)))
            yield PROJECT_POINT

            dt_t = dt.T
            ldt_t = jnp.log(dt_t)
            sub = lax.broadcasted_iota(jnp.int32, (LANES, Q), 0)
            adj_t = (jnp.where(sub < H, cum.T, rev.T) - ldt_t) * LOG2E
            dg_t = jnp.log(dt_t[0:H, :] + dt_t[H:2 * H, :]) * LOG2E
            cum2 = cum * LOG2E
            rev2 = rev * LOG2E
            wtail_f = dt * jnp.exp(tot - cum)
            ew_x, af_x = expand(jnp.concatenate([jnp.exp(cum), wtail_f], axis=0), jnp.exp(tot), exf_ref)
            ecum_x = ew_x[0:Q]
            wtail_x = ew_x[Q:2 * Q]
            lt = rowq > colq
            gt = rowq < colq
            lane_g = lax.broadcasted_iota(jnp.int32, (Q, GROUP_W), 1) // SSD_HEAD_DIM
            yield

            for g in range(SSD_GROUPS):
                gc = slice(g * GROUP_W, (g + 1) * GROUP_W)
                xg = xbc[:, gc]
                ws, xs = [], []
                for j in range(4):
                    h = 4 * g + j
                    hb = H + h
                    arg = jnp.where(lt, cum2[:, h:h + 1] - adj_t[h:h + 1, :],
                                    jnp.where(gt, rev2[:, hb:hb + 1] - adj_t[hb:hb + 1, :], dg_t[h:h + 1, :]))
                    ws.append((gms[g] * jnp.exp2(arg)).astype(BF16))
                    xs.append(jnp.where(lane_g == j, xg, jnp.zeros_like(xg)))
                y = _dot(jnp.concatenate(ws, axis=1), jnp.concatenate(xs, axis=0))
                yas[pl.ds(r0, Q), gc] = y + css[g] * ecum_x[:, gc]
            yield
            ssd_state_update(0, wtail_x, af_x)

        def phase2():
            for pr in range(RET_HEADS // 2):
                qp = qr[:, pr * LANES:(pr + 1) * LANES]
                pc = slice(pr * PAIR_W, (pr + 1) * PAIR_W)
                y2 = yar[pl.ds(r0, Q), pc] + _dot(qp, ret_state_blockdiag(1, pr)) * eb_ref[pr]
                for hh in range(2):
                    h = 2 * pr + hh
                    hc = slice(h * LANES, (h + 1) * LANES)
                    y = y2[:, hh * LANES:(hh + 1) * LANES]
                    mu = jnp.mean(y, axis=-1, keepdims=True)
                    yc = y - mu
                    var = jnp.mean(yc * yc, axis=-1, keepdims=True)
                    yn = yc * lax.rsqrt(var + EPS) * gsr[:, hc].astype(F32)
                    mixs_ref[slot_w, mix_rows, SSD_WIDTH + h * LANES:SSD_WIDTH + (h + 1) * LANES] = (
                        yn.astype(BF16))
            ret_state_update(1, tb_ref, [k_transposed(pr) for pr in range(RET_HEADS // 2)])
            yield

            dt = dtr[...]
            cum = cums[pl.ds(r0, Q), :]
            rev = cum[Q - 1:Q, :] - cum + dt * nega_ref[...]
            first = rev[0:1, :]
            wtail_b = dt * jnp.exp(first - rev)
            ew_x, ab_x = expand(jnp.concatenate([jnp.exp(rev), wtail_b], axis=0), jnp.exp(first), exb_ref)
            erev_x = ew_x[0:Q]
            wtail_x = ew_x[Q:2 * Q]

            parts = []
            ssq = None
            for g in range(SSD_GROUPS):
                cg = xbc[:, XBC_C + g * SSD_STATE:XBC_C + (g + 1) * SSD_STATE]
                gc = slice(g * GROUP_W, (g + 1) * GROUP_W)
                cs = _dot(cg, ss[1, :, gc].astype(BF16))
                y = (yas[pl.ds(r0, Q), gc] + cs * erev_x[:, gc]
                     + dsk_ref[:, gc] * xbc[:, gc].astype(F32))
                y = y * zs[:, gc].astype(F32)
                parts.append(y)
                ssq = y * y if ssq is None else ssq + y * y
            yield PROJECT_POINT

            inv = lax.rsqrt(jnp.sum(ssq, axis=-1, keepdims=True) * (1.0 / SSD_WIDTH) + EPS)
            for g, y in enumerate(parts):
                gc = slice(g * GROUP_W, (g + 1) * GROUP_W)
                mixs_ref[slot_w, mix_rows, gc] = (y * inv).astype(BF16)
            ssd_state_update(1, wtail_x, ab_x)

        return types.SimpleNamespace(init_state=init_state, emit_states=emit_states, phase1=phase1,
                                     phase2=phase2)

    streams = [make_stream(u) for u in range(U)]

    def run_interleaved(stage_gens, with_out_proj):
        proj = None
        live = list(stage_gens)
        while live:
            marks, nxt = [], []
            for gen in live:
                try:
                    marks.append(next(gen))
                    nxt.append(gen)
                except StopIteration:
                    pass
            if with_out_proj and PROJECT_POINT in marks:
                proj = out_proj_matmul()
            live = nxt
        return proj

    @pl.when(is_real & (s == 0))
    def _init_state():
        for st in streams:
            st.init_state()

    @pl.when(is_real & jnp.logical_not(phase2))
    def _forward_step():
        run_interleaved([st.phase1() for st in streams], False)

    @pl.when(is_real & phase2)
    def _backward_step():
        proj = run_interleaved([st.phase2() for st in streams], True)
        out_proj_finish(proj)
        mixs_ref[slot_r] = mixs_ref[slot_w]

    @pl.when(jnp.logical_not(is_real))
    def _drain():
        out_proj_finish(out_proj_matmul())

    if emit_state:
        @pl.when(is_real & (s == nc - 1))
        def _emit_fwd():
            for st in streams:
                st.emit_states(0)

        @pl.when(is_real & (s == 2 * nc - 1))
        def _emit_bwd():
            for st in streams:
                st.emit_states(1)


def _mixer_call(zs, xbc, dt, q, k, v, gs, nega, dsk, lamx, w_out_bf, x, mod, npost, s0_ssd, s0_ret,
                emit_state):
    nb, L, _ = zs.shape
    nc = L // CHUNK
    Q = CHUNK
    U = STREAMS
    assert nb % U == 0
    has_s0 = s0_ssd is not None
    n_groups = nb // U
    steps = 2 * nc
    n_real = n_groups * steps
    per_seq_mod = mod.shape[0] > 1

    def cur(i):
        ic = jnp.minimum(i, n_real - 1)
        return ic // steps, ic % steps

    def chunk_map(i):
        b, s = cur(i)
        return (b, jnp.where(s < nc, s, steps - 1 - s), 0)

    def phase2_map(i):
        b, s = cur(i)
        return (b, jnp.where(s < nc, nc - 1, steps - 1 - s), 0)

    def lag(i):
        b, s = i // steps, i % steps
        early = s <= nc
        return (jnp.where(early, jnp.maximum(b - 1, 0), b),
                jnp.where(early, jnp.where(b == 0, nc - 1, 0), steps - s))

    def lag_map(i):
        bl, cl = lag(i)
        return (bl, cl, 0)

    def lag_mod_map(i):
        bl, _ = lag(i)
        return (bl if per_seq_mod else 0, 0, 0)

    const2 = lambda i: (0, 0)
    in_specs = [
        pl.BlockSpec((U, Q, SSD_WIDTH), phase2_map),
        pl.BlockSpec((U, Q, CONV_CH), chunk_map),
        pl.BlockSpec((U, Q, DT_PAD), chunk_map),
        pl.BlockSpec((U, Q, RET_QK_WIDTH), chunk_map),
        pl.BlockSpec((U, Q, RET_QK_WIDTH), chunk_map),
        pl.BlockSpec((U, Q, RET_V_WIDTH), chunk_map),
        pl.BlockSpec((U, Q, RET_V_WIDTH), phase2_map),
        pl.BlockSpec((1, DT_PAD), const2),
        pl.BlockSpec((1, SSD_WIDTH), const2),
        pl.BlockSpec((2 * RET_HEADS, LANES), const2),
        pl.BlockSpec((MIX_WIDTH, D_MODEL), const2, pipeline_mode=pl.Buffered(1)),
        pl.BlockSpec((U, Q, D_MODEL), lag_map),
        pl.BlockSpec((U if per_seq_mod else 1, 1, 3 * D_MODEL), lag_mod_map),
        pl.BlockSpec((1, D_MODEL), const2),
    ]
    args = [zs, xbc, dt, q, k, v, gs, nega, dsk, lamx, w_out_bf, x, mod, npost]
    state_map = lambda i: (cur(i)[0], 0, 0, 0, 0, 0)
    ssd_state_block = (U, 1, 2, SSD_HEADS, SSD_HEAD_DIM, SSD_STATE)
    ret_state_block = (U, 1, 2, RET_HEADS, RET_QK_DIM, RET_V_DIM)
    if has_s0:
        in_specs += [pl.BlockSpec(ssd_state_block, state_map, pipeline_mode=pl.Buffered(1)),
                     pl.BlockSpec(ret_state_block, state_map, pipeline_mode=pl.Buffered(1))]
        args += [s0_ssd, s0_ret]
    out_specs = [pl.BlockSpec((U, Q, D_MODEL), lag_map)]
    out_shape = [jax.ShapeDtypeStruct((nb, L, D_MODEL), F32)]
    if emit_state:
        out_specs += [pl.BlockSpec(ssd_state_block, state_map), pl.BlockSpec(ret_state_block, state_map)]
        out_shape += [jax.ShapeDtypeStruct((nb,) + ssd_state_block[1:], F32),
                      jax.ShapeDtypeStruct((nb,) + ret_state_block[1:], F32)]
    n_pairs = RET_HEADS // 2
    scratch = [
        pltpu.VMEM((U, L, SSD_WIDTH), F32),
        pltpu.VMEM((U, L, RET_V_WIDTH), F32),
        pltpu.VMEM((U, 2, SSD_STATE, SSD_WIDTH), F32),
        pltpu.VMEM((U, 2, RET_HEADS, RET_QK_DIM, RET_V_DIM), F32),
        pltpu.VMEM((n_pairs, Q, 2 * Q), F32),
        pltpu.VMEM((n_pairs, Q, PAIR_W), F32),
        pltpu.VMEM((n_pairs, Q, PAIR_W), F32),
        pltpu.VMEM((n_pairs, LANES, Q), F32),
        pltpu.VMEM((n_pairs, LANES, Q), F32),
        pltpu.VMEM((2 * RET_HEADS, LANES), F32),
        pltpu.VMEM((LANES, SSD_WIDTH), BF16),
        pltpu.VMEM((LANES, SSD_WIDTH), BF16),
        pltpu.VMEM((U, L, DT_PAD), F32),
        pltpu.VMEM((2, U * Q, MIX_WIDTH), BF16),
    ]
    outs = pl.pallas_call(
        functools.partial(_mixer_kernel, n_streams=U, n_groups=n_groups, nc=nc, has_s0=has_s0,
                          emit_state=emit_state,
                          per_seq_mod=per_seq_mod),
        grid=(n_real + 1,),
        in_specs=in_specs,
        out_specs=out_specs,
        out_shape=out_shape,
        scratch_shapes=scratch,
        compiler_params=pltpu.CompilerParams(
            dimension_semantics=("arbitrary",), vmem_limit_bytes=VMEM_LIMIT),
        name="mixer_sample" if has_s0 else "mixer_prompt",
    )(*args)
    return outs


def _rope_tables(L):
    rows = L // GRID_W
    row = np.repeat(np.arange(rows, dtype=np.float64), GRID_W)
    col = np.tile(np.arange(GRID_W, dtype=np.float64), rows)
    half = RET_QK_DIM // 2
    inv = ROPE_BASE ** (-np.arange(0, half, 2, dtype=np.float64) / half)
    ang_r = row[:, None] * inv
    ang_c = col[:, None] * inv
    cos_h = np.concatenate([np.cos(ang_r), np.cos(ang_r), np.cos(ang_c), np.cos(ang_c)], axis=-1)
    sin_h = np.concatenate([-np.sin(ang_r), np.sin(ang_r), -np.sin(ang_c), np.sin(ang_c)], axis=-1)
    return (jnp.asarray(np.tile(cos_h, (1, 2)), dtype=F32), jnp.asarray(np.tile(sin_h, (1, 2)), dtype=F32))


def kernel(x_prompt, x_sample, state_ssd, state_ret, c, c_ctx, w_mod, b_mod, norm_pre_w, norm_post_w,
           w_in, conv_w, conv_b, ssd_A_log, ssd_dt_bias, ssd_D, ssd_norm_w, ret_decay, ret_norm_w, w_out):
    assert w_in.shape[0] == 1, "one layer (DEPTH == 1) is supported"
    l = 0

    n_dt = 2 * SSD_HEADS
    mix_norm_w = jnp.concatenate([ssd_norm_w[l], ret_norm_w[l]]).reshape(MIX_WIDTH, 1)
    w_packed, w_out_bf, mod_s, mod_p = _prep_call(
        jnp.swapaxes(w_in[l], 0, 1), w_out[l], mix_norm_w, c, c_ctx.reshape(1, D_MODEL), w_mod[l],
        b_mod[l].reshape(1, 3 * D_MODEL))
    dtb = jnp.pad(ssd_dt_bias[l].reshape(1, n_dt), ((0, 0), (0, DT_PAD - n_dt)))
    nega = jnp.pad(-jnp.exp(ssd_A_log[l].reshape(1, n_dt)), ((0, 0), (0, DT_PAD - n_dt)))
    dsk = jnp.repeat(ssd_D[l], SSD_HEAD_DIM).reshape(1, SSD_WIDTH)
    lamx = jnp.broadcast_to(-jnp.exp(ret_decay[l].reshape(2 * RET_HEADS, 1)), (2 * RET_HEADS, LANES))
    npre = norm_pre_w[l].reshape(1, D_MODEL)
    npost = norm_post_w[l].reshape(1, D_MODEL)
    cb = conv_b[l].reshape(1, CONV_CH)
    cos_t, sin_t = _rope_tables(x_sample.shape[1])

    pp = _in_proj_call(x_prompt, mod_p, npre, w_packed, conv_w[l], cb, dtb, None, None)
    y_p, st_ssd_t, st_ret = _mixer_call(*pp, nega, dsk, lamx, w_out_bf, x_prompt, mod_p, npost,
                                        None, None, True)

    ps = _in_proj_call(x_sample, mod_s, npre, w_packed, conv_w[l], cb, dtb, cos_t, sin_t)
    (y_s,) = _mixer_call(*ps, nega, dsk, lamx, w_out_bf, x_sample, mod_s, npost,
                         jnp.swapaxes(state_ssd, -1, -2), state_ret, False)
    return (y_p, y_s, jnp.swapaxes(st_ssd_t, -1, -2), st_ret)
```

```python
import functools
import types

import jax
import jax.numpy as jnp
import numpy as np
from jax import lax
from jax.experimental import pallas as pl
from jax.experimental.pallas import tpu as pltpu

F32 = jnp.float32
BF16 = jnp.bfloat16

D_MODEL = 1024
CHUNK = 128
GRID_W = 64
EPS = 1e-6
SSD_WIDTH = 1024
SSD_HEAD_DIM = 64
SSD_HEADS = 16
SSD_GROUPS = 4
SSD_STATE = 128
CONV_CH = SSD_WIDTH + 2 * SSD_GROUPS * SSD_STATE
RET_HEADS = 8
RET_QK_DIM = 64
RET_V_DIM = 128
RET_QK_WIDTH = RET_HEADS * RET_QK_DIM
RET_V_WIDTH = RET_HEADS * RET_V_DIM
MIX_WIDTH = SSD_WIDTH + RET_V_WIDTH
ROPE_BASE = 10000.0
LANES = 128
DT_PAD = LANES
GROUP_W = SSD_WIDTH // SSD_GROUPS
PAIR_W = 2 * RET_V_DIM
IN_PROJ_PIECE = 256
IN_PROJ_ROWS = 1024
STREAMS = 2
BF16_ROWS = 16
XBC_B = SSD_WIDTH
XBC_C = XBC_B + SSD_GROUPS * SSD_STATE
SUBLANES = 8
PROJECT_POINT = "project"

OFF_Z = 0
OFF_XBC = OFF_Z + SSD_WIDTH
OFF_Q = OFF_XBC + CONV_CH
OFF_K = OFF_Q + RET_QK_WIDTH
OFF_V = OFF_K + RET_QK_WIDTH
OFF_G = OFF_V + RET_V_WIDTH
OFF_DT = OFF_G + RET_V_WIDTH
IN_COLS_PACKED = OFF_DT + DT_PAD

LOG2E = 1.4426950408889634
VMEM_LIMIT = 56 * 1024 * 1024


def _silu(x):
    return x * (1.0 / (1.0 + jnp.exp(-x)))


def _dot(a, b):
    return jnp.dot(a, b, preferred_element_type=F32)


def _dot_nt(a, b):
    return lax.dot_general(a, b, (((1,), (1,)), ((), ())), preferred_element_type=F32)


def _split3(x):
    hi = x.astype(BF16)
    r1 = x - hi.astype(F32)
    mid = r1.astype(BF16)
    lo = (r1 - mid.astype(F32)).astype(BF16)
    return hi, mid, lo


PACK_COLS = 1024
PACK_STEPS = -(-IN_COLS_PACKED // PACK_COLS)


def _pack_kernel(wt_ref, out_ref):
    t = wt_ref[...].T
    lane = lax.broadcasted_iota(jnp.int32, t.shape, 1)
    is_dt = pl.program_id(0) == PACK_STEPS - 1
    t = jnp.where(jnp.logical_and(is_dt, lane >= 2 * SSD_HEADS), 0.0, t)
    out_ref[...] = t.astype(BF16)


def _scale_rows_kernel(w_ref, s_ref, out_ref):
    out_ref[...] = (w_ref[...] * s_ref[...]).astype(BF16)


def _mod_kernel(c_ref, cctx_ref, w_ref, b_ref, mod_s_ref, mod_p_ref):
    rows = c_ref.shape[0]
    ctx = jnp.broadcast_to(cctx_ref[...], (SUBLANES, D_MODEL))
    cond = jnp.concatenate([c_ref[...], ctx], axis=0)
    mod = _dot(_silu(cond).astype(BF16), w_ref[...].astype(BF16)) + b_ref[...]
    for r in range(rows):
        mod_s_ref[r] = mod[r:r + 1]
    mod_p_ref[0] = mod[rows:rows + 1]


FOLD_ROWS = 512
FOLD_STEPS = MIX_WIDTH // FOLD_ROWS
MOD_COLS = 1024
MOD_STEPS = 3 * D_MODEL // MOD_COLS


def _prep_kernel(wt_ref, wout_ref, nw_ref, c_ref, cctx_ref, wmod_ref, bmod_ref,
                 packed_ref, wout_bf_ref, mod_s_ref, mod_p_ref):
    i = pl.program_id(0)
    _pack_kernel(wt_ref, packed_ref)

    @pl.when(i < FOLD_STEPS)
    def _fold():
        _scale_rows_kernel(wout_ref, nw_ref, wout_bf_ref)

    @pl.when(i < MOD_STEPS)
    def _mod():
        _mod_kernel(c_ref, cctx_ref, wmod_ref, bmod_ref, mod_s_ref, mod_p_ref)


def _prep_call(w_in_t, w_out, mix_norm_w, c, c_ctx, w_mod, b_mod):
    assert PACK_STEPS >= max(FOLD_STEPS, MOD_STEPS)
    rows = c.shape[0]
    n_head = (OFF_Q - OFF_Z) // PACK_COLS
    src_dt = SSD_WIDTH + CONV_CH
    src_tail = src_dt + 2 * SSD_HEADS

    def src_row(i):
        t8 = jnp.where(i < n_head, i * (PACK_COLS // 8),
                       jnp.where(i < PACK_STEPS - 1, src_tail // 8 + (i - n_head) * (PACK_COLS // 8),
                                 src_dt // 8))
        return t8 * 8

    fold_blk = lambda i: (jnp.minimum(i, FOLD_STEPS - 1), 0)
    mod_blk = lambda i: (0, jnp.minimum(i, MOD_STEPS - 1))
    mod_blk3 = lambda i: (0, 0, jnp.minimum(i, MOD_STEPS - 1))
    return pl.pallas_call(
        _prep_kernel,
        grid=(PACK_STEPS,),
        in_specs=[
            pl.BlockSpec((pl.Element(PACK_COLS), pl.Element(D_MODEL)), lambda i: (src_row(i), 0)),
            pl.BlockSpec((FOLD_ROWS, D_MODEL), fold_blk),
            pl.BlockSpec((FOLD_ROWS, 1), fold_blk),
            pl.BlockSpec((rows, D_MODEL), lambda i: (0, 0)),
            pl.BlockSpec((1, D_MODEL), lambda i: (0, 0)),
            pl.BlockSpec((D_MODEL, MOD_COLS), mod_blk),
            pl.BlockSpec((1, MOD_COLS), mod_blk),
        ],
        out_specs=[
            pl.BlockSpec((D_MODEL, PACK_COLS), lambda i: (0, i)),
            pl.BlockSpec((FOLD_ROWS, D_MODEL), fold_blk),
            pl.BlockSpec((rows, 1, MOD_COLS), mod_blk3),
            pl.BlockSpec((1, 1, MOD_COLS), mod_blk3),
        ],
        out_shape=[
            jax.ShapeDtypeStruct((D_MODEL, IN_COLS_PACKED), BF16),
            jax.ShapeDtypeStruct((MIX_WIDTH, D_MODEL), BF16),
            jax.ShapeDtypeStruct((rows, 1, 3 * D_MODEL), F32),
            jax.ShapeDtypeStruct((1, 1, 3 * D_MODEL), F32),
        ],
        compiler_params=pltpu.CompilerParams(
            dimension_semantics=("arbitrary",), vmem_limit_bytes=VMEM_LIMIT),
        name="prep_params",
    )(w_in_t, w_out, mix_norm_w, c, c_ctx, w_mod, b_mod)


def _modnorm(x, mod_ref, npw_ref):
    ms = jnp.mean(x * x, axis=-1, keepdims=True)
    shift = mod_ref[0, :, 0:D_MODEL]
    scale = mod_ref[0, :, D_MODEL:2 * D_MODEL]
    gain = npw_ref[...] * (1.0 + scale)
    return (x * lax.rsqrt(ms + EPS) * gain + shift).astype(BF16)


def _in_proj_kernel(*refs, seq_len, rope):
    if rope:
        (x_ref, mod_ref, npw_ref, w_ref, cw_ref, cb_ref, dtb_ref,
         cos_ref, sin_ref, zs_ref, xbc_ref, dt_ref, q_ref, k_ref, v_ref, gs_ref, h_ref) = refs
    else:
        (x_ref, mod_ref, npw_ref, w_ref, cw_ref, cb_ref, dtb_ref,
         zs_ref, xbc_ref, dt_ref, q_ref, k_ref, v_ref, gs_ref, h_ref) = refs
    rows = seq_len
    h_ref[...] = _modnorm(x_ref[0], mod_ref, npw_ref)
    piece = IN_PROJ_PIECE

    def put(ref, cols, val):
        ref[0, :, cols] = val

    row = lax.broadcasted_iota(jnp.int32, (rows, piece), 0)
    seq_first = row == 0
    seq_last = row == rows - 1

    def sec_xbc(c0):
        cols = slice(c0, c0 + piece)
        acc = _dot(h_ref[...], w_ref[:, OFF_XBC + c0:OFF_XBC + c0 + piece])
        up = jnp.where(seq_first, 0.0, pltpu.roll(acc, 1, axis=0))
        dn = jnp.where(seq_last, 0.0, pltpu.roll(acc, rows - 1, axis=0))
        y = cw_ref[0:1, cols] * up + cw_ref[1:2, cols] * acc + cw_ref[2:3, cols] * dn + cb_ref[:, cols]
        put(xbc_ref, cols, _silu(y).astype(BF16))

    lane = lax.broadcasted_iota(jnp.int32, (rows, LANES), 1)
    first_half = (lane % 32) < 16

    def sec_qk(off, ref, scl, c0):
        acc = _dot(h_ref[...], w_ref[:, off + c0:off + c0 + piece]) * scl
        if rope:
            for s0 in range(0, piece, LANES):
                xs = acc[:, s0:s0 + LANES]
                partner = jnp.where(first_half, pltpu.roll(xs, LANES - 16, axis=1),
                                    pltpu.roll(xs, 16, axis=1))
                put(ref, slice(c0 + s0, c0 + s0 + LANES),
                    (xs * cos_ref[...] + partner * sin_ref[...]).astype(BF16))
        else:
            put(ref, slice(c0, c0 + piece), acc.astype(BF16))

    def sec_q(c0):
        sec_qk(OFF_Q, q_ref, 1.0, c0)

    def sec_k(c0):
        sec_qk(OFF_K, k_ref, RET_QK_DIM ** -0.5, c0)

    def sec_z(c0):
        acc = _dot(h_ref[...], w_ref[:, OFF_Z + c0:OFF_Z + c0 + piece])
        put(zs_ref, slice(c0, c0 + piece), _silu(acc).astype(BF16))

    def sec_g(c0):
        acc = _dot(h_ref[...], w_ref[:, OFF_G + c0:OFF_G + c0 + piece])
        put(gs_ref, slice(c0, c0 + piece), _silu(acc).astype(BF16))

    def sec_v(c0):
        acc = _dot(h_ref[...], w_ref[:, OFF_V + c0:OFF_V + c0 + piece])
        put(v_ref, slice(c0, c0 + piece), acc.astype(BF16))

    def sec_dt(c0):
        acc = _dot(h_ref[...], w_ref[:, OFF_DT:OFF_DT + DT_PAD]) + dtb_ref[...]
        put(dt_ref, slice(0, DT_PAD), jnp.maximum(acc, 0.0) + jnp.log1p(jnp.exp(-jnp.abs(acc))))

    def pieces(fn, width):
        return [(fn, c0) for c0 in range(0, width, piece)]

    conv = pieces(sec_xbc, CONV_CH)
    light = pieces(sec_v, RET_V_WIDTH) + [(sec_dt, 0)]
    order = pieces(sec_q, RET_QK_WIDTH) + pieces(sec_k, RET_QK_WIDTH)
    stride = -(-len(conv) // len(light))
    for i, item in enumerate(conv):
        order.append(item)
        if i % stride == stride - 1 and light:
            order.append(light.pop(0))
    order += light + pieces(sec_z, SSD_WIDTH) + pieces(sec_g, RET_V_WIDTH)
    for fn, c0 in order:
        fn(c0)


def _in_proj_call(x, mod, npw, w_packed, conv_w, conv_b, dtb, cos_t, sin_t):
    nb, L, _ = x.shape
    rope = cos_t is not None
    per_seq_mod = mod.shape[0] > 1
    assert L <= IN_PROJ_ROWS
    mod_map = (lambda s: (s, 0, 0)) if per_seq_mod else (lambda s: (0, 0, 0))
    const2 = lambda s: (0, 0)
    in_specs = [
        pl.BlockSpec((1, L, D_MODEL), lambda s: (s, 0, 0)),
        pl.BlockSpec((1, 1, 3 * D_MODEL), mod_map),
        pl.BlockSpec((1, D_MODEL), const2),
        pl.BlockSpec((D_MODEL, IN_COLS_PACKED), const2, pipeline_mode=pl.Buffered(1)),
        pl.BlockSpec((3, CONV_CH), const2),
        pl.BlockSpec((1, CONV_CH), const2),
        pl.BlockSpec((1, DT_PAD), const2),
    ]
    args = [x, mod, npw, w_packed, conv_w, conv_b, dtb]
    if rope:
        in_specs += [pl.BlockSpec((L, LANES), const2), pl.BlockSpec((L, LANES), const2)]
        args += [cos_t, sin_t]
    widths = (SSD_WIDTH, CONV_CH, DT_PAD, RET_QK_WIDTH, RET_QK_WIDTH, RET_V_WIDTH, RET_V_WIDTH)
    dtypes = (BF16, BF16, F32, BF16, BF16, BF16, BF16)
    out_specs = [pl.BlockSpec((1, L, w), lambda s: (s, 0, 0)) for w in widths]
    out_shape = [jax.ShapeDtypeStruct((nb, L, w), d) for w, d in zip(widths, dtypes)]
    return pl.pallas_call(
        functools.partial(_in_proj_kernel, seq_len=L, rope=rope),
        grid=(nb,),
        in_specs=in_specs,
        out_specs=out_specs,
        out_shape=out_shape,
        scratch_shapes=[pltpu.VMEM((L, D_MODEL), BF16)],
        compiler_params=pltpu.CompilerParams(
            dimension_semantics=("parallel",), vmem_limit_bytes=VMEM_LIMIT),
        name="in_proj_rope" if rope else "in_proj",
    )(*args)


def _mixer_kernel(*refs, n_streams, n_groups, nc, has_s0, emit_state, per_seq_mod):
    Q = CHUNK
    H = SSD_HEADS
    U = n_streams
    it = iter(refs)
    zs_ref, xbc_ref, dt_ref, q_ref, k_ref, v_ref, gs_ref = (next(it) for _ in range(7))
    nega_ref, dsk_ref, lam_ref = (next(it) for _ in range(3))
    wout_ref, xres_ref, modl_ref, npost_ref = (next(it) for _ in range(4))
    if has_s0:
        s0s_ref, s0r_ref = next(it), next(it)
    y_ref = next(it)
    if emit_state:
        os_ref, or_ref = next(it), next(it)
    (yas_ref, yar_ref, ss_ref, sr_ref, dsum_ref, ef_ref, eb_ref, tf_ref, tb_ref,
     ar_ref, exf_ref, exb_ref, cum_ref, mixs_ref) = (next(it) for _ in range(14))

    i = pl.program_id(0)
    n_real = n_groups * 2 * nc
    is_real = i < n_real
    ic = jnp.minimum(i, n_real - 1)
    s = ic % (2 * nc)
    phase2 = s >= nc
    c = jnp.where(phase2, 2 * nc - 1 - s, s)
    r0 = pl.multiple_of(c * Q, Q)
    slot_w = 1
    slot_r = 0

    rowq = lax.broadcasted_iota(jnp.int32, (Q, Q), 0)
    colq = lax.broadcasted_iota(jnp.int32, (Q, Q), 1)

    def out_proj_matmul():
        return _dot(mixs_ref[slot_r], wout_ref[...])

    def out_proj_finish(out):
        ms = jnp.mean(out * out, axis=-1, keepdims=True)
        o = out * lax.rsqrt(ms + EPS) * npost_ref[...]
        for u in range(U):
            gate = modl_ref[u if per_seq_mod else 0, :, 2 * D_MODEL:3 * D_MODEL]
            y_ref[u] = xres_ref[u] + gate * o[u * Q:(u + 1) * Q]

    @pl.when(i == 0)
    def _init_tables():
        mixs_ref[...] = jnp.zeros_like(mixs_ref)
        diff = (rowq - colq).astype(F32)
        rowf = lax.broadcasted_iota(jnp.int32, (Q, LANES), 0).astype(F32)
        colf = lax.broadcasted_iota(jnp.int32, (RET_QK_DIM, Q), 1).astype(F32)
        for h in range(RET_HEADS):
            pr, hh = divmod(h, 2)
            lf = lam_ref[h:h + 1, :]
            lb = lam_ref[RET_HEADS + h:RET_HEADS + h + 1, :]
            e = jnp.where(rowq >= colq, lf * diff, lb * (-diff))
            dsum_ref[pr, :, hh * Q:(hh + 1) * Q] = jnp.exp(e) * jnp.where(rowq == colq, 2.0, 1.0)
            hc = slice(hh * LANES, (hh + 1) * LANES)
            ef_ref[pr, :, hc] = jnp.exp(lf * (rowf + 1.0))
            eb_ref[pr, :, hc] = jnp.exp(lb * (Q - rowf))
            kr = slice(hh * RET_QK_DIM, (hh + 1) * RET_QK_DIM)
            tf_ref[pr, kr, :] = jnp.exp(lf * (Q - 1.0 - colf))
            tb_ref[pr, kr, :] = jnp.exp(lb * colf)
        ar_ref[...] = jnp.exp(lam_ref[...] * float(Q))
        er = lax.broadcasted_iota(jnp.int32, (LANES, SSD_WIDTH), 0)
        ec = lax.broadcasted_iota(jnp.int32, (LANES, SSD_WIDTH), 1) // SSD_HEAD_DIM
        exf_ref[...] = jnp.where(er == ec, 1.0, 0.0).astype(BF16)
        exb_ref[...] = jnp.where(er == ec + H, 1.0, 0.0).astype(BF16)

    def expand(w, a_row, ex_ref):
        r = BF16_ROWS
        a3 = jnp.concatenate(_split3(jnp.broadcast_to(a_row, (r, LANES))), axis=0)
        out = _dot(jnp.concatenate([w.astype(BF16), a3], axis=0), ex_ref[...])
        n = w.shape[0]
        return out[0:n], out[n:n + 1] + out[n + r:n + r + 1] + out[n + 2 * r:n + 2 * r + 1]

    def make_stream(u):
        zs, xbc, dtr, qr, kr_, vr, gsr = (r.at[u] for r in (zs_ref, xbc_ref, dt_ref, q_ref, k_ref,
                                                            v_ref, gs_ref))
        yas, yar, ss, sr, cums = (r.at[u] for r in (yas_ref, yar_ref, ss_ref, sr_ref, cum_ref))
        mix_rows = slice(u * Q, (u + 1) * Q)

        def init_state():
            if has_s0:
                for d in range(2):
                    for p in range(SSD_HEADS // 2):
                        pair_t = jnp.concatenate(
                            [s0s_ref[u, 0, d, 2 * p], s0s_ref[u, 0, d, 2 * p + 1]], axis=0)
                        ss[d, :, p * LANES:(p + 1) * LANES] = pair_t.T
                sr[...] = s0r_ref[u, 0]
            else:
                ss[...] = jnp.zeros_like(ss)
                sr[...] = jnp.zeros_like(sr)

        def ssd_state_update(d, wtail_x, a_x):
            for g in range(SSD_GROUPS):
                bgt = xbc[:, XBC_B + g * SSD_STATE:XBC_B + (g + 1) * SSD_STATE].T
                gc = slice(g * GROUP_W, (g + 1) * GROUP_W)
                xt = xbc[:, gc] * wtail_x[:, gc].astype(BF16)
                ss[d, :, gc] = ss[d, :, gc] * a_x[:, gc] + _dot(bgt, xt)

        def k_transposed(pr):
            return kr_[:, pr * LANES:(pr + 1) * LANES].T

        def ret_state_update(d, tail_ref, kts):
            for pr, kt in enumerate(kts):
                pc = slice(pr * PAIR_W, (pr + 1) * PAIR_W)
                ktt = (kt.astype(F32) * tail_ref[pr]).astype(BF16)
                ds = _dot(ktt, vr[:, pc])
                for hh in range(2):
                    h = 2 * pr + hh
                    a = ar_ref[d * RET_HEADS + h:d * RET_HEADS + h + 1, :]
                    sr[d, h] = (sr[d, h] * a
                                + ds[hh * RET_QK_DIM:(hh + 1) * RET_QK_DIM, hh * LANES:(hh + 1) * LANES])

        def emit_states(d):
            for p in range(SSD_HEADS // 2):
                pair_t = ss[d, :, p * LANES:(p + 1) * LANES].T
                os_ref[u, 0, d, 2 * p] = pair_t[0:SSD_HEAD_DIM]
                os_ref[u, 0, d, 2 * p + 1] = pair_t[SSD_HEAD_DIM:2 * SSD_HEAD_DIM]
            or_ref[u, 0, d] = sr[d]

        def ret_state_blockdiag(d, pr):
            z = jnp.zeros((RET_QK_DIM, RET_V_DIM), F32)
            top = jnp.concatenate([sr[d, 2 * pr], z], axis=1)
            bot = jnp.concatenate([z, sr[d, 2 * pr + 1]], axis=1)
            return jnp.concatenate([top, bot], axis=0).astype(BF16)

        def phase1():
            dt = dtr[...]
            la = dt * nega_ref[...]
            tl_bf = jnp.where(rowq >= colq, 1.0, 0.0).astype(BF16)
            c3 = _dot(tl_bf, jnp.concatenate(_split3(la), axis=1))
            cum = c3[:, 0:LANES] + c3[:, LANES:2 * LANES] + c3[:, 2 * LANES:3 * LANES]
            cums[pl.ds(r0, Q), :] = cum
            tot = cum[Q - 1:Q, :]
            rev = tot - cum + la
            yield

            sub_k = lax.broadcasted_iota(jnp.int32, (LANES, Q), 0) < RET_QK_DIM
            lane_v = lax.broadcasted_iota(jnp.int32, (Q, PAIR_W), 1) < RET_V_DIM
            kts = [k_transposed(pr) for pr in range(RET_HEADS // 2)]
            for pr, kt in enumerate(kts):
                qp = qr[:, pr * LANES:(pr + 1) * LANES]
                zk = jnp.zeros_like(kt)
                kbd = jnp.concatenate([jnp.where(sub_k, kt, zk), jnp.where(sub_k, zk, kt)], axis=1)
                a2 = _dot(qp, kbd)
                ad = (a2 * dsum_ref[pr]).astype(BF16)
                pc = slice(pr * PAIR_W, (pr + 1) * PAIR_W)
                vp = vr[:, pc]
                zv = jnp.zeros_like(vp)
                vbd = jnp.concatenate([jnp.where(lane_v, vp, zv), jnp.where(lane_v, zv, vp)], axis=0)
                y = _dot(ad, vbd) + _dot(qp, ret_state_blockdiag(0, pr)) * ef_ref[pr]
                yar[pl.ds(r0, Q), pc] = y
            ret_state_update(0, tf_ref, kts)
            yield

            gms, css = [], []
            for g in range(SSD_GROUPS):
                cg = xbc[:, XBC_C + g * SSD_STATE:XBC_C + (g + 1) * SSD_STATE]
                bg = xbc[:, XBC_B + g * SSD_STATE:XBC_B + (g + 1) * SSD_STATE]
                gms.append(_dot_nt(cg, bg))
                css.append(_dot(cg, ss[0, :, g * GROUP_W:(g + 1) * GROUP_W].astype(BF16)))
            yield PROJECT_POINT

            dt_t = dt.T
            ldt_t = jnp.log(dt_t)
            sub = lax.broadcasted_iota(jnp.int32, (LANES, Q), 0)
            adj_t = (jnp.where(sub < H, cum.T, rev.T) - ldt_t) * LOG2E
            dg_t = jnp.log(dt_t[0:H, :] + dt_t[H:2 * H, :]) * LOG2E
            cum2 = cum * LOG2E
            rev2 = rev * LOG2E
            wtail_f = dt * jnp.exp(tot - cum)
            ew_x, af_x = expand(jnp.concatenate([jnp.exp(cum), wtail_f], axis=0), jnp.exp(tot), exf_ref)
            ecum_x = ew_x[0:Q]
            wtail_x = ew_x[Q:2 * Q]
            lt = rowq > colq
            gt = rowq < colq
            lane_g = lax.broadcasted_iota(jnp.int32, (Q, GROUP_W), 1) // SSD_HEAD_DIM
            yield

            for g in range(SSD_GROUPS):
                gc = slice(g * GROUP_W, (g + 1) * GROUP_W)
                xg = xbc[:, gc]
                ws, xs = [], []
                for j in range(4):
                    h = 4 * g + j
                    hb = H + h
                    arg = jnp.where(lt, cum2[:, h:h + 1] - adj_t[h:h + 1, :],
                                    jnp.where(gt, rev2[:, hb:hb + 1] - adj_t[hb:hb + 1, :], dg_t[h:h + 1, :]))
                    ws.append((gms[g] * jnp.exp2(arg)).astype(BF16))
                    xs.append(jnp.where(lane_g == j, xg, jnp.zeros_like(xg)))
                y = _dot(jnp.concatenate(ws, axis=1), jnp.concatenate(xs, axis=0))
                yas[pl.ds(r0, Q), gc] = y + css[g] * ecum_x[:, gc]
            yield
            ssd_state_update(0, wtail_x, af_x)

        def phase2():
            for pr in range(RET_HEADS // 2):
                qp = qr[:, pr * LANES:(pr + 1) * LANES]
                pc = slice(pr * PAIR_W, (pr + 1) * PAIR_W)
                y2 = yar[pl.ds(r0, Q), pc] + _dot(qp, ret_state_blockdiag(1, pr)) * eb_ref[pr]
                for hh in range(2):
                    h = 2 * pr + hh
                    hc = slice(h * LANES, (h + 1) * LANES)
                    y = y2[:, hh * LANES:(hh + 1) * LANES]
                    mu = jnp.mean(y, axis=-1, keepdims=True)
                    yc = y - mu
                    var = jnp.mean(yc * yc, axis=-1, keepdims=True)
                    yn = yc * lax.rsqrt(var + EPS) * gsr[:, hc].astype(F32)
                    mixs_ref[slot_w, mix_rows, SSD_WIDTH + h * LANES:SSD_WIDTH + (h + 1) * LANES] = (
                        yn.astype(BF16))
            ret_state_update(1, tb_ref, [k_transposed(pr) for pr in range(RET_HEADS // 2)])
            yield

            dt = dtr[...]
            cum = cums[pl.ds(r0, Q), :]
            rev = cum[Q - 1:Q, :] - cum + dt * nega_ref[...]
            first = rev[0:1, :]
            wtail_b = dt * jnp.exp(first - rev)
            ew_x, ab_x = expand(jnp.concatenate([jnp.exp(rev), wtail_b], axis=0), jnp.exp(first), exb_ref)
            erev_x = ew_x[0:Q]
            wtail_x = ew_x[Q:2 * Q]

            parts = []
            ssq = None
            for g in range(SSD_GROUPS):
                cg = xbc[:, XBC_C + g * SSD_STATE:XBC_C + (g + 1) * SSD_STATE]
                gc = slice(g * GROUP_W, (g + 1) * GROUP_W)
                cs = _dot(cg, ss[1, :, gc].astype(BF16))
                y = (yas[pl.ds(r0, Q), gc] + cs * erev_x[:, gc]
                     + dsk_ref[:, gc] * xbc[:, gc].astype(F32))
                y = y * zs[:, gc].astype(F32)
                parts.append(y)
                ssq = y * y if ssq is None else ssq + y * y
            yield PROJECT_POINT

            inv = lax.rsqrt(jnp.sum(ssq, axis=-1, keepdims=True) * (1.0 / SSD_WIDTH) + EPS)
            for g, y in enumerate(parts):
                gc = slice(g * GROUP_W, (g + 1) * GROUP_W)
                mixs_ref[slot_w, mix_rows, gc] = (y * inv).astype(BF16)
            ssd_state_update(1, wtail_x, ab_x)

        return types.SimpleNamespace(init_state=init_state, emit_states=emit_states, phase1=phase1,
                                     phase2=phase2)

    streams = [make_stream(u) for u in range(U)]

    def run_interleaved(stage_gens, with_out_proj):
        proj = None
        live = list(stage_gens)
        while live:
            marks, nxt = [], []
            for gen in live:
                try:
                    marks.append(next(gen))
                    nxt.append(gen)
                except StopIteration:
                    pass
            if with_out_proj and PROJECT_POINT in marks:
                proj = out_proj_matmul()
            live = nxt
        return proj

    @pl.when(is_real & (s == 0))
    def _init_state():
        for st in streams:
            st.init_state()

    @pl.when(is_real & jnp.logical_not(phase2))
    def _forward_step():
        run_interleaved([st.phase1() for st in streams], False)

    @pl.when(is_real & phase2)
    def _backward_step():
        proj = run_interleaved([st.phase2() for st in streams], True)
        out_proj_finish(proj)
        mixs_ref[slot_r] = mixs_ref[slot_w]

    @pl.when(jnp.logical_not(is_real))
    def _drain():
        out_proj_finish(out_proj_matmul())

    if emit_state:
        @pl.when(is_real & (s == nc - 1))
        def _emit_fwd():
            for st in streams:
                st.emit_states(0)

        @pl.when(is_real & (s == 2 * nc - 1))
        def _emit_bwd():
            for st in streams:
                st.emit_states(1)


def _mixer_call(zs, xbc, dt, q, k, v, gs, nega, dsk, lamx, w_out_bf, x, mod, npost, s0_ssd, s0_ret,
                emit_state):
    nb, L, _ = zs.shape
    nc = L // CHUNK
    Q = CHUNK
    U = STREAMS
    assert nb % U == 0
    has_s0 = s0_ssd is not None
    n_groups = nb // U
    steps = 2 * nc
    n_real = n_groups * steps
    per_seq_mod = mod.shape[0] > 1

    def cur(i):
        ic = jnp.minimum(i, n_real - 1)
        return ic // steps, ic % steps

    def chunk_map(i):
        b, s = cur(i)
        return (b, jnp.where(s < nc, s, steps - 1 - s), 0)

    def phase2_map(i):
        b, s = cur(i)
        return (b, jnp.where(s < nc, nc - 1, steps - 1 - s), 0)

    def lag(i):
        b, s = i // steps, i % steps
        early = s <= nc
        return (jnp.where(early, jnp.maximum(b - 1, 0), b),
                jnp.where(early, jnp.where(b == 0, nc - 1, 0), steps - s))

    def lag_map(i):
        bl, cl = lag(i)
        return (bl, cl, 0)

    def lag_mod_map(i):
        bl, _ = lag(i)
        return (bl if per_seq_mod else 0, 0, 0)

    const2 = lambda i: (0, 0)
    in_specs = [
        pl.BlockSpec((U, Q, SSD_WIDTH), phase2_map),
        pl.BlockSpec((U, Q, CONV_CH), chunk_map),
        pl.BlockSpec((U, Q, DT_PAD), chunk_map),
        pl.BlockSpec((U, Q, RET_QK_WIDTH), chunk_map),
        pl.BlockSpec((U, Q, RET_QK_WIDTH), chunk_map),
        pl.BlockSpec((U, Q, RET_V_WIDTH), chunk_map),
        pl.BlockSpec((U, Q, RET_V_WIDTH), phase2_map),
        pl.BlockSpec((1, DT_PAD), const2),
        pl.BlockSpec((1, SSD_WIDTH), const2),
        pl.BlockSpec((2 * RET_HEADS, LANES), const2),
        pl.BlockSpec((MIX_WIDTH, D_MODEL), const2, pipeline_mode=pl.Buffered(1)),
        pl.BlockSpec((U, Q, D_MODEL), lag_map),
        pl.BlockSpec((U if per_seq_mod else 1, 1, 3 * D_MODEL), lag_mod_map),
        pl.BlockSpec((1, D_MODEL), const2),
    ]
    args = [zs, xbc, dt, q, k, v, gs, nega, dsk, lamx, w_out_bf, x, mod, npost]
    state_map = lambda i: (cur(i)[0], 0, 0, 0, 0, 0)
    ssd_state_block = (U, 1, 2, SSD_HEADS, SSD_HEAD_DIM, SSD_STATE)
    ret_state_block = (U, 1, 2, RET_HEADS, RET_QK_DIM, RET_V_DIM)
    if has_s0:
        in_specs += [pl.BlockSpec(ssd_state_block, state_map, pipeline_mode=pl.Buffered(1)),
                     pl.BlockSpec(ret_state_block, state_map, pipeline_mode=pl.Buffered(1))]
        args += [s0_ssd, s0_ret]
    out_specs = [pl.BlockSpec((U, Q, D_MODEL), lag_map)]
    out_shape = [jax.ShapeDtypeStruct((nb, L, D_MODEL), F32)]
    if emit_state:
        out_specs += [pl.BlockSpec(ssd_state_block, state_map), pl.BlockSpec(ret_state_block, state_map)]
        out_shape += [jax.ShapeDtypeStruct((nb,) + ssd_state_block[1:], F32),
                      jax.ShapeDtypeStruct((nb,) + ret_state_block[1:], F32)]
    n_pairs = RET_HEADS // 2
    scratch = [
        pltpu.VMEM((U, L, SSD_WIDTH), F32),
        pltpu.VMEM((U, L, RET_V_WIDTH), F32),
        pltpu.VMEM((U, 2, SSD_STATE, SSD_WIDTH), F32),
        pltpu.VMEM((U, 2, RET_HEADS, RET_QK_DIM, RET_V_DIM), F32),
        pltpu.VMEM((n_pairs, Q, 2 * Q), F32),
        pltpu.VMEM((n_pairs, Q, PAIR_W), F32),
        pltpu.VMEM((n_pairs, Q, PAIR_W), F32),
        pltpu.VMEM((n_pairs, LANES, Q), F32),
        pltpu.VMEM((n_pairs, LANES, Q), F32),
        pltpu.VMEM((2 * RET_HEADS, LANES), F32),
        pltpu.VMEM((LANES, SSD_WIDTH), BF16),
        pltpu.VMEM((LANES, SSD_WIDTH), BF16),
        pltpu.VMEM((U, L, DT_PAD), F32),
        pltpu.VMEM((2, U * Q, MIX_WIDTH), BF16),
    ]
    outs = pl.pallas_call(
        functools.partial(_mixer_kernel, n_streams=U, n_groups=n_groups, nc=nc, has_s0=has_s0,
                          emit_state=emit_state,
                          per_seq_mod=per_seq_mod),
        grid=(n_real + 1,),
        in_specs=in_specs,
        out_specs=out_specs,
        out_shape=out_shape,
        scratch_shapes=scratch,
        compiler_params=pltpu.CompilerParams(
            dimension_semantics=("arbitrary",), vmem_limit_bytes=VMEM_LIMIT),
        name="mixer_sample" if has_s0 else "mixer_prompt",
    )(*args)
    return outs


def _rope_tables(L):
    rows = L // GRID_W
    row = np.repeat(np.arange(rows, dtype=np.float64), GRID_W)
    col = np.tile(np.arange(GRID_W, dtype=np.float64), rows)
    half = RET_QK_DIM // 2
    inv = ROPE_BASE ** (-np.arange(0, half, 2, dtype=np.float64) / half)
    ang_r = row[:, None] * inv
    ang_c = col[:, None] * inv
    cos_h = np.concatenate([np.cos(ang_r), np.cos(ang_r), np.cos(ang_c), np.cos(ang_c)], axis=-1)
    sin_h = np.concatenate([-np.sin(ang_r), np.sin(ang_r), -np.sin(ang_c), np.sin(ang_c)], axis=-1)
    return (jnp.asarray(np.tile(cos_h, (1, 2)), dtype=F32), jnp.asarray(np.tile(sin_h, (1, 2)), dtype=F32))


def kernel(x_prompt, x_sample, state_ssd, state_ret, c, c_ctx, w_mod, b_mod, norm_pre_w, norm_post_w,
           w_in, conv_w, conv_b, ssd_A_log, ssd_dt_bias, ssd_D, ssd_norm_w, ret_decay, ret_norm_w, w_out):
    assert w_in.shape[0] == 1, "one layer (DEPTH == 1) is supported"
    l = 0

    n_dt = 2 * SSD_HEADS
    mix_norm_w = jnp.concatenate([ssd_norm_w[l], ret_norm_w[l]]).reshape(MIX_WIDTH, 1)
    w_packed, w_out_bf, mod_s, mod_p = _prep_call(
        jnp.swapaxes(w_in[l], 0, 1), w_out[l], mix_norm_w, c, c_ctx.reshape(1, D_MODEL), w_mod[l],
        b_mod[l].reshape(1, 3 * D_MODEL))
    dtb = jnp.pad(ssd_dt_bias[l].reshape(1, n_dt), ((0, 0), (0, DT_PAD - n_dt)))
    nega = jnp.pad(-jnp.exp(ssd_A_log[l].reshape(1, n_dt)), ((0, 0), (0, DT_PAD - n_dt)))
    dsk = jnp.repeat(ssd_D[l], SSD_HEAD_DIM).reshape(1, SSD_WIDTH)
    lamx = jnp.broadcast_to(-jnp.exp(ret_decay[l].reshape(2 * RET_HEADS, 1)), (2 * RET_HEADS, LANES))
    npre = norm_pre_w[l].reshape(1, D_MODEL)
    npost = norm_post_w[l].reshape(1, D_MODEL)
    cb = conv_b[l].reshape(1, CONV_CH)
    cos_t, sin_t = _rope_tables(x_sample.shape[1])

    pp = _in_proj_call(x_prompt, mod_p, npre, w_packed, conv_w[l], cb, dtb, None, None)
    y_p, st_ssd_t, st_ret = _mixer_call(*pp, nega, dsk, lamx, w_out_bf, x_prompt, mod_p, npost,
                                        None, None, True)

    ps = _in_proj_call(x_sample, mod_s, npre, w_packed, conv_w[l], cb, dtb, cos_t, sin_t)
    (y_s,) = _mixer_call(*ps, nega, dsk, lamx, w_out_bf, x_sample, mod_s, npost,
                         jnp.swapaxes(state_ssd, -1, -2), state_ret, False)
    return (y_p, y_s, jnp.swapaxes(st_ssd_t, -1, -2), st_ret)
```

```python
import functools
import types

import jax
import jax.numpy as jnp
import numpy as np
from jax import lax
from jax.experimental import pallas as pl
from jax.experimental.pallas import tpu as pltpu

F32 = jnp.float32
BF16 = jnp.bfloat16

D_MODEL = 1024
CHUNK = 128
GRID_W = 64
EPS = 1e-6
SSD_WIDTH = 1024
SSD_HEAD_DIM = 64
SSD_HEADS = 16
SSD_GROUPS = 4
SSD_STATE = 128
CONV_CH = SSD_WIDTH + 2 * SSD_GROUPS * SSD_STATE
RET_HEADS = 8
RET_QK_DIM = 64
RET_V_DIM = 128
RET_QK_WIDTH = RET_HEADS * RET_QK_DIM
RET_V_WIDTH = RET_HEADS * RET_V_DIM
MIX_WIDTH = SSD_WIDTH + RET_V_WIDTH
ROPE_BASE = 10000.0
LANES = 128
DT_PAD = LANES
GROUP_W = SSD_WIDTH // SSD_GROUPS
PAIR_W = 2 * RET_V_DIM
IN_PROJ_PIECE = 256
IN_PROJ_ROWS = 1024
STREAMS = 2
BF16_ROWS = 16
XBC_B = SSD_WIDTH
XBC_C = XBC_B + SSD_GROUPS * SSD_STATE
SUBLANES = 8
PROJECT_POINT = "project"

OFF_Z = 0
OFF_XBC = OFF_Z + SSD_WIDTH
OFF_Q = OFF_XBC + CONV_CH
OFF_K = OFF_Q + RET_QK_WIDTH
OFF_V = OFF_K + RET_QK_WIDTH
OFF_G = OFF_V + RET_V_WIDTH
OFF_DT = OFF_G + RET_V_WIDTH
IN_COLS_PACKED = OFF_DT + DT_PAD

LOG2E = 1.4426950408889634
VMEM_LIMIT = 56 * 1024 * 1024


def _silu(x):
    return x * (1.0 / (1.0 + jnp.exp(-x)))


def _dot(a, b):
    return jnp.dot(a, b, preferred_element_type=F32)


def _dot_nt(a, b):
    return lax.dot_general(a, b, (((1,), (1,)), ((), ())), preferred_element_type=F32)


def _split3(x):
    hi = x.astype(BF16)
    r1 = x - hi.astype(F32)
    mid = r1.astype(BF16)
    lo = (r1 - mid.astype(F32)).astype(BF16)
    return hi, mid, lo


PACK_COLS = 1024
PACK_STEPS = -(-IN_COLS_PACKED // PACK_COLS)


def _pack_kernel(wt_ref, out_ref):
    t = wt_ref[...].T
    lane = lax.broadcasted_iota(jnp.int32, t.shape, 1)
    is_dt = pl.program_id(0) == PACK_STEPS - 1
    t = jnp.where(jnp.logical_and(is_dt, lane >= 2 * SSD_HEADS), 0.0, t)
    out_ref[...] = t.astype(BF16)


def _scale_rows_kernel(w_ref, s_ref, out_ref):
    out_ref[...] = (w_ref[...] * s_ref[...]).astype(BF16)


def _mod_kernel(c_ref, cctx_ref, w_ref, b_ref, mod_s_ref, mod_p_ref):
    rows = c_ref.shape[0]
    ctx = jnp.broadcast_to(cctx_ref[...], (SUBLANES, D_MODEL))
    cond = jnp.concatenate([c_ref[...], ctx], axis=0)
    mod = _dot(_silu(cond).astype(BF16), w_ref[...].astype(BF16)) + b_ref[...]
    for r in range(rows):
        mod_s_ref[r] = mod[r:r + 1]
    mod_p_ref[0] = mod[rows:rows + 1]


FOLD_ROWS = 512
FOLD_STEPS = MIX_WIDTH // FOLD_ROWS
MOD_COLS = 1024
MOD_STEPS = 3 * D_MODEL // MOD_COLS


def _prep_kernel(wt_ref, wout_ref, nw_ref, c_ref, cctx_ref, wmod_ref, bmod_ref,
                 packed_ref, wout_bf_ref, mod_s_ref, mod_p_ref):
    i = pl.program_id(0)
    _pack_kernel(wt_ref, packed_ref)

    @pl.when(i < FOLD_STEPS)
    def _fold():
        _scale_rows_kernel(wout_ref, nw_ref, wout_bf_ref)

    @pl.when(i < MOD_STEPS)
    def _mod():
        _mod_kernel(c_ref, cctx_ref, wmod_ref, bmod_ref, mod_s_ref, mod_p_ref)


def _prep_call(w_in_t, w_out, mix_norm_w, c, c_ctx, w_mod, b_mod):
    assert PACK_STEPS >= max(FOLD_STEPS, MOD_STEPS)
    rows = c.shape[0]
    n_head = (OFF_Q - OFF_Z) // PACK_COLS
    src_dt = SSD_WIDTH + CONV_CH
    src_tail = src_dt + 2 * SSD_HEADS

    def src_row(i):
        t8 = jnp.where(i < n_head, i * (PACK_COLS // 8),
                       jnp.where(i < PACK_STEPS - 1, src_tail // 8 + (i - n_head) * (PACK_COLS // 8),
                                 src_dt // 8))
        return t8 * 8

    fold_blk = lambda i: (jnp.minimum(i, FOLD_STEPS - 1), 0)
    mod_blk = lambda i: (0, jnp.minimum(i, MOD_STEPS - 1))
    mod_blk3 = lambda i: (0, 0, jnp.minimum(i, MOD_STEPS - 1))
    return pl.pallas_call(
        _prep_kernel,
        grid=(PACK_STEPS,),
        in_specs=[
            pl.BlockSpec((pl.Element(PACK_COLS), pl.Element(D_MODEL)), lambda i: (src_row(i), 0)),
            pl.BlockSpec((FOLD_ROWS, D_MODEL), fold_blk),
            pl.BlockSpec((FOLD_ROWS, 1), fold_blk),
            pl.BlockSpec((rows, D_MODEL), lambda i: (0, 0)),
            pl.BlockSpec((1, D_MODEL), lambda i: (0, 0)),
            pl.BlockSpec((D_MODEL, MOD_COLS), mod_blk),
            pl.BlockSpec((1, MOD_COLS), mod_blk),
        ],
        out_specs=[
            pl.BlockSpec((D_MODEL, PACK_COLS), lambda i: (0, i)),
            pl.BlockSpec((FOLD_ROWS, D_MODEL), fold_blk),
            pl.BlockSpec((rows, 1, MOD_COLS), mod_blk3),
            pl.BlockSpec((1, 1, MOD_COLS), mod_blk3),
        ],
        out_shape=[
            jax.ShapeDtypeStruct((D_MODEL, IN_COLS_PACKED), BF16),
            jax.ShapeDtypeStruct((MIX_WIDTH, D_MODEL), BF16),
            jax.ShapeDtypeStruct((rows, 1, 3 * D_MODEL), F32),
            jax.ShapeDtypeStruct((1, 1, 3 * D_MODEL), F32),
        ],
        compiler_params=pltpu.CompilerParams(
            dimension_semantics=("arbitrary",), vmem_limit_bytes=VMEM_LIMIT),
        name="prep_params",
    )(w_in_t, w_out, mix_norm_w, c, c_ctx, w_mod, b_mod)


def _modnorm(x, mod_ref, npw_ref):
    ms = jnp.mean(x * x, axis=-1, keepdims=True)
    shift = mod_ref[0, :, 0:D_MODEL]
    scale = mod_ref[0, :, D_MODEL:2 * D_MODEL]
    gain = npw_ref[...] * (1.0 + scale)
    return (x * lax.rsqrt(ms + EPS) * gain + shift).astype(BF16)


def _in_proj_kernel(*refs, seq_len, rope):
    if rope:
        (x_ref, mod_ref, npw_ref, w_ref, cw_ref, cb_ref, dtb_ref,
         cos_ref, sin_ref, zs_ref, xbc_ref, dt_ref, q_ref, k_ref, v_ref, gs_ref, h_ref) = refs
    else:
        (x_ref, mod_ref, npw_ref, w_ref, cw_ref, cb_ref, dtb_ref,
         zs_ref, xbc_ref, dt_ref, q_ref, k_ref, v_ref, gs_ref, h_ref) = refs
    rows = seq_len
    h_ref[...] = _modnorm(x_ref[0], mod_ref, npw_ref)
    piece = IN_PROJ_PIECE

    def put(ref, cols, val):
        ref[0, :, cols] = val

    row = lax.broadcasted_iota(jnp.int32, (rows, piece), 0)
    seq_first = row == 0
    seq_last = row == rows - 1

    def sec_xbc(c0):
        cols = slice(c0, c0 + piece)
        acc = _dot(h_ref[...], w_ref[:, OFF_XBC + c0:OFF_XBC + c0 + piece])
        up = jnp.where(seq_first, 0.0, pltpu.roll(acc, 1, axis=0))
        dn = jnp.where(seq_last, 0.0, pltpu.roll(acc, rows - 1, axis=0))
        y = cw_ref[0:1, cols] * up + cw_ref[1:2, cols] * acc + cw_ref[2:3, cols] * dn + cb_ref[:, cols]
        put(xbc_ref, cols, _silu(y).astype(BF16))

    lane = lax.broadcasted_iota(jnp.int32, (rows, LANES), 1)
    first_half = (lane % 32) < 16

    def sec_qk(off, ref, scl, c0):
        acc = _dot(h_ref[...], w_ref[:, off + c0:off + c0 + piece]) * scl
        if rope:
            for s0 in range(0, piece, LANES):
                xs = acc[:, s0:s0 + LANES]
                partner = jnp.where(first_half, pltpu.roll(xs, LANES - 16, axis=1),
                                    pltpu.roll(xs, 16, axis=1))
                put(ref, slice(c0 + s0, c0 + s0 + LANES),
                    (xs * cos_ref[...] + partner * sin_ref[...]).astype(BF16))
        else:
            put(ref, slice(c0, c0 + piece), acc.astype(BF16))

    def sec_q(c0):
        sec_qk(OFF_Q, q_ref, 1.0, c0)

    def sec_k(c0):
        sec_qk(OFF_K, k_ref, RET_QK_DIM ** -0.5, c0)

    def sec_z(c0):
        acc = _dot(h_ref[...], w_ref[:, OFF_Z + c0:OFF_Z + c0 + piece])
        put(zs_ref, slice(c0, c0 + piece), _silu(acc).astype(BF16))

    def sec_g(c0):
        acc = _dot(h_ref[...], w_ref[:, OFF_G + c0:OFF_G + c0 + piece])
        put(gs_ref, slice(c0, c0 + piece), _silu(acc).astype(BF16))

    def sec_v(c0):
        acc = _dot(h_ref[...], w_ref[:, OFF_V + c0:OFF_V + c0 + piece])
        put(v_ref, slice(c0, c0 + piece), acc.astype(BF16))

    def sec_dt(c0):
        acc = _dot(h_ref[...], w_ref[:, OFF_DT:OFF_DT + DT_PAD]) + dtb_ref[...]
        put(dt_ref, slice(0, DT_PAD), jnp.maximum(acc, 0.0) + jnp.log1p(jnp.exp(-jnp.abs(acc))))

    def pieces(fn, width):
        return [(fn, c0) for c0 in range(0, width, piece)]

    conv = pieces(sec_xbc, CONV_CH)
    light = pieces(sec_v, RET_V_WIDTH) + [(sec_dt, 0)]
    order = pieces(sec_q, RET_QK_WIDTH) + pieces(sec_k, RET_QK_WIDTH)
    stride = -(-len(conv) // len(light))
    for i, item in enumerate(conv):
        order.append(item)
        if i % stride == stride - 1 and light:
            order.append(light.pop(0))
    order += light + pieces(sec_z, SSD_WIDTH) + pieces(sec_g, RET_V_WIDTH)
    for fn, c0 in order:
        fn(c0)


def _in_proj_call(x, mod, npw, w_packed, conv_w, conv_b, dtb, cos_t, sin_t):
    nb, L, _ = x.shape
    rope = cos_t is not None
    per_seq_mod = mod.shape[0] > 1
    assert L <= IN_PROJ_ROWS
    mod_map = (lambda s: (s, 0, 0)) if per_seq_mod else (lambda s: (0, 0, 0))
    const2 = lambda s: (0, 0)
    in_specs = [
        pl.BlockSpec((1, L, D_MODEL), lambda s: (s, 0, 0)),
        pl.BlockSpec((1, 1, 3 * D_MODEL), mod_map),
        pl.BlockSpec((1, D_MODEL), const2),
        pl.BlockSpec((D_MODEL, IN_COLS_PACKED), const2, pipeline_mode=pl.Buffered(1)),
        pl.BlockSpec((3, CONV_CH), const2),
        pl.BlockSpec((1, CONV_CH), const2),
        pl.BlockSpec((1, DT_PAD), const2),
    ]
    args = [x, mod, npw, w_packed, conv_w, conv_b, dtb]
    if rope:
        in_specs += [pl.BlockSpec((L, LANES), const2), pl.BlockSpec((L, LANES), const2)]
        args += [cos_t, sin_t]
    widths = (SSD_WIDTH, CONV_CH, DT_PAD, RET_QK_WIDTH, RET_QK_WIDTH, RET_V_WIDTH, RET_V_WIDTH)
    dtypes = (BF16, BF16, F32, BF16, BF16, BF16, BF16)
    out_specs = [pl.BlockSpec((1, L, w), lambda s: (s, 0, 0)) for w in widths]
    out_shape = [jax.ShapeDtypeStruct((nb, L, w), d) for w, d in zip(widths, dtypes)]
    return pl.pallas_call(
        functools.partial(_in_proj_kernel, seq_len=L, rope=rope),
        grid=(nb,),
        in_specs=in_specs,
        out_specs=out_specs,
        out_shape=out_shape,
        scratch_shapes=[pltpu.VMEM((L, D_MODEL), BF16)],
        compiler_params=pltpu.CompilerParams(
            dimension_semantics=("parallel",), vmem_limit_bytes=VMEM_LIMIT),
        name="in_proj_rope" if rope else "in_proj",
    )(*args)


def _mixer_kernel(*refs, n_streams, n_groups, nc, has_s0, emit_state, per_seq_mod):
    Q = CHUNK
    H = SSD_HEADS
    U = n_streams
    it = iter(refs)
    zs_ref, xbc_ref, dt_ref, q_ref, k_ref, v_ref, gs_ref = (next(it) for _ in range(7))
    nega_ref, dsk_ref, lam_ref = (next(it) for _ in range(3))
    wout_ref, xres_ref, modl_ref, npost_ref = (next(it) for _ in range(4))
    if has_s0:
        s0s_ref, s0r_ref = next(it), next(it)
    y_ref = next(it)
    if emit_state:
        os_ref, or_ref = next(it), next(it)
    (yas_ref, yar_ref, ss_ref, sr_ref, dsum_ref, ef_ref, eb_ref, tf_ref, tb_ref,
     ar_ref, exf_ref, exb_ref, cum_ref, mixs_ref) = (next(it) for _ in range(14))

    i = pl.program_id(0)
    n_real = n_groups * 2 * nc
    is_real = i < n_real
    ic = jnp.minimum(i, n_real - 1)
    s = ic % (2 * nc)
    phase2 = s >= nc
    c = jnp.where(phase2, 2 * nc - 1 - s, s)
    r0 = pl.multiple_of(c * Q, Q)
    rowq = lax.broadcasted_iota(jnp.int32, (Q, Q), 0)
    colq = lax.broadcasted_iota(jnp.int32, (Q, Q), 1)

    def out_proj_matmul(slot):
        return _dot(mixs_ref[slot], wout_ref[...])

    def out_proj_finish(out):
        ms = jnp.mean(out * out, axis=-1, keepdims=True)
        o = out * lax.rsqrt(ms + EPS) * npost_ref[...]
        for u in range(U):
            gate = modl_ref[u if per_seq_mod else 0, :, 2 * D_MODEL:3 * D_MODEL]
            y_ref[u] = xres_ref[u] + gate * o[u * Q:(u + 1) * Q]

    @pl.when(i == 0)
    def _init_tables():
        mixs_ref[...] = jnp.zeros_like(mixs_ref)
        diff = (rowq - colq).astype(F32)
        rowf = lax.broadcasted_iota(jnp.int32, (Q, LANES), 0).astype(F32)
        colf = lax.broadcasted_iota(jnp.int32, (RET_QK_DIM, Q), 1).astype(F32)
        for h in range(RET_HEADS):
            pr, hh = divmod(h, 2)
            lf = lam_ref[h:h + 1, :]
            lb = lam_ref[RET_HEADS + h:RET_HEADS + h + 1, :]
            e = jnp.where(rowq >= colq, lf * diff, lb * (-diff))
            dsum_ref[pr, :, hh * Q:(hh + 1) * Q] = jnp.exp(e) * jnp.where(rowq == colq, 2.0, 1.0)
            hc = slice(hh * LANES, (hh + 1) * LANES)
            ef_ref[pr, :, hc] = jnp.exp(lf * (rowf + 1.0))
            eb_ref[pr, :, hc] = jnp.exp(lb * (Q - rowf))
            kr = slice(hh * RET_QK_DIM, (hh + 1) * RET_QK_DIM)
            tf_ref[pr, kr, :] = jnp.exp(lf * (Q - 1.0 - colf))
            tb_ref[pr, kr, :] = jnp.exp(lb * colf)
        ar_ref[...] = jnp.exp(lam_ref[...] * float(Q))
        er = lax.broadcasted_iota(jnp.int32, (LANES, SSD_WIDTH), 0)
        ec = lax.broadcasted_iota(jnp.int32, (LANES, SSD_WIDTH), 1) // SSD_HEAD_DIM
        exf_ref[...] = jnp.where(er == ec, 1.0, 0.0).astype(BF16)
        exb_ref[...] = jnp.where(er == ec + H, 1.0, 0.0).astype(BF16)

    def expand(w, a_row, ex_ref):
        r = BF16_ROWS
        a3 = jnp.concatenate(_split3(jnp.broadcast_to(a_row, (r, LANES))), axis=0)
        out = _dot(jnp.concatenate([w.astype(BF16), a3], axis=0), ex_ref[...])
        n = w.shape[0]
        return out[0:n], out[n:n + 1] + out[n + r:n + r + 1] + out[n + 2 * r:n + 2 * r + 1]

    def make_stream(u):
        zs, xbc, dtr, qr, kr_, vr, gsr = (r.at[u] for r in (zs_ref, xbc_ref, dt_ref, q_ref, k_ref,
                                                            v_ref, gs_ref))
        yas, yar, ss, sr, cums = (r.at[u] for r in (yas_ref, yar_ref, ss_ref, sr_ref, cum_ref))
        mix_rows = slice(u * Q, (u + 1) * Q)

        def init_state():
            if has_s0:
                for d in range(2):
                    for p in range(SSD_HEADS // 2):
                        pair_t = jnp.concatenate(
                            [s0s_ref[u, 0, d, 2 * p], s0s_ref[u, 0, d, 2 * p + 1]], axis=0)
                        ss[d, :, p * LANES:(p + 1) * LANES] = pair_t.T
                sr[...] = s0r_ref[u, 0]
            else:
                ss[...] = jnp.zeros_like(ss)
                sr[...] = jnp.zeros_like(sr)

        def ssd_state_update(d, wtail_x, a_x):
            for g in range(SSD_GROUPS):
                bgt = xbc[:, XBC_B + g * SSD_STATE:XBC_B + (g + 1) * SSD_STATE].T
                gc = slice(g * GROUP_W, (g + 1) * GROUP_W)
                xt = xbc[:, gc] * wtail_x[:, gc].astype(BF16)
                ss[d, :, gc] = ss[d, :, gc] * a_x[:, gc] + _dot(bgt, xt)

        def k_transposed(pr):
            return kr_[:, pr * LANES:(pr + 1) * LANES].T

        def ret_state_update(d, tail_ref, kts):
            for pr, kt in enumerate(kts):
                pc = slice(pr * PAIR_W, (pr + 1) * PAIR_W)
                ktt = (kt.astype(F32) * tail_ref[pr]).astype(BF16)
                ds = _dot(ktt, vr[:, pc])
                for hh in range(2):
                    h = 2 * pr + hh
                    a = ar_ref[d * RET_HEADS + h:d * RET_HEADS + h + 1, :]
                    sr[d, h] = (sr[d, h] * a
                                + ds[hh * RET_QK_DIM:(hh + 1) * RET_QK_DIM, hh * LANES:(hh + 1) * LANES])

        def emit_states(d):
            for p in range(SSD_HEADS // 2):
                pair_t = ss[d, :, p * LANES:(p + 1) * LANES].T
                os_ref[u, 0, d, 2 * p] = pair_t[0:SSD_HEAD_DIM]
                os_ref[u, 0, d, 2 * p + 1] = pair_t[SSD_HEAD_DIM:2 * SSD_HEAD_DIM]
            or_ref[u, 0, d] = sr[d]

        def ret_state_blockdiag(d, pr):
            z = jnp.zeros((RET_QK_DIM, RET_V_DIM), F32)
            top = jnp.concatenate([sr[d, 2 * pr], z], axis=1)
            bot = jnp.concatenate([z, sr[d, 2 * pr + 1]], axis=1)
            return jnp.concatenate([top, bot], axis=0).astype(BF16)

        def phase1():
            dt = dtr[...]
            la = dt * nega_ref[...]
            tl_bf = jnp.where(rowq >= colq, 1.0, 0.0).astype(BF16)
            c3 = _dot(tl_bf, jnp.concatenate(_split3(la), axis=1))
            cum = c3[:, 0:LANES] + c3[:, LANES:2 * LANES] + c3[:, 2 * LANES:3 * LANES]
            cums[pl.ds(r0, Q), :] = cum
            tot = cum[Q - 1:Q, :]
            rev = tot - cum + la
            yield

            sub_k = lax.broadcasted_iota(jnp.int32, (LANES, Q), 0) < RET_QK_DIM
            lane_v = lax.broadcasted_iota(jnp.int32, (Q, PAIR_W), 1) < RET_V_DIM
            kts = [k_transposed(pr) for pr in range(RET_HEADS // 2)]
            for pr, kt in enumerate(kts):
                qp = qr[:, pr * LANES:(pr + 1) * LANES]
                zk = jnp.zeros_like(kt)
                kbd = jnp.concatenate([jnp.where(sub_k, kt, zk), jnp.where(sub_k, zk, kt)], axis=1)
                a2 = _dot(qp, kbd)
                ad = (a2 * dsum_ref[pr]).astype(BF16)
                pc = slice(pr * PAIR_W, (pr + 1) * PAIR_W)
                vp = vr[:, pc]
                zv = jnp.zeros_like(vp)
                vbd = jnp.concatenate([jnp.where(lane_v, vp, zv), jnp.where(lane_v, zv, vp)], axis=0)
                y = _dot(ad, vbd) + _dot(qp, ret_state_blockdiag(0, pr)) * ef_ref[pr]
                yar[pl.ds(r0, Q), pc] = y
            ret_state_update(0, tf_ref, kts)
            yield

            gms, css = [], []
            for g in range(SSD_GROUPS):
                cg = xbc[:, XBC_C + g * SSD_STATE:XBC_C + (g + 1) * SSD_STATE]
                bg = xbc[:, XBC_B + g * SSD_STATE:XBC_B + (g + 1) * SSD_STATE]
                gms.append(_dot_nt(cg, bg))
                css.append(_dot(cg, ss[0, :, g * GROUP_W:(g + 1) * GROUP_W].astype(BF16)))
            yield PROJECT_POINT

            dt_t = dt.T
            ldt_t = jnp.log(dt_t)
            sub = lax.broadcasted_iota(jnp.int32, (LANES, Q), 0)
            adj_t = (jnp.where(sub < H, cum.T, rev.T) - ldt_t) * LOG2E
            dg_t = jnp.log(dt_t[0:H, :] + dt_t[H:2 * H, :]) * LOG2E
            cum2 = cum * LOG2E
            rev2 = rev * LOG2E
            wtail_f = dt * jnp.exp(tot - cum)
            ew_x, af_x = expand(jnp.concatenate([jnp.exp(cum), wtail_f], axis=0), jnp.exp(tot), exf_ref)
            ecum_x = ew_x[0:Q]
            wtail_x = ew_x[Q:2 * Q]
            lt = rowq > colq
            gt = rowq < colq
            lane_g = lax.broadcasted_iota(jnp.int32, (Q, GROUP_W), 1) // SSD_HEAD_DIM
            yield

            for g in range(SSD_GROUPS):
                gc = slice(g * GROUP_W, (g + 1) * GROUP_W)
                xg = xbc[:, gc]
                ws, xs = [], []
                for j in range(4):
                    h = 4 * g + j
                    hb = H + h
                    arg = jnp.where(lt, cum2[:, h:h + 1] - adj_t[h:h + 1, :],
                                    jnp.where(gt, rev2[:, hb:hb + 1] - adj_t[hb:hb + 1, :], dg_t[h:h + 1, :]))
                    ws.append((gms[g] * jnp.exp2(arg)).astype(BF16))
                    xs.append(jnp.where(lane_g == j, xg, jnp.zeros_like(xg)))
                y = _dot(jnp.concatenate(ws, axis=1), jnp.concatenate(xs, axis=0))
                yas[pl.ds(r0, Q), gc] = y + css[g] * ecum_x[:, gc]
            yield
            ssd_state_update(0, wtail_x, af_x)

        def phase2(slot_w):
            for pr in range(RET_HEADS // 2):
                qp = qr[:, pr * LANES:(pr + 1) * LANES]
                pc = slice(pr * PAIR_W, (pr + 1) * PAIR_W)
                y2 = yar[pl.ds(r0, Q), pc] + _dot(qp, ret_state_blockdiag(1, pr)) * eb_ref[pr]
                for hh in range(2):
                    h = 2 * pr + hh
                    hc = slice(h * LANES, (h + 1) * LANES)
                    y = y2[:, hh * LANES:(hh + 1) * LANES]
                    mu = jnp.mean(y, axis=-1, keepdims=True)
                    yc = y - mu
                    var = jnp.mean(yc * yc, axis=-1, keepdims=True)
                    yn = yc * lax.rsqrt(var + EPS) * gsr[:, hc].astype(F32)
                    mixs_ref[slot_w, mix_rows, SSD_WIDTH + h * LANES:SSD_WIDTH + (h + 1) * LANES] = (
                        yn.astype(BF16))
            ret_state_update(1, tb_ref, [k_transposed(pr) for pr in range(RET_HEADS // 2)])
            yield

            dt = dtr[...]
            cum = cums[pl.ds(r0, Q), :]
            rev = cum[Q - 1:Q, :] - cum + dt * nega_ref[...]
            first = rev[0:1, :]
            wtail_b = dt * jnp.exp(first - rev)
            ew_x, ab_x = expand(jnp.concatenate([jnp.exp(rev), wtail_b], axis=0), jnp.exp(first), exb_ref)
            erev_x = ew_x[0:Q]
            wtail_x = ew_x[Q:2 * Q]

            parts = []
            ssq = None
            for g in range(SSD_GROUPS):
                cg = xbc[:, XBC_C + g * SSD_STATE:XBC_C + (g + 1) * SSD_STATE]
                gc = slice(g * GROUP_W, (g + 1) * GROUP_W)
                cs = _dot(cg, ss[1, :, gc].astype(BF16))
                y = (yas[pl.ds(r0, Q), gc] + cs * erev_x[:, gc]
                     + dsk_ref[:, gc] * xbc[:, gc].astype(F32))
                y = y * zs[:, gc].astype(F32)
                parts.append(y)
                ssq = y * y if ssq is None else ssq + y * y
            yield PROJECT_POINT

            inv = lax.rsqrt(jnp.sum(ssq, axis=-1, keepdims=True) * (1.0 / SSD_WIDTH) + EPS)
            for g, y in enumerate(parts):
                gc = slice(g * GROUP_W, (g + 1) * GROUP_W)
                mixs_ref[slot_w, mix_rows, gc] = (y * inv).astype(BF16)
            ssd_state_update(1, wtail_x, ab_x)

        return types.SimpleNamespace(init_state=init_state, emit_states=emit_states, phase1=phase1,
                                     phase2=phase2)

    streams = [make_stream(u) for u in range(U)]

    def run_interleaved(stage_gens, proj_slot=None):
        proj = None
        live = list(stage_gens)
        while live:
            marks, nxt = [], []
            for gen in live:
                try:
                    marks.append(next(gen))
                    nxt.append(gen)
                except StopIteration:
                    pass
            if proj_slot is not None and PROJECT_POINT in marks:
                proj = out_proj_matmul(proj_slot)
            live = nxt
        return proj

    @pl.when(is_real & (s == 0))
    def _init_state():
        for st in streams:
            st.init_state()

    @pl.when(is_real & jnp.logical_not(phase2))
    def _forward_step():
        run_interleaved([st.phase1() for st in streams])

    for parity in range(2):
        @pl.when(is_real & phase2 & (c % 2 == parity))
        def _backward_step():
            proj = run_interleaved([st.phase2(parity) for st in streams], proj_slot=1 - parity)
            out_proj_finish(proj)

    @pl.when(jnp.logical_not(is_real))
    def _drain():
        out_proj_finish(out_proj_matmul(0))

    if emit_state:
        @pl.when(is_real & (s == nc - 1))
        def _emit_fwd():
            for st in streams:
                st.emit_states(0)

        @pl.when(is_real & (s == 2 * nc - 1))
        def _emit_bwd():
            for st in streams:
                st.emit_states(1)


def _mixer_call(zs, xbc, dt, q, k, v, gs, nega, dsk, lamx, w_out_bf, x, mod, npost, s0_ssd, s0_ret,
                emit_state):
    nb, L, _ = zs.shape
    nc = L // CHUNK
    Q = CHUNK
    U = STREAMS
    assert nb % U == 0
    assert nc % 2 == 0
    has_s0 = s0_ssd is not None
    n_groups = nb // U
    steps = 2 * nc
    n_real = n_groups * steps
    per_seq_mod = mod.shape[0] > 1

    def cur(i):
        ic = jnp.minimum(i, n_real - 1)
        return ic // steps, ic % steps

    def chunk_map(i):
        b, s = cur(i)
        return (b, jnp.where(s < nc, s, steps - 1 - s), 0)

    def phase2_map(i):
        b, s = cur(i)
        return (b, jnp.where(s < nc, nc - 1, steps - 1 - s), 0)

    def lag(i):
        b, s = i // steps, i % steps
        early = s <= nc
        return (jnp.where(early, jnp.maximum(b - 1, 0), b),
                jnp.where(early, jnp.where(b == 0, nc - 1, 0), steps - s))

    def lag_map(i):
        bl, cl = lag(i)
        return (bl, cl, 0)

    def lag_mod_map(i):
        bl, _ = lag(i)
        return (bl if per_seq_mod else 0, 0, 0)

    const2 = lambda i: (0, 0)
    in_specs = [
        pl.BlockSpec((U, Q, SSD_WIDTH), phase2_map),
        pl.BlockSpec((U, Q, CONV_CH), chunk_map),
        pl.BlockSpec((U, Q, DT_PAD), chunk_map),
        pl.BlockSpec((U, Q, RET_QK_WIDTH), chunk_map),
        pl.BlockSpec((U, Q, RET_QK_WIDTH), chunk_map),
        pl.BlockSpec((U, Q, RET_V_WIDTH), chunk_map),
        pl.BlockSpec((U, Q, RET_V_WIDTH), phase2_map),
        pl.BlockSpec((1, DT_PAD), const2),
        pl.BlockSpec((1, SSD_WIDTH), const2),
        pl.BlockSpec((2 * RET_HEADS, LANES), const2),
        pl.BlockSpec((MIX_WIDTH, D_MODEL), const2, pipeline_mode=pl.Buffered(1)),
        pl.BlockSpec((U, Q, D_MODEL), lag_map),
        pl.BlockSpec((U if per_seq_mod else 1, 1, 3 * D_MODEL), lag_mod_map),
        pl.BlockSpec((1, D_MODEL), const2),
    ]
    args = [zs, xbc, dt, q, k, v, gs, nega, dsk, lamx, w_out_bf, x, mod, npost]
    state_map = lambda i: (cur(i)[0], 0, 0, 0, 0, 0)
    ssd_state_block = (U, 1, 2, SSD_HEADS, SSD_HEAD_DIM, SSD_STATE)
    ret_state_block = (U, 1, 2, RET_HEADS, RET_QK_DIM, RET_V_DIM)
    if has_s0:
        in_specs += [pl.BlockSpec(ssd_state_block, state_map, pipeline_mode=pl.Buffered(1)),
                     pl.BlockSpec(ret_state_block, state_map, pipeline_mode=pl.Buffered(1))]
        args += [s0_ssd, s0_ret]
    out_specs = [pl.BlockSpec((U, Q, D_MODEL), lag_map)]
    out_shape = [jax.ShapeDtypeStruct((nb, L, D_MODEL), F32)]
    if emit_state:
        out_specs += [pl.BlockSpec(ssd_state_block, state_map), pl.BlockSpec(ret_state_block, state_map)]
        out_shape += [jax.ShapeDtypeStruct((nb,) + ssd_state_block[1:], F32),
                      jax.ShapeDtypeStruct((nb,) + ret_state_block[1:], F32)]
    n_pairs = RET_HEADS // 2
    scratch = [
        pltpu.VMEM((U, L, SSD_WIDTH), F32),
        pltpu.VMEM((U, L, RET_V_WIDTH), F32),
        pltpu.VMEM((U, 2, SSD_STATE, SSD_WIDTH), F32),
        pltpu.VMEM((U, 2, RET_HEADS, RET_QK_DIM, RET_V_DIM), F32),
        pltpu.VMEM((n_pairs, Q, 2 * Q), F32),
        pltpu.VMEM((n_pairs, Q, PAIR_W), F32),
        pltpu.VMEM((n_pairs, Q, PAIR_W), F32),
        pltpu.VMEM((n_pairs, LANES, Q), F32),
        pltpu.VMEM((n_pairs, LANES, Q), F32),
        pltpu.VMEM((2 * RET_HEADS, LANES), F32),
        pltpu.VMEM((LANES, SSD_WIDTH), BF16),
        pltpu.VMEM((LANES, SSD_WIDTH), BF16),
        pltpu.VMEM((U, L, DT_PAD), F32),
        pltpu.VMEM((2, U * Q, MIX_WIDTH), BF16),
    ]
    outs = pl.pallas_call(
        functools.partial(_mixer_kernel, n_streams=U, n_groups=n_groups, nc=nc, has_s0=has_s0,
                          emit_state=emit_state,
                          per_seq_mod=per_seq_mod),
        grid=(n_real + 1,),
        in_specs=in_specs,
        out_specs=out_specs,
        out_shape=out_shape,
        scratch_shapes=scratch,
        compiler_params=pltpu.CompilerParams(
            dimension_semantics=("arbitrary",), vmem_limit_bytes=VMEM_LIMIT),
        name="mixer_sample" if has_s0 else "mixer_prompt",
    )(*args)
    return outs


def _rope_tables(L):
    rows = L // GRID_W
    row = np.repeat(np.arange(rows, dtype=np.float64), GRID_W)
    col = np.tile(np.arange(GRID_W, dtype=np.float64), rows)
    half = RET_QK_DIM // 2
    inv = ROPE_BASE ** (-np.arange(0, half, 2, dtype=np.float64) / half)
    ang_r = row[:, None] * inv
    ang_c = col[:, None] * inv
    cos_h = np.concatenate([np.cos(ang_r), np.cos(ang_r), np.cos(ang_c), np.cos(ang_c)], axis=-1)
    sin_h = np.concatenate([-np.sin(ang_r), np.sin(ang_r), -np.sin(ang_c), np.sin(ang_c)], axis=-1)
    return (jnp.asarray(np.tile(cos_h, (1, 2)), dtype=F32), jnp.asarray(np.tile(sin_h, (1, 2)), dtype=F32))


def kernel(x_prompt, x_sample, state_ssd, state_ret, c, c_ctx, w_mod, b_mod, norm_pre_w, norm_post_w,
           w_in, conv_w, conv_b, ssd_A_log, ssd_dt_bias, ssd_D, ssd_norm_w, ret_decay, ret_norm_w, w_out):
    assert w_in.shape[0] == 1, "one layer (DEPTH == 1) is supported"
    l = 0

    n_dt = 2 * SSD_HEADS
    mix_norm_w = jnp.concatenate([ssd_norm_w[l], ret_norm_w[l]]).reshape(MIX_WIDTH, 1)
    w_packed, w_out_bf, mod_s, mod_p = _prep_call(
        jnp.swapaxes(w_in[l], 0, 1), w_out[l], mix_norm_w, c, c_ctx.reshape(1, D_MODEL), w_mod[l],
        b_mod[l].reshape(1, 3 * D_MODEL))
    dtb = jnp.pad(ssd_dt_bias[l].reshape(1, n_dt), ((0, 0), (0, DT_PAD - n_dt)))
    nega = jnp.pad(-jnp.exp(ssd_A_log[l].reshape(1, n_dt)), ((0, 0), (0, DT_PAD - n_dt)))
    dsk = jnp.repeat(ssd_D[l], SSD_HEAD_DIM).reshape(1, SSD_WIDTH)
    lamx = jnp.broadcast_to(-jnp.exp(ret_decay[l].reshape(2 * RET_HEADS, 1)), (2 * RET_HEADS, LANES))
    npre = norm_pre_w[l].reshape(1, D_MODEL)
    npost = norm_post_w[l].reshape(1, D_MODEL)
    cb = conv_b[l].reshape(1, CONV_CH)
    cos_t, sin_t = _rope_tables(x_sample.shape[1])

    pp = _in_proj_call(x_prompt, mod_p, npre, w_packed, conv_w[l], cb, dtb, None, None)
    y_p, st_ssd_t, st_ret = _mixer_call(*pp, nega, dsk, lamx, w_out_bf, x_prompt, mod_p, npost,
                                        None, None, True)

    ps = _in_proj_call(x_sample, mod_s, npre, w_packed, conv_w[l], cb, dtb, cos_t, sin_t)
    (y_s,) = _mixer_call(*ps, nega, dsk, lamx, w_out_bf, x_sample, mod_s, npost,
                         jnp.swapaxes(state_ssd, -1, -2), state_ret, False)
    return (y_p, y_s, jnp.swapaxes(st_ssd_t, -1, -2), st_ret)
```

```python
import functools
import types

import jax
import jax.numpy as jnp
import numpy as np
from jax import lax
from jax.experimental import pallas as pl
from jax.experimental.pallas import tpu as pltpu

F32 = jnp.float32
BF16 = jnp.bfloat16

D_MODEL = 1024
CHUNK = 128
GRID_W = 64
EPS = 1e-6
SSD_WIDTH = 1024
SSD_HEAD_DIM = 64
SSD_HEADS = 16
SSD_GROUPS = 4
SSD_STATE = 128
CONV_CH = SSD_WIDTH + 2 * SSD_GROUPS * SSD_STATE
RET_HEADS = 8
RET_QK_DIM = 64
RET_V_DIM = 128
RET_QK_WIDTH = RET_HEADS * RET_QK_DIM
RET_V_WIDTH = RET_HEADS * RET_V_DIM
MIX_WIDTH = SSD_WIDTH + RET_V_WIDTH
ROPE_BASE = 10000.0
LANES = 128
DT_PAD = LANES
GROUP_W = SSD_WIDTH // SSD_GROUPS
PAIR_W = 2 * RET_V_DIM
IN_PROJ_PIECE = 256
IN_PROJ_ROWS = 1024
STREAMS = 2
BF16_ROWS = 16
XBC_B = SSD_WIDTH
XBC_C = XBC_B + SSD_GROUPS * SSD_STATE
SUBLANES = 8
PROJECT_POINT = "project"

OFF_Z = 0
OFF_XBC = OFF_Z + SSD_WIDTH
OFF_Q = OFF_XBC + CONV_CH
OFF_K = OFF_Q + RET_QK_WIDTH
OFF_V = OFF_K + RET_QK_WIDTH
OFF_G = OFF_V + RET_V_WIDTH
OFF_DT = OFF_G + RET_V_WIDTH
IN_COLS_PACKED = OFF_DT + DT_PAD

LOG2E = 1.4426950408889634
VMEM_LIMIT = 56 * 1024 * 1024


def _silu(x):
    return x * (1.0 / (1.0 + jnp.exp(-x)))


def _dot(a, b):
    return jnp.dot(a, b, preferred_element_type=F32)


def _dot_nt(a, b):
    return lax.dot_general(a, b, (((1,), (1,)), ((), ())), preferred_element_type=F32)


def _split3(x):
    hi = x.astype(BF16)
    r1 = x - hi.astype(F32)
    mid = r1.astype(BF16)
    lo = (r1 - mid.astype(F32)).astype(BF16)
    return hi, mid, lo


PACK_COLS = 1024
PACK_STEPS = -(-IN_COLS_PACKED // PACK_COLS)


def _pack_kernel(wt_ref, out_ref):
    t = wt_ref[...].T
    lane = lax.broadcasted_iota(jnp.int32, t.shape, 1)
    is_dt = pl.program_id(0) == PACK_STEPS - 1
    t = jnp.where(jnp.logical_and(is_dt, lane >= 2 * SSD_HEADS), 0.0, t)
    out_ref[...] = t.astype(BF16)


def _scale_rows_kernel(w_ref, s_ref, out_ref):
    out_ref[...] = (w_ref[...] * s_ref[...]).astype(BF16)


def _mod_kernel(c_ref, cctx_ref, w_ref, b_ref, mod_s_ref, mod_p_ref):
    rows = c_ref.shape[0]
    ctx = jnp.broadcast_to(cctx_ref[...], (SUBLANES, D_MODEL))
    cond = jnp.concatenate([c_ref[...], ctx], axis=0)
    mod = _dot(_silu(cond).astype(BF16), w_ref[...].astype(BF16)) + b_ref[...]
    for r in range(rows):
        mod_s_ref[r] = mod[r:r + 1]
    mod_p_ref[0] = mod[rows:rows + 1]


FOLD_ROWS = 512
FOLD_STEPS = MIX_WIDTH // FOLD_ROWS
MOD_COLS = 1024
MOD_STEPS = 3 * D_MODEL // MOD_COLS


def _prep_kernel(wt_ref, wout_ref, nw_ref, c_ref, cctx_ref, wmod_ref, bmod_ref,
                 packed_ref, wout_bf_ref, mod_s_ref, mod_p_ref):
    i = pl.program_id(0)
    _pack_kernel(wt_ref, packed_ref)

    @pl.when(i < FOLD_STEPS)
    def _fold():
        _scale_rows_kernel(wout_ref, nw_ref, wout_bf_ref)

    @pl.when(i < MOD_STEPS)
    def _mod():
        _mod_kernel(c_ref, cctx_ref, wmod_ref, bmod_ref, mod_s_ref, mod_p_ref)


def _prep_call(w_in_t, w_out, mix_norm_w, c, c_ctx, w_mod, b_mod):
    assert PACK_STEPS >= max(FOLD_STEPS, MOD_STEPS)
    rows = c.shape[0]
    n_head = (OFF_Q - OFF_Z) // PACK_COLS
    src_dt = SSD_WIDTH + CONV_CH
    src_tail = src_dt + 2 * SSD_HEADS

    def src_row(i):
        t8 = jnp.where(i < n_head, i * (PACK_COLS // 8),
                       jnp.where(i < PACK_STEPS - 1, src_tail // 8 + (i - n_head) * (PACK_COLS // 8),
                                 src_dt // 8))
        return t8 * 8

    fold_blk = lambda i: (jnp.minimum(i, FOLD_STEPS - 1), 0)
    mod_blk = lambda i: (0, jnp.minimum(i, MOD_STEPS - 1))
    mod_blk3 = lambda i: (0, 0, jnp.minimum(i, MOD_STEPS - 1))
    return pl.pallas_call(
        _prep_kernel,
        grid=(PACK_STEPS,),
        in_specs=[
            pl.BlockSpec((pl.Element(PACK_COLS), pl.Element(D_MODEL)), lambda i: (src_row(i), 0)),
            pl.BlockSpec((FOLD_ROWS, D_MODEL), fold_blk),
            pl.BlockSpec((FOLD_ROWS, 1), fold_blk),
            pl.BlockSpec((rows, D_MODEL), lambda i: (0, 0)),
            pl.BlockSpec((1, D_MODEL), lambda i: (0, 0)),
            pl.BlockSpec((D_MODEL, MOD_COLS), mod_blk),
            pl.BlockSpec((1, MOD_COLS), mod_blk),
        ],
        out_specs=[
            pl.BlockSpec((D_MODEL, PACK_COLS), lambda i: (0, i)),
            pl.BlockSpec((FOLD_ROWS, D_MODEL), fold_blk),
            pl.BlockSpec((rows, 1, MOD_COLS), mod_blk3),
            pl.BlockSpec((1, 1, MOD_COLS), mod_blk3),
        ],
        out_shape=[
            jax.ShapeDtypeStruct((D_MODEL, IN_COLS_PACKED), BF16),
            jax.ShapeDtypeStruct((MIX_WIDTH, D_MODEL), BF16),
            jax.ShapeDtypeStruct((rows, 1, 3 * D_MODEL), F32),
            jax.ShapeDtypeStruct((1, 1, 3 * D_MODEL), F32),
        ],
        compiler_params=pltpu.CompilerParams(
            dimension_semantics=("arbitrary",), vmem_limit_bytes=VMEM_LIMIT),
        name="prep_params",
    )(w_in_t, w_out, mix_norm_w, c, c_ctx, w_mod, b_mod)


def _modnorm(x, mod_ref, npw_ref):
    ms = jnp.mean(x * x, axis=-1, keepdims=True)
    shift = mod_ref[0, :, 0:D_MODEL]
    scale = mod_ref[0, :, D_MODEL:2 * D_MODEL]
    gain = npw_ref[...] * (1.0 + scale)
    return (x * lax.rsqrt(ms + EPS) * gain + shift).astype(BF16)


def _in_proj_kernel(*refs, seq_len, rope):
    if rope:
        (x_ref, mod_ref, npw_ref, w_ref, cw_ref, cb_ref, dtb_ref,
         cos_ref, sin_ref, zs_ref, xbc_ref, dt_ref, q_ref, k_ref, v_ref, gs_ref, h_ref) = refs
    else:
        (x_ref, mod_ref, npw_ref, w_ref, cw_ref, cb_ref, dtb_ref,
         zs_ref, xbc_ref, dt_ref, q_ref, k_ref, v_ref, gs_ref, h_ref) = refs
    rows = seq_len
    h_ref[...] = _modnorm(x_ref[0], mod_ref, npw_ref)
    piece = IN_PROJ_PIECE

    def put(ref, cols, val):
        ref[0, :, cols] = val

    row = lax.broadcasted_iota(jnp.int32, (rows, piece), 0)
    seq_first = row == 0
    seq_last = row == rows - 1

    def sec_xbc(c0):
        cols = slice(c0, c0 + piece)
        acc = _dot(h_ref[...], w_ref[:, OFF_XBC + c0:OFF_XBC + c0 + piece])
        up = jnp.where(seq_first, 0.0, pltpu.roll(acc, 1, axis=0))
        dn = jnp.where(seq_last, 0.0, pltpu.roll(acc, rows - 1, axis=0))
        y = cw_ref[0:1, cols] * up + cw_ref[1:2, cols] * acc + cw_ref[2:3, cols] * dn + cb_ref[:, cols]
        put(xbc_ref, cols, _silu(y).astype(BF16))

    lane = lax.broadcasted_iota(jnp.int32, (rows, LANES), 1)
    first_half = (lane % 32) < 16

    def sec_qk(off, ref, scl, c0):
        acc = _dot(h_ref[...], w_ref[:, off + c0:off + c0 + piece]) * scl
        if rope:
            for s0 in range(0, piece, LANES):
                xs = acc[:, s0:s0 + LANES]
                partner = jnp.where(first_half, pltpu.roll(xs, LANES - 16, axis=1),
                                    pltpu.roll(xs, 16, axis=1))
                put(ref, slice(c0 + s0, c0 + s0 + LANES),
                    (xs * cos_ref[...] + partner * sin_ref[...]).astype(BF16))
        else:
            put(ref, slice(c0, c0 + piece), acc.astype(BF16))

    def sec_q(c0):
        sec_qk(OFF_Q, q_ref, 1.0, c0)

    def sec_k(c0):
        sec_qk(OFF_K, k_ref, RET_QK_DIM ** -0.5, c0)

    def sec_z(c0):
        acc = _dot(h_ref[...], w_ref[:, OFF_Z + c0:OFF_Z + c0 + piece])
        put(zs_ref, slice(c0, c0 + piece), _silu(acc).astype(BF16))

    def sec_g(c0):
        acc = _dot(h_ref[...], w_ref[:, OFF_G + c0:OFF_G + c0 + piece])
        put(gs_ref, slice(c0, c0 + piece), _silu(acc).astype(BF16))

    def sec_v(c0):
        acc = _dot(h_ref[...], w_ref[:, OFF_V + c0:OFF_V + c0 + piece])
        put(v_ref, slice(c0, c0 + piece), acc.astype(BF16))

    def sec_dt(c0):
        acc = _dot(h_ref[...], w_ref[:, OFF_DT:OFF_DT + DT_PAD]) + dtb_ref[...]
        put(dt_ref, slice(0, DT_PAD), jnp.maximum(acc, 0.0) + jnp.log1p(jnp.exp(-jnp.abs(acc))))

    def pieces(fn, width):
        return [(fn, c0) for c0 in range(0, width, piece)]

    conv = pieces(sec_xbc, CONV_CH)
    light = pieces(sec_v, RET_V_WIDTH) + [(sec_dt, 0)]
    order = pieces(sec_q, RET_QK_WIDTH) + pieces(sec_k, RET_QK_WIDTH)
    stride = -(-len(conv) // len(light))
    for i, item in enumerate(conv):
        order.append(item)
        if i % stride == stride - 1 and light:
            order.append(light.pop(0))
    order += light + pieces(sec_z, SSD_WIDTH) + pieces(sec_g, RET_V_WIDTH)
    for fn, c0 in order:
        fn(c0)


def _in_proj_call(x, mod, npw, w_packed, conv_w, conv_b, dtb, cos_t, sin_t):
    nb, L, _ = x.shape
    rope = cos_t is not None
    per_seq_mod = mod.shape[0] > 1
    assert L <= IN_PROJ_ROWS
    mod_map = (lambda s: (s, 0, 0)) if per_seq_mod else (lambda s: (0, 0, 0))
    const2 = lambda s: (0, 0)
    in_specs = [
        pl.BlockSpec((1, L, D_MODEL), lambda s: (s, 0, 0)),
        pl.BlockSpec((1, 1, 3 * D_MODEL), mod_map),
        pl.BlockSpec((1, D_MODEL), const2),
        pl.BlockSpec((D_MODEL, IN_COLS_PACKED), const2, pipeline_mode=pl.Buffered(1)),
        pl.BlockSpec((3, CONV_CH), const2),
        pl.BlockSpec((1, CONV_CH), const2),
        pl.BlockSpec((1, DT_PAD), const2),
    ]
    args = [x, mod, npw, w_packed, conv_w, conv_b, dtb]
    if rope:
        in_specs += [pl.BlockSpec((L, LANES), const2), pl.BlockSpec((L, LANES), const2)]
        args += [cos_t, sin_t]
    widths = (SSD_WIDTH, CONV_CH, DT_PAD, RET_QK_WIDTH, RET_QK_WIDTH, RET_V_WIDTH, RET_V_WIDTH)
    dtypes = (BF16, BF16, F32, BF16, BF16, BF16, BF16)
    out_specs = [pl.BlockSpec((1, L, w), lambda s: (s, 0, 0)) for w in widths]
    out_shape = [jax.ShapeDtypeStruct((nb, L, w), d) for w, d in zip(widths, dtypes)]
    return pl.pallas_call(
        functools.partial(_in_proj_kernel, seq_len=L, rope=rope),
        grid=(nb,),
        in_specs=in_specs,
        out_specs=out_specs,
        out_shape=out_shape,
        scratch_shapes=[pltpu.VMEM((L, D_MODEL), BF16)],
        compiler_params=pltpu.CompilerParams(
            dimension_semantics=("parallel",), vmem_limit_bytes=VMEM_LIMIT),
        name="in_proj_rope" if rope else "in_proj",
    )(*args)


def _mixer_kernel(*refs, n_streams, n_groups, nc, has_s0, emit_state, per_seq_mod):
    Q = CHUNK
    H = SSD_HEADS
    U = n_streams
    it = iter(refs)
    zs_ref, xbc_ref, dt_ref, q_ref, k_ref, v_ref, gs_ref = (next(it) for _ in range(7))
    nega_ref, dsk_ref, lam_ref = (next(it) for _ in range(3))
    wout_ref, xres_ref, modl_ref, npost_ref = (next(it) for _ in range(4))
    if has_s0:
        s0s_ref, s0r_ref = next(it), next(it)
    y_ref = next(it)
    if emit_state:
        os_ref, or_ref = next(it), next(it)
    (yas_ref, yar_ref, ss_ref, sr_ref, dsum_ref, ef_ref, eb_ref, tf_ref, tb_ref,
     ar_ref, exf_ref, exb_ref, cum_ref, mixs_ref) = (next(it) for _ in range(14))

    i = pl.program_id(0)
    n_real = n_groups * 2 * nc
    is_real = i < n_real
    ic = jnp.minimum(i, n_real - 1)
    s = ic % (2 * nc)
    phase2 = s >= nc
    c = jnp.where(phase2, 2 * nc - 1 - s, s)
    r0 = pl.multiple_of(c * Q, Q)
    rowq = lax.broadcasted_iota(jnp.int32, (Q, Q), 0)
    colq = lax.broadcasted_iota(jnp.int32, (Q, Q), 1)

    def out_proj_matmul(slot):
        return _dot(mixs_ref[slot], wout_ref[...])

    def out_proj_finish(out):
        ms = jnp.mean(out * out, axis=-1, keepdims=True)
        o = out * lax.rsqrt(ms + EPS) * npost_ref[...]
        for u in range(U):
            gate = modl_ref[u if per_seq_mod else 0, :, 2 * D_MODEL:3 * D_MODEL]
            y_ref[u] = xres_ref[u] + gate * o[u * Q:(u + 1) * Q]

    @pl.when(i == 0)
    def _init_tables():
        mixs_ref[...] = jnp.zeros_like(mixs_ref)
        diff = (rowq - colq).astype(F32)
        rowf = lax.broadcasted_iota(jnp.int32, (Q, LANES), 0).astype(F32)
        colf = lax.broadcasted_iota(jnp.int32, (RET_QK_DIM, Q), 1).astype(F32)
        for h in range(RET_HEADS):
            pr, hh = divmod(h, 2)
            lf = lam_ref[h:h + 1, :]
            lb = lam_ref[RET_HEADS + h:RET_HEADS + h + 1, :]
            e = jnp.where(rowq >= colq, lf * diff, lb * (-diff))
            dsum_ref[pr, :, hh * Q:(hh + 1) * Q] = jnp.exp(e) * jnp.where(rowq == colq, 2.0, 1.0)
            hc = slice(hh * LANES, (hh + 1) * LANES)
            ef_ref[pr, :, hc] = jnp.exp(lf * (rowf + 1.0))
            eb_ref[pr, :, hc] = jnp.exp(lb * (Q - rowf))
            kr = slice(hh * RET_QK_DIM, (hh + 1) * RET_QK_DIM)
            tf_ref[pr, kr, :] = jnp.exp(lf * (Q - 1.0 - colf))
            tb_ref[pr, kr, :] = jnp.exp(lb * colf)
        ar_ref[...] = jnp.exp(lam_ref[...] * float(Q))
        er = lax.broadcasted_iota(jnp.int32, (LANES, SSD_WIDTH), 0)
        ec = lax.broadcasted_iota(jnp.int32, (LANES, SSD_WIDTH), 1) // SSD_HEAD_DIM
        exf_ref[...] = jnp.where(er == ec, 1.0, 0.0).astype(BF16)
        exb_ref[...] = jnp.where(er == ec + H, 1.0, 0.0).astype(BF16)

    def expand(w, a_row, ex_ref):
        r = BF16_ROWS
        a3 = jnp.concatenate(_split3(jnp.broadcast_to(a_row, (r, LANES))), axis=0)
        out = _dot(jnp.concatenate([w.astype(BF16), a3], axis=0), ex_ref[...])
        n = w.shape[0]
        return out[0:n], out[n:n + 1] + out[n + r:n + r + 1] + out[n + 2 * r:n + 2 * r + 1]

    def make_stream(u):
        zs, xbc, dtr, qr, kr_, vr, gsr = (r.at[u] for r in (zs_ref, xbc_ref, dt_ref, q_ref, k_ref,
                                                            v_ref, gs_ref))
        yas, yar, ss, sr, cums = (r.at[u] for r in (yas_ref, yar_ref, ss_ref, sr_ref, cum_ref))
        mix_rows = slice(u * Q, (u + 1) * Q)

        def init_state():
            if has_s0:
                for d in range(2):
                    for p in range(SSD_HEADS // 2):
                        pair_t = jnp.concatenate(
                            [s0s_ref[u, 0, d, 2 * p], s0s_ref[u, 0, d, 2 * p + 1]], axis=0)
                        ss[d, :, p * LANES:(p + 1) * LANES] = pair_t.T
                sr[...] = s0r_ref[u, 0]
            else:
                ss[...] = jnp.zeros_like(ss)
                sr[...] = jnp.zeros_like(sr)

        def ssd_state_update(d, wtail_x, a_x):
            for g in range(SSD_GROUPS):
                bgt = xbc[:, XBC_B + g * SSD_STATE:XBC_B + (g + 1) * SSD_STATE].T
                gc = slice(g * GROUP_W, (g + 1) * GROUP_W)
                xt = xbc[:, gc] * wtail_x[:, gc].astype(BF16)
                ss[d, :, gc] = ss[d, :, gc] * a_x[:, gc] + _dot(bgt, xt)

        def k_transposed(pr):
            return kr_[:, pr * LANES:(pr + 1) * LANES].T

        def ret_state_update(d, tail_ref, kts):
            for pr, kt in enumerate(kts):
                pc = slice(pr * PAIR_W, (pr + 1) * PAIR_W)
                ktt = (kt.astype(F32) * tail_ref[pr]).astype(BF16)
                ds = _dot(ktt, vr[:, pc])
                for hh in range(2):
                    h = 2 * pr + hh
                    a = ar_ref[d * RET_HEADS + h:d * RET_HEADS + h + 1, :]
                    sr[d, h] = (sr[d, h] * a
                                + ds[hh * RET_QK_DIM:(hh + 1) * RET_QK_DIM, hh * LANES:(hh + 1) * LANES])

        def emit_states(d):
            for p in range(SSD_HEADS // 2):
                pair_t = ss[d, :, p * LANES:(p + 1) * LANES].T
                os_ref[u, 0, d, 2 * p] = pair_t[0:SSD_HEAD_DIM]
                os_ref[u, 0, d, 2 * p + 1] = pair_t[SSD_HEAD_DIM:2 * SSD_HEAD_DIM]
            or_ref[u, 0, d] = sr[d]

        def ret_state_blockdiag(d, pr):
            z = jnp.zeros((RET_QK_DIM, RET_V_DIM), F32)
            top = jnp.concatenate([sr[d, 2 * pr], z], axis=1)
            bot = jnp.concatenate([z, sr[d, 2 * pr + 1]], axis=1)
            return jnp.concatenate([top, bot], axis=0).astype(BF16)

        def phase1(update_state):
            dt = dtr[...]
            la = dt * nega_ref[...]
            tl_bf = jnp.where(rowq >= colq, 1.0, 0.0).astype(BF16)
            c3 = _dot(tl_bf, jnp.concatenate(_split3(la), axis=1))
            cum = c3[:, 0:LANES] + c3[:, LANES:2 * LANES] + c3[:, 2 * LANES:3 * LANES]
            cums[pl.ds(r0, Q), :] = cum
            tot = cum[Q - 1:Q, :]
            rev = tot - cum + la
            yield

            sub_k = lax.broadcasted_iota(jnp.int32, (LANES, Q), 0) < RET_QK_DIM
            lane_v = lax.broadcasted_iota(jnp.int32, (Q, PAIR_W), 1) < RET_V_DIM
            kts = [k_transposed(pr) for pr in range(RET_HEADS // 2)]
            for pr, kt in enumerate(kts):
                qp = qr[:, pr * LANES:(pr + 1) * LANES]
                zk = jnp.zeros_like(kt)
                kbd = jnp.concatenate([jnp.where(sub_k, kt, zk), jnp.where(sub_k, zk, kt)], axis=1)
                a2 = _dot(qp, kbd)
                ad = (a2 * dsum_ref[pr]).astype(BF16)
                pc = slice(pr * PAIR_W, (pr + 1) * PAIR_W)
                vp = vr[:, pc]
                zv = jnp.zeros_like(vp)
                vbd = jnp.concatenate([jnp.where(lane_v, vp, zv), jnp.where(lane_v, zv, vp)], axis=0)
                y = _dot(ad, vbd) + _dot(qp, ret_state_blockdiag(0, pr)) * ef_ref[pr]
                yar[pl.ds(r0, Q), pc] = y
            if update_state:
                ret_state_update(0, tf_ref, kts)
            yield

            gms, css = [], []
            for g in range(SSD_GROUPS):
                cg = xbc[:, XBC_C + g * SSD_STATE:XBC_C + (g + 1) * SSD_STATE]
                bg = xbc[:, XBC_B + g * SSD_STATE:XBC_B + (g + 1) * SSD_STATE]
                gms.append(_dot_nt(cg, bg))
                css.append(_dot(cg, ss[0, :, g * GROUP_W:(g + 1) * GROUP_W].astype(BF16)))
            yield PROJECT_POINT

            dt_t = dt.T
            ldt_t = jnp.log(dt_t)
            sub = lax.broadcasted_iota(jnp.int32, (LANES, Q), 0)
            adj_t = (jnp.where(sub < H, cum.T, rev.T) - ldt_t) * LOG2E
            dg_t = jnp.log(dt_t[0:H, :] + dt_t[H:2 * H, :]) * LOG2E
            cum2 = cum * LOG2E
            rev2 = rev * LOG2E
            wtail_f = dt * jnp.exp(tot - cum)
            ew_x, af_x = expand(jnp.concatenate([jnp.exp(cum), wtail_f], axis=0), jnp.exp(tot), exf_ref)
            ecum_x = ew_x[0:Q]
            wtail_x = ew_x[Q:2 * Q]
            lt = rowq > colq
            gt = rowq < colq
            lane_g = lax.broadcasted_iota(jnp.int32, (Q, GROUP_W), 1) // SSD_HEAD_DIM
            yield

            for g in range(SSD_GROUPS):
                gc = slice(g * GROUP_W, (g + 1) * GROUP_W)
                xg = xbc[:, gc]
                ws, xs = [], []
                for j in range(4):
                    h = 4 * g + j
                    hb = H + h
                    arg = jnp.where(lt, cum2[:, h:h + 1] - adj_t[h:h + 1, :],
                                    jnp.where(gt, rev2[:, hb:hb + 1] - adj_t[hb:hb + 1, :], dg_t[h:h + 1, :]))
                    ws.append((gms[g] * jnp.exp2(arg)).astype(BF16))
                    xs.append(jnp.where(lane_g == j, xg, jnp.zeros_like(xg)))
                y = _dot(jnp.concatenate(ws, axis=1), jnp.concatenate(xs, axis=0))
                yas[pl.ds(r0, Q), gc] = y + css[g] * ecum_x[:, gc]
            yield
            if update_state:
                ssd_state_update(0, wtail_x, af_x)

        def phase2(slot_w, update_state):
            for pr in range(RET_HEADS // 2):
                qp = qr[:, pr * LANES:(pr + 1) * LANES]
                pc = slice(pr * PAIR_W, (pr + 1) * PAIR_W)
                y2 = yar[pl.ds(r0, Q), pc] + _dot(qp, ret_state_blockdiag(1, pr)) * eb_ref[pr]
                for hh in range(2):
                    h = 2 * pr + hh
                    hc = slice(h * LANES, (h + 1) * LANES)
                    y = y2[:, hh * LANES:(hh + 1) * LANES]
                    mu = jnp.mean(y, axis=-1, keepdims=True)
                    yc = y - mu
                    var = jnp.mean(yc * yc, axis=-1, keepdims=True)
                    yn = yc * lax.rsqrt(var + EPS) * gsr[:, hc].astype(F32)
                    mixs_ref[slot_w, mix_rows, SSD_WIDTH + h * LANES:SSD_WIDTH + (h + 1) * LANES] = (
                        yn.astype(BF16))
            if update_state:
                ret_state_update(1, tb_ref, [k_transposed(pr) for pr in range(RET_HEADS // 2)])
            yield

            dt = dtr[...]
            cum = cums[pl.ds(r0, Q), :]
            rev = cum[Q - 1:Q, :] - cum + dt * nega_ref[...]
            first = rev[0:1, :]
            wtail_b = dt * jnp.exp(first - rev)
            ew_x, ab_x = expand(jnp.concatenate([jnp.exp(rev), wtail_b], axis=0), jnp.exp(first), exb_ref)
            erev_x = ew_x[0:Q]
            wtail_x = ew_x[Q:2 * Q]

            parts = []
            ssq = None
            for g in range(SSD_GROUPS):
                cg = xbc[:, XBC_C + g * SSD_STATE:XBC_C + (g + 1) * SSD_STATE]
                gc = slice(g * GROUP_W, (g + 1) * GROUP_W)
                cs = _dot(cg, ss[1, :, gc].astype(BF16))
                y = (yas[pl.ds(r0, Q), gc] + cs * erev_x[:, gc]
                     + dsk_ref[:, gc] * xbc[:, gc].astype(F32))
                y = y * zs[:, gc].astype(F32)
                parts.append(y)
                ssq = y * y if ssq is None else ssq + y * y
            yield PROJECT_POINT

            inv = lax.rsqrt(jnp.sum(ssq, axis=-1, keepdims=True) * (1.0 / SSD_WIDTH) + EPS)
            for g, y in enumerate(parts):
                gc = slice(g * GROUP_W, (g + 1) * GROUP_W)
                mixs_ref[slot_w, mix_rows, gc] = (y * inv).astype(BF16)
            if update_state:
                ssd_state_update(1, wtail_x, ab_x)

        return types.SimpleNamespace(init_state=init_state, emit_states=emit_states, phase1=phase1,
                                     phase2=phase2)

    streams = [make_stream(u) for u in range(U)]

    def run_interleaved(stage_gens, proj_slot=None):
        proj = None
        live = list(stage_gens)
        while live:
            marks, nxt = [], []
            for gen in live:
                try:
                    marks.append(next(gen))
                    nxt.append(gen)
                except StopIteration:
                    pass
            if proj_slot is not None and PROJECT_POINT in marks:
                proj = out_proj_matmul(proj_slot)
            live = nxt
        return proj

    @pl.when(is_real & (s == 0))
    def _init_state():
        for st in streams:
            st.init_state()

    last_f = s == nc - 1
    last_b = s == 2 * nc - 1
    forward = is_real & jnp.logical_not(phase2)
    if emit_state:
        forward_variants = [(forward, True)]
    else:
        forward_variants = [(forward & jnp.logical_not(last_f), True), (forward & last_f, False)]
    for cond, update_state in forward_variants:
        @pl.when(cond)
        def _forward_step():
            run_interleaved([st.phase1(update_state) for st in streams])

    backward_variants = []
    for parity in range(2):
        cond = is_real & phase2 & (c % 2 == parity)
        if emit_state or parity == 1:
            backward_variants.append((cond, parity, True))
        else:
            backward_variants += [(cond & jnp.logical_not(last_b), parity, True),
                                  (cond & last_b, parity, False)]
    for cond, parity, update_state in backward_variants:
        @pl.when(cond)
        def _backward_step():
            proj = run_interleaved([st.phase2(parity, update_state) for st in streams],
                                   proj_slot=1 - parity)
            out_proj_finish(proj)

    @pl.when(jnp.logical_not(is_real))
    def _drain():
        out_proj_finish(out_proj_matmul(0))

    if emit_state:
        @pl.when(is_real & (s == nc - 1))
        def _emit_fwd():
            for st in streams:
                st.emit_states(0)

        @pl.when(is_real & (s == 2 * nc - 1))
        def _emit_bwd():
            for st in streams:
                st.emit_states(1)


def _mixer_call(zs, xbc, dt, q, k, v, gs, nega, dsk, lamx, w_out_bf, x, mod, npost, s0_ssd, s0_ret,
                emit_state):
    nb, L, _ = zs.shape
    nc = L // CHUNK
    Q = CHUNK
    U = STREAMS
    assert nb % U == 0
    assert nc % 2 == 0
    has_s0 = s0_ssd is not None
    n_groups = nb // U
    steps = 2 * nc
    n_real = n_groups * steps
    per_seq_mod = mod.shape[0] > 1

    def cur(i):
        ic = jnp.minimum(i, n_real - 1)
        return ic // steps, ic % steps

    def chunk_map(i):
        b, s = cur(i)
        return (b, jnp.where(s < nc, s, steps - 1 - s), 0)

    def phase2_map(i):
        b, s = cur(i)
        return (b, jnp.where(s < nc, nc - 1, steps - 1 - s), 0)

    def lag(i):
        b, s = i // steps, i % steps
        early = s <= nc
        return (jnp.where(early, jnp.maximum(b - 1, 0), b),
                jnp.where(early, jnp.where(b == 0, nc - 1, 0), steps - s))

    def lag_map(i):
        bl, cl = lag(i)
        return (bl, cl, 0)

    def lag_mod_map(i):
        bl, _ = lag(i)
        return (bl if per_seq_mod else 0, 0, 0)

    const2 = lambda i: (0, 0)
    in_specs = [
        pl.BlockSpec((U, Q, SSD_WIDTH), phase2_map),
        pl.BlockSpec((U, Q, CONV_CH), chunk_map),
        pl.BlockSpec((U, Q, DT_PAD), chunk_map),
        pl.BlockSpec((U, Q, RET_QK_WIDTH), chunk_map),
        pl.BlockSpec((U, Q, RET_QK_WIDTH), chunk_map),
        pl.BlockSpec((U, Q, RET_V_WIDTH), chunk_map),
        pl.BlockSpec((U, Q, RET_V_WIDTH), phase2_map),
        pl.BlockSpec((1, DT_PAD), const2),
        pl.BlockSpec((1, SSD_WIDTH), const2),
        pl.BlockSpec((2 * RET_HEADS, LANES), const2),
        pl.BlockSpec((MIX_WIDTH, D_MODEL), const2, pipeline_mode=pl.Buffered(1)),
        pl.BlockSpec((U, Q, D_MODEL), lag_map),
        pl.BlockSpec((U if per_seq_mod else 1, 1, 3 * D_MODEL), lag_mod_map),
        pl.BlockSpec((1, D_MODEL), const2),
    ]
    args = [zs, xbc, dt, q, k, v, gs, nega, dsk, lamx, w_out_bf, x, mod, npost]
    state_map = lambda i: (cur(i)[0], 0, 0, 0, 0, 0)
    ssd_state_block = (U, 1, 2, SSD_HEADS, SSD_HEAD_DIM, SSD_STATE)
    ret_state_block = (U, 1, 2, RET_HEADS, RET_QK_DIM, RET_V_DIM)
    if has_s0:
        in_specs += [pl.BlockSpec(ssd_state_block, state_map, pipeline_mode=pl.Buffered(1)),
                     pl.BlockSpec(ret_state_block, state_map, pipeline_mode=pl.Buffered(1))]
        args += [s0_ssd, s0_ret]
    out_specs = [pl.BlockSpec((U, Q, D_MODEL), lag_map)]
    out_shape = [jax.ShapeDtypeStruct((nb, L, D_MODEL), F32)]
    if emit_state:
        out_specs += [pl.BlockSpec(ssd_state_block, state_map), pl.BlockSpec(ret_state_block, state_map)]
        out_shape += [jax.ShapeDtypeStruct((nb,) + ssd_state_block[1:], F32),
                      jax.ShapeDtypeStruct((nb,) + ret_state_block[1:], F32)]
    n_pairs = RET_HEADS // 2
    scratch = [
        pltpu.VMEM((U, L, SSD_WIDTH), F32),
        pltpu.VMEM((U, L, RET_V_WIDTH), F32),
        pltpu.VMEM((U, 2, SSD_STATE, SSD_WIDTH), F32),
        pltpu.VMEM((U, 2, RET_HEADS, RET_QK_DIM, RET_V_DIM), F32),
        pltpu.VMEM((n_pairs, Q, 2 * Q), F32),
        pltpu.VMEM((n_pairs, Q, PAIR_W), F32),
        pltpu.VMEM((n_pairs, Q, PAIR_W), F32),
        pltpu.VMEM((n_pairs, LANES, Q), F32),
        pltpu.VMEM((n_pairs, LANES, Q), F32),
        pltpu.VMEM((2 * RET_HEADS, LANES), F32),
        pltpu.VMEM((LANES, SSD_WIDTH), BF16),
        pltpu.VMEM((LANES, SSD_WIDTH), BF16),
        pltpu.VMEM((U, L, DT_PAD), F32),
        pltpu.VMEM((2, U * Q, MIX_WIDTH), BF16),
    ]
    outs = pl.pallas_call(
        functools.partial(_mixer_kernel, n_streams=U, n_groups=n_groups, nc=nc, has_s0=has_s0,
                          emit_state=emit_state,
                          per_seq_mod=per_seq_mod),
        grid=(n_real + 1,),
        in_specs=in_specs,
        out_specs=out_specs,
        out_shape=out_shape,
        scratch_shapes=scratch,
        compiler_params=pltpu.CompilerParams(
            dimension_semantics=("arbitrary",), vmem_limit_bytes=VMEM_LIMIT),
        name="mixer_sample" if has_s0 else "mixer_prompt",
    )(*args)
    return outs


def _rope_tables(L):
    rows = L // GRID_W
    row = np.repeat(np.arange(rows, dtype=np.float64), GRID_W)
    col = np.tile(np.arange(GRID_W, dtype=np.float64), rows)
    half = RET_QK_DIM // 2
    inv = ROPE_BASE ** (-np.arange(0, half, 2, dtype=np.float64) / half)
    ang_r = row[:, None] * inv
    ang_c = col[:, None] * inv
    cos_h = np.concatenate([np.cos(ang_r), np.cos(ang_r), np.cos(ang_c), np.cos(ang_c)], axis=-1)
    sin_h = np.concatenate([-np.sin(ang_r), np.sin(ang_r), -np.sin(ang_c), np.sin(ang_c)], axis=-1)
    return (jnp.asarray(np.tile(cos_h, (1, 2)), dtype=F32), jnp.asarray(np.tile(sin_h, (1, 2)), dtype=F32))


def kernel(x_prompt, x_sample, state_ssd, state_ret, c, c_ctx, w_mod, b_mod, norm_pre_w, norm_post_w,
           w_in, conv_w, conv_b, ssd_A_log, ssd_dt_bias, ssd_D, ssd_norm_w, ret_decay, ret_norm_w, w_out):
    assert w_in.shape[0] == 1, "one layer (DEPTH == 1) is supported"
    l = 0

    n_dt = 2 * SSD_HEADS
    mix_norm_w = jnp.concatenate([ssd_norm_w[l], ret_norm_w[l]]).reshape(MIX_WIDTH, 1)
    w_packed, w_out_bf, mod_s, mod_p = _prep_call(
        jnp.swapaxes(w_in[l], 0, 1), w_out[l], mix_norm_w, c, c_ctx.reshape(1, D_MODEL), w_mod[l],
        b_mod[l].reshape(1, 3 * D_MODEL))
    dtb = jnp.pad(ssd_dt_bias[l].reshape(1, n_dt), ((0, 0), (0, DT_PAD - n_dt)))
    nega = jnp.pad(-jnp.exp(ssd_A_log[l].reshape(1, n_dt)), ((0, 0), (0, DT_PAD - n_dt)))
    dsk = jnp.repeat(ssd_D[l], SSD_HEAD_DIM).reshape(1, SSD_WIDTH)
    lamx = jnp.broadcast_to(-jnp.exp(ret_decay[l].reshape(2 * RET_HEADS, 1)), (2 * RET_HEADS, LANES))
    npre = norm_pre_w[l].reshape(1, D_MODEL)
    npost = norm_post_w[l].reshape(1, D_MODEL)
    cb = conv_b[l].reshape(1, CONV_CH)
    cos_t, sin_t = _rope_tables(x_sample.shape[1])

    pp = _in_proj_call(x_prompt, mod_p, npre, w_packed, conv_w[l], cb, dtb, None, None)
    y_p, st_ssd_t, st_ret = _mixer_call(*pp, nega, dsk, lamx, w_out_bf, x_prompt, mod_p, npost,
                                        None, None, True)

    ps = _in_proj_call(x_sample, mod_s, npre, w_packed, conv_w[l], cb, dtb, cos_t, sin_t)
    (y_s,) = _mixer_call(*ps, nega, dsk, lamx, w_out_bf, x_sample, mod_s, npost,
                         jnp.swapaxes(state_ssd, -1, -2), state_ret, False)
    return (y_p, y_s, jnp.swapaxes(st_ssd_t, -1, -2), st_ret)
```
